```python
import math
import jax, jax.numpy as jnp
from jax import lax
import numpy as np

D_MODEL = 1024
BATCH = 4
SEQ = 4096
DEPTH = 4
DEC_BATCH = 128
DEC_SEQ = 8
PAST_LEN = 2048
PAGE_SIZE = 128

N_A_LAYERS = DEPTH // 2
N_B_LAYERS = DEPTH - N_A_LAYERS
SSM_GROUP = 16
SSM_GROUPS = D_MODEL // SSM_GROUP
SSM_STATE = 64
N_HEADS = 16
HEAD_DIM = D_MODEL // N_HEADS
N_KV_HEADS = 4
Q_PER_KV = N_HEADS // N_KV_HEADS
CMP_LEN = 32
CMP_STRIDE = 16
CMP_HIDDEN = 2 * HEAD_DIM
SEL_BLOCK = 64
SEL_TOPK = 16
WINDOW = 512
N_BRANCH = 3
KV_SLOTS_PAGED = 4
KV_SLOTS_WIN = 2
Q_BLOCK = 64
N_BUCKETS = 32
MAX_DISTANCE = 128
N_EXPERT_GROUPS = 4
EXPERTS_PER_GROUP = 4
N_EXPERTS = N_EXPERT_GROUPS * EXPERTS_PER_GROUP
EXPERT_TOPK = 2
D_EXPERT = 256
D_PLE = 256
RMS_EPS = 1e-6
NEG_INF = -1e30
FORCE = 1e4

kernel_name = 'yoco_s5_nsa_hmoe_step'


def rmsnorm(x, g):
    xf = x.astype(jnp.float32)
    y = xf * lax.rsqrt(jnp.mean(xf * xf, axis=-1, keepdims=True) + RMS_EPS)
    return (y * g.astype(jnp.float32)).astype(x.dtype)


def cmul(ar, ai, br, bi):
    return ar * br - ai * bi, ar * bi + ai * br


def _ssm_combine(e1, e2):
    a1r, a1i, b1r, b1i = e1
    a2r, a2i, b2r, b2i = e2
    ar, ai = cmul(a2r, a2i, a1r, a1i)
    br, bi = cmul(a2r, a2i, b1r, b1i)
    return ar, ai, br + b2r, bi + b2i


def s5_mixer(u, s0_re, s0_im, a_re, a_im, log_dt, b_re, b_im, c_re, c_im, d_skip, w_glu):
    f32 = jnp.float32
    bsz, t_len, _ = u.shape
    uf = u.astype(f32).reshape(bsz, t_len, SSM_GROUPS, SSM_GROUP)
    dt = jnp.exp(log_dt.astype(f32))[:, None]
    lam_re, lam_im = a_re.astype(f32), a_im.astype(f32)
    decay = jnp.exp(lam_re * dt)
    ab_re, ab_im = decay * jnp.cos(lam_im * dt), decay * jnp.sin(lam_im * dt)
    den = lam_re * lam_re + lam_im * lam_im
    num_re = ab_re - 1.0
    q_re = (num_re * lam_re + ab_im * lam_im) / den
    q_im = (ab_im * lam_re - num_re * lam_im) / den
    bb_re, bb_im = cmul(q_re[..., None], q_im[..., None], b_re.astype(f32), b_im.astype(f32))
    bu_re = jnp.einsum('btgc,gpc->btgp', uf, bb_re)
    bu_im = jnp.einsum('btgc,gpc->btgp', uf, bb_im)
    i0_re, i0_im = cmul(ab_re, ab_im, s0_re.astype(f32), s0_im.astype(f32))
    bu_re = bu_re.at[:, 0].add(i0_re)
    bu_im = bu_im.at[:, 0].add(i0_im)
    a_seq_re = jnp.broadcast_to(ab_re, (1, t_len) + ab_re.shape)
    a_seq_im = jnp.broadcast_to(ab_im, (1, t_len) + ab_im.shape)
    _, _, s_re, s_im = lax.associative_scan(_ssm_combine, (a_seq_re, a_seq_im, bu_re, bu_im), axis=1)
    y = (jnp.einsum('btgp,gcp->btgc', s_re, c_re.astype(f32))
         - jnp.einsum('btgp,gcp->btgc', s_im, c_im.astype(f32))
         + d_skip.astype(f32) * uf)
    z = jax.nn.gelu(y.reshape(bsz, t_len, D_MODEL)).astype(u.dtype)
    a, g = jnp.split(z @ w_glu, 2, axis=-1)
    return a * jax.nn.sigmoid(g), s_re[:, -1], s_im[:, -1]


def hmoe(h, w_rg, b_rg, w_re, b_re, w_up, w_down):
    bsz, t_len, d = h.shape
    hf = h.reshape(-1, d)
    n_tok = hf.shape[0]
    pg = jax.nn.softmax((hf @ w_rg).astype(jnp.float32) + b_rg.astype(jnp.float32), axis=-1)
    gp, gi = lax.top_k(pg, 1)
    le = ((hf @ w_re).astype(jnp.float32) + b_re.astype(jnp.float32)).reshape(n_tok, N_EXPERT_GROUPS, EXPERTS_PER_GROUP)
    le_g = jnp.take_along_axis(le, gi[:, :, None], axis=1)[:, 0]
    ev, ei = lax.top_k(le_g, EXPERT_TOPK)
    wts = jax.nn.softmax(ev, axis=-1) * gp
    eid = gi * EXPERTS_PER_GROUP + ei
    comb = jnp.sum(jax.nn.one_hot(eid, N_EXPERTS, dtype=jnp.float32) * wts[..., None], axis=1)
    up = jnp.einsum('nd,edf->nef', hf, w_up)
    a, b = jnp.split(up, 2, axis=-1)
    act = jax.nn.silu(a) * b * comb[:, :, None].astype(h.dtype)
    out = jnp.einsum('nef,efd->nd', act, w_down)
    return out.reshape(bsz, t_len, d)


def rel_bucket(dist):
    max_exact = N_BUCKETS // 2
    d = jnp.maximum(dist, 0)
    df = jnp.maximum(d, 1).astype(jnp.float32)
    large = max_exact + (jnp.log(df / max_exact) / math.log(MAX_DISTANCE / max_exact)
                         * (N_BUCKETS - max_exact)).astype(jnp.int32)
    large = jnp.minimum(large, N_BUCKETS - 1)
    return jnp.where(d < max_exact, d, large)


def shared_bias(rel_bias, dist):
    b = rel_bias.astype(jnp.float32)[rel_bucket(dist)]
    return jnp.moveaxis(b, -1, 0).reshape((N_KV_HEADS, Q_PER_KV) + dist.shape)


def per_head_bias(rel_bias, dist):
    tab = rel_bias.astype(jnp.float32).T.reshape(N_KV_HEADS, Q_PER_KV, N_BUCKETS)
    bk = rel_bucket(dist)
    f = lambda tab_h, bk_h: jnp.moveaxis(tab_h[:, bk_h], 0, 1)
    return jax.vmap(f, in_axes=(0, 1), out_axes=1)(tab, bk)


def masked_softmax(s, mask):
    p = jax.nn.softmax(jnp.where(mask, s, NEG_INF), axis=-1)
    return jnp.where(mask, p, 0.0)


def compress(rows, pos_emb, w1, w2):
    bsz, tk = rows.shape[:2]
    n_cmp = (tk - CMP_LEN) // CMP_STRIDE + 1
    idx = jnp.arange(n_cmp)[:, None] * CMP_STRIDE + jnp.arange(CMP_LEN)[None, :]
    blk = rows[:, idx] + pos_emb[None, None, :, None, :]
    blk = jnp.moveaxis(blk, 3, 2).reshape(bsz, n_cmp, N_KV_HEADS, CMP_LEN * HEAD_DIM)
    return jax.nn.gelu(blk @ w1) @ w2


def prepare_kv(kv_rows, win_rows, win_base, prm):
    bsz, tk = kv_rows.shape[:2]
    k_cmp = compress(kv_rows[:, :, 0], prm['cmp_pos_k'], prm['cmp_w1_k'], prm['cmp_w2_k'])
    v_cmp = compress(kv_rows[:, :, 1], prm['cmp_pos_v'], prm['cmp_w1_v'], prm['cmp_w2_v'])
    n_sel = -(-tk // SEL_BLOCK)
    sel = jnp.pad(kv_rows[:, :, 2:4], ((0, 0), (0, n_sel * SEL_BLOCK - tk), (0, 0), (0, 0), (0, 0)))
    sel = jnp.transpose(sel.reshape(bsz, n_sel, SEL_BLOCK, 2, N_KV_HEADS, HEAD_DIM), (3, 0, 4, 1, 2, 5))
    win = jnp.pad(win_rows, ((0, 0), (WINDOW, 0), (0, 0), (0, 0), (0, 0)))
    return {'kc': k_cmp, 'vc': v_cmp, 'ks': sel[0], 'vs': sel[1],
            'kw': win[:, :, 0], 'vw': win[:, :, 1], 'win_base': win_base}


def nsa_block(qb, gb, q0, side, rel_bias):
    scale = HEAD_DIM ** -0.5
    bsz, n_q = qb.shape[:2]
    t = q0 + jnp.arange(n_q, dtype=jnp.int32)
    kc, vc, ks, vs = side['kc'], side['vc'], side['ks'], side['vs']
    n_cmp, n_sel = kc.shape[1], ks.shape[2]
    s_c = jnp.einsum('bqhgd,bnhd->bhgqn', qb, kc).astype(jnp.float32) * scale
    c_start = jnp.arange(n_cmp, dtype=jnp.int32) * CMP_STRIDE
    d_c = t[:, None] - (c_start + CMP_LEN - 1)[None, :]
    p_c = masked_softmax(s_c + shared_bias(rel_bias, d_c), d_c >= 0)
    o_c = jnp.einsum('bhgqn,bnhd->bqhgd', p_c, vc)
    s_start = jnp.arange(n_sel, dtype=jnp.int32) * SEL_BLOCK
    overlap = ((c_start[:, None] < s_start[None, :] + SEL_BLOCK)
               & (c_start[:, None] + CMP_LEN > s_start[None, :])).astype(jnp.float32)
    imp = jnp.einsum('bhgqn,ns->bhqs', p_c, overlap)
    blk = jnp.arange(n_sel, dtype=jnp.int32)[None, :]
    cur = (t // SEL_BLOCK)[:, None]
    valid_b = blk * SEL_BLOCK <= t[:, None]
    forced = (blk == 0) | (blk == cur) | (blk == cur - 1)
    score = jnp.where(forced, FORCE, jnp.where(valid_b, imp, -FORCE))
    k_top = min(SEL_TOPK, n_sel)
    _, idx = lax.top_k(score, k_top)
    gather = jax.vmap(jax.vmap(lambda blocks, ix: blocks[ix]))
    k_s = gather(ks, idx).reshape(bsz, N_KV_HEADS, n_q, k_top * SEL_BLOCK, HEAD_DIM)
    v_s = gather(vs, idx).reshape(bsz, N_KV_HEADS, n_q, k_top * SEL_BLOCK, HEAD_DIM)
    pos_s = (idx[..., None] * SEL_BLOCK + jnp.arange(SEL_BLOCK, dtype=jnp.int32)).reshape(bsz, N_KV_HEADS, n_q, -1)
    d_s = t[None, None, :, None] - pos_s
    s_s = jnp.einsum('bqhgd,bhqsd->bhgqs', qb, k_s).astype(jnp.float32) * scale + per_head_bias(rel_bias, d_s)
    p_s = masked_softmax(s_s, (d_s >= 0)[:, :, None])
    o_s = jnp.einsum('bhgqs,bhqsd->bqhgd', p_s, v_s)
    win_base = side['win_base']
    span = WINDOW + n_q
    k_w = lax.dynamic_slice_in_dim(side['kw'], q0 - win_base, span, axis=1)
    v_w = lax.dynamic_slice_in_dim(side['vw'], q0 - win_base, span, axis=1)
    pos_w = q0 - WINDOW + jnp.arange(span, dtype=jnp.int32)
    d_w = t[:, None] - pos_w[None, :]
    m_w = (d_w >= 0) & (d_w <= WINDOW) & (pos_w[None, :] >= win_base)
    s_w = jnp.einsum('bqhgd,blhd->bhgql', qb, k_w).astype(jnp.float32) * scale + shared_bias(rel_bias, d_w)
    p_w = masked_softmax(s_w, m_w)
    o_w = jnp.einsum('bhgql,blhd->bqhgd', p_w, v_w)
    out = gb[..., 0:1] * o_c + gb[..., 1:2] * o_s + gb[..., 2:3] * o_w
    return out.astype(qb.dtype)


def nsa_mixer(h, pos0, side, w_qg, w_o, rel_bias):
    bsz, t_len, _ = h.shape
    qg = h @ w_qg
    q = qg[..., :N_HEADS * HEAD_DIM].reshape(bsz, t_len, N_KV_HEADS, Q_PER_KV, HEAD_DIM)
    gates = jax.nn.sigmoid(qg[..., N_HEADS * HEAD_DIM:].astype(jnp.float32)).reshape(
        bsz, t_len, N_KV_HEADS, Q_PER_KV, N_BRANCH)
    qb = math.gcd(Q_BLOCK, t_len)
    nb = t_len // qb
    q_blocks = jnp.moveaxis(q.reshape(bsz, nb, qb, N_KV_HEADS, Q_PER_KV, HEAD_DIM), 1, 0)
    g_blocks = jnp.moveaxis(gates.reshape(bsz, nb, qb, N_KV_HEADS, Q_PER_KV, N_BRANCH), 1, 0)
    q0s = pos0 + qb * jnp.arange(nb, dtype=jnp.int32)
    out = lax.map(lambda blk: nsa_block(blk[0], blk[1], blk[2], side, rel_bias), (q_blocks, g_blocks, q0s))
    out = jnp.moveaxis(out, 0, 1).reshape(bsz, t_len, N_HEADS * HEAD_DIM)
    return out @ w_o


def run_trunk(x, p, s0_re, s0_im, past_kv, win_buf, pos0, prm):
    bsz, t_len, _ = x.shape
    ssm_re, ssm_im = [], []
    side = None
    kv_rows_new = None
    win_state = None
    for i in range(DEPTH):
        h = rmsnorm(x, prm['g_mix'][i])
        if i < N_A_LAYERS:
            mix, s_re, s_im = s5_mixer(h, s0_re[i], s0_im[i], prm['ssm_a_re'][i], prm['ssm_a_im'][i],
                                       prm['ssm_log_dt'][i], prm['ssm_b_re'][i], prm['ssm_b_im'][i],
                                       prm['ssm_c_re'][i], prm['ssm_c_im'][i], prm['ssm_d'][i], prm['w_glu'][i])
            ssm_re.append(s_re)
            ssm_im.append(s_im)
        else:
            j = i - N_A_LAYERS
            mix = nsa_mixer(h, pos0, side, prm['w_qg'][j], prm['w_o'][j], prm['rel_bias'])
        x = x + mix
        x = x + hmoe(rmsnorm(x, prm['g_ffn'][i]), prm['w_route_group'][i], prm['b_route_group'][i],
                     prm['w_route_expert'][i], prm['b_route_expert'][i], prm['w_exp_up'][i], prm['w_exp_down'][i])
        gate = jax.nn.sigmoid(rmsnorm(x, prm['g_ple'][i]) @ prm['w_ple_gate'][i])
        x = x + (p[i] @ prm['w_ple_proj'][i]) * gate
        if i == N_A_LAYERS - 1:
            kv = (rmsnorm(x, prm['g_kv']) @ prm['w_kv']).reshape(
                bsz, t_len, KV_SLOTS_PAGED + KV_SLOTS_WIN, N_KV_HEADS, HEAD_DIM)
            kv_rows_new, win_new = kv[:, :, :KV_SLOTS_PAGED], kv[:, :, KV_SLOTS_PAGED:]
            if past_kv is None:
                kv_rows, win_rows, win_base = kv_rows_new, win_new, 0
                win_state = win_new[:, -min(WINDOW, t_len):]
            else:
                kv_rows = jnp.concatenate([past_kv.astype(kv.dtype), kv_rows_new], axis=1)
                w_buf = win_buf.shape[1]
                win_rows = jnp.concatenate([win_buf.astype(kv.dtype), win_new], axis=1)
                win_base = pos0 - w_buf
                win_state = win_rows[:, -w_buf:]
            side = prepare_kv(kv_rows, win_rows, win_base, prm)
    y = rmsnorm(x, prm['g_final'])
    return y, kv_rows_new, win_state, jnp.stack(ssm_re), jnp.stack(ssm_im)


def setup_inputs(seed: int = 0) -> dict:
    key = jax.random.key(seed)
    ks = iter(jax.random.split(key, 64))
    f32 = jnp.float32

    def nrm(shape, scale):
        return jax.random.normal(next(ks), shape, f32) * scale

    n_pages = PAST_LEN // PAGE_SIZE
    n_phys = (DEC_BATCH * n_pages * 5) // 4
    w_buf = min(WINDOW, PAST_LEN)
    page_table = jax.random.permutation(next(ks), n_phys)[:DEC_BATCH * n_pages].reshape(
        DEC_BATCH, n_pages).astype(jnp.int32)
    a_im = jnp.pi * jnp.broadcast_to(jnp.arange(SSM_STATE, dtype=f32), (N_A_LAYERS, SSM_GROUPS, SSM_STATE))
    return {
        'x_prompt': nrm((BATCH, SEQ, D_MODEL), 1.0),
        'x_sample': nrm((DEC_BATCH, DEC_SEQ, D_MODEL), 1.0),
        'p_prompt': nrm((DEPTH, BATCH, SEQ, D_PLE), 1.0),
        'p_sample': nrm((DEPTH, DEC_BATCH, DEC_SEQ, D_PLE), 1.0),
        'cache_kv': nrm((n_phys, PAGE_SIZE, KV_SLOTS_PAGED, N_KV_HEADS, HEAD_DIM), 1.0),
        'cache_win': nrm((DEC_BATCH, w_buf, KV_SLOTS_WIN, N_KV_HEADS, HEAD_DIM), 1.0),
        'state_ssm_re': nrm((N_A_LAYERS, DEC_BATCH, SSM_GROUPS, SSM_STATE), 0.3),
        'state_ssm_im': nrm((N_A_LAYERS, DEC_BATCH, SSM_GROUPS, SSM_STATE), 0.3),
        'page_table': page_table,
        'g_mix': 1.0 + nrm((DEPTH, D_MODEL), 0.02),
        'g_ffn': 1.0 + nrm((DEPTH, D_MODEL), 0.02),
        'g_ple': 1.0 + nrm((DEPTH, D_MODEL), 0.02),
        'g_kv': 1.0 + nrm((D_MODEL,), 0.02),
        'g_final': 1.0 + nrm((D_MODEL,), 0.02),
        'ssm_a_re': -0.5 + nrm((N_A_LAYERS, SSM_GROUPS, SSM_STATE), 0.01),
        'ssm_a_im': a_im + nrm((N_A_LAYERS, SSM_GROUPS, SSM_STATE), 0.01),
        'ssm_log_dt': jax.random.uniform(next(ks), (N_A_LAYERS, SSM_GROUPS), f32, math.log(1e-3), math.log(1e-1)),
        'ssm_b_re': nrm((N_A_LAYERS, SSM_GROUPS, SSM_STATE, SSM_GROUP), SSM_GROUP ** -0.5),
        'ssm_b_im': nrm((N_A_LAYERS, SSM_GROUPS, SSM_STATE, SSM_GROUP), SSM_GROUP ** -0.5),
        'ssm_c_re': nrm((N_A_LAYERS, SSM_GROUPS, SSM_GROUP, SSM_STATE), SSM_STATE ** -0.5),
        'ssm_c_im': nrm((N_A_LAYERS, SSM_GROUPS, SSM_GROUP, SSM_STATE), SSM_STATE ** -0.5),
        'ssm_d': nrm((N_A_LAYERS, SSM_GROUPS, SSM_GROUP), 1.0),
        'w_glu': nrm((N_A_LAYERS, D_MODEL, 2 * D_MODEL), D_MODEL ** -0.5),
        'w_kv': nrm((D_MODEL, (KV_SLOTS_PAGED + KV_SLOTS_WIN) * N_KV_HEADS * HEAD_DIM), D_MODEL ** -0.5),
        'cmp_pos_k': nrm((CMP_LEN, HEAD_DIM), 0.02),
        'cmp_pos_v': nrm((CMP_LEN, HEAD_DIM), 0.02),
        'cmp_w1_k': nrm((CMP_LEN * HEAD_DIM, CMP_HIDDEN), (CMP_LEN * HEAD_DIM) ** -0.5),
        'cmp_w2_k': nrm((CMP_HIDDEN, HEAD_DIM), CMP_HIDDEN ** -0.5),
        'cmp_w1_v': nrm((CMP_LEN * HEAD_DIM, CMP_HIDDEN), (CMP_LEN * HEAD_DIM) ** -0.5),
        'cmp_w2_v': nrm((CMP_HIDDEN, HEAD_DIM), CMP_HIDDEN ** -0.5),
        'w_qg': nrm((N_B_LAYERS, D_MODEL, N_HEADS * HEAD_DIM + N_BRANCH * N_HEADS), D_MODEL ** -0.5),
        'w_o': nrm((N_B_LAYERS, N_HEADS * HEAD_DIM, D_MODEL), (N_HEADS * HEAD_DIM) ** -0.5),
        'rel_bias': nrm((N_BUCKETS, N_HEADS), 0.1),
        'w_route_group': nrm((DEPTH, D_MODEL, N_EXPERT_GROUPS), D_MODEL ** -0.5),
        'b_route_group': nrm((DEPTH, N_EXPERT_GROUPS), 0.01),
        'w_route_expert': nrm((DEPTH, D_MODEL, N_EXPERTS), D_MODEL ** -0.5),
        'b_route_expert': nrm((DEPTH, N_EXPERTS), 0.01),
        'w_exp_up': nrm((DEPTH, N_EXPERTS, D_MODEL, 2 * D_EXPERT), D_MODEL ** -0.5),
        'w_exp_down': nrm((DEPTH, N_EXPERTS, D_EXPERT, D_MODEL), D_EXPERT ** -0.5),
        'w_ple_proj': nrm((DEPTH, D_PLE, D_MODEL), D_PLE ** -0.5),
        'w_ple_gate': nrm((DEPTH, D_MODEL, D_MODEL), D_MODEL ** -0.5),
    }


def reference(x_prompt, x_sample, p_prompt, p_sample, cache_kv, cache_win, state_ssm_re, state_ssm_im, page_table,
              g_mix, g_ffn, g_ple, g_kv, g_final,
              ssm_a_re, ssm_a_im, ssm_log_dt, ssm_b_re, ssm_b_im, ssm_c_re, ssm_c_im, ssm_d, w_glu,
              w_kv, cmp_pos_k, cmp_pos_v, cmp_w1_k, cmp_w2_k, cmp_w1_v, cmp_w2_v, w_qg, w_o, rel_bias,
              w_route_group, b_route_group, w_route_expert, b_route_expert, w_exp_up, w_exp_down,
              w_ple_proj, w_ple_gate):
    prm = {
        'g_mix': g_mix, 'g_ffn': g_ffn, 'g_ple': g_ple, 'g_kv': g_kv, 'g_final': g_final,
        'ssm_a_re': ssm_a_re, 'ssm_a_im': ssm_a_im, 'ssm_log_dt': ssm_log_dt,
        'ssm_b_re': ssm_b_re, 'ssm_b_im': ssm_b_im, 'ssm_c_re': ssm_c_re, 'ssm_c_im': ssm_c_im,
        'ssm_d': ssm_d, 'w_glu': w_glu,
        'w_kv': w_kv, 'cmp_pos_k': cmp_pos_k, 'cmp_pos_v': cmp_pos_v,
        'cmp_w1_k': cmp_w1_k, 'cmp_w2_k': cmp_w2_k, 'cmp_w1_v': cmp_w1_v, 'cmp_w2_v': cmp_w2_v,
        'w_qg': w_qg, 'w_o': w_o, 'rel_bias': rel_bias,
        'w_route_group': w_route_group, 'b_route_group': b_route_group,
        'w_route_expert': w_route_expert, 'b_route_expert': b_route_expert,
        'w_exp_up': w_exp_up, 'w_exp_down': w_exp_down,
        'w_ple_proj': w_ple_proj, 'w_ple_gate': w_ple_gate,
    }
    zero_state = jnp.zeros((N_A_LAYERS, x_prompt.shape[0], SSM_GROUPS, SSM_STATE), jnp.float32)
    y_prompt, kv_prompt, win_prompt, ssm_re_prompt, ssm_im_prompt = run_trunk(
        x_prompt, p_prompt, zero_state, zero_state, None, None, 0, prm)
    n_seq, n_pages = page_table.shape
    past_len = n_pages * cache_kv.shape[1]
    past_kv = cache_kv[page_table].reshape((n_seq, past_len) + cache_kv.shape[2:])
    y_sample, kv_sample, win_sample, ssm_re_sample, ssm_im_sample = run_trunk(
        x_sample, p_sample, state_ssm_re, state_ssm_im, past_kv, cache_win, past_len, prm)
    return (y_prompt, y_sample, kv_prompt, win_prompt, ssm_re_prompt, ssm_im_prompt,
            kv_sample, win_sample, ssm_re_sample, ssm_im_sample)
```

```python
import functools
import math

import jax
import jax.numpy as jnp
from jax import lax
from jax.experimental import pallas as pl
from jax.experimental.pallas import tpu as pltpu

D_MODEL = 1024
DEPTH = 4
N_A_LAYERS = DEPTH // 2
SSM_GROUP = 16
SSM_GROUPS = D_MODEL // SSM_GROUP
SSM_STATE = 64
N_HEADS = 16
HEAD_DIM = D_MODEL // N_HEADS
N_KV_HEADS = 4
Q_PER_KV = N_HEADS // N_KV_HEADS
CMP_LEN = 32
CMP_STRIDE = 16
SEL_BLOCK = 64
SEL_TOPK = 16
WINDOW = 512
N_BRANCH = 3
KV_SLOTS_PAGED = 4
KV_SLOTS_WIN = 2
Q_BLOCK = 64
N_BUCKETS = 32
MAX_DISTANCE = 128
N_EXPERT_GROUPS = 4
EXPERTS_PER_GROUP = 4
N_EXPERTS = N_EXPERT_GROUPS * EXPERTS_PER_GROUP
EXPERT_TOPK = 2
D_EXPERT = 256
RMS_EPS = 1e-6
NEG_INF = -1e30
FORCE = 1e4

LANES = 128
VMEM_LIMIT = 48 * 1024 * 1024

_F32 = jnp.float32
_BF16 = jnp.bfloat16


def _moe_kernel(x_ref, g_ref, wr_ref, br_ref, wup_ref, wdn_ref, o_ref, h_scr, comb_scr, acc_scr):
    e = pl.program_id(1)
    lane = lax.broadcasted_iota(jnp.int32, comb_scr.shape, 1)

    @pl.when(e == 0)
    def _route():
        x = x_ref[...]
        h = x * lax.rsqrt(jnp.mean(x * x, axis=-1, keepdims=True) + RMS_EPS) * g_ref[...]
        h_scr[...] = h.astype(_BF16)
        logits = jnp.dot(h, wr_ref[...], preferred_element_type=_F32,
                         precision=lax.Precision.HIGHEST) + br_ref[...]
        is_grp = (lane >= N_EXPERTS) & (lane < N_EXPERTS + N_EXPERT_GROUPS)
        lg = jnp.where(is_grp, logits, -jnp.inf)
        gmax = jnp.max(lg, axis=-1, keepdims=True)
        gi = jnp.min(jnp.where(lg == gmax, lane, LANES), axis=-1, keepdims=True) - N_EXPERTS
        gp = 1.0 / jnp.sum(jnp.where(is_grp, jnp.exp(lg - gmax), 0.0), axis=-1, keepdims=True)
        in_grp = (lane < N_EXPERTS) & ((lane // EXPERTS_PER_GROUP) == gi)
        le = jnp.where(in_grp, logits, -jnp.inf)
        m1 = jnp.max(le, axis=-1, keepdims=True)
        i1 = jnp.min(jnp.where(le == m1, lane, LANES), axis=-1, keepdims=True)
        le2 = jnp.where(lane == i1, -jnp.inf, le)
        m2 = jnp.max(le2, axis=-1, keepdims=True)
        i2 = jnp.min(jnp.where(le2 == m2, lane, LANES), axis=-1, keepdims=True)
        e2 = jnp.exp(m2 - m1)
        den = 1.0 + e2
        comb_scr[...] = jnp.where(lane == i1, gp / den, jnp.where(lane == i2, gp * e2 / den, 0.0))
        acc_scr[...] = x

    up = jnp.dot(h_scr[...], wup_ref[0], preferred_element_type=_F32)
    a = up[:, :D_EXPERT]
    b = up[:, D_EXPERT:]
    c = jnp.sum(jnp.where(lane == e, comb_scr[...], 0.0), axis=-1, keepdims=True)
    act = (a * jax.nn.sigmoid(a)) * b * c
    acc_scr[...] += jnp.dot(act.astype(_BF16), wdn_ref[0], preferred_element_type=_F32)

    @pl.when(e == N_EXPERTS - 1)
    def _store():
        o_ref[...] = acc_scr[...]


def _moe_layer(x2, g, w_rg, b_rg, w_re, b_re, w_up, w_dn):
    n_tok, d = x2.shape
    tm = min(n_tok, 1024)
    wr = jnp.zeros((d, LANES), _F32).at[:, :N_EXPERTS].set(w_re).at[:, N_EXPERTS:N_EXPERTS + N_EXPERT_GROUPS].set(w_rg)
    br = jnp.zeros((1, LANES), _F32).at[0, :N_EXPERTS].set(b_re).at[0, N_EXPERTS:N_EXPERTS + N_EXPERT_GROUPS].set(b_rg)
    return pl.pallas_call(
        _moe_kernel,
        grid=(n_tok // tm, N_EXPERTS),
        in_specs=[
            pl.BlockSpec((tm, d), lambda i, e: (i, 0)),
            pl.BlockSpec((1, d), lambda i, e: (0, 0)),
            pl.BlockSpec((d, LANES), lambda i, e: (0, 0)),
            pl.BlockSpec((1, LANES), lambda i, e: (0, 0)),
            pl.BlockSpec((1, d, 2 * D_EXPERT), lambda i, e: (e, 0, 0)),
            pl.BlockSpec((1, D_EXPERT, d), lambda i, e: (e, 0, 0)),
        ],
        out_specs=pl.BlockSpec((tm, d), lambda i, e: (i, 0)),
        out_shape=jax.ShapeDtypeStruct((n_tok, d), _F32),
        scratch_shapes=[
            pltpu.VMEM((tm, d), _BF16),
            pltpu.VMEM((tm, LANES), _F32),
            pltpu.VMEM((tm, d), _F32),
        ],
        compiler_params=pltpu.CompilerParams(
            dimension_semantics=("parallel", "arbitrary"), vmem_limit_bytes=VMEM_LIMIT),
        name="hmoe",
    )(x2, g.reshape(1, d), wr, br, w_up.astype(_BF16), w_dn.astype(_BF16))


def _rmsnorm(x, g):
    xf = x.astype(_F32)
    y = xf * lax.rsqrt(jnp.mean(xf * xf, axis=-1, keepdims=True) + RMS_EPS)
    return (y * g.astype(_F32)).astype(x.dtype)


def _cmul(ar, ai, br, bi):
    return ar * br - ai * bi, ar * bi + ai * br


def _ssm_combine(e1, e2):
    a1r, a1i, b1r, b1i = e1
    a2r, a2i, b2r, b2i = e2
    ar, ai = _cmul(a2r, a2i, a1r, a1i)
    br, bi = _cmul(a2r, a2i, b1r, b1i)
    return ar, ai, br + b2r, bi + b2i


def _s5_mixer(u, s0_re, s0_im, a_re, a_im, log_dt, b_re, b_im, c_re, c_im, d_skip, w_glu):
    bsz, t_len, _ = u.shape
    uf = u.reshape(bsz, t_len, SSM_GROUPS, SSM_GROUP)
    dt = jnp.exp(log_dt)[:, None]
    decay = jnp.exp(a_re * dt)
    ab_re, ab_im = decay * jnp.cos(a_im * dt), decay * jnp.sin(a_im * dt)
    den = a_re * a_re + a_im * a_im
    num_re = ab_re - 1.0
    q_re = (num_re * a_re + ab_im * a_im) / den
    q_im = (ab_im * a_re - num_re * a_im) / den
    bb_re, bb_im = _cmul(q_re[..., None], q_im[..., None], b_re, b_im)
    bu_re = jnp.einsum('btgc,gpc->btgp', uf, bb_re)
    bu_im = jnp.einsum('btgc,gpc->btgp', uf, bb_im)
    i0_re, i0_im = _cmul(ab_re, ab_im, s0_re, s0_im)
    bu_re = bu_re.at[:, 0].add(i0_re)
    bu_im = bu_im.at[:, 0].add(i0_im)
    a_seq_re = jnp.broadcast_to(ab_re, (1, t_len) + ab_re.shape)
    a_seq_im = jnp.broadcast_to(ab_im, (1, t_len) + ab_im.shape)
    _, _, s_re, s_im = lax.associative_scan(_ssm_combine, (a_seq_re, a_seq_im, bu_re, bu_im), axis=1)
    y = (jnp.einsum('btgp,gcp->btgc', s_re, c_re) - jnp.einsum('btgp,gcp->btgc', s_im, c_im) + d_skip * uf)
    z = jax.nn.gelu(y.reshape(bsz, t_len, D_MODEL))
    a, g = jnp.split(z @ w_glu, 2, axis=-1)
    return a * jax.nn.sigmoid(g), s_re[:, -1], s_im[:, -1]


def _rel_bucket(dist):
    max_exact = N_BUCKETS // 2
    d = jnp.maximum(dist, 0)
    df = jnp.maximum(d, 1).astype(_F32)
    large = max_exact + (jnp.log(df / max_exact) / math.log(MAX_DISTANCE / max_exact)
                         * (N_BUCKETS - max_exact)).astype(jnp.int32)
    large = jnp.minimum(large, N_BUCKETS - 1)
    return jnp.where(d < max_exact, d, large)


def _shared_bias(rel_bias, dist):
    b = rel_bias[_rel_bucket(dist)]
    return jnp.moveaxis(b, -1, 0).reshape((N_KV_HEADS, Q_PER_KV) + dist.shape)


def _per_head_bias(rel_bias, dist):
    tab = rel_bias.T.reshape(N_KV_HEADS, Q_PER_KV, N_BUCKETS)
    bk = _rel_bucket(dist)
    f = lambda tab_h, bk_h: jnp.moveaxis(tab_h[:, bk_h], 0, 1)
    return jax.vmap(f, in_axes=(0, 1), out_axes=1)(tab, bk)


def _masked_softmax(s, mask):
    p = jax.nn.softmax(jnp.where(mask, s, NEG_INF), axis=-1)
    return jnp.where(mask, p, 0.0)


def _compress(rows, pos_emb, w1, w2):
    bsz, tk = rows.shape[:2]
    n_cmp = (tk - CMP_LEN) // CMP_STRIDE + 1
    idx = jnp.arange(n_cmp)[:, None] * CMP_STRIDE + jnp.arange(CMP_LEN)[None, :]
    blk = rows[:, idx] + pos_emb[None, None, :, None, :]
    blk = jnp.moveaxis(blk, 3, 2).reshape(bsz, n_cmp, N_KV_HEADS, CMP_LEN * HEAD_DIM)
    return jax.nn.gelu(blk @ w1) @ w2


def _prepare_kv(kv_rows, win_rows, win_base, prm):
    bsz, tk = kv_rows.shape[:2]
    k_cmp = _compress(kv_rows[:, :, 0], prm['cmp_pos_k'], prm['cmp_w1_k'], prm['cmp_w2_k'])
    v_cmp = _compress(kv_rows[:, :, 1], prm['cmp_pos_v'], prm['cmp_w1_v'], prm['cmp_w2_v'])
    n_sel = -(-tk // SEL_BLOCK)
    sel = jnp.pad(kv_rows[:, :, 2:4], ((0, 0), (0, n_sel * SEL_BLOCK - tk), (0, 0), (0, 0), (0, 0)))
    sel = jnp.transpose(sel.reshape(bsz, n_sel, SEL_BLOCK, 2, N_KV_HEADS, HEAD_DIM), (3, 0, 4, 1, 2, 5))
    win = jnp.pad(win_rows, ((0, 0), (WINDOW, 0), (0, 0), (0, 0), (0, 0)))
    return {'kc': k_cmp, 'vc': v_cmp, 'ks': sel[0], 'vs': sel[1],
            'kw': win[:, :, 0], 'vw': win[:, :, 1], 'win_base': win_base}


def _nsa_block(qb, gb, q0, side, rel_bias):
    scale = HEAD_DIM ** -0.5
    bsz, n_q = qb.shape[:2]
    t = q0 + jnp.arange(n_q, dtype=jnp.int32)
    kc, vc, ks, vs = side['kc'], side['vc'], side['ks'], side['vs']
    n_cmp, n_sel = kc.shape[1], ks.shape[2]
    s_c = jnp.einsum('bqhgd,bnhd->bhgqn', qb, kc) * scale
    c_start = jnp.arange(n_cmp, dtype=jnp.int32) * CMP_STRIDE
    d_c = t[:, None] - (c_start + CMP_LEN - 1)[None, :]
    p_c = _masked_softmax(s_c + _shared_bias(rel_bias, d_c), d_c >= 0)
    o_c = jnp.einsum('bhgqn,bnhd->bqhgd', p_c, vc)
    s_start = jnp.arange(n_sel, dtype=jnp.int32) * SEL_BLOCK
    overlap = ((c_start[:, None] < s_start[None, :] + SEL_BLOCK)
               & (c_start[:, None] + CMP_LEN > s_start[None, :])).astype(_F32)
    imp = jnp.einsum('bhgqn,ns->bhqs', p_c, overlap)
    blk = jnp.arange(n_sel, dtype=jnp.int32)[None, :]
    cur = (t // SEL_BLOCK)[:, None]
    valid_b = blk * SEL_BLOCK <= t[:, None]
    forced = (blk == 0) | (blk == cur) | (blk == cur - 1)
    score = jnp.where(forced, FORCE, jnp.where(valid_b, imp, -FORCE))
    k_top = min(SEL_TOPK, n_sel)
    _, idx = lax.top_k(score, k_top)
    gather = jax.vmap(jax.vmap(lambda blocks, ix: blocks[ix]))
    k_s = gather(ks, idx).reshape(bsz, N_KV_HEADS, n_q, k_top * SEL_BLOCK, HEAD_DIM)
    v_s = gather(vs, idx).reshape(bsz, N_KV_HEADS, n_q, k_top * SEL_BLOCK, HEAD_DIM)
    pos_s = (idx[..., None] * SEL_BLOCK + jnp.arange(SEL_BLOCK, dtype=jnp.int32)).reshape(bsz, N_KV_HEADS, n_q, -1)
    d_s = t[None, None, :, None] - pos_s
    s_s = jnp.einsum('bqhgd,bhqsd->bhgqs', qb, k_s) * scale + _per_head_bias(rel_bias, d_s)
    p_s = _masked_softmax(s_s, (d_s >= 0)[:, :, None])
    o_s = jnp.einsum('bhgqs,bhqsd->bqhgd', p_s, v_s)
    win_base = side['win_base']
    span = WINDOW + n_q
    k_w = lax.dynamic_slice_in_dim(side['kw'], q0 - win_base, span, axis=1)
    v_w = lax.dynamic_slice_in_dim(side['vw'], q0 - win_base, span, axis=1)
    pos_w = q0 - WINDOW + jnp.arange(span, dtype=jnp.int32)
    d_w = t[:, None] - pos_w[None, :]
    m_w = (d_w >= 0) & (d_w <= WINDOW) & (pos_w[None, :] >= win_base)
    s_w = jnp.einsum('bqhgd,blhd->bhgql', qb, k_w) * scale + _shared_bias(rel_bias, d_w)
    p_w = _masked_softmax(s_w, m_w)
    o_w = jnp.einsum('bhgql,blhd->bqhgd', p_w, v_w)
    return gb[..., 0:1] * o_c + gb[..., 1:2] * o_s + gb[..., 2:3] * o_w


def _nsa_mixer(h, pos0, side, w_qg, w_o, rel_bias):
    bsz, t_len, _ = h.shape
    qg = h @ w_qg
    q = qg[..., :N_HEADS * HEAD_DIM].reshape(bsz, t_len, N_KV_HEADS, Q_PER_KV, HEAD_DIM)
    gates = jax.nn.sigmoid(qg[..., N_HEADS * HEAD_DIM:]).reshape(bsz, t_len, N_KV_HEADS, Q_PER_KV, N_BRANCH)
    qb = math.gcd(Q_BLOCK, t_len)
    nb = t_len // qb
    q_blocks = jnp.moveaxis(q.reshape(bsz, nb, qb, N_KV_HEADS, Q_PER_KV, HEAD_DIM), 1, 0)
    g_blocks = jnp.moveaxis(gates.reshape(bsz, nb, qb, N_KV_HEADS, Q_PER_KV, N_BRANCH), 1, 0)
    q0s = pos0 + qb * jnp.arange(nb, dtype=jnp.int32)
    out = lax.map(lambda blk: _nsa_block(blk[0], blk[1], blk[2], side, rel_bias), (q_blocks, g_blocks, q0s))
    out = jnp.moveaxis(out, 0, 1).reshape(bsz, t_len, N_HEADS * HEAD_DIM)
    return out @ w_o


def _run_trunk(x, p, s0_re, s0_im, past_kv, win_buf, pos0, prm):
    bsz, t_len, _ = x.shape
    ssm_re, ssm_im = [], []
    side = None
    kv_rows_new = None
    win_state = None
    for i in range(DEPTH):
        h = _rmsnorm(x, prm['g_mix'][i])
        if i < N_A_LAYERS:
            mix, s_re, s_im = _s5_mixer(h, s0_re[i], s0_im[i], prm['ssm_a_re'][i], prm['ssm_a_im'][i],
                                        prm['ssm_log_dt'][i], prm['ssm_b_re'][i], prm['ssm_b_im'][i],
                                        prm['ssm_c_re'][i], prm['ssm_c_im'][i], prm['ssm_d'][i], prm['w_glu'][i])
            ssm_re.append(s_re)
            ssm_im.append(s_im)
        else:
            j = i - N_A_LAYERS
            mix = _nsa_mixer(h, pos0, side, prm['w_qg'][j], prm['w_o'][j], prm['rel_bias'])
        x = x + mix
        x = _moe_layer(x.reshape(-1, D_MODEL), prm['g_ffn'][i], prm['w_route_group'][i], prm['b_route_group'][i],
                       prm['w_route_expert'][i], prm['b_route_expert'][i], prm['w_exp_up'][i],
                       prm['w_exp_down'][i]).reshape(bsz, t_len, D_MODEL)
        gate = jax.nn.sigmoid(_rmsnorm(x, prm['g_ple'][i]) @ prm['w_ple_gate'][i])
        x = x + (p[i] @ prm['w_ple_proj'][i]) * gate
        if i == N_A_LAYERS - 1:
            kv = (_rmsnorm(x, prm['g_kv']) @ prm['w_kv']).reshape(
                bsz, t_len, KV_SLOTS_PAGED + KV_SLOTS_WIN, N_KV_HEADS, HEAD_DIM)
            kv_rows_new, win_new = kv[:, :, :KV_SLOTS_PAGED], kv[:, :, KV_SLOTS_PAGED:]
            if past_kv is None:
                kv_rows, win_rows, win_base = kv_rows_new, win_new, 0
                win_state = win_new[:, -min(WINDOW, t_len):]
            else:
                kv_rows = jnp.concatenate([past_kv, kv_rows_new], axis=1)
                w_buf = win_buf.shape[1]
                win_rows = jnp.concatenate([win_buf, win_new], axis=1)
                win_base = pos0 - w_buf
                win_state = win_rows[:, -w_buf:]
            side = _prepare_kv(kv_rows, win_rows, win_base, prm)
    y = _rmsnorm(x, prm['g_final'])
    return y, kv_rows_new, win_state, jnp.stack(ssm_re), jnp.stack(ssm_im)


def kernel(x_prompt, x_sample, p_prompt, p_sample, cache_kv, cache_win, state_ssm_re, state_ssm_im, page_table,
           g_mix, g_ffn, g_ple, g_kv, g_final,
           ssm_a_re, ssm_a_im, ssm_log_dt, ssm_b_re, ssm_b_im, ssm_c_re, ssm_c_im, ssm_d, w_glu,
           w_kv, cmp_pos_k, cmp_pos_v, cmp_w1_k, cmp_w2_k, cmp_w1_v, cmp_w2_v, w_qg, w_o, rel_bias,
           w_route_group, b_route_group, w_route_expert, b_route_expert, w_exp_up, w_exp_down,
           w_ple_proj, w_ple_gate):
    prm = {
        'g_mix': g_mix, 'g_ffn': g_ffn, 'g_ple': g_ple, 'g_kv': g_kv, 'g_final': g_final,
        'ssm_a_re': ssm_a_re, 'ssm_a_im': ssm_a_im, 'ssm_log_dt': ssm_log_dt,
        'ssm_b_re': ssm_b_re, 'ssm_b_im': ssm_b_im, 'ssm_c_re': ssm_c_re, 'ssm_c_im': ssm_c_im,
        'ssm_d': ssm_d, 'w_glu': w_glu,
        'w_kv': w_kv, 'cmp_pos_k': cmp_pos_k, 'cmp_pos_v': cmp_pos_v,
        'cmp_w1_k': cmp_w1_k, 'cmp_w2_k': cmp_w2_k, 'cmp_w1_v': cmp_w1_v, 'cmp_w2_v': cmp_w2_v,
        'w_qg': w_qg, 'w_o': w_o, 'rel_bias': rel_bias,
        'w_route_group': w_route_group, 'b_route_group': b_route_group,
        'w_route_expert': w_route_expert, 'b_route_expert': b_route_expert,
        'w_exp_up': w_exp_up, 'w_exp_down': w_exp_down,
        'w_ple_proj': w_ple_proj, 'w_ple_gate': w_ple_gate,
    }
    zero_state = jnp.zeros((N_A_LAYERS, x_prompt.shape[0], SSM_GROUPS, SSM_STATE), _F32)
    y_prompt, kv_prompt, win_prompt, ssm_re_prompt, ssm_im_prompt = _run_trunk(
        x_prompt, p_prompt, zero_state, zero_state, None, None, 0, prm)
    n_seq, n_pages = page_table.shape
    past_len = n_pages * cache_kv.shape[1]
    past_kv = cache_kv[page_table].reshape((n_seq, past_len) + cache_kv.shape[2:])
    y_sample, kv_sample, win_sample, ssm_re_sample, ssm_im_sample = _run_trunk(
        x_sample, p_sample, state_ssm_re, state_ssm_im, past_kv, cache_win, past_len, prm)
    return (y_prompt, y_sample, kv_prompt, win_prompt, ssm_re_prompt, ssm_im_prompt,
            kv_sample, win_sample, ssm_re_sample, ssm_im_sample)
```

```python
import functools
import math

import jax
import jax.numpy as jnp
from jax import lax
from jax.experimental import pallas as pl
from jax.experimental.pallas import tpu as pltpu

D_MODEL = 1024
DEPTH = 4
N_A_LAYERS = DEPTH // 2
SSM_GROUP = 16
SSM_GROUPS = D_MODEL // SSM_GROUP
SSM_STATE = 64
N_HEADS = 16
HEAD_DIM = D_MODEL // N_HEADS
N_KV_HEADS = 4
Q_PER_KV = N_HEADS // N_KV_HEADS
CMP_LEN = 32
CMP_STRIDE = 16
SEL_BLOCK = 64
SEL_TOPK = 16
WINDOW = 512
N_BRANCH = 3
KV_SLOTS_PAGED = 4
KV_SLOTS_WIN = 2
Q_BLOCK = 64
N_BUCKETS = 32
MAX_DISTANCE = 128
N_EXPERT_GROUPS = 4
EXPERTS_PER_GROUP = 4
N_EXPERTS = N_EXPERT_GROUPS * EXPERTS_PER_GROUP
EXPERT_TOPK = 2
D_EXPERT = 256
RMS_EPS = 1e-6
NEG_INF = -1e30
FORCE = 1e4

LANES = 128
VMEM_LIMIT = 48 * 1024 * 1024

_F32 = jnp.float32
_BF16 = jnp.bfloat16


def _moe_kernel(x_ref, g_ref, wr_ref, br_ref, wup_ref, wdn_ref, o_ref, h_scr, comb_scr, acc_scr):
    e = pl.program_id(1)
    lane = lax.broadcasted_iota(jnp.int32, comb_scr.shape, 1)

    @pl.when(e == 0)
    def _route():
        x = x_ref[...]
        h = x * lax.rsqrt(jnp.mean(x * x, axis=-1, keepdims=True) + RMS_EPS) * g_ref[...]
        h_scr[...] = h.astype(_BF16)
        logits = jnp.dot(h, wr_ref[...], preferred_element_type=_F32,
                         precision=lax.Precision.HIGHEST) + br_ref[...]
        is_grp = (lane >= N_EXPERTS) & (lane < N_EXPERTS + N_EXPERT_GROUPS)
        lg = jnp.where(is_grp, logits, -jnp.inf)
        gmax = jnp.max(lg, axis=-1, keepdims=True)
        gi = jnp.min(jnp.where(lg == gmax, lane, LANES), axis=-1, keepdims=True) - N_EXPERTS
        gp = 1.0 / jnp.sum(jnp.where(is_grp, jnp.exp(lg - gmax), 0.0), axis=-1, keepdims=True)
        in_grp = (lane < N_EXPERTS) & ((lane // EXPERTS_PER_GROUP) == gi)
        le = jnp.where(in_grp, logits, -jnp.inf)
        m1 = jnp.max(le, axis=-1, keepdims=True)
        i1 = jnp.min(jnp.where(le == m1, lane, LANES), axis=-1, keepdims=True)
        le2 = jnp.where(lane == i1, -jnp.inf, le)
        m2 = jnp.max(le2, axis=-1, keepdims=True)
        i2 = jnp.min(jnp.where(le2 == m2, lane, LANES), axis=-1, keepdims=True)
        e2 = jnp.exp(m2 - m1)
        den = 1.0 + e2
        comb_scr[...] = jnp.where(lane == i1, gp / den, jnp.where(lane == i2, gp * e2 / den, 0.0))
        acc_scr[...] = x

    up = jnp.dot(h_scr[...], wup_ref[0], preferred_element_type=_F32)
    a = up[:, :D_EXPERT]
    b = up[:, D_EXPERT:]
    c = jnp.sum(jnp.where(lane == e, comb_scr[...], 0.0), axis=-1, keepdims=True)
    act = (a * jax.nn.sigmoid(a)) * b * c
    acc_scr[...] += jnp.dot(act.astype(_BF16), wdn_ref[0], preferred_element_type=_F32)

    @pl.when(e == N_EXPERTS - 1)
    def _store():
        o_ref[...] = acc_scr[...]


def _moe_layer(x2, g, w_rg, b_rg, w_re, b_re, w_up, w_dn):
    n_tok, d = x2.shape
    tm = min(n_tok, 1024)
    wr = jnp.zeros((d, LANES), _F32).at[:, :N_EXPERTS].set(w_re).at[:, N_EXPERTS:N_EXPERTS + N_EXPERT_GROUPS].set(w_rg)
    br = jnp.zeros((1, LANES), _F32).at[0, :N_EXPERTS].set(b_re).at[0, N_EXPERTS:N_EXPERTS + N_EXPERT_GROUPS].set(b_rg)
    return pl.pallas_call(
        _moe_kernel,
        grid=(n_tok // tm, N_EXPERTS),
        in_specs=[
            pl.BlockSpec((tm, d), lambda i, e: (i, 0)),
            pl.BlockSpec((1, d), lambda i, e: (0, 0)),
            pl.BlockSpec((d, LANES), lambda i, e: (0, 0)),
            pl.BlockSpec((1, LANES), lambda i, e: (0, 0)),
            pl.BlockSpec((1, d, 2 * D_EXPERT), lambda i, e: (e, 0, 0)),
            pl.BlockSpec((1, D_EXPERT, d), lambda i, e: (e, 0, 0)),
        ],
        out_specs=pl.BlockSpec((tm, d), lambda i, e: (i, 0)),
        out_shape=jax.ShapeDtypeStruct((n_tok, d), _F32),
        scratch_shapes=[
            pltpu.VMEM((tm, d), _BF16),
            pltpu.VMEM((tm, LANES), _F32),
            pltpu.VMEM((tm, d), _F32),
        ],
        compiler_params=pltpu.CompilerParams(
            dimension_semantics=("parallel", "arbitrary"), vmem_limit_bytes=VMEM_LIMIT),
        name="hmoe",
    )(x2, g.reshape(1, d), wr, br, w_up.astype(_BF16), w_dn.astype(_BF16))


def _rmsnorm(x, g):
    xf = x.astype(_F32)
    y = xf * lax.rsqrt(jnp.mean(xf * xf, axis=-1, keepdims=True) + RMS_EPS)
    return (y * g.astype(_F32)).astype(x.dtype)


def _cmul(ar, ai, br, bi):
    return ar * br - ai * bi, ar * bi + ai * br


def _s5_operators(a_re, a_im, log_dt, b_re, b_im, c_re, c_im, chunk):
    hp = lax.Precision.HIGHEST
    dt = jnp.exp(log_dt)[:, None]
    decay = jnp.exp(a_re * dt)
    ab_re, ab_im = decay * jnp.cos(a_im * dt), decay * jnp.sin(a_im * dt)
    den = a_re * a_re + a_im * a_im
    num_re = ab_re - 1.0
    q_re = (num_re * a_re + ab_im * a_im) / den
    q_im = (ab_im * a_re - num_re * a_im) / den
    bb_re, bb_im = _cmul(q_re[..., None], q_im[..., None], b_re, b_im)
    pw_re, pw_im = [jnp.ones_like(ab_re)], [jnp.zeros_like(ab_im)]
    for _ in range(chunk):
        nr, ni = _cmul(pw_re[-1], pw_im[-1], ab_re, ab_im)
        pw_re.append(nr)
        pw_im.append(ni)
    pw_re, pw_im = jnp.stack(pw_re), jnp.stack(pw_im)
    w_re, w_im = _cmul(pw_re[:chunk, :, :, None], pw_im[:chunk, :, :, None], bb_re[None], bb_im[None])
    k_lag = (jnp.einsum('gcp,jgpd->jgcd', c_re, w_re, precision=hp)
             - jnp.einsum('gcp,jgpd->jgcd', c_im, w_im, precision=hp))
    tt = jnp.arange(chunk)
    lag = tt[None, :] - tt[:, None]
    m = jnp.where((lag >= 0)[:, :, None, None, None], k_lag[jnp.clip(lag, 0, chunk - 1)], 0.0)
    m = jnp.transpose(m, (2, 0, 4, 1, 3)).reshape(SSM_GROUPS, chunk * SSM_GROUP, chunk * SSM_GROUP)
    h_re = jnp.transpose(w_re[::-1], (1, 0, 3, 2)).reshape(SSM_GROUPS, chunk * SSM_GROUP, SSM_STATE)
    h_im = jnp.transpose(w_im[::-1], (1, 0, 3, 2)).reshape(SSM_GROUPS, chunk * SSM_GROUP, SSM_STATE)
    ca_re, ca_im = _cmul(c_re[None], c_im[None], pw_re[1:, :, None, :], pw_im[1:, :, None, :])
    gm_re = jnp.transpose(ca_re, (1, 3, 0, 2)).reshape(SSM_GROUPS, SSM_STATE, chunk * SSM_GROUP)
    gm_im = -jnp.transpose(ca_im, (1, 3, 0, 2)).reshape(SSM_GROUPS, SSM_STATE, chunk * SSM_GROUP)
    return m, h_re, h_im, gm_re, gm_im, pw_re[chunk][:, None, :], pw_im[chunk][:, None, :]


def _s5_kernel(u_ref, m_ref, hre_ref, him_ref, gre_ref, gim_ref, alre_ref, alim_ref, d_ref, s0re_ref, s0im_ref,
               y_ref, sre_ref, sim_ref, ere, eim, sinre, sinim, *, n_chunks, rows, groups):
    hp = lax.Precision.HIGHEST
    for g in range(groups):
        u = u_ref[g]
        ere[g] = jnp.dot(u, hre_ref[g], preferred_element_type=_F32, precision=hp)
        eim[g] = jnp.dot(u, him_ref[g], preferred_element_type=_F32, precision=hp)

    def step(j, carry):
        row = pl.ds(pl.multiple_of(j * rows, rows), rows)
        out = []
        for g in range(groups):
            sr, si = carry[2 * g], carry[2 * g + 1]
            sinre[g, row, :] = sr
            sinim[g, row, :] = si
            ar, ai = alre_ref[g], alim_ref[g]
            out.append(ar * sr - ai * si + ere[g, row, :])
            out.append(ar * si + ai * sr + eim[g, row, :])
        return tuple(out)

    init = []
    for g in range(groups):
        init += [s0re_ref[g], s0im_ref[g]]
    fin = lax.fori_loop(0, n_chunks, step, tuple(init))
    for g in range(groups):
        sre_ref[g] = fin[2 * g]
        sim_ref[g] = fin[2 * g + 1]
        u = u_ref[g]
        y = jnp.dot(u.astype(_BF16), m_ref[g], preferred_element_type=_F32)
        y += jnp.dot(sinre[g].astype(_BF16), gre_ref[g], preferred_element_type=_F32)
        y += jnp.dot(sinim[g].astype(_BF16), gim_ref[g], preferred_element_type=_F32)
        y_ref[g] = y + d_ref[g] * u


def _s5_scan(u, s0_re, s0_im, a_re, a_im, log_dt, b_re, b_im, c_re, c_im, d_skip):
    bsz, t_len, _ = u.shape
    chunk = min(t_len, 16)
    n_chunks = t_len // chunk
    rows = -(-bsz // 8) * 8
    width = chunk * SSM_GROUP
    groups = 2
    m, h_re, h_im, gm_re, gm_im, al_re, al_im = _s5_operators(a_re, a_im, log_dt, b_re, b_im, c_re, c_im, chunk)
    uf = u.reshape(bsz, n_chunks, chunk, SSM_GROUPS, SSM_GROUP)
    uf = jnp.transpose(uf, (3, 1, 0, 2, 4))
    uf = jnp.pad(uf, ((0, 0), (0, 0), (0, rows - bsz), (0, 0), (0, 0))).reshape(SSM_GROUPS, n_chunks * rows, width)
    pad_state = lambda s: jnp.pad(jnp.transpose(s, (1, 0, 2)), ((0, 0), (0, rows - bsz), (0, 0)))
    d_tile = jnp.tile(d_skip, (1, chunk))[:, None, :]
    gspec = lambda *shape: pl.BlockSpec((groups,) + shape, lambda i: (i,) + (0,) * len(shape))
    y, s_re, s_im = pl.pallas_call(
        functools.partial(_s5_kernel, n_chunks=n_chunks, rows=rows, groups=groups),
        grid=(SSM_GROUPS // groups,),
        in_specs=[gspec(n_chunks * rows, width), gspec(width, width), gspec(width, SSM_STATE), gspec(width, SSM_STATE),
                  gspec(SSM_STATE, width), gspec(SSM_STATE, width), gspec(1, SSM_STATE), gspec(1, SSM_STATE),
                  gspec(1, width), gspec(rows, SSM_STATE), gspec(rows, SSM_STATE)],
        out_specs=[gspec(n_chunks * rows, width), gspec(rows, SSM_STATE), gspec(rows, SSM_STATE)],
        out_shape=[jax.ShapeDtypeStruct((SSM_GROUPS, n_chunks * rows, width), _F32),
                   jax.ShapeDtypeStruct((SSM_GROUPS, rows, SSM_STATE), _F32),
                   jax.ShapeDtypeStruct((SSM_GROUPS, rows, SSM_STATE), _F32)],
        scratch_shapes=[pltpu.VMEM((groups, n_chunks * rows, SSM_STATE), _F32) for _ in range(4)],
        compiler_params=pltpu.CompilerParams(dimension_semantics=("parallel",), vmem_limit_bytes=VMEM_LIMIT),
        name="s5_scan",
    )(uf, m.astype(_BF16), h_re, h_im, gm_re.astype(_BF16), gm_im.astype(_BF16), al_re, al_im, d_tile,
      pad_state(s0_re), pad_state(s0_im))
    y = y.reshape(SSM_GROUPS, n_chunks, rows, chunk, SSM_GROUP)[:, :, :bsz]
    y = jnp.transpose(y, (2, 1, 3, 0, 4)).reshape(bsz, t_len, D_MODEL)
    unpad = lambda s: jnp.transpose(s[:, :bsz], (1, 0, 2))
    return y, unpad(s_re), unpad(s_im)


def _s5_mixer(u, s0_re, s0_im, a_re, a_im, log_dt, b_re, b_im, c_re, c_im, d_skip, w_glu):
    y, s_re, s_im = _s5_scan(u, s0_re, s0_im, a_re, a_im, log_dt, b_re, b_im, c_re, c_im, d_skip)
    z = jax.nn.gelu(y)
    a, g = jnp.split(z @ w_glu, 2, axis=-1)
    return a * jax.nn.sigmoid(g), s_re, s_im


def _rel_bucket(dist):
    max_exact = N_BUCKETS // 2
    d = jnp.maximum(dist, 0)
    df = jnp.maximum(d, 1).astype(_F32)
    large = max_exact + (jnp.log(df / max_exact) / math.log(MAX_DISTANCE / max_exact)
                         * (N_BUCKETS - max_exact)).astype(jnp.int32)
    large = jnp.minimum(large, N_BUCKETS - 1)
    return jnp.where(d < max_exact, d, large)


def _shared_bias(rel_bias, dist):
    b = rel_bias[_rel_bucket(dist)]
    return jnp.moveaxis(b, -1, 0).reshape((N_KV_HEADS, Q_PER_KV) + dist.shape)


def _per_head_bias(rel_bias, dist):
    tab = rel_bias.T.reshape(N_KV_HEADS, Q_PER_KV, N_BUCKETS)
    bk = _rel_bucket(dist)
    f = lambda tab_h, bk_h: jnp.moveaxis(tab_h[:, bk_h], 0, 1)
    return jax.vmap(f, in_axes=(0, 1), out_axes=1)(tab, bk)


def _masked_softmax(s, mask):
    p = jax.nn.softmax(jnp.where(mask, s, NEG_INF), axis=-1)
    return jnp.where(mask, p, 0.0)


def _compress(rows, pos_emb, w1, w2):
    bsz, tk = rows.shape[:2]
    n_cmp = (tk - CMP_LEN) // CMP_STRIDE + 1
    idx = jnp.arange(n_cmp)[:, None] * CMP_STRIDE + jnp.arange(CMP_LEN)[None, :]
    blk = rows[:, idx] + pos_emb[None, None, :, None, :]
    blk = jnp.moveaxis(blk, 3, 2).reshape(bsz, n_cmp, N_KV_HEADS, CMP_LEN * HEAD_DIM)
    return jax.nn.gelu(blk @ w1) @ w2


def _prepare_kv(kv_rows, win_rows, win_base, prm):
    bsz, tk = kv_rows.shape[:2]
    k_cmp = _compress(kv_rows[:, :, 0], prm['cmp_pos_k'], prm['cmp_w1_k'], prm['cmp_w2_k'])
    v_cmp = _compress(kv_rows[:, :, 1], prm['cmp_pos_v'], prm['cmp_w1_v'], prm['cmp_w2_v'])
    n_sel = -(-tk // SEL_BLOCK)
    sel = jnp.pad(kv_rows[:, :, 2:4], ((0, 0), (0, n_sel * SEL_BLOCK - tk), (0, 0), (0, 0), (0, 0)))
    sel = jnp.transpose(sel.reshape(bsz, n_sel, SEL_BLOCK, 2, N_KV_HEADS, HEAD_DIM), (3, 0, 4, 1, 2, 5))
    win = jnp.pad(win_rows, ((0, 0), (WINDOW, 0), (0, 0), (0, 0), (0, 0)))
    return {'kc': k_cmp, 'vc': v_cmp, 'ks': sel[0], 'vs': sel[1],
            'kw': win[:, :, 0], 'vw': win[:, :, 1], 'win_base': win_base}


def _nsa_block(qb, gb, q0, side, rel_bias):
    scale = HEAD_DIM ** -0.5
    bsz, n_q = qb.shape[:2]
    t = q0 + jnp.arange(n_q, dtype=jnp.int32)
    kc, vc, ks, vs = side['kc'], side['vc'], side['ks'], side['vs']
    n_cmp, n_sel = kc.shape[1], ks.shape[2]
    s_c = jnp.einsum('bqhgd,bnhd->bhgqn', qb, kc) * scale
    c_start = jnp.arange(n_cmp, dtype=jnp.int32) * CMP_STRIDE
    d_c = t[:, None] - (c_start + CMP_LEN - 1)[None, :]
    p_c = _masked_softmax(s_c + _shared_bias(rel_bias, d_c), d_c >= 0)
    o_c = jnp.einsum('bhgqn,bnhd->bqhgd', p_c, vc)
    s_start = jnp.arange(n_sel, dtype=jnp.int32) * SEL_BLOCK
    overlap = ((c_start[:, None] < s_start[None, :] + SEL_BLOCK)
               & (c_start[:, None] + CMP_LEN > s_start[None, :])).astype(_F32)
    imp = jnp.einsum('bhgqn,ns->bhqs', p_c, overlap)
    blk = jnp.arange(n_sel, dtype=jnp.int32)[None, :]
    cur = (t // SEL_BLOCK)[:, None]
    valid_b = blk * SEL_BLOCK <= t[:, None]
    forced = (blk == 0) | (blk == cur) | (blk == cur - 1)
    score = jnp.where(forced, FORCE, jnp.where(valid_b, imp, -FORCE))
    k_top = min(SEL_TOPK, n_sel)
    _, idx = lax.top_k(score, k_top)
    gather = jax.vmap(jax.vmap(lambda blocks, ix: blocks[ix]))
    k_s = gather(ks, idx).reshape(bsz, N_KV_HEADS, n_q, k_top * SEL_BLOCK, HEAD_DIM)
    v_s = gather(vs, idx).reshape(bsz, N_KV_HEADS, n_q, k_top * SEL_BLOCK, HEAD_DIM)
    pos_s = (idx[..., None] * SEL_BLOCK + jnp.arange(SEL_BLOCK, dtype=jnp.int32)).reshape(bsz, N_KV_HEADS, n_q, -1)
    d_s = t[None, None, :, None] - pos_s
    s_s = jnp.einsum('bqhgd,bhqsd->bhgqs', qb, k_s) * scale + _per_head_bias(rel_bias, d_s)
    p_s = _masked_softmax(s_s, (d_s >= 0)[:, :, None])
    o_s = jnp.einsum('bhgqs,bhqsd->bqhgd', p_s, v_s)
    win_base = side['win_base']
    span = WINDOW + n_q
    k_w = lax.dynamic_slice_in_dim(side['kw'], q0 - win_base, span, axis=1)
    v_w = lax.dynamic_slice_in_dim(side['vw'], q0 - win_base, span, axis=1)
    pos_w = q0 - WINDOW + jnp.arange(span, dtype=jnp.int32)
    d_w = t[:, None] - pos_w[None, :]
    m_w = (d_w >= 0) & (d_w <= WINDOW) & (pos_w[None, :] >= win_base)
    s_w = jnp.einsum('bqhgd,blhd->bhgql', qb, k_w) * scale + _shared_bias(rel_bias, d_w)
    p_w = _masked_softmax(s_w, m_w)
    o_w = jnp.einsum('bhgql,blhd->bqhgd', p_w, v_w)
    return gb[..., 0:1] * o_c + gb[..., 1:2] * o_s + gb[..., 2:3] * o_w


def _nsa_mixer(h, pos0, side, w_qg, w_o, rel_bias):
    bsz, t_len, _ = h.shape
    qg = h @ w_qg
    q = qg[..., :N_HEADS * HEAD_DIM].reshape(bsz, t_len, N_KV_HEADS, Q_PER_KV, HEAD_DIM)
    gates = jax.nn.sigmoid(qg[..., N_HEADS * HEAD_DIM:]).reshape(bsz, t_len, N_KV_HEADS, Q_PER_KV, N_BRANCH)
    qb = math.gcd(Q_BLOCK, t_len)
    nb = t_len // qb
    q_blocks = jnp.moveaxis(q.reshape(bsz, nb, qb, N_KV_HEADS, Q_PER_KV, HEAD_DIM), 1, 0)
    g_blocks = jnp.moveaxis(gates.reshape(bsz, nb, qb, N_KV_HEADS, Q_PER_KV, N_BRANCH), 1, 0)
    q0s = pos0 + qb * jnp.arange(nb, dtype=jnp.int32)
    out = lax.map(lambda blk: _nsa_block(blk[0], blk[1], blk[2], side, rel_bias), (q_blocks, g_blocks, q0s))
    out = jnp.moveaxis(out, 0, 1).reshape(bsz, t_len, N_HEADS * HEAD_DIM)
    return out @ w_o


TQ = 128
TK = 128
NEAR_BUCKET_DIST = 113


def _bucket_table(dist):
    import numpy as np
    max_exact = N_BUCKETS // 2
    d = np.maximum(np.asarray(dist, np.int64), 0)
    df = np.maximum(d, 1).astype(np.float64)
    large = max_exact + (np.log(df / max_exact) / math.log(MAX_DISTANCE / max_exact)
                         * (N_BUCKETS - max_exact)).astype(np.int64)
    return np.where(d < max_exact, d, np.minimum(large, N_BUCKETS - 1)).astype(np.int32)


def _bias_tiles(rel_bias):
    import numpy as np
    rel_bias = rel_bias - rel_bias[N_BUCKETS - 1]
    r = np.arange(TQ)[:, None]
    c = np.arange(2 * LANES)[None, :]
    d_cmp = np.where(c < 16, r - (CMP_LEN - 1) + CMP_STRIDE * (8 - c), 10 ** 6)
    g_cmp = rel_bias[_bucket_table(d_cmp)]
    g_cmp = jnp.transpose(g_cmp, (2, 0, 1)).reshape(N_KV_HEADS, Q_PER_KV * TQ, 2 * LANES)
    k = np.arange(2 * TK)[None, :]
    d_near = TK + r - k
    near = rel_bias[_bucket_table(d_near)] + jnp.asarray(np.where(d_near >= 0, 0.0, NEG_INF), _F32)[:, :, None]
    near = jnp.transpose(near, (2, 0, 1)).reshape(N_KV_HEADS, Q_PER_KV * TQ, 2 * TK)
    return g_cmp, near


def _stack_heads(q):
    return jnp.concatenate([q[:, g * HEAD_DIM:(g + 1) * HEAD_DIM] for g in range(Q_PER_KV)], axis=0)


def _unstack_heads(o):
    return jnp.concatenate([o[g] for g in range(Q_PER_KV)], axis=1)


def _nsa_cmp_kernel(q_ref, kct_ref, vc_ref, g_ref, ovt_ref, oc_ref, selt_ref):
    i = pl.program_id(2)
    t0 = i * TQ
    n_cmp_pad = kct_ref.shape[-1]
    qs = _stack_heads(q_ref[...] * (HEAD_DIM ** -0.5)).astype(_BF16)
    s = jnp.dot(qs, kct_ref[...], preferred_element_type=_F32)
    shift = (i * (TQ // CMP_STRIDE) + n_cmp_pad - 8) % n_cmp_pad
    bias = pltpu.roll(g_ref[...], shift, 1)
    s3 = (s + bias).reshape(Q_PER_KV, TQ, n_cmp_pad)
    r = lax.broadcasted_iota(jnp.int32, (TQ, n_cmp_pad), 0)
    n = lax.broadcasted_iota(jnp.int32, (TQ, n_cmp_pad), 1)
    valid = (t0 + r - CMP_STRIDE * n - (CMP_LEN - 1)) >= 0
    sm = jnp.where(valid, s3, NEG_INF)
    m = jnp.max(sm, axis=-1, keepdims=True)
    p = jnp.where(valid, jnp.exp(sm - m), 0.0)
    l = jnp.sum(p, axis=-1, keepdims=True)
    pb = (p * jnp.where(l > 0.0, 1.0 / l, 0.0)).astype(_BF16)
    oc = jnp.dot(pb.reshape(Q_PER_KV * TQ, n_cmp_pad), vc_ref[...], preferred_element_type=_F32)
    oc_ref[...] = _unstack_heads(oc.reshape(Q_PER_KV, TQ, HEAD_DIM))
    imp = jnp.zeros((ovt_ref.shape[0], TQ), _F32)
    for g in range(Q_PER_KV):
        imp += lax.dot_general(ovt_ref[...], pb[g], (((1,), (1,)), ((), ())), preferred_element_type=_F32)
    n_sel = imp.shape[0]
    blk = lax.broadcasted_iota(jnp.int32, (n_sel, TQ), 0)
    t = t0 + lax.broadcasted_iota(jnp.int32, (n_sel, TQ), 1)
    cur = t // SEL_BLOCK
    forced = (blk == 0) | (blk == cur) | (blk == cur - 1)
    score = jnp.where(forced, FORCE, jnp.where(blk * SEL_BLOCK <= t, imp, -FORCE))
    rank = jnp.zeros((n_sel, TQ), jnp.int32)
    for sp in range(n_sel):
        row = score[sp:sp + 1, :]
        beats = (row > score) | ((row == score) & (blk > sp))
        rank += beats.astype(jnp.int32)
    selt_ref[...] = (rank < SEL_TOPK).astype(_F32)


def _flash_step(qs, kt, v, add, m_scr, acc_scr):
    n_k = kt.shape[-1]
    s3 = jnp.dot(qs, kt, preferred_element_type=_F32).reshape(Q_PER_KV, TQ, n_k)
    if add is not None:
        s3 = s3 + add
    chunks = [s3[..., k * LANES:(k + 1) * LANES] for k in range(n_k // LANES)]
    m_old = m_scr[...]
    m_new = jnp.maximum(m_old, jnp.max(functools.reduce(jnp.maximum, chunks), axis=-1, keepdims=True))
    alpha = jnp.exp(m_old - m_new)
    m_scr[...] = m_new
    pv = None
    for k, sk in enumerate(chunks):
        p = jnp.exp(sk - m_new).astype(_BF16).reshape(Q_PER_KV * TQ, LANES)
        d = jnp.dot(p, v[k * LANES:(k + 1) * LANES], preferred_element_type=_F32)
        pv = d if pv is None else pv + d
    acc_scr[...] = alpha * acc_scr[...] + pv.reshape(Q_PER_KV, TQ, 2 * HEAD_DIM)


FAR_TILES = 4


def _nsa_sel_win_kernel(q_ref, oc_ref, glog_ref, selt_ref, kst_ref, vs_ref, kwt_ref, vw_ref, near_ref,
                        exp_ref, gexp_ref, o_ref, seladd, m_s, acc_s, m_w, acc_w):
    i = pl.program_id(2)
    qs = _stack_heads(q_ref[...] * (HEAD_DIM ** -0.5)).astype(_BF16)
    sel = jnp.dot(selt_ref[...].T.astype(_BF16), exp_ref[...], preferred_element_type=_F32)
    seladd[...] = (sel - 1.0) * (-NEG_INF)
    for m_scr, acc_scr in ((m_s, acc_s), (m_w, acc_w)):
        m_scr[...] = jnp.full(m_scr.shape, NEG_INF, _F32)
        acc_scr[...] = jnp.zeros(acc_scr.shape, _F32)

    def keys(j, n_tiles):
        return pl.ds(pl.multiple_of(j * TK, TK), n_tiles * TK)

    def sel_step(j, n_tiles, bias):
        cols = keys(j, n_tiles)
        add = seladd[:, cols]
        if bias is not None:
            add = add + bias
        _flash_step(qs, kst_ref[:, cols], vs_ref[cols, :], add, m_s, acc_s)

    def win_step(j, n_tiles, add):
        cols = keys(j, n_tiles)
        _flash_step(qs, kwt_ref[:, cols], vw_ref[cols, :], add, m_w, acc_w)

    near = lambda lo: near_ref[:, lo * TK:].reshape(Q_PER_KV, TQ, (2 - lo) * TK)

    n_far = jnp.maximum(i - 1, 0)

    def far_many(j, carry):
        sel_step(j * FAR_TILES, FAR_TILES, None)
        return carry

    def far_one(j, carry):
        sel_step(j, 1, None)
        return carry

    lax.fori_loop(0, n_far // FAR_TILES, far_many, 0)
    lax.fori_loop((n_far // FAR_TILES) * FAR_TILES, n_far, far_one, 0)

    @pl.when(i >= 1)
    def _():
        sel_step(i - 1, 2, near(0))
        win_step(i - 1, 2, near(0))

    @pl.when(i == 0)
    def _():
        sel_step(0, 1, near(1))
        win_step(0, 1, near(1))

    n_lag = WINDOW // TK

    @pl.when(i >= n_lag - 1)
    def _():
        win_step(i - (n_lag - 1), n_lag - 2, None)

    for lag in range(2, n_lag - 1):
        @pl.when(i == lag)
        def _():
            win_step(0, lag - 1, None)

    @pl.when(i >= n_lag)
    def _():
        r = lax.broadcasted_iota(jnp.int32, (TQ, TK), 0)
        c = lax.broadcasted_iota(jnp.int32, (TQ, TK), 1)
        win_step(i - n_lag, 1, jnp.where(c >= r, 0.0, NEG_INF))

    def finish(acc_scr):
        acc = acc_scr[...]
        return _unstack_heads(acc[..., :HEAD_DIM] / acc[..., HEAD_DIM:])

    gates = jnp.dot(jax.nn.sigmoid(glog_ref[...]), gexp_ref[...], preferred_element_type=_F32,
                    precision=lax.Precision.HIGHEST)
    width = Q_PER_KV * HEAD_DIM
    o_ref[...] = (gates[:, :width] * oc_ref[...] + gates[:, width:2 * width] * finish(acc_s)
                  + gates[:, 2 * width:] * finish(acc_w))


def _nsa_prompt_side(kv, k_cmp, v_cmp, rel_bias):
    import numpy as np
    bsz, t_len = kv.shape[:2]
    n_cmp = k_cmp.shape[1]
    n_cmp_pad = -(-n_cmp // LANES) * LANES
    n_sel = t_len // SEL_BLOCK
    kvb = kv.astype(_BF16)
    t_last = lambda a: jnp.transpose(a, (0, 2, 3, 1))
    t_rows = lambda a: jnp.transpose(a, (0, 2, 1, 3))
    pad_c = lambda a: jnp.pad(a.astype(_BF16), ((0, 0), (0, n_cmp_pad - n_cmp), (0, 0), (0, 0)))
    c_start = np.arange(n_cmp_pad)[None, :] * CMP_STRIDE
    s_start = np.arange(n_sel)[:, None] * SEL_BLOCK
    ovt = ((c_start < s_start + SEL_BLOCK) & (c_start + CMP_LEN > s_start) & (np.arange(n_cmp_pad)[None, :] < n_cmp))
    expand = np.arange(n_sel)[:, None] == (np.arange(t_len)[None, :] // SEL_BLOCK)
    lanes = np.arange(LANES)[:, None]
    cols = np.arange(N_BRANCH * Q_PER_KV * HEAD_DIM)[None, :]
    gexp = lanes == (cols // HEAD_DIM % Q_PER_KV) * N_BRANCH + cols // (Q_PER_KV * HEAD_DIM)
    g_cmp, near = _bias_tiles(rel_bias)
    with_ones = lambda a: jnp.concatenate([t_rows(a), jnp.ones((bsz, N_KV_HEADS, t_len, HEAD_DIM), _BF16)], axis=-1)
    return {
        'kct': t_last(pad_c(k_cmp)), 'vc': t_rows(pad_c(v_cmp)),
        'kst': t_last(kvb[:, :, 2]), 'vs': with_ones(kvb[:, :, 3]),
        'kwt': t_last(kvb[:, :, 4]), 'vw': with_ones(kvb[:, :, 5]),
        'ovt': jnp.asarray(ovt, _BF16), 'expand': jnp.asarray(expand, _BF16), 'gexp': jnp.asarray(gexp, _F32),
        'g_cmp': g_cmp, 'near': near,
    }


def _nsa_prompt_attention(q, glog, side):
    bsz, t_len, _ = q.shape
    n_t = t_len // TQ
    width = Q_PER_KV * HEAD_DIM
    n_cmp_pad = side['kct'].shape[-1]
    n_sel = side['ovt'].shape[0]
    grid = (bsz, N_KV_HEADS, n_t)
    params = pltpu.CompilerParams(dimension_semantics=("parallel", "parallel", "arbitrary"),
                                  vmem_limit_bytes=VMEM_LIMIT)
    q_spec = pl.BlockSpec((None, TQ, width), lambda b, h, i: (b, i, h))
    per_bh = lambda *shape: pl.BlockSpec((None, None) + shape, lambda b, h, i: (b, h, 0, 0))
    per_h = lambda *shape: pl.BlockSpec((None,) + shape, lambda b, h, i: (h,) + (0,) * len(shape))
    const = lambda *shape: pl.BlockSpec(shape, lambda b, h, i: (0,) * len(shape))
    selt_spec = pl.BlockSpec((None, None, n_sel, TQ), lambda b, h, i: (b, h, 0, i))
    o_c, selt = pl.pallas_call(
        _nsa_cmp_kernel,
        grid=grid,
        in_specs=[q_spec, per_bh(HEAD_DIM, n_cmp_pad), per_bh(n_cmp_pad, HEAD_DIM),
                  per_h(Q_PER_KV * TQ, n_cmp_pad), const(n_sel, n_cmp_pad)],
        out_specs=[q_spec, selt_spec],
        out_shape=[jax.ShapeDtypeStruct((bsz, t_len, N_HEADS * HEAD_DIM), _F32),
                   jax.ShapeDtypeStruct((bsz, N_KV_HEADS, n_sel, t_len), _F32)],
        compiler_params=params,
        name="nsa_cmp_select",
    )(q, side['kct'], side['vc'], side['g_cmp'], side['ovt'])
    gl = jnp.transpose(glog.reshape(bsz, t_len, N_KV_HEADS, Q_PER_KV * N_BRANCH), (0, 2, 1, 3))
    gl = jnp.pad(gl, ((0, 0), (0, 0), (0, 0), (0, LANES - Q_PER_KV * N_BRANCH)))
    stat = pltpu.VMEM((Q_PER_KV, TQ, 2 * HEAD_DIM), _F32)
    return pl.pallas_call(
        _nsa_sel_win_kernel,
        grid=grid,
        in_specs=[q_spec, q_spec, pl.BlockSpec((None, None, TQ, LANES), lambda b, h, i: (b, h, i, 0)), selt_spec,
                  per_bh(HEAD_DIM, t_len), per_bh(t_len, 2 * HEAD_DIM), per_bh(HEAD_DIM, t_len),
                  per_bh(t_len, 2 * HEAD_DIM), per_h(Q_PER_KV * TQ, 2 * TK), const(n_sel, t_len),
                  const(LANES, N_BRANCH * width)],
        out_specs=q_spec,
        out_shape=jax.ShapeDtypeStruct((bsz, t_len, N_HEADS * HEAD_DIM), _F32),
        scratch_shapes=[pltpu.VMEM((TQ, t_len), _F32), stat, stat, stat, stat],
        compiler_params=params,
        name="nsa_select_window",
    )(q, o_c, gl, selt, side['kst'], side['vs'], side['kwt'], side['vw'], side['near'],
      side['expand'], side['gexp'])


def _nsa_prompt_mixer(h, side, w_qg, w_o):
    qg = h @ w_qg
    mixed = _nsa_prompt_attention(qg[..., :N_HEADS * HEAD_DIM], qg[..., N_HEADS * HEAD_DIM:], side)
    return mixed @ w_o


def _run_trunk(x, p, s0_re, s0_im, past_kv, win_buf, pos0, prm):
    bsz, t_len, _ = x.shape
    ssm_re, ssm_im = [], []
    side = None
    kv_rows_new = None
    win_state = None
    for i in range(DEPTH):
        h = _rmsnorm(x, prm['g_mix'][i])
        if i < N_A_LAYERS:
            mix, s_re, s_im = _s5_mixer(h, s0_re[i], s0_im[i], prm['ssm_a_re'][i], prm['ssm_a_im'][i],
                                        prm['ssm_log_dt'][i], prm['ssm_b_re'][i], prm['ssm_b_im'][i],
                                        prm['ssm_c_re'][i], prm['ssm_c_im'][i], prm['ssm_d'][i], prm['w_glu'][i])
            ssm_re.append(s_re)
            ssm_im.append(s_im)
        else:
            j = i - N_A_LAYERS
            if past_kv is None:
                mix = _nsa_prompt_mixer(h, side, prm['w_qg'][j], prm['w_o'][j])
            else:
                mix = _nsa_mixer(h, pos0, side, prm['w_qg'][j], prm['w_o'][j], prm['rel_bias'])
        x = x + mix
        x = _moe_layer(x.reshape(-1, D_MODEL), prm['g_ffn'][i], prm['w_route_group'][i], prm['b_route_group'][i],
                       prm['w_route_expert'][i], prm['b_route_expert'][i], prm['w_exp_up'][i],
                       prm['w_exp_down'][i]).reshape(bsz, t_len, D_MODEL)
        gate = jax.nn.sigmoid(_rmsnorm(x, prm['g_ple'][i]) @ prm['w_ple_gate'][i])
        x = x + (p[i] @ prm['w_ple_proj'][i]) * gate
        if i == N_A_LAYERS - 1:
            kv = (_rmsnorm(x, prm['g_kv']) @ prm['w_kv']).reshape(
                bsz, t_len, KV_SLOTS_PAGED + KV_SLOTS_WIN, N_KV_HEADS, HEAD_DIM)
            kv_rows_new, win_new = kv[:, :, :KV_SLOTS_PAGED], kv[:, :, KV_SLOTS_PAGED:]
            if past_kv is None:
                win_state = win_new[:, -min(WINDOW, t_len):]
                k_cmp = _compress(kv[:, :, 0], prm['cmp_pos_k'], prm['cmp_w1_k'], prm['cmp_w2_k'])
                v_cmp = _compress(kv[:, :, 1], prm['cmp_pos_v'], prm['cmp_w1_v'], prm['cmp_w2_v'])
                side = _nsa_prompt_side(kv, k_cmp, v_cmp, prm['rel_bias'])
            else:
                kv_rows = jnp.concatenate([past_kv, kv_rows_new], axis=1)
                w_buf = win_buf.shape[1]
                win_rows = jnp.concatenate([win_buf, win_new], axis=1)
                win_base = pos0 - w_buf
                win_state = win_rows[:, -w_buf:]
                side = _prepare_kv(kv_rows, win_rows, win_base, prm)
    y = _rmsnorm(x, prm['g_final'])
    return y, kv_rows_new, win_state, jnp.stack(ssm_re), jnp.stack(ssm_im)


def kernel(x_prompt, x_sample, p_prompt, p_sample, cache_kv, cache_win, state_ssm_re, state_ssm_im, page_table,
           g_mix, g_ffn, g_ple, g_kv, g_final,
           ssm_a_re, ssm_a_im, ssm_log_dt, ssm_b_re, ssm_b_im, ssm_c_re, ssm_c_im, ssm_d, w_glu,
           w_kv, cmp_pos_k, cmp_pos_v, cmp_w1_k, cmp_w2_k, cmp_w1_v, cmp_w2_v, w_qg, w_o, rel_bias,
           w_route_group, b_route_group, w_route_expert, b_route_expert, w_exp_up, w_exp_down,
           w_ple_proj, w_ple_gate):
    prm = {
        'g_mix': g_mix, 'g_ffn': g_ffn, 'g_ple': g_ple, 'g_kv': g_kv, 'g_final': g_final,
        'ssm_a_re': ssm_a_re, 'ssm_a_im': ssm_a_im, 'ssm_log_dt': ssm_log_dt,
        'ssm_b_re': ssm_b_re, 'ssm_b_im': ssm_b_im, 'ssm_c_re': ssm_c_re, 'ssm_c_im': ssm_c_im,
        'ssm_d': ssm_d, 'w_glu': w_glu,
        'w_kv': w_kv, 'cmp_pos_k': cmp_pos_k, 'cmp_pos_v': cmp_pos_v,
        'cmp_w1_k': cmp_w1_k, 'cmp_w2_k': cmp_w2_k, 'cmp_w1_v': cmp_w1_v, 'cmp_w2_v': cmp_w2_v,
        'w_qg': w_qg, 'w_o': w_o, 'rel_bias': rel_bias,
        'w_route_group': w_route_group, 'b_route_group': b_route_group,
        'w_route_expert': w_route_expert, 'b_route_expert': b_route_expert,
        'w_exp_up': w_exp_up, 'w_exp_down': w_exp_down,
        'w_ple_proj': w_ple_proj, 'w_ple_gate': w_ple_gate,
    }
    zero_state = jnp.zeros((N_A_LAYERS, x_prompt.shape[0], SSM_GROUPS, SSM_STATE), _F32)
    y_prompt, kv_prompt, win_prompt, ssm_re_prompt, ssm_im_prompt = _run_trunk(
        x_prompt, p_prompt, zero_state, zero_state, None, None, 0, prm)
    n_seq, n_pages = page_table.shape
    past_len = n_pages * cache_kv.shape[1]
    past_kv = cache_kv[page_table].reshape((n_seq, past_len) + cache_kv.shape[2:])
    y_sample, kv_sample, win_sample, ssm_re_sample, ssm_im_sample = _run_trunk(
        x_sample, p_sample, state_ssm_re, state_ssm_im, past_kv, cache_win, past_len, prm)
    return (y_prompt, y_sample, kv_prompt, win_prompt, ssm_re_prompt, ssm_im_prompt,
            kv_sample, win_sample, ssm_re_sample, ssm_im_sample)
```

```python
import functools
import math

import jax
import jax.numpy as jnp
from jax import lax
from jax.experimental import pallas as pl
from jax.experimental.pallas import tpu as pltpu

D_MODEL = 1024
DEPTH = 4
N_A_LAYERS = DEPTH // 2
SSM_GROUP = 16
SSM_GROUPS = D_MODEL // SSM_GROUP
SSM_STATE = 64
N_HEADS = 16
HEAD_DIM = D_MODEL // N_HEADS
N_KV_HEADS = 4
Q_PER_KV = N_HEADS // N_KV_HEADS
CMP_LEN = 32
CMP_STRIDE = 16
SEL_BLOCK = 64
SEL_TOPK = 16
WINDOW = 512
N_BRANCH = 3
KV_SLOTS_PAGED = 4
KV_SLOTS_WIN = 2
Q_BLOCK = 64
N_BUCKETS = 32
MAX_DISTANCE = 128
N_EXPERT_GROUPS = 4
EXPERTS_PER_GROUP = 4
N_EXPERTS = N_EXPERT_GROUPS * EXPERTS_PER_GROUP
EXPERT_TOPK = 2
D_EXPERT = 256
RMS_EPS = 1e-6
NEG_INF = -1e30
FORCE = 1e4

LANES = 128
VMEM_LIMIT = 48 * 1024 * 1024

_F32 = jnp.float32
_BF16 = jnp.bfloat16


def _moe_kernel(x_ref, g_ref, wr_ref, br_ref, wup_ref, wdn_ref, o_ref, h_scr, comb_scr, acc_scr):
    e = pl.program_id(1)
    lane = lax.broadcasted_iota(jnp.int32, comb_scr.shape, 1)

    @pl.when(e == 0)
    def _route():
        x = x_ref[...]
        h = x * lax.rsqrt(jnp.mean(x * x, axis=-1, keepdims=True) + RMS_EPS) * g_ref[...]
        h_scr[...] = h.astype(_BF16)
        logits = jnp.dot(h, wr_ref[...], preferred_element_type=_F32,
                         precision=lax.Precision.HIGHEST) + br_ref[...]
        is_grp = (lane >= N_EXPERTS) & (lane < N_EXPERTS + N_EXPERT_GROUPS)
        lg = jnp.where(is_grp, logits, -jnp.inf)
        gmax = jnp.max(lg, axis=-1, keepdims=True)
        gi = jnp.min(jnp.where(lg == gmax, lane, LANES), axis=-1, keepdims=True) - N_EXPERTS
        gp = 1.0 / jnp.sum(jnp.where(is_grp, jnp.exp(lg - gmax), 0.0), axis=-1, keepdims=True)
        in_grp = (lane < N_EXPERTS) & ((lane // EXPERTS_PER_GROUP) == gi)
        le = jnp.where(in_grp, logits, -jnp.inf)
        m1 = jnp.max(le, axis=-1, keepdims=True)
        i1 = jnp.min(jnp.where(le == m1, lane, LANES), axis=-1, keepdims=True)
        le2 = jnp.where(lane == i1, -jnp.inf, le)
        m2 = jnp.max(le2, axis=-1, keepdims=True)
        i2 = jnp.min(jnp.where(le2 == m2, lane, LANES), axis=-1, keepdims=True)
        e2 = jnp.exp(m2 - m1)
        den = 1.0 + e2
        comb_scr[...] = jnp.where(lane == i1, gp / den, jnp.where(lane == i2, gp * e2 / den, 0.0))
        acc_scr[...] = x

    up = jnp.dot(h_scr[...], wup_ref[0], preferred_element_type=_F32)
    a = up[:, :D_EXPERT]
    b = up[:, D_EXPERT:]
    c = jnp.sum(jnp.where(lane == e, comb_scr[...], 0.0), axis=-1, keepdims=True)
    act = (a * jax.nn.sigmoid(a)) * b * c
    acc_scr[...] += jnp.dot(act.astype(_BF16), wdn_ref[0], preferred_element_type=_F32)

    @pl.when(e == N_EXPERTS - 1)
    def _store():
        o_ref[...] = acc_scr[...]


def _moe_layer(x2, g, w_rg, b_rg, w_re, b_re, w_up, w_dn):
    n_tok, d = x2.shape
    tm = min(n_tok, 1024)
    wr = jnp.zeros((d, LANES), _F32).at[:, :N_EXPERTS].set(w_re).at[:, N_EXPERTS:N_EXPERTS + N_EXPERT_GROUPS].set(w_rg)
    br = jnp.zeros((1, LANES), _F32).at[0, :N_EXPERTS].set(b_re).at[0, N_EXPERTS:N_EXPERTS + N_EXPERT_GROUPS].set(b_rg)
    return pl.pallas_call(
        _moe_kernel,
        grid=(n_tok // tm, N_EXPERTS),
        in_specs=[
            pl.BlockSpec((tm, d), lambda i, e: (i, 0)),
            pl.BlockSpec((1, d), lambda i, e: (0, 0)),
            pl.BlockSpec((d, LANES), lambda i, e: (0, 0)),
            pl.BlockSpec((1, LANES), lambda i, e: (0, 0)),
            pl.BlockSpec((1, d, 2 * D_EXPERT), lambda i, e: (e, 0, 0)),
            pl.BlockSpec((1, D_EXPERT, d), lambda i, e: (e, 0, 0)),
        ],
        out_specs=pl.BlockSpec((tm, d), lambda i, e: (i, 0)),
        out_shape=jax.ShapeDtypeStruct((n_tok, d), _F32),
        scratch_shapes=[
            pltpu.VMEM((tm, d), _BF16),
            pltpu.VMEM((tm, LANES), _F32),
            pltpu.VMEM((tm, d), _F32),
        ],
        compiler_params=pltpu.CompilerParams(
            dimension_semantics=("parallel", "arbitrary"), vmem_limit_bytes=VMEM_LIMIT),
        name="hmoe",
    )(x2, g.reshape(1, d), wr, br, w_up.astype(_BF16), w_dn.astype(_BF16))


def _rmsnorm(x, g):
    xf = x.astype(_F32)
    y = xf * lax.rsqrt(jnp.mean(xf * xf, axis=-1, keepdims=True) + RMS_EPS)
    return (y * g.astype(_F32)).astype(x.dtype)


def _cmul(ar, ai, br, bi):
    return ar * br - ai * bi, ar * bi + ai * br


def _s5_operators(a_re, a_im, log_dt, b_re, b_im, c_re, c_im, chunk):
    hp = lax.Precision.HIGHEST
    dt = jnp.exp(log_dt)[:, None]
    decay = jnp.exp(a_re * dt)
    ab_re, ab_im = decay * jnp.cos(a_im * dt), decay * jnp.sin(a_im * dt)
    den = a_re * a_re + a_im * a_im
    num_re = ab_re - 1.0
    q_re = (num_re * a_re + ab_im * a_im) / den
    q_im = (ab_im * a_re - num_re * a_im) / den
    bb_re, bb_im = _cmul(q_re[..., None], q_im[..., None], b_re, b_im)
    pw_re, pw_im = [jnp.ones_like(ab_re)], [jnp.zeros_like(ab_im)]
    for _ in range(chunk):
        nr, ni = _cmul(pw_re[-1], pw_im[-1], ab_re, ab_im)
        pw_re.append(nr)
        pw_im.append(ni)
    pw_re, pw_im = jnp.stack(pw_re), jnp.stack(pw_im)
    w_re, w_im = _cmul(pw_re[:chunk, :, :, None], pw_im[:chunk, :, :, None], bb_re[None], bb_im[None])
    k_lag = (jnp.einsum('gcp,jgpd->jgcd', c_re, w_re, precision=hp)
             - jnp.einsum('gcp,jgpd->jgcd', c_im, w_im, precision=hp))
    tt = jnp.arange(chunk)
    lag = tt[None, :] - tt[:, None]
    m = jnp.where((lag >= 0)[:, :, None, None, None], k_lag[jnp.clip(lag, 0, chunk - 1)], 0.0)
    m = jnp.transpose(m, (2, 0, 4, 1, 3)).reshape(SSM_GROUPS, chunk * SSM_GROUP, chunk * SSM_GROUP)
    h_re = jnp.transpose(w_re[::-1], (1, 0, 3, 2)).reshape(SSM_GROUPS, chunk * SSM_GROUP, SSM_STATE)
    h_im = jnp.transpose(w_im[::-1], (1, 0, 3, 2)).reshape(SSM_GROUPS, chunk * SSM_GROUP, SSM_STATE)
    ca_re, ca_im = _cmul(c_re[None], c_im[None], pw_re[1:, :, None, :], pw_im[1:, :, None, :])
    gm_re = jnp.transpose(ca_re, (1, 3, 0, 2)).reshape(SSM_GROUPS, SSM_STATE, chunk * SSM_GROUP)
    gm_im = -jnp.transpose(ca_im, (1, 3, 0, 2)).reshape(SSM_GROUPS, SSM_STATE, chunk * SSM_GROUP)
    return m, h_re, h_im, gm_re, gm_im, pw_re[chunk][:, None, :], pw_im[chunk][:, None, :]


def _s5_kernel(u_ref, m_ref, hre_ref, him_ref, gre_ref, gim_ref, alre_ref, alim_ref, d_ref, s0re_ref, s0im_ref,
               y_ref, sre_ref, sim_ref, ere, eim, sinre, sinim, *, n_chunks, rows, groups):
    hp = lax.Precision.HIGHEST
    for g in range(groups):
        u = u_ref[g]
        ere[g] = jnp.dot(u, hre_ref[g], preferred_element_type=_F32, precision=hp)
        eim[g] = jnp.dot(u, him_ref[g], preferred_element_type=_F32, precision=hp)

    def step(j, carry):
        row = pl.ds(pl.multiple_of(j * rows, rows), rows)
        out = []
        for g in range(groups):
            sr, si = carry[2 * g], carry[2 * g + 1]
            sinre[g, row, :] = sr
            sinim[g, row, :] = si
            ar, ai = alre_ref[g], alim_ref[g]
            out.append(ar * sr - ai * si + ere[g, row, :])
            out.append(ar * si + ai * sr + eim[g, row, :])
        return tuple(out)

    init = []
    for g in range(groups):
        init += [s0re_ref[g], s0im_ref[g]]
    fin = lax.fori_loop(0, n_chunks, step, tuple(init))
    for g in range(groups):
        sre_ref[g] = fin[2 * g]
        sim_ref[g] = fin[2 * g + 1]
        u = u_ref[g]
        y = jnp.dot(u.astype(_BF16), m_ref[g], preferred_element_type=_F32)
        y += jnp.dot(sinre[g].astype(_BF16), gre_ref[g], preferred_element_type=_F32)
        y += jnp.dot(sinim[g].astype(_BF16), gim_ref[g], preferred_element_type=_F32)
        y_ref[g] = y + d_ref[g] * u


def _s5_scan(u, s0_re, s0_im, a_re, a_im, log_dt, b_re, b_im, c_re, c_im, d_skip):
    bsz, t_len, _ = u.shape
    chunk = min(t_len, 16)
    n_chunks = t_len // chunk
    rows = -(-bsz // 8) * 8
    width = chunk * SSM_GROUP
    groups = 2
    m, h_re, h_im, gm_re, gm_im, al_re, al_im = _s5_operators(a_re, a_im, log_dt, b_re, b_im, c_re, c_im, chunk)
    uf = u.reshape(bsz, n_chunks, chunk, SSM_GROUPS, SSM_GROUP)
    uf = jnp.transpose(uf, (3, 1, 0, 2, 4))
    uf = jnp.pad(uf, ((0, 0), (0, 0), (0, rows - bsz), (0, 0), (0, 0))).reshape(SSM_GROUPS, n_chunks * rows, width)
    pad_state = lambda s: jnp.pad(jnp.transpose(s, (1, 0, 2)), ((0, 0), (0, rows - bsz), (0, 0)))
    d_tile = jnp.tile(d_skip, (1, chunk))[:, None, :]
    gspec = lambda *shape: pl.BlockSpec((groups,) + shape, lambda i: (i,) + (0,) * len(shape))
    y, s_re, s_im = pl.pallas_call(
        functools.partial(_s5_kernel, n_chunks=n_chunks, rows=rows, groups=groups),
        grid=(SSM_GROUPS // groups,),
        in_specs=[gspec(n_chunks * rows, width), gspec(width, width), gspec(width, SSM_STATE), gspec(width, SSM_STATE),
                  gspec(SSM_STATE, width), gspec(SSM_STATE, width), gspec(1, SSM_STATE), gspec(1, SSM_STATE),
                  gspec(1, width), gspec(rows, SSM_STATE), gspec(rows, SSM_STATE)],
        out_specs=[gspec(n_chunks * rows, width), gspec(rows, SSM_STATE), gspec(rows, SSM_STATE)],
        out_shape=[jax.ShapeDtypeStruct((SSM_GROUPS, n_chunks * rows, width), _F32),
                   jax.ShapeDtypeStruct((SSM_GROUPS, rows, SSM_STATE), _F32),
                   jax.ShapeDtypeStruct((SSM_GROUPS, rows, SSM_STATE), _F32)],
        scratch_shapes=[pltpu.VMEM((groups, n_chunks * rows, SSM_STATE), _F32) for _ in range(4)],
        compiler_params=pltpu.CompilerParams(dimension_semantics=("parallel",), vmem_limit_bytes=VMEM_LIMIT),
        name="s5_scan",
    )(uf, m.astype(_BF16), h_re, h_im, gm_re.astype(_BF16), gm_im.astype(_BF16), al_re, al_im, d_tile,
      pad_state(s0_re), pad_state(s0_im))
    y = y.reshape(SSM_GROUPS, n_chunks, rows, chunk, SSM_GROUP)[:, :, :bsz]
    y = jnp.transpose(y, (2, 1, 3, 0, 4)).reshape(bsz, t_len, D_MODEL)
    unpad = lambda s: jnp.transpose(s[:, :bsz], (1, 0, 2))
    return y, unpad(s_re), unpad(s_im)


def _s5_mixer(u, s0_re, s0_im, a_re, a_im, log_dt, b_re, b_im, c_re, c_im, d_skip, w_glu):
    y, s_re, s_im = _s5_scan(u, s0_re, s0_im, a_re, a_im, log_dt, b_re, b_im, c_re, c_im, d_skip)
    z = jax.nn.gelu(y)
    a, g = jnp.split(z @ w_glu, 2, axis=-1)
    return a * jax.nn.sigmoid(g), s_re, s_im


def _compress(rows, pos_emb, w1, w2):
    bsz, tk = rows.shape[:2]
    n_cmp = (tk - CMP_LEN) // CMP_STRIDE + 1
    idx = jnp.arange(n_cmp)[:, None] * CMP_STRIDE + jnp.arange(CMP_LEN)[None, :]
    blk = rows[:, idx] + pos_emb[None, None, :, None, :]
    blk = jnp.moveaxis(blk, 3, 2).reshape(bsz, n_cmp, N_KV_HEADS, CMP_LEN * HEAD_DIM)
    return jax.nn.gelu(blk @ w1) @ w2


TQ = 128
TK = 128
NEAR_BUCKET_DIST = 113


def _bucket_table(dist):
    import numpy as np
    max_exact = N_BUCKETS // 2
    d = np.maximum(np.asarray(dist, np.int64), 0)
    df = np.maximum(d, 1).astype(np.float64)
    large = max_exact + (np.log(df / max_exact) / math.log(MAX_DISTANCE / max_exact)
                         * (N_BUCKETS - max_exact)).astype(np.int64)
    return np.where(d < max_exact, d, np.minimum(large, N_BUCKETS - 1)).astype(np.int32)


def _bias_tiles(rel_bias):
    import numpy as np
    rel_bias = rel_bias - rel_bias[N_BUCKETS - 1]
    r = np.arange(TQ)[:, None]
    c = np.arange(2 * LANES)[None, :]
    d_cmp = np.where(c < 16, r - (CMP_LEN - 1) + CMP_STRIDE * (8 - c), 10 ** 6)
    g_cmp = rel_bias[_bucket_table(d_cmp)]
    g_cmp = jnp.transpose(g_cmp, (2, 0, 1)).reshape(N_KV_HEADS, Q_PER_KV * TQ, 2 * LANES)
    k = np.arange(2 * TK)[None, :]
    d_near = TK + r - k
    near = rel_bias[_bucket_table(d_near)] + jnp.asarray(np.where(d_near >= 0, 0.0, NEG_INF), _F32)[:, :, None]
    near = jnp.transpose(near, (2, 0, 1)).reshape(N_KV_HEADS, Q_PER_KV * TQ, 2 * TK)
    return g_cmp, near


def _stack_heads(q):
    return jnp.concatenate([q[:, g * HEAD_DIM:(g + 1) * HEAD_DIM] for g in range(Q_PER_KV)], axis=0)


def _unstack_heads(o):
    return jnp.concatenate([o[g] for g in range(Q_PER_KV)], axis=1)


def _nsa_cmp_kernel(q_ref, kct_ref, vc_ref, g_ref, ovt_ref, oc_ref, selt_ref):
    i = pl.program_id(2)
    t0 = i * TQ
    n_cmp_pad = kct_ref.shape[-1]
    qs = _stack_heads(q_ref[...] * (HEAD_DIM ** -0.5)).astype(_BF16)
    s = jnp.dot(qs, kct_ref[...], preferred_element_type=_F32)
    shift = (i * (TQ // CMP_STRIDE) + n_cmp_pad - 8) % n_cmp_pad
    bias = pltpu.roll(g_ref[...], shift, 1)
    s3 = (s + bias).reshape(Q_PER_KV, TQ, n_cmp_pad)
    r = lax.broadcasted_iota(jnp.int32, (TQ, n_cmp_pad), 0)
    n = lax.broadcasted_iota(jnp.int32, (TQ, n_cmp_pad), 1)
    valid = (t0 + r - CMP_STRIDE * n - (CMP_LEN - 1)) >= 0
    sm = jnp.where(valid, s3, NEG_INF)
    m = jnp.max(sm, axis=-1, keepdims=True)
    p = jnp.where(valid, jnp.exp(sm - m), 0.0)
    l = jnp.sum(p, axis=-1, keepdims=True)
    pb = (p * jnp.where(l > 0.0, 1.0 / l, 0.0)).astype(_BF16)
    oc = jnp.dot(pb.reshape(Q_PER_KV * TQ, n_cmp_pad), vc_ref[...], preferred_element_type=_F32)
    oc_ref[...] = _unstack_heads(oc.reshape(Q_PER_KV, TQ, HEAD_DIM))
    imp = jnp.zeros((ovt_ref.shape[0], TQ), _F32)
    for g in range(Q_PER_KV):
        imp += lax.dot_general(ovt_ref[...], pb[g], (((1,), (1,)), ((), ())), preferred_element_type=_F32)
    n_sel = imp.shape[0]
    blk = lax.broadcasted_iota(jnp.int32, (n_sel, TQ), 0)
    t = t0 + lax.broadcasted_iota(jnp.int32, (n_sel, TQ), 1)
    cur = t // SEL_BLOCK
    forced = (blk == 0) | (blk == cur) | (blk == cur - 1)
    score = jnp.where(forced, FORCE, jnp.where(blk * SEL_BLOCK <= t, imp, -FORCE))
    rank = jnp.zeros((n_sel, TQ), jnp.int32)
    for sp in range(n_sel):
        row = score[sp:sp + 1, :]
        beats = (row > score) | ((row == score) & (blk > sp))
        rank += beats.astype(jnp.int32)
    selt_ref[...] = (rank < SEL_TOPK).astype(_F32)


def _flash_step(qs, kt, v, add, m_scr, acc_scr):
    n_k = kt.shape[-1]
    s3 = jnp.dot(qs, kt, preferred_element_type=_F32).reshape(Q_PER_KV, TQ, n_k)
    if add is not None:
        s3 = s3 + add
    chunks = [s3[..., k * LANES:(k + 1) * LANES] for k in range(n_k // LANES)]
    m_old = m_scr[...]
    m_new = jnp.maximum(m_old, jnp.max(functools.reduce(jnp.maximum, chunks), axis=-1, keepdims=True))
    alpha = jnp.exp(m_old - m_new)
    m_scr[...] = m_new
    pv = None
    for k, sk in enumerate(chunks):
        p = jnp.exp(sk - m_new).astype(_BF16).reshape(Q_PER_KV * TQ, LANES)
        d = jnp.dot(p, v[k * LANES:(k + 1) * LANES], preferred_element_type=_F32)
        pv = d if pv is None else pv + d
    acc_scr[...] = alpha * acc_scr[...] + pv.reshape(Q_PER_KV, TQ, 2 * HEAD_DIM)


FAR_TILES = 4


def _nsa_sel_win_kernel(q_ref, oc_ref, glog_ref, selt_ref, kst_ref, vs_ref, kwt_ref, vw_ref, near_ref,
                        exp_ref, gexp_ref, o_ref, seladd, m_s, acc_s, m_w, acc_w):
    i = pl.program_id(2)
    qs = _stack_heads(q_ref[...] * (HEAD_DIM ** -0.5)).astype(_BF16)
    sel = jnp.dot(selt_ref[...].T.astype(_BF16), exp_ref[...], preferred_element_type=_F32)
    seladd[...] = (sel - 1.0) * (-NEG_INF)
    for m_scr, acc_scr in ((m_s, acc_s), (m_w, acc_w)):
        m_scr[...] = jnp.full(m_scr.shape, NEG_INF, _F32)
        acc_scr[...] = jnp.zeros(acc_scr.shape, _F32)

    def keys(j, n_tiles):
        return pl.ds(pl.multiple_of(j * TK, TK), n_tiles * TK)

    def sel_step(j, n_tiles, bias):
        cols = keys(j, n_tiles)
        add = seladd[:, cols]
        if bias is not None:
            add = add + bias
        _flash_step(qs, kst_ref[:, cols], vs_ref[cols, :], add, m_s, acc_s)

    def win_step(j, n_tiles, add):
        cols = keys(j, n_tiles)
        _flash_step(qs, kwt_ref[:, cols], vw_ref[cols, :], add, m_w, acc_w)

    near = lambda lo: near_ref[:, lo * TK:].reshape(Q_PER_KV, TQ, (2 - lo) * TK)

    n_far = jnp.maximum(i - 1, 0)

    def far_many(j, carry):
        sel_step(j * FAR_TILES, FAR_TILES, None)
        return carry

    def far_one(j, carry):
        sel_step(j, 1, None)
        return carry

    lax.fori_loop(0, n_far // FAR_TILES, far_many, 0)
    lax.fori_loop((n_far // FAR_TILES) * FAR_TILES, n_far, far_one, 0)

    @pl.when(i >= 1)
    def _():
        sel_step(i - 1, 2, near(0))
        win_step(i - 1, 2, near(0))

    @pl.when(i == 0)
    def _():
        sel_step(0, 1, near(1))
        win_step(0, 1, near(1))

    n_lag = WINDOW // TK

    @pl.when(i >= n_lag - 1)
    def _():
        win_step(i - (n_lag - 1), n_lag - 2, None)

    for lag in range(2, n_lag - 1):
        @pl.when(i == lag)
        def _():
            win_step(0, lag - 1, None)

    @pl.when(i >= n_lag)
    def _():
        r = lax.broadcasted_iota(jnp.int32, (TQ, TK), 0)
        c = lax.broadcasted_iota(jnp.int32, (TQ, TK), 1)
        win_step(i - n_lag, 1, jnp.where(c >= r, 0.0, NEG_INF))

    def finish(acc_scr):
        acc = acc_scr[...]
        return _unstack_heads(acc[..., :HEAD_DIM] / acc[..., HEAD_DIM:])

    gates = jnp.dot(jax.nn.sigmoid(glog_ref[...]), gexp_ref[...], preferred_element_type=_F32,
                    precision=lax.Precision.HIGHEST)
    width = Q_PER_KV * HEAD_DIM
    o_ref[...] = (gates[:, :width] * oc_ref[...] + gates[:, width:2 * width] * finish(acc_s)
                  + gates[:, 2 * width:] * finish(acc_w))


def _nsa_prompt_side(kv, k_cmp, v_cmp, rel_bias):
    import numpy as np
    bsz, t_len = kv.shape[:2]
    n_cmp = k_cmp.shape[1]
    n_cmp_pad = -(-n_cmp // LANES) * LANES
    n_sel = t_len // SEL_BLOCK
    kvb = kv.astype(_BF16)
    t_last = lambda a: jnp.transpose(a, (0, 2, 3, 1))
    t_rows = lambda a: jnp.transpose(a, (0, 2, 1, 3))
    pad_c = lambda a: jnp.pad(a.astype(_BF16), ((0, 0), (0, n_cmp_pad - n_cmp), (0, 0), (0, 0)))
    c_start = np.arange(n_cmp_pad)[None, :] * CMP_STRIDE
    s_start = np.arange(n_sel)[:, None] * SEL_BLOCK
    ovt = ((c_start < s_start + SEL_BLOCK) & (c_start + CMP_LEN > s_start) & (np.arange(n_cmp_pad)[None, :] < n_cmp))
    expand = np.arange(n_sel)[:, None] == (np.arange(t_len)[None, :] // SEL_BLOCK)
    lanes = np.arange(LANES)[:, None]
    cols = np.arange(N_BRANCH * Q_PER_KV * HEAD_DIM)[None, :]
    gexp = lanes == (cols // HEAD_DIM % Q_PER_KV) * N_BRANCH + cols // (Q_PER_KV * HEAD_DIM)
    g_cmp, near = _bias_tiles(rel_bias)
    with_ones = lambda a: jnp.concatenate([t_rows(a), jnp.ones((bsz, N_KV_HEADS, t_len, HEAD_DIM), _BF16)], axis=-1)
    return {
        'kct': t_last(pad_c(k_cmp)), 'vc': t_rows(pad_c(v_cmp)),
        'kst': t_last(kvb[:, :, 2]), 'vs': with_ones(kvb[:, :, 3]),
        'kwt': t_last(kvb[:, :, 4]), 'vw': with_ones(kvb[:, :, 5]),
        'ovt': jnp.asarray(ovt, _BF16), 'expand': jnp.asarray(expand, _BF16), 'gexp': jnp.asarray(gexp, _F32),
        'g_cmp': g_cmp, 'near': near,
    }


def _nsa_prompt_attention(q, glog, side):
    bsz, t_len, _ = q.shape
    n_t = t_len // TQ
    width = Q_PER_KV * HEAD_DIM
    n_cmp_pad = side['kct'].shape[-1]
    n_sel = side['ovt'].shape[0]
    grid = (bsz, N_KV_HEADS, n_t)
    params = pltpu.CompilerParams(dimension_semantics=("parallel", "parallel", "arbitrary"),
                                  vmem_limit_bytes=VMEM_LIMIT)
    q_spec = pl.BlockSpec((None, TQ, width), lambda b, h, i: (b, i, h))
    per_bh = lambda *shape: pl.BlockSpec((None, None) + shape, lambda b, h, i: (b, h, 0, 0))
    per_h = lambda *shape: pl.BlockSpec((None,) + shape, lambda b, h, i: (h,) + (0,) * len(shape))
    const = lambda *shape: pl.BlockSpec(shape, lambda b, h, i: (0,) * len(shape))
    selt_spec = pl.BlockSpec((None, None, n_sel, TQ), lambda b, h, i: (b, h, 0, i))
    o_c, selt = pl.pallas_call(
        _nsa_cmp_kernel,
        grid=grid,
        in_specs=[q_spec, per_bh(HEAD_DIM, n_cmp_pad), per_bh(n_cmp_pad, HEAD_DIM),
                  per_h(Q_PER_KV * TQ, n_cmp_pad), const(n_sel, n_cmp_pad)],
        out_specs=[q_spec, selt_spec],
        out_shape=[jax.ShapeDtypeStruct((bsz, t_len, N_HEADS * HEAD_DIM), _F32),
                   jax.ShapeDtypeStruct((bsz, N_KV_HEADS, n_sel, t_len), _F32)],
        compiler_params=params,
        name="nsa_cmp_select",
    )(q, side['kct'], side['vc'], side['g_cmp'], side['ovt'])
    gl = jnp.transpose(glog.reshape(bsz, t_len, N_KV_HEADS, Q_PER_KV * N_BRANCH), (0, 2, 1, 3))
    gl = jnp.pad(gl, ((0, 0), (0, 0), (0, 0), (0, LANES - Q_PER_KV * N_BRANCH)))
    stat = pltpu.VMEM((Q_PER_KV, TQ, 2 * HEAD_DIM), _F32)
    return pl.pallas_call(
        _nsa_sel_win_kernel,
        grid=grid,
        in_specs=[q_spec, q_spec, pl.BlockSpec((None, None, TQ, LANES), lambda b, h, i: (b, h, i, 0)), selt_spec,
                  per_bh(HEAD_DIM, t_len), per_bh(t_len, 2 * HEAD_DIM), per_bh(HEAD_DIM, t_len),
                  per_bh(t_len, 2 * HEAD_DIM), per_h(Q_PER_KV * TQ, 2 * TK), const(n_sel, t_len),
                  const(LANES, N_BRANCH * width)],
        out_specs=q_spec,
        out_shape=jax.ShapeDtypeStruct((bsz, t_len, N_HEADS * HEAD_DIM), _F32),
        scratch_shapes=[pltpu.VMEM((TQ, t_len), _F32), stat, stat, stat, stat],
        compiler_params=params,
        name="nsa_select_window",
    )(q, o_c, gl, selt, side['kst'], side['vs'], side['kwt'], side['vw'], side['near'],
      side['expand'], side['gexp'])


def _nsa_prompt_mixer(h, side, w_qg, w_o):
    qg = h @ w_qg
    mixed = _nsa_prompt_attention(qg[..., :N_HEADS * HEAD_DIM], qg[..., N_HEADS * HEAD_DIM:], side)
    return mixed @ w_o


PAGES_PER_STEP = 4
GROUPS_PER_PAGE_ROWS = CMP_STRIDE


def _dot_nt(a, b):
    return lax.dot_general(a, b, (((1,), (1,)), ((), ())), preferred_element_type=_F32)


def _cmp_pages_kernel(pt_ref, *refs, n_pages):
    del pt_ref
    pages = refs[:n_pages]
    posa_ref, posb_ref, w1a_ref, w1b_ref, w2_ref, o_ref = refs[n_pages:]
    x = jnp.concatenate([pg[...] for pg in pages], axis=0)
    z0 = jnp.dot((x + posa_ref[...]).astype(_BF16), w1a_ref[...], preferred_element_type=_F32)
    z1 = jnp.dot((x + posb_ref[...]).astype(_BF16), w1b_ref[...], preferred_element_type=_F32)
    pre = z0 + pltpu.roll(z1, z1.shape[0] - 1, 0)
    o_ref[...] = jnp.dot(jax.nn.gelu(pre).astype(_BF16), w2_ref[...], preferred_element_type=_F32).astype(o_ref.dtype)


def _cmp_pages(grouped, page_table, pos_emb, w1, w2):
    n_seq, n_pages = page_table.shape
    groups_per_page, flat = grouped.shape[1:]
    width = N_KV_HEADS * HEAD_DIM
    hidden = w1.shape[1]
    eye = jnp.eye(N_KV_HEADS, dtype=_F32)
    w1r = w1.reshape(2, CMP_STRIDE, HEAD_DIM, hidden)
    bd1 = jnp.einsum('hk,abdf->abhdkf', eye, w1r).reshape(2, flat, N_KV_HEADS * hidden).astype(_BF16)
    bd2 = jnp.einsum('hk,fd->hfkd', eye, w2).reshape(N_KV_HEADS * hidden, width).astype(_BF16)
    pos_t = jnp.broadcast_to(pos_emb.reshape(2, CMP_STRIDE, 1, HEAD_DIM),
                             (2, CMP_STRIDE, N_KV_HEADS, HEAD_DIM)).reshape(2, 1, flat)
    n_rows = n_pages * groups_per_page
    page_spec = lambda k: pl.BlockSpec((None, groups_per_page, flat), lambda b, pt: (pt[b, k], 0, 0))
    const = lambda *shape: pl.BlockSpec(shape, lambda b, pt: (0,) * len(shape))
    return pl.pallas_call(
        functools.partial(_cmp_pages_kernel, n_pages=n_pages),
        grid_spec=pltpu.PrefetchScalarGridSpec(
            num_scalar_prefetch=1,
            grid=(n_seq,),
            in_specs=[page_spec(k) for k in range(n_pages)] + [
                const(1, flat), const(1, flat), const(flat, N_KV_HEADS * hidden), const(flat, N_KV_HEADS * hidden),
                const(N_KV_HEADS * hidden, width)],
            out_specs=pl.BlockSpec((None, n_rows, width), lambda b, pt: (b, 0, 0)),
        ),
        out_shape=jax.ShapeDtypeStruct((n_seq, n_rows, width), _BF16),
        compiler_params=pltpu.CompilerParams(dimension_semantics=("parallel",), vmem_limit_bytes=VMEM_LIMIT),
        name="cmp_pages",
    )(page_table, *([grouped] * n_pages), pos_t[0], pos_t[1], bd1[0], bd1[1], bd2)


def _pick_own_kv_head(x, n_q):
    rows_per = Q_PER_KV * n_q
    return jnp.concatenate([x[h * rows_per:(h + 1) * rows_per, h * HEAD_DIM:(h + 1) * HEAD_DIM]
                            for h in range(N_KV_HEADS)], axis=0)


def _nsa_sample_kernel(pt_ref, q_ref, gl_ref, kc_ref, vc_ref, win_ref, new_ref, *rest, n_q, past_len, n_sel):
    del pt_ref
    pages = rest[:PAGES_PER_STEP]
    (tcmp_ref, tsel_ref, twin_ref, exp_ref, ov_ref, o_ref,
     addsel, m_scr, l_scr, acc_scr, oc_scr, ow_scr) = rest[PAGES_PER_STEP:]
    j = pl.program_id(1)
    rows = N_HEADS * n_q
    width = N_KV_HEADS * HEAD_DIM
    w_buf = win_ref.shape[0]
    step_keys = PAGES_PER_STEP * pages[0].shape[0]
    rr = lax.broadcasted_iota(jnp.int32, (rows, width), 0)
    cc = lax.broadcasted_iota(jnp.int32, (rows, width), 1)
    q4 = jnp.concatenate([q_ref[...] * (HEAD_DIM ** -0.5)] * N_KV_HEADS, axis=1)
    qbd = jnp.where(rr // (Q_PER_KV * n_q) == cc // HEAD_DIM, q4, 0.0).astype(_BF16)
    new_rows = jnp.concatenate([new_ref[...], jnp.zeros((LANES - n_q, new_ref.shape[1]), _F32)], axis=0).astype(_BF16)

    @pl.when(j == 0)
    def _first():
        s = _dot_nt(qbd, kc_ref[...]) + tcmp_ref[...]
        p = jnp.exp(s - jnp.max(s, axis=-1, keepdims=True))
        pb = (p / jnp.sum(p, axis=-1, keepdims=True)).astype(_BF16)
        oc_scr[...] = _pick_own_kv_head(jnp.dot(pb, vc_ref[...], preferred_element_type=_F32), n_q)
        pm = jnp.dot(pb, ov_ref[...], preferred_element_type=_F32)
        imp = jnp.concatenate(
            [sum(pm[(h * Q_PER_KV + g) * n_q:(h * Q_PER_KV + g + 1) * n_q] for g in range(Q_PER_KV))
             for h in range(N_KV_HEADS)], axis=0)
        blk = lax.broadcasted_iota(jnp.int32, imp.shape, 1)
        t = past_len + lax.broadcasted_iota(jnp.int32, imp.shape, 0) % n_q
        cur = t // SEL_BLOCK
        forced = (blk == 0) | (blk == cur) | (blk == cur - 1)
        score = jnp.where(forced, FORCE, jnp.where(blk * SEL_BLOCK <= t, imp, -FORCE))
        score = jnp.where(blk < n_sel, score, -jnp.inf)
        rank = jnp.zeros(imp.shape, jnp.int32)
        for sp in range(n_sel):
            col = score[:, sp:sp + 1]
            rank += ((col > score) | ((col == score) & (blk > sp))).astype(jnp.int32)
        sel = (rank < min(SEL_TOPK, n_sel)).astype(_BF16)
        sel_rows = jnp.concatenate([sel[h * n_q:(h + 1) * n_q] for h in range(N_KV_HEADS) for _ in range(Q_PER_KV)],
                                   axis=0)
        addsel[...] = (jnp.dot(sel_rows, exp_ref[...], preferred_element_type=_F32) - 1.0) * (-NEG_INF) + tsel_ref[...]
        kw = win_ref[:, :width].astype(_BF16)
        vw = win_ref[:, width:].astype(_BF16)
        s1 = _dot_nt(qbd, kw) + twin_ref[:, :w_buf]
        s2 = _dot_nt(qbd, new_rows[:, 2 * width:3 * width]) + twin_ref[:, w_buf:]
        m = jnp.maximum(jnp.max(s1, axis=-1, keepdims=True), jnp.max(s2, axis=-1, keepdims=True))
        p1 = jnp.exp(s1 - m)
        p2 = jnp.exp(s2 - m)
        l = jnp.sum(p1, axis=-1, keepdims=True) + jnp.sum(p2, axis=-1, keepdims=True)
        ow = (jnp.dot(p1.astype(_BF16), vw, preferred_element_type=_F32)
              + jnp.dot(p2.astype(_BF16), new_rows[:, 3 * width:], preferred_element_type=_F32))
        ow_scr[...] = _pick_own_kv_head(ow / l, n_q)
        m_scr[...] = jnp.full(m_scr.shape, NEG_INF, _F32)
        l_scr[...] = jnp.zeros(l_scr.shape, _F32)
        acc_scr[...] = jnp.zeros(acc_scr.shape, _F32)

    def flash(k, v, add):
        s = _dot_nt(qbd, k) + add
        m_old = m_scr[...]
        m_new = jnp.maximum(m_old, jnp.max(s, axis=-1, keepdims=True))
        alpha = jnp.exp(m_old - m_new)
        p = jnp.exp(s - m_new)
        m_scr[...] = m_new
        l_scr[...] = alpha * l_scr[...] + jnp.sum(p, axis=-1, keepdims=True)
        acc_scr[...] = alpha * acc_scr[...] + jnp.dot(p.astype(_BF16), v, preferred_element_type=_F32)

    k_cat = jnp.concatenate([pg[:, :width] for pg in pages], axis=0).astype(_BF16)
    v_cat = jnp.concatenate([pg[:, width:] for pg in pages], axis=0).astype(_BF16)
    flash(k_cat, v_cat, addsel[:, pl.ds(pl.multiple_of(j * step_keys, step_keys), step_keys)])

    @pl.when(j == pl.num_programs(1) - 1)
    def _last():
        flash(new_rows[:, :width], new_rows[:, width:2 * width], addsel[:, past_len:])
        o_s = _pick_own_kv_head(acc_scr[...] / l_scr[...], n_q)
        gates = jax.nn.sigmoid(gl_ref[...])
        o_ref[...] = gates[:, 0:1] * oc_scr[...] + gates[:, 1:2] * o_s + gates[:, 2:3] * ow_scr[...]


def _nsa_sample_side(cache_kv, page_table, cache_win, kv_new, rel_bias, prm):
    import numpy as np
    n_seq, n_pages = page_table.shape
    n_phys, page_rows = cache_kv.shape[:2]
    n_q = kv_new.shape[1]
    width = N_KV_HEADS * HEAD_DIM
    past_len = n_pages * page_rows
    w_buf = cache_win.shape[1]
    tk = past_len + n_q
    n_cmp = (tk - CMP_LEN) // CMP_STRIDE + 1
    n_cmp_pad = past_len // CMP_STRIDE
    n_sel = -(-tk // SEL_BLOCK)
    assert n_cmp == n_cmp_pad - 1 and n_cmp_pad == LANES and n_sel <= LANES and n_q <= LANES
    assert page_rows % SEL_BLOCK == 0 and n_pages % PAGES_PER_STEP == 0 and w_buf == min(WINDOW, past_len)
    cache2d = cache_kv.reshape(n_phys, page_rows, KV_SLOTS_PAGED * width)
    gpp = page_rows // CMP_STRIDE
    grouped = jnp.transpose(cache_kv[:, :, :2].reshape(n_phys, gpp, CMP_STRIDE, 2, width), (3, 0, 1, 2, 4))
    grouped = grouped.reshape(2, n_phys, gpp, CMP_STRIDE * width)
    kc = _cmp_pages(grouped[0], page_table, prm['cmp_pos_k'], prm['cmp_w1_k'], prm['cmp_w2_k'])
    vc = _cmp_pages(grouped[1], page_table, prm['cmp_pos_v'], prm['cmp_w1_v'], prm['cmp_w2_v'])
    rows = N_HEADS * n_q
    head = np.arange(rows)[:, None] // n_q
    t = past_len + np.arange(rows)[:, None] % n_q
    bias = lambda d: rel_bias[_bucket_table(d), head]
    masked = lambda ok, d: jnp.where(jnp.asarray(ok), bias(d), NEG_INF)
    n = np.arange(n_cmp_pad)[None, :]
    d_c = t - (n * CMP_STRIDE + CMP_LEN - 1)
    t_cmp = masked((d_c >= 0) & (n < n_cmp), d_c)
    pos = np.arange(past_len + LANES)[None, :]
    t_sel = masked((t - pos >= 0) & (pos < tk), t - pos)
    l = np.arange(w_buf + LANES)[None, :]
    d_w = t - (past_len - w_buf + l)
    t_win = masked((d_w >= 0) & (d_w <= WINDOW) & (l < w_buf + n_q), d_w)
    expand = (np.arange(LANES)[:, None] == pos // SEL_BLOCK) & (np.arange(LANES)[:, None] < n_sel)
    c_start = np.arange(n_cmp_pad)[:, None] * CMP_STRIDE
    s_start = np.arange(LANES)[None, :] * SEL_BLOCK
    ov = ((c_start < s_start + SEL_BLOCK) & (c_start + CMP_LEN > s_start) & (np.arange(n_cmp_pad)[:, None] < n_cmp)
          & (np.arange(LANES)[None, :] < n_sel))
    return {
        'cache2d': cache2d, 'page_table': page_table, 'kc': kc, 'vc': vc,
        'win': cache_win.reshape(n_seq, w_buf, KV_SLOTS_WIN * width),
        'new': kv_new.reshape(n_seq, n_q, -1)[:, :, 2 * width:],
        't_cmp': t_cmp, 't_sel': t_sel, 't_win': t_win,
        'expand': jnp.asarray(expand, _BF16), 'ov': jnp.asarray(ov, _BF16), 'n_sel': n_sel, 'past_len': past_len,
    }


def _nsa_sample_attention(q, glog, side):
    n_seq, n_q, _ = q.shape
    rows = N_HEADS * n_q
    width = N_KV_HEADS * HEAD_DIM
    page_table = side['page_table']
    n_pages = page_table.shape[1]
    page_rows = side['cache2d'].shape[1]
    past_len = side['past_len']
    w_buf = side['win'].shape[1]
    n_steps = n_pages // PAGES_PER_STEP
    q_rows = jnp.transpose(q.reshape(n_seq, n_q, N_HEADS, HEAD_DIM), (0, 2, 1, 3)).reshape(n_seq, rows, HEAD_DIM)
    gl = jnp.transpose(glog.reshape(n_seq, n_q, N_HEADS, N_BRANCH), (0, 2, 1, 3)).reshape(n_seq, rows, N_BRANCH)
    gl = jnp.pad(gl, ((0, 0), (0, 0), (0, LANES - N_BRANCH)))
    per_seq = lambda *shape: pl.BlockSpec((None,) + shape, lambda b, j, pt: (b,) + (0,) * len(shape))
    const = lambda *shape: pl.BlockSpec(shape, lambda b, j, pt: (0,) * len(shape))
    page_spec = lambda k: pl.BlockSpec((None, page_rows, 2 * width),
                                       lambda b, j, pt: (pt[b, j * PAGES_PER_STEP + k], 0, 1))
    out = pl.pallas_call(
        functools.partial(_nsa_sample_kernel, n_q=n_q, past_len=past_len, n_sel=side['n_sel']),
        grid_spec=pltpu.PrefetchScalarGridSpec(
            num_scalar_prefetch=1,
            grid=(n_seq, n_steps),
            in_specs=[per_seq(rows, HEAD_DIM), per_seq(rows, LANES), per_seq(LANES, width), per_seq(LANES, width),
                      per_seq(w_buf, 2 * width), per_seq(n_q, 4 * width)]
            + [page_spec(k) for k in range(PAGES_PER_STEP)]
            + [const(rows, LANES), const(rows, past_len + LANES), const(rows, w_buf + LANES),
               const(LANES, past_len + LANES), const(LANES, LANES)],
            out_specs=per_seq(rows, HEAD_DIM),
            scratch_shapes=[pltpu.VMEM((rows, past_len + LANES), _F32), pltpu.VMEM((rows, 1), _F32),
                            pltpu.VMEM((rows, 1), _F32), pltpu.VMEM((rows, width), _F32),
                            pltpu.VMEM((rows, HEAD_DIM), _F32), pltpu.VMEM((rows, HEAD_DIM), _F32)],
        ),
        out_shape=jax.ShapeDtypeStruct((n_seq, rows, HEAD_DIM), _F32),
        compiler_params=pltpu.CompilerParams(dimension_semantics=("parallel", "arbitrary"),
                                             vmem_limit_bytes=VMEM_LIMIT),
        name="nsa_sample",
    )(page_table, q_rows, gl, side['kc'], side['vc'], side['win'], side['new'],
      *([side['cache2d']] * PAGES_PER_STEP), side['t_cmp'], side['t_sel'], side['t_win'], side['expand'], side['ov'])
    return jnp.transpose(out.reshape(n_seq, N_HEADS, n_q, HEAD_DIM), (0, 2, 1, 3)).reshape(n_seq, n_q, N_HEADS * HEAD_DIM)


def _nsa_sample_mixer(h, side, w_qg, w_o):
    qg = h @ w_qg
    mixed = _nsa_sample_attention(qg[..., :N_HEADS * HEAD_DIM], qg[..., N_HEADS * HEAD_DIM:], side)
    return mixed @ w_o


def _run_trunk(x, p, s0_re, s0_im, paged, win_buf, prm):
    bsz, t_len, _ = x.shape
    ssm_re, ssm_im = [], []
    side = None
    kv_rows_new = None
    win_state = None
    for i in range(DEPTH):
        h = _rmsnorm(x, prm['g_mix'][i])
        if i < N_A_LAYERS:
            mix, s_re, s_im = _s5_mixer(h, s0_re[i], s0_im[i], prm['ssm_a_re'][i], prm['ssm_a_im'][i],
                                        prm['ssm_log_dt'][i], prm['ssm_b_re'][i], prm['ssm_b_im'][i],
                                        prm['ssm_c_re'][i], prm['ssm_c_im'][i], prm['ssm_d'][i], prm['w_glu'][i])
            ssm_re.append(s_re)
            ssm_im.append(s_im)
        else:
            j = i - N_A_LAYERS
            mixer = _nsa_prompt_mixer if paged is None else _nsa_sample_mixer
            mix = mixer(h, side, prm['w_qg'][j], prm['w_o'][j])
        x = x + mix
        x = _moe_layer(x.reshape(-1, D_MODEL), prm['g_ffn'][i], prm['w_route_group'][i], prm['b_route_group'][i],
                       prm['w_route_expert'][i], prm['b_route_expert'][i], prm['w_exp_up'][i],
                       prm['w_exp_down'][i]).reshape(bsz, t_len, D_MODEL)
        gate = jax.nn.sigmoid(_rmsnorm(x, prm['g_ple'][i]) @ prm['w_ple_gate'][i])
        x = x + (p[i] @ prm['w_ple_proj'][i]) * gate
        if i == N_A_LAYERS - 1:
            kv = (_rmsnorm(x, prm['g_kv']) @ prm['w_kv']).reshape(
                bsz, t_len, KV_SLOTS_PAGED + KV_SLOTS_WIN, N_KV_HEADS, HEAD_DIM)
            kv_rows_new, win_new = kv[:, :, :KV_SLOTS_PAGED], kv[:, :, KV_SLOTS_PAGED:]
            if paged is None:
                win_state = win_new[:, -min(WINDOW, t_len):]
                k_cmp = _compress(kv[:, :, 0], prm['cmp_pos_k'], prm['cmp_w1_k'], prm['cmp_w2_k'])
                v_cmp = _compress(kv[:, :, 1], prm['cmp_pos_v'], prm['cmp_w1_v'], prm['cmp_w2_v'])
                side = _nsa_prompt_side(kv, k_cmp, v_cmp, prm['rel_bias'])
            else:
                w_buf = win_buf.shape[1]
                win_state = jnp.concatenate([win_buf, win_new], axis=1)[:, -w_buf:]
                side = _nsa_sample_side(paged[0], paged[1], win_buf, kv, prm['rel_bias'], prm)
    y = _rmsnorm(x, prm['g_final'])
    return y, kv_rows_new, win_state, jnp.stack(ssm_re), jnp.stack(ssm_im)


def kernel(x_prompt, x_sample, p_prompt, p_sample, cache_kv, cache_win, state_ssm_re, state_ssm_im, page_table,
           g_mix, g_ffn, g_ple, g_kv, g_final,
           ssm_a_re, ssm_a_im, ssm_log_dt, ssm_b_re, ssm_b_im, ssm_c_re, ssm_c_im, ssm_d, w_glu,
           w_kv, cmp_pos_k, cmp_pos_v, cmp_w1_k, cmp_w2_k, cmp_w1_v, cmp_w2_v, w_qg, w_o, rel_bias,
           w_route_group, b_route_group, w_route_expert, b_route_expert, w_exp_up, w_exp_down,
           w_ple_proj, w_ple_gate):
    prm = {
        'g_mix': g_mix, 'g_ffn': g_ffn, 'g_ple': g_ple, 'g_kv': g_kv, 'g_final': g_final,
        'ssm_a_re': ssm_a_re, 'ssm_a_im': ssm_a_im, 'ssm_log_dt': ssm_log_dt,
        'ssm_b_re': ssm_b_re, 'ssm_b_im': ssm_b_im, 'ssm_c_re': ssm_c_re, 'ssm_c_im': ssm_c_im,
        'ssm_d': ssm_d, 'w_glu': w_glu,
        'w_kv': w_kv, 'cmp_pos_k': cmp_pos_k, 'cmp_pos_v': cmp_pos_v,
        'cmp_w1_k': cmp_w1_k, 'cmp_w2_k': cmp_w2_k, 'cmp_w1_v': cmp_w1_v, 'cmp_w2_v': cmp_w2_v,
        'w_qg': w_qg, 'w_o': w_o, 'rel_bias': rel_bias,
        'w_route_group': w_route_group, 'b_route_group': b_route_group,
        'w_route_expert': w_route_expert, 'b_route_expert': b_route_expert,
        'w_exp_up': w_exp_up, 'w_exp_down': w_exp_down,
        'w_ple_proj': w_ple_proj, 'w_ple_gate': w_ple_gate,
    }
    zero_state = jnp.zeros((N_A_LAYERS, x_prompt.shape[0], SSM_GROUPS, SSM_STATE), _F32)
    y_prompt, kv_prompt, win_prompt, ssm_re_prompt, ssm_im_prompt = _run_trunk(
        x_prompt, p_prompt, zero_state, zero_state, None, None, prm)
    y_sample, kv_sample, win_sample, ssm_re_sample, ssm_im_sample = _run_trunk(
        x_sample, p_sample, state_ssm_re, state_ssm_im, (cache_kv, page_table), cache_win, prm)
    return (y_prompt, y_sample, kv_prompt, win_prompt, ssm_re_prompt, ssm_im_prompt,
            kv_sample, win_sample, ssm_re_sample, ssm_im_sample)
```

```python
import functools
import math

import jax
import jax.numpy as jnp
from jax import lax
from jax.experimental import pallas as pl
from jax.experimental.pallas import tpu as pltpu

D_MODEL = 1024
DEPTH = 4
N_A_LAYERS = DEPTH // 2
SSM_GROUP = 16
SSM_GROUPS = D_MODEL // SSM_GROUP
SSM_STATE = 64
N_HEADS = 16
HEAD_DIM = D_MODEL // N_HEADS
N_KV_HEADS = 4
Q_PER_KV = N_HEADS // N_KV_HEADS
CMP_LEN = 32
CMP_STRIDE = 16
SEL_BLOCK = 64
SEL_TOPK = 16
WINDOW = 512
N_BRANCH = 3
KV_SLOTS_PAGED = 4
KV_SLOTS_WIN = 2
Q_BLOCK = 64
N_BUCKETS = 32
MAX_DISTANCE = 128
N_EXPERT_GROUPS = 4
EXPERTS_PER_GROUP = 4
N_EXPERTS = N_EXPERT_GROUPS * EXPERTS_PER_GROUP
EXPERT_TOPK = 2
D_EXPERT = 256
RMS_EPS = 1e-6
NEG_INF = -1e30
FORCE = 1e4

LANES = 128
VMEM_LIMIT = 48 * 1024 * 1024

_F32 = jnp.float32
_BF16 = jnp.bfloat16


def _moe_kernel(x_ref, g_ref, wr_ref, br_ref, wup_ref, wdn_ref, o_ref, h_scr, comb_scr, acc_scr):
    e = pl.program_id(1)
    lane = lax.broadcasted_iota(jnp.int32, comb_scr.shape, 1)

    @pl.when(e == 0)
    def _route():
        x = x_ref[...]
        h = x * lax.rsqrt(jnp.mean(x * x, axis=-1, keepdims=True) + RMS_EPS) * g_ref[...]
        h_scr[...] = h.astype(_BF16)
        logits = jnp.dot(h, wr_ref[...], preferred_element_type=_F32,
                         precision=lax.Precision.HIGHEST) + br_ref[...]
        is_grp = (lane >= N_EXPERTS) & (lane < N_EXPERTS + N_EXPERT_GROUPS)
        lg = jnp.where(is_grp, logits, -jnp.inf)
        gmax = jnp.max(lg, axis=-1, keepdims=True)
        gi = jnp.min(jnp.where(lg == gmax, lane, LANES), axis=-1, keepdims=True) - N_EXPERTS
        gp = 1.0 / jnp.sum(jnp.where(is_grp, jnp.exp(lg - gmax), 0.0), axis=-1, keepdims=True)
        in_grp = (lane < N_EXPERTS) & ((lane // EXPERTS_PER_GROUP) == gi)
        le = jnp.where(in_grp, logits, -jnp.inf)
        m1 = jnp.max(le, axis=-1, keepdims=True)
        i1 = jnp.min(jnp.where(le == m1, lane, LANES), axis=-1, keepdims=True)
        le2 = jnp.where(lane == i1, -jnp.inf, le)
        m2 = jnp.max(le2, axis=-1, keepdims=True)
        i2 = jnp.min(jnp.where(le2 == m2, lane, LANES), axis=-1, keepdims=True)
        e2 = jnp.exp(m2 - m1)
        den = 1.0 + e2
        comb_scr[...] = jnp.where(lane == i1, gp / den, jnp.where(lane == i2, gp * e2 / den, 0.0))
        acc_scr[...] = x

    up = jnp.dot(h_scr[...], wup_ref[0], preferred_element_type=_F32)
    a = up[:, :D_EXPERT]
    b = up[:, D_EXPERT:]
    c = jnp.sum(jnp.where(lane == e, comb_scr[...], 0.0), axis=-1, keepdims=True)
    act = (a * jax.nn.sigmoid(a)) * b * c
    acc_scr[...] += jnp.dot(act.astype(_BF16), wdn_ref[0], preferred_element_type=_F32)

    @pl.when(e == N_EXPERTS - 1)
    def _store():
        o_ref[...] = acc_scr[...]


def _moe_layer(x2, g, w_rg, b_rg, w_re, b_re, w_up, w_dn):
    n_tok, d = x2.shape
    tm = min(n_tok, 1024)
    wr = jnp.zeros((d, LANES), _F32).at[:, :N_EXPERTS].set(w_re).at[:, N_EXPERTS:N_EXPERTS + N_EXPERT_GROUPS].set(w_rg)
    br = jnp.zeros((1, LANES), _F32).at[0, :N_EXPERTS].set(b_re).at[0, N_EXPERTS:N_EXPERTS + N_EXPERT_GROUPS].set(b_rg)
    return pl.pallas_call(
        _moe_kernel,
        grid=(n_tok // tm, N_EXPERTS),
        in_specs=[
            pl.BlockSpec((tm, d), lambda i, e: (i, 0)),
            pl.BlockSpec((1, d), lambda i, e: (0, 0)),
            pl.BlockSpec((d, LANES), lambda i, e: (0, 0)),
            pl.BlockSpec((1, LANES), lambda i, e: (0, 0)),
            pl.BlockSpec((1, d, 2 * D_EXPERT), lambda i, e: (e, 0, 0)),
            pl.BlockSpec((1, D_EXPERT, d), lambda i, e: (e, 0, 0)),
        ],
        out_specs=pl.BlockSpec((tm, d), lambda i, e: (i, 0)),
        out_shape=jax.ShapeDtypeStruct((n_tok, d), _F32),
        scratch_shapes=[
            pltpu.VMEM((tm, d), _BF16),
            pltpu.VMEM((tm, LANES), _F32),
            pltpu.VMEM((tm, d), _F32),
        ],
        compiler_params=pltpu.CompilerParams(
            dimension_semantics=("parallel", "arbitrary"), vmem_limit_bytes=VMEM_LIMIT),
        name="hmoe",
    )(x2, g.reshape(1, d), wr, br, w_up.astype(_BF16), w_dn.astype(_BF16))


def _rmsnorm(x, g):
    xf = x.astype(_F32)
    y = xf * lax.rsqrt(jnp.mean(xf * xf, axis=-1, keepdims=True) + RMS_EPS)
    return (y * g.astype(_F32)).astype(x.dtype)


def _cmul(ar, ai, br, bi):
    return ar * br - ai * bi, ar * bi + ai * br


def _s5_operators(a_re, a_im, log_dt, b_re, b_im, c_re, c_im, chunk):
    hp = lax.Precision.HIGHEST
    dt = jnp.exp(log_dt)[:, None]
    decay = jnp.exp(a_re * dt)
    ab_re, ab_im = decay * jnp.cos(a_im * dt), decay * jnp.sin(a_im * dt)
    den = a_re * a_re + a_im * a_im
    num_re = ab_re - 1.0
    q_re = (num_re * a_re + ab_im * a_im) / den
    q_im = (ab_im * a_re - num_re * a_im) / den
    bb_re, bb_im = _cmul(q_re[..., None], q_im[..., None], b_re, b_im)
    pw_re, pw_im = [jnp.ones_like(ab_re)], [jnp.zeros_like(ab_im)]
    for _ in range(chunk):
        nr, ni = _cmul(pw_re[-1], pw_im[-1], ab_re, ab_im)
        pw_re.append(nr)
        pw_im.append(ni)
    pw_re, pw_im = jnp.stack(pw_re), jnp.stack(pw_im)
    w_re, w_im = _cmul(pw_re[:chunk, :, :, None], pw_im[:chunk, :, :, None], bb_re[None], bb_im[None])
    k_lag = (jnp.einsum('gcp,jgpd->jgcd', c_re, w_re, precision=hp)
             - jnp.einsum('gcp,jgpd->jgcd', c_im, w_im, precision=hp))
    tt = jnp.arange(chunk)
    lag = tt[None, :] - tt[:, None]
    m = jnp.where((lag >= 0)[:, :, None, None, None], k_lag[jnp.clip(lag, 0, chunk - 1)], 0.0)
    m = jnp.transpose(m, (2, 0, 4, 1, 3)).reshape(SSM_GROUPS, chunk * SSM_GROUP, chunk * SSM_GROUP)
    h_re = jnp.transpose(w_re[::-1], (1, 0, 3, 2)).reshape(SSM_GROUPS, chunk * SSM_GROUP, SSM_STATE)
    h_im = jnp.transpose(w_im[::-1], (1, 0, 3, 2)).reshape(SSM_GROUPS, chunk * SSM_GROUP, SSM_STATE)
    ca_re, ca_im = _cmul(c_re[None], c_im[None], pw_re[1:, :, None, :], pw_im[1:, :, None, :])
    gm_re = jnp.transpose(ca_re, (1, 3, 0, 2)).reshape(SSM_GROUPS, SSM_STATE, chunk * SSM_GROUP)
    gm_im = -jnp.transpose(ca_im, (1, 3, 0, 2)).reshape(SSM_GROUPS, SSM_STATE, chunk * SSM_GROUP)
    return m, h_re, h_im, gm_re, gm_im, pw_re[chunk][:, None, :], pw_im[chunk][:, None, :]


def _s5_kernel(u_ref, m_ref, hre_ref, him_ref, gre_ref, gim_ref, alre_ref, alim_ref, d_ref, s0re_ref, s0im_ref,
               y_ref, sre_ref, sim_ref, ere, eim, sinre, sinim, *, n_chunks, rows, groups):
    hp = lax.Precision.HIGHEST
    for g in range(groups):
        u = u_ref[g]
        ere[g] = jnp.dot(u, hre_ref[g], preferred_element_type=_F32, precision=hp)
        eim[g] = jnp.dot(u, him_ref[g], preferred_element_type=_F32, precision=hp)

    def step(j, carry):
        row = pl.ds(pl.multiple_of(j * rows, rows), rows)
        out = []
        for g in range(groups):
            sr, si = carry[2 * g], carry[2 * g + 1]
            sinre[g, row, :] = sr
            sinim[g, row, :] = si
            ar, ai = alre_ref[g], alim_ref[g]
            out.append(ar * sr - ai * si + ere[g, row, :])
            out.append(ar * si + ai * sr + eim[g, row, :])
        return tuple(out)

    init = []
    for g in range(groups):
        init += [s0re_ref[g], s0im_ref[g]]
    fin = lax.fori_loop(0, n_chunks, step, tuple(init))
    for g in range(groups):
        sre_ref[g] = fin[2 * g]
        sim_ref[g] = fin[2 * g + 1]
        u = u_ref[g]
        y = jnp.dot(u.astype(_BF16), m_ref[g], preferred_element_type=_F32)
        y += jnp.dot(sinre[g].astype(_BF16), gre_ref[g], preferred_element_type=_F32)
        y += jnp.dot(sinim[g].astype(_BF16), gim_ref[g], preferred_element_type=_F32)
        y_ref[g] = y + d_ref[g] * u


def _s5_scan(u, s0_re, s0_im, a_re, a_im, log_dt, b_re, b_im, c_re, c_im, d_skip):
    bsz, t_len, _ = u.shape
    chunk = min(t_len, 16)
    n_chunks = t_len // chunk
    rows = -(-bsz // 8) * 8
    width = chunk * SSM_GROUP
    groups = 2
    m, h_re, h_im, gm_re, gm_im, al_re, al_im = _s5_operators(a_re, a_im, log_dt, b_re, b_im, c_re, c_im, chunk)
    uf = u.reshape(bsz, n_chunks, chunk, SSM_GROUPS, SSM_GROUP)
    uf = jnp.transpose(uf, (3, 1, 0, 2, 4))
    uf = jnp.pad(uf, ((0, 0), (0, 0), (0, rows - bsz), (0, 0), (0, 0))).reshape(SSM_GROUPS, n_chunks * rows, width)
    pad_state = lambda s: jnp.pad(jnp.transpose(s, (1, 0, 2)), ((0, 0), (0, rows - bsz), (0, 0)))
    d_tile = jnp.tile(d_skip, (1, chunk))[:, None, :]
    gspec = lambda *shape: pl.BlockSpec((groups,) + shape, lambda i: (i,) + (0,) * len(shape))
    y, s_re, s_im = pl.pallas_call(
        functools.partial(_s5_kernel, n_chunks=n_chunks, rows=rows, groups=groups),
        grid=(SSM_GROUPS // groups,),
        in_specs=[gspec(n_chunks * rows, width), gspec(width, width), gspec(width, SSM_STATE), gspec(width, SSM_STATE),
                  gspec(SSM_STATE, width), gspec(SSM_STATE, width), gspec(1, SSM_STATE), gspec(1, SSM_STATE),
                  gspec(1, width), gspec(rows, SSM_STATE), gspec(rows, SSM_STATE)],
        out_specs=[gspec(n_chunks * rows, width), gspec(rows, SSM_STATE), gspec(rows, SSM_STATE)],
        out_shape=[jax.ShapeDtypeStruct((SSM_GROUPS, n_chunks * rows, width), _F32),
                   jax.ShapeDtypeStruct((SSM_GROUPS, rows, SSM_STATE), _F32),
                   jax.ShapeDtypeStruct((SSM_GROUPS, rows, SSM_STATE), _F32)],
        scratch_shapes=[pltpu.VMEM((groups, n_chunks * rows, SSM_STATE), _F32) for _ in range(4)],
        compiler_params=pltpu.CompilerParams(dimension_semantics=("parallel",), vmem_limit_bytes=VMEM_LIMIT),
        name="s5_scan",
    )(uf, m.astype(_BF16), h_re, h_im, gm_re.astype(_BF16), gm_im.astype(_BF16), al_re, al_im, d_tile,
      pad_state(s0_re), pad_state(s0_im))
    y = y.reshape(SSM_GROUPS, n_chunks, rows, chunk, SSM_GROUP)[:, :, :bsz]
    y = jnp.transpose(y, (2, 1, 3, 0, 4)).reshape(bsz, t_len, D_MODEL)
    unpad = lambda s: jnp.transpose(s[:, :bsz], (1, 0, 2))
    return y, unpad(s_re), unpad(s_im)


def _s5_mixer(u, s0_re, s0_im, a_re, a_im, log_dt, b_re, b_im, c_re, c_im, d_skip, w_glu):
    y, s_re, s_im = _s5_scan(u, s0_re, s0_im, a_re, a_im, log_dt, b_re, b_im, c_re, c_im, d_skip)
    z = jax.nn.gelu(y)
    a, g = jnp.split(z @ w_glu, 2, axis=-1)
    return a * jax.nn.sigmoid(g), s_re, s_im


TQ = 128
TK = 128
NEAR_BUCKET_DIST = 113


def _bucket_table(dist):
    import numpy as np
    max_exact = N_BUCKETS // 2
    d = np.maximum(np.asarray(dist, np.int64), 0)
    df = np.maximum(d, 1).astype(np.float64)
    large = max_exact + (np.log(df / max_exact) / math.log(MAX_DISTANCE / max_exact)
                         * (N_BUCKETS - max_exact)).astype(np.int64)
    return np.where(d < max_exact, d, np.minimum(large, N_BUCKETS - 1)).astype(np.int32)


def _bias_by_distance(rel_bias, dist):
    import numpy as np
    onehot = _bucket_table(dist)[None] == np.arange(N_BUCKETS)[:, None, None]
    return jnp.einsum('kqc,kh->hqc', jnp.asarray(onehot, _BF16).astype(_F32), rel_bias,
                      precision=lax.Precision.HIGHEST)


def _bias_tiles(rel_bias):
    import numpy as np
    rel_bias = rel_bias - rel_bias[N_BUCKETS - 1]
    r = np.arange(TQ)[:, None]
    c = np.arange(2 * LANES)[None, :]
    d_cmp = np.where(c < 16, r - (CMP_LEN - 1) + CMP_STRIDE * (8 - c), 10 ** 6)
    g_cmp = _bias_by_distance(rel_bias, d_cmp).reshape(N_KV_HEADS, Q_PER_KV * TQ, 2 * LANES)
    k = np.arange(2 * TK)[None, :]
    d_near = TK + r - k
    near = _bias_by_distance(rel_bias, d_near) + jnp.asarray(np.where(d_near >= 0, 0.0, NEG_INF), _F32)
    near = near.reshape(N_KV_HEADS, Q_PER_KV * TQ, 2 * TK)
    return g_cmp, near


def _stack_heads(q):
    return jnp.concatenate([q[:, g * HEAD_DIM:(g + 1) * HEAD_DIM] for g in range(Q_PER_KV)], axis=0)


def _unstack_heads(o):
    return jnp.concatenate([o[g] for g in range(Q_PER_KV)], axis=1)


def _nsa_cmp_kernel(q_ref, kct_ref, vc_ref, g_ref, ovt_ref, oc_ref, selt_ref):
    i = pl.program_id(2)
    t0 = i * TQ
    n_cmp_pad = kct_ref.shape[-1]
    qs = _stack_heads(q_ref[...] * (HEAD_DIM ** -0.5)).astype(_BF16)
    s = jnp.dot(qs, kct_ref[...], preferred_element_type=_F32)
    shift = (i * (TQ // CMP_STRIDE) + n_cmp_pad - 8) % n_cmp_pad
    bias = pltpu.roll(g_ref[...], shift, 1)
    s3 = (s + bias).reshape(Q_PER_KV, TQ, n_cmp_pad)
    r = lax.broadcasted_iota(jnp.int32, (TQ, n_cmp_pad), 0)
    n = lax.broadcasted_iota(jnp.int32, (TQ, n_cmp_pad), 1)
    valid = (t0 + r - CMP_STRIDE * n - (CMP_LEN - 1)) >= 0
    sm = jnp.where(valid, s3, NEG_INF)
    m = jnp.max(sm, axis=-1, keepdims=True)
    p = jnp.where(valid, jnp.exp(sm - m), 0.0)
    l = jnp.sum(p, axis=-1, keepdims=True)
    pb = (p * jnp.where(l > 0.0, 1.0 / l, 0.0)).astype(_BF16)
    oc = jnp.dot(pb.reshape(Q_PER_KV * TQ, n_cmp_pad), vc_ref[...], preferred_element_type=_F32)
    oc_ref[...] = _unstack_heads(oc.reshape(Q_PER_KV, TQ, HEAD_DIM))
    imp = jnp.zeros((ovt_ref.shape[0], TQ), _F32)
    for g in range(Q_PER_KV):
        imp += lax.dot_general(ovt_ref[...], pb[g], (((1,), (1,)), ((), ())), preferred_element_type=_F32)
    n_sel = imp.shape[0]
    blk = lax.broadcasted_iota(jnp.int32, (n_sel, TQ), 0)
    t = t0 + lax.broadcasted_iota(jnp.int32, (n_sel, TQ), 1)
    cur = t // SEL_BLOCK
    forced = (blk == 0) | (blk == cur) | (blk == cur - 1)
    score = jnp.where(forced, FORCE, jnp.where(blk * SEL_BLOCK <= t, imp, -FORCE))
    sub = 8
    tiles = [score[v * sub:(v + 1) * sub] for v in range(n_sel // sub)]
    ranks = [jnp.zeros((sub, TQ), _F32) for _ in tiles]
    for sp in range(n_sel):
        row = jnp.broadcast_to(score[sp:sp + 1, :], (sub, TQ))
        for v, tile in enumerate(tiles):
            if v > sp // sub:
                beats = row >= tile
            elif v < sp // sub:
                beats = row > tile
            else:
                beats = (row > tile) | ((row == tile) & (blk[:sub] > sp % sub))
            ranks[v] += jnp.where(beats, 1.0, 0.0)
    selt_ref[...] = (jnp.concatenate(ranks, axis=0) < SEL_TOPK).astype(_F32)


def _flash_step(qs, kt, v, add, m_scr, acc_scr):
    s3 = _scores(qs, kt, add)
    _softmax_pv(lambda k: s3[..., k * LANES:(k + 1) * LANES], kt.shape[-1] // LANES, v, m_scr, acc_scr)


def _scores(qs, kt, add):
    s3 = jnp.dot(qs, kt, preferred_element_type=_F32).reshape(Q_PER_KV, TQ, kt.shape[-1])
    return s3 if add is None else s3 + add


def _softmax_pv(chunk, n_chunks, v, m_scr, acc_scr):
    m_old = m_scr[...]
    m_tile = functools.reduce(jnp.maximum, [chunk(k) for k in range(n_chunks)])
    m_new = jnp.maximum(m_old, jnp.max(m_tile, axis=-1, keepdims=True))
    alpha = jnp.exp(m_old - m_new)
    m_scr[...] = m_new
    pv = None
    for k in range(n_chunks):
        p = jnp.exp(chunk(k) - m_new).astype(_BF16).reshape(Q_PER_KV * TQ, LANES)
        d = jnp.dot(p, v[k * LANES:(k + 1) * LANES], preferred_element_type=_F32)
        pv = d if pv is None else pv + d
    acc_scr[...] = alpha * acc_scr[...] + pv.reshape(Q_PER_KV, TQ, 2 * HEAD_DIM)


FAR_TILES = 4


def _nsa_sel_win_kernel(q_ref, oc_ref, glog_ref, selt_ref, kst_ref, vs_ref, kwt_ref, vw_ref, near_ref,
                        exp_ref, gexp_ref, o_ref, seladd, m_s, acc_s, m_w, acc_w):
    i = pl.program_id(2)
    qs = _stack_heads(q_ref[...] * (HEAD_DIM ** -0.5)).astype(_BF16)
    sel = jnp.dot(selt_ref[...].T.astype(_BF16), exp_ref[...], preferred_element_type=_F32)
    seladd[...] = (sel - 1.0) * (-NEG_INF)
    for m_scr, acc_scr in ((m_s, acc_s), (m_w, acc_w)):
        m_scr[...] = jnp.full(m_scr.shape, NEG_INF, _F32)
        acc_scr[...] = jnp.zeros(acc_scr.shape, _F32)

    def keys(j, n_tiles):
        return pl.ds(pl.multiple_of(j * TK, TK), n_tiles * TK)

    def sel_step(j, n_tiles, bias):
        cols = keys(j, n_tiles)
        add = seladd[:, cols]
        if bias is not None:
            add = add + bias
        _flash_step(qs, kst_ref[:, cols], vs_ref[cols, :], add, m_s, acc_s)

    def win_step(j, n_tiles, add):
        cols = keys(j, n_tiles)
        _flash_step(qs, kwt_ref[:, cols], vw_ref[cols, :], add, m_w, acc_w)

    near = lambda lo: near_ref[:, lo * TK:].reshape(Q_PER_KV, TQ, (2 - lo) * TK)

    n_far = jnp.maximum(i - 1, 0)

    def far_many(j, carry):
        sel_step(j * FAR_TILES, FAR_TILES, None)
        return carry

    def far_one(j, carry):
        sel_step(j, 1, None)
        return carry

    lax.fori_loop(0, n_far // FAR_TILES, far_many, 0)
    lax.fori_loop((n_far // FAR_TILES) * FAR_TILES, n_far, far_one, 0)

    @pl.when(i >= 1)
    def _():
        sel_step(i - 1, 2, near(0))
        win_step(i - 1, 2, near(0))

    @pl.when(i == 0)
    def _():
        sel_step(0, 1, near(1))
        win_step(0, 1, near(1))

    n_lag = WINDOW // TK

    @pl.when(i >= n_lag - 1)
    def _():
        win_step(i - (n_lag - 1), n_lag - 2, None)

    for lag in range(2, n_lag - 1):
        @pl.when(i == lag)
        def _():
            win_step(0, lag - 1, None)

    @pl.when(i >= n_lag)
    def _():
        r = lax.broadcasted_iota(jnp.int32, (TQ, TK), 0)
        c = lax.broadcasted_iota(jnp.int32, (TQ, TK), 1)
        win_step(i - n_lag, 1, jnp.where(c >= r, 0.0, NEG_INF))

    def finish(acc_scr):
        acc = acc_scr[...]
        return _unstack_heads(acc[..., :HEAD_DIM] / acc[..., HEAD_DIM:])

    gates = jnp.dot(jax.nn.sigmoid(glog_ref[...]), gexp_ref[...], preferred_element_type=_F32,
                    precision=lax.Precision.HIGHEST)
    width = Q_PER_KV * HEAD_DIM
    o_ref[...] = (gates[:, :width] * oc_ref[...] + gates[:, width:2 * width] * finish(acc_s)
                  + gates[:, 2 * width:] * finish(acc_w))


def _nsa_prompt_side(kv, prm):
    import numpy as np
    bsz, t_len = kv.shape[:2]
    n_cmp = (t_len - CMP_LEN) // CMP_STRIDE + 1
    n_cmp_pad = t_len // CMP_STRIDE
    assert n_cmp == n_cmp_pad - 1 and n_cmp_pad % LANES == 0 and t_len % TQ == 0
    n_sel = t_len // SEL_BLOCK
    grouped = lambda slot: kv[:, :, slot].reshape(bsz, n_cmp_pad, CMP_STRIDE * N_KV_HEADS * HEAD_DIM)
    one_page = jnp.arange(bsz, dtype=jnp.int32)[:, None]
    k_cmp = _cmp_pages(grouped(0), one_page, prm['cmp_pos_k'], prm['cmp_w1_k'], prm['cmp_w2_k'])
    v_cmp = _cmp_pages(grouped(1), one_page, prm['cmp_pos_v'], prm['cmp_w1_v'], prm['cmp_w2_v'])
    kvb = kv.astype(_BF16)
    t_last = lambda a: jnp.transpose(a, (0, 2, 3, 1))
    t_rows = lambda a: jnp.transpose(a, (0, 2, 1, 3))
    pad_c = lambda a: a.reshape(bsz, n_cmp_pad, N_KV_HEADS, HEAD_DIM)
    rel_bias = prm['rel_bias']
    c_start = np.arange(n_cmp_pad)[None, :] * CMP_STRIDE
    s_start = np.arange(n_sel)[:, None] * SEL_BLOCK
    ovt = ((c_start < s_start + SEL_BLOCK) & (c_start + CMP_LEN > s_start) & (np.arange(n_cmp_pad)[None, :] < n_cmp))
    expand = np.arange(n_sel)[:, None] == (np.arange(t_len)[None, :] // SEL_BLOCK)
    lanes = np.arange(LANES)[:, None]
    cols = np.arange(N_BRANCH * Q_PER_KV * HEAD_DIM)[None, :]
    gexp = lanes == (cols // HEAD_DIM % Q_PER_KV) * N_BRANCH + cols // (Q_PER_KV * HEAD_DIM)
    g_cmp, near = _bias_tiles(rel_bias)
    with_ones = lambda a: jnp.concatenate([t_rows(a), jnp.ones((bsz, N_KV_HEADS, t_len, HEAD_DIM), _BF16)], axis=-1)
    return {
        'kct': t_last(pad_c(k_cmp)), 'vc': t_rows(pad_c(v_cmp)),
        'kst': t_last(kvb[:, :, 2]), 'vs': with_ones(kvb[:, :, 3]),
        'kwt': t_last(kvb[:, :, 4]), 'vw': with_ones(kvb[:, :, 5]),
        'ovt': jnp.asarray(ovt, _BF16), 'expand': jnp.asarray(expand, _BF16), 'gexp': jnp.asarray(gexp, _F32),
        'g_cmp': g_cmp, 'near': near,
    }


def _nsa_prompt_attention(q, glog, side):
    bsz, t_len, _ = q.shape
    n_t = t_len // TQ
    width = Q_PER_KV * HEAD_DIM
    n_cmp_pad = side['kct'].shape[-1]
    n_sel = side['ovt'].shape[0]
    grid = (bsz, N_KV_HEADS, n_t)
    params = pltpu.CompilerParams(dimension_semantics=("parallel", "parallel", "arbitrary"),
                                  vmem_limit_bytes=VMEM_LIMIT)
    q_spec = pl.BlockSpec((None, TQ, width), lambda b, h, i: (b, i, h))
    per_bh = lambda *shape: pl.BlockSpec((None, None) + shape, lambda b, h, i: (b, h, 0, 0))
    per_h = lambda *shape: pl.BlockSpec((None,) + shape, lambda b, h, i: (h,) + (0,) * len(shape))
    const = lambda *shape: pl.BlockSpec(shape, lambda b, h, i: (0,) * len(shape))
    selt_spec = pl.BlockSpec((None, None, n_sel, TQ), lambda b, h, i: (b, h, 0, i))
    o_c, selt = pl.pallas_call(
        _nsa_cmp_kernel,
        grid=grid,
        in_specs=[q_spec, per_bh(HEAD_DIM, n_cmp_pad), per_bh(n_cmp_pad, HEAD_DIM),
                  per_h(Q_PER_KV * TQ, n_cmp_pad), const(n_sel, n_cmp_pad)],
        out_specs=[q_spec, selt_spec],
        out_shape=[jax.ShapeDtypeStruct((bsz, t_len, N_HEADS * HEAD_DIM), _F32),
                   jax.ShapeDtypeStruct((bsz, N_KV_HEADS, n_sel, t_len), _F32)],
        compiler_params=params,
        name="nsa_cmp_select",
    )(q, side['kct'], side['vc'], side['g_cmp'], side['ovt'])
    gl = jnp.transpose(glog.reshape(bsz, t_len, N_KV_HEADS, Q_PER_KV * N_BRANCH), (0, 2, 1, 3))
    gl = jnp.pad(gl, ((0, 0), (0, 0), (0, 0), (0, LANES - Q_PER_KV * N_BRANCH)))
    stat = pltpu.VMEM((Q_PER_KV, TQ, 2 * HEAD_DIM), _F32)
    return pl.pallas_call(
        _nsa_sel_win_kernel,
        grid=grid,
        in_specs=[q_spec, q_spec, pl.BlockSpec((None, None, TQ, LANES), lambda b, h, i: (b, h, i, 0)), selt_spec,
                  per_bh(HEAD_DIM, t_len), per_bh(t_len, 2 * HEAD_DIM), per_bh(HEAD_DIM, t_len),
                  per_bh(t_len, 2 * HEAD_DIM), per_h(Q_PER_KV * TQ, 2 * TK), const(n_sel, t_len),
                  const(LANES, N_BRANCH * width)],
        out_specs=q_spec,
        out_shape=jax.ShapeDtypeStruct((bsz, t_len, N_HEADS * HEAD_DIM), _F32),
        scratch_shapes=[pltpu.VMEM((TQ, t_len), _F32), stat, stat, stat, stat],
        compiler_params=params,
        name="nsa_select_window",
    )(q, o_c, gl, selt, side['kst'], side['vs'], side['kwt'], side['vw'], side['near'],
      side['expand'], side['gexp'])


def _nsa_prompt_mixer(h, side, w_qg, w_o):
    qg = h @ w_qg
    mixed = _nsa_prompt_attention(qg[..., :N_HEADS * HEAD_DIM], qg[..., N_HEADS * HEAD_DIM:], side)
    return mixed @ w_o


PAGES_PER_STEP = 4
GROUPS_PER_PAGE_ROWS = CMP_STRIDE


def _dot_nt(a, b):
    return lax.dot_general(a, b, (((1,), (1,)), ((), ())), preferred_element_type=_F32)


def _cmp_pages_kernel(pt_ref, *refs, n_pages):
    del pt_ref
    pages = refs[:n_pages]
    posa_ref, posb_ref, w1a_ref, w1b_ref, w2_ref, o_ref = refs[n_pages:]
    x = jnp.concatenate([pg[...] for pg in pages], axis=0)
    z0 = jnp.dot((x + posa_ref[...]).astype(_BF16), w1a_ref[...], preferred_element_type=_F32)
    z1 = jnp.dot((x + posb_ref[...]).astype(_BF16), w1b_ref[...], preferred_element_type=_F32)
    pre = z0 + pltpu.roll(z1, z1.shape[0] - 1, 0)
    o_ref[...] = jnp.dot(jax.nn.gelu(pre).astype(_BF16), w2_ref[...], preferred_element_type=_F32).astype(o_ref.dtype)


def _cmp_pages(grouped, page_table, pos_emb, w1, w2):
    n_seq, n_pages = page_table.shape
    groups_per_page, flat = grouped.shape[1:]
    width = N_KV_HEADS * HEAD_DIM
    hidden = w1.shape[1]
    eye = jnp.eye(N_KV_HEADS, dtype=_F32)
    w1r = w1.reshape(2, CMP_STRIDE, HEAD_DIM, hidden)
    bd1 = jnp.einsum('hk,abdf->abhdkf', eye, w1r).reshape(2, flat, N_KV_HEADS * hidden).astype(_BF16)
    bd2 = jnp.einsum('hk,fd->hfkd', eye, w2).reshape(N_KV_HEADS * hidden, width).astype(_BF16)
    pos_t = jnp.broadcast_to(pos_emb.reshape(2, CMP_STRIDE, 1, HEAD_DIM),
                             (2, CMP_STRIDE, N_KV_HEADS, HEAD_DIM)).reshape(2, 1, flat)
    n_rows = n_pages * groups_per_page
    page_spec = lambda k: pl.BlockSpec((None, groups_per_page, flat), lambda b, pt: (pt[b, k], 0, 0))
    const = lambda *shape: pl.BlockSpec(shape, lambda b, pt: (0,) * len(shape))
    return pl.pallas_call(
        functools.partial(_cmp_pages_kernel, n_pages=n_pages),
        grid_spec=pltpu.PrefetchScalarGridSpec(
            num_scalar_prefetch=1,
            grid=(n_seq,),
            in_specs=[page_spec(k) for k in range(n_pages)] + [
                const(1, flat), const(1, flat), const(flat, N_KV_HEADS * hidden), const(flat, N_KV_HEADS * hidden),
                const(N_KV_HEADS * hidden, width)],
            out_specs=pl.BlockSpec((None, n_rows, width), lambda b, pt: (b, 0, 0)),
        ),
        out_shape=jax.ShapeDtypeStruct((n_seq, n_rows, width), _BF16),
        compiler_params=pltpu.CompilerParams(dimension_semantics=("parallel",), vmem_limit_bytes=VMEM_LIMIT),
        name="cmp_pages",
    )(page_table, *([grouped] * n_pages), pos_t[0], pos_t[1], bd1[0], bd1[1], bd2)


def _pick_own_kv_head(x, n_q):
    rows_per = Q_PER_KV * n_q
    return jnp.concatenate([x[h * rows_per:(h + 1) * rows_per, h * HEAD_DIM:(h + 1) * HEAD_DIM]
                            for h in range(N_KV_HEADS)], axis=0)


def _nsa_sample_kernel(pt_ref, q_ref, gl_ref, kc_ref, vc_ref, win_ref, new_ref, *rest, n_q, past_len, n_sel):
    del pt_ref
    pages = rest[:PAGES_PER_STEP]
    (tcmp_ref, tsel_ref, twin_ref, exp_ref, ov_ref, o_ref,
     addsel, m_scr, l_scr, acc_scr, oc_scr, ow_scr) = rest[PAGES_PER_STEP:]
    j = pl.program_id(1)
    rows = N_HEADS * n_q
    width = N_KV_HEADS * HEAD_DIM
    w_buf = win_ref.shape[0]
    step_keys = PAGES_PER_STEP * pages[0].shape[0]
    rr = lax.broadcasted_iota(jnp.int32, (rows, width), 0)
    cc = lax.broadcasted_iota(jnp.int32, (rows, width), 1)
    q4 = jnp.concatenate([q_ref[...] * (HEAD_DIM ** -0.5)] * N_KV_HEADS, axis=1)
    qbd = jnp.where(rr // (Q_PER_KV * n_q) == cc // HEAD_DIM, q4, 0.0).astype(_BF16)
    new_rows = jnp.concatenate([new_ref[...], jnp.zeros((LANES - n_q, new_ref.shape[1]), _F32)], axis=0).astype(_BF16)

    @pl.when(j == 0)
    def _first():
        s = _dot_nt(qbd, kc_ref[...]) + tcmp_ref[...]
        p = jnp.exp(s - jnp.max(s, axis=-1, keepdims=True))
        pb = (p / jnp.sum(p, axis=-1, keepdims=True)).astype(_BF16)
        oc_scr[...] = _pick_own_kv_head(jnp.dot(pb, vc_ref[...], preferred_element_type=_F32), n_q)
        pm = jnp.dot(pb, ov_ref[...], preferred_element_type=_F32)
        imp = jnp.concatenate(
            [sum(pm[(h * Q_PER_KV + g) * n_q:(h * Q_PER_KV + g + 1) * n_q] for g in range(Q_PER_KV))
             for h in range(N_KV_HEADS)], axis=0)
        blk = lax.broadcasted_iota(jnp.int32, imp.shape, 1)
        t = past_len + lax.broadcasted_iota(jnp.int32, imp.shape, 0) % n_q
        cur = t // SEL_BLOCK
        forced = (blk == 0) | (blk == cur) | (blk == cur - 1)
        score = jnp.where(forced, FORCE, jnp.where(blk * SEL_BLOCK <= t, imp, -FORCE))
        score = jnp.where(blk < n_sel, score, -jnp.inf)
        rank = jnp.zeros(imp.shape, jnp.int32)
        for sp in range(n_sel):
            col = score[:, sp:sp + 1]
            rank += ((col > score) | ((col == score) & (blk > sp))).astype(jnp.int32)
        sel = (rank < min(SEL_TOPK, n_sel)).astype(_BF16)
        sel_rows = jnp.concatenate([sel[h * n_q:(h + 1) * n_q] for h in range(N_KV_HEADS) for _ in range(Q_PER_KV)],
                                   axis=0)
        addsel[...] = (jnp.dot(sel_rows, exp_ref[...], preferred_element_type=_F32) - 1.0) * (-NEG_INF) + tsel_ref[...]
        kw = win_ref[:, :width].astype(_BF16)
        vw = win_ref[:, width:].astype(_BF16)
        s1 = _dot_nt(qbd, kw) + twin_ref[:, :w_buf]
        s2 = _dot_nt(qbd, new_rows[:, 2 * width:3 * width]) + twin_ref[:, w_buf:]
        m = jnp.maximum(jnp.max(s1, axis=-1, keepdims=True), jnp.max(s2, axis=-1, keepdims=True))
        p1 = jnp.exp(s1 - m)
        p2 = jnp.exp(s2 - m)
        l = jnp.sum(p1, axis=-1, keepdims=True) + jnp.sum(p2, axis=-1, keepdims=True)
        ow = (jnp.dot(p1.astype(_BF16), vw, preferred_element_type=_F32)
              + jnp.dot(p2.astype(_BF16), new_rows[:, 3 * width:], preferred_element_type=_F32))
        ow_scr[...] = _pick_own_kv_head(ow / l, n_q)
        m_scr[...] = jnp.full(m_scr.shape, NEG_INF, _F32)
        l_scr[...] = jnp.zeros(l_scr.shape, _F32)
        acc_scr[...] = jnp.zeros(acc_scr.shape, _F32)

    def flash(k, v, add):
        s = _dot_nt(qbd, k) + add
        m_old = m_scr[...]
        m_new = jnp.maximum(m_old, jnp.max(s, axis=-1, keepdims=True))
        alpha = jnp.exp(m_old - m_new)
        p = jnp.exp(s - m_new)
        m_scr[...] = m_new
        l_scr[...] = alpha * l_scr[...] + jnp.sum(p, axis=-1, keepdims=True)
        acc_scr[...] = alpha * acc_scr[...] + jnp.dot(p.astype(_BF16), v, preferred_element_type=_F32)

    k_cat = jnp.concatenate([pg[:, :width] for pg in pages], axis=0).astype(_BF16)
    v_cat = jnp.concatenate([pg[:, width:] for pg in pages], axis=0).astype(_BF16)
    flash(k_cat, v_cat, addsel[:, pl.ds(pl.multiple_of(j * step_keys, step_keys), step_keys)])

    @pl.when(j == pl.num_programs(1) - 1)
    def _last():
        flash(new_rows[:, :width], new_rows[:, width:2 * width], addsel[:, past_len:])
        o_s = _pick_own_kv_head(acc_scr[...] / l_scr[...], n_q)
        gates = jax.nn.sigmoid(gl_ref[...])
        o_ref[...] = gates[:, 0:1] * oc_scr[...] + gates[:, 1:2] * o_s + gates[:, 2:3] * ow_scr[...]


def _nsa_sample_side(cache_kv, page_table, cache_win, kv_new, rel_bias, prm):
    import numpy as np
    n_seq, n_pages = page_table.shape
    n_phys, page_rows = cache_kv.shape[:2]
    n_q = kv_new.shape[1]
    width = N_KV_HEADS * HEAD_DIM
    past_len = n_pages * page_rows
    w_buf = cache_win.shape[1]
    tk = past_len + n_q
    n_cmp = (tk - CMP_LEN) // CMP_STRIDE + 1
    n_cmp_pad = past_len // CMP_STRIDE
    n_sel = -(-tk // SEL_BLOCK)
    assert n_cmp == n_cmp_pad - 1 and n_cmp_pad == LANES and n_sel <= LANES and n_q <= LANES
    assert page_rows % SEL_BLOCK == 0 and n_pages % PAGES_PER_STEP == 0 and w_buf == min(WINDOW, past_len)
    cache2d = cache_kv.reshape(n_phys, page_rows, KV_SLOTS_PAGED * width)
    gpp = page_rows // CMP_STRIDE
    grouped = cache2d.reshape(n_phys, gpp, CMP_STRIDE, KV_SLOTS_PAGED, width)[:, :, :, :2]
    grouped = jnp.transpose(grouped, (3, 0, 1, 2, 4)).reshape(2, n_phys, gpp, CMP_STRIDE * width)
    kc = _cmp_pages(grouped[0], page_table, prm['cmp_pos_k'], prm['cmp_w1_k'], prm['cmp_w2_k'])
    vc = _cmp_pages(grouped[1], page_table, prm['cmp_pos_v'], prm['cmp_w1_v'], prm['cmp_w2_v'])
    rows = N_HEADS * n_q
    t = past_len + np.arange(n_q)[:, None]
    masked = lambda ok, d: jnp.where(jnp.asarray(ok), _bias_by_distance(rel_bias, d), NEG_INF).reshape(rows, -1)
    n = np.arange(n_cmp_pad)[None, :]
    d_c = t - (n * CMP_STRIDE + CMP_LEN - 1)
    t_cmp = masked((d_c >= 0) & (n < n_cmp), d_c)
    pos = np.arange(past_len + LANES)[None, :]
    t_sel = masked((t - pos >= 0) & (pos < tk), t - pos)
    l = np.arange(w_buf + LANES)[None, :]
    d_w = t - (past_len - w_buf + l)
    t_win = masked((d_w >= 0) & (d_w <= WINDOW) & (l < w_buf + n_q), d_w)
    expand = (np.arange(LANES)[:, None] == pos // SEL_BLOCK) & (np.arange(LANES)[:, None] < n_sel)
    c_start = np.arange(n_cmp_pad)[:, None] * CMP_STRIDE
    s_start = np.arange(LANES)[None, :] * SEL_BLOCK
    ov = ((c_start < s_start + SEL_BLOCK) & (c_start + CMP_LEN > s_start) & (np.arange(n_cmp_pad)[:, None] < n_cmp)
          & (np.arange(LANES)[None, :] < n_sel))
    return {
        'cache2d': cache2d, 'page_table': page_table, 'kc': kc, 'vc': vc,
        'win': cache_win.reshape(n_seq, w_buf, KV_SLOTS_WIN * width),
        'new': kv_new.reshape(n_seq, n_q, -1)[:, :, 2 * width:],
        't_cmp': t_cmp, 't_sel': t_sel, 't_win': t_win,
        'expand': jnp.asarray(expand, _BF16), 'ov': jnp.asarray(ov, _BF16), 'n_sel': n_sel, 'past_len': past_len,
    }


def _nsa_sample_attention(q, glog, side):
    n_seq, n_q, _ = q.shape
    rows = N_HEADS * n_q
    width = N_KV_HEADS * HEAD_DIM
    page_table = side['page_table']
    n_pages = page_table.shape[1]
    page_rows = side['cache2d'].shape[1]
    past_len = side['past_len']
    w_buf = side['win'].shape[1]
    n_steps = n_pages // PAGES_PER_STEP
    q_rows = jnp.transpose(q.reshape(n_seq, n_q, N_HEADS, HEAD_DIM), (0, 2, 1, 3)).reshape(n_seq, rows, HEAD_DIM)
    gl = jnp.transpose(glog.reshape(n_seq, n_q, N_HEADS, N_BRANCH), (0, 2, 1, 3)).reshape(n_seq, rows, N_BRANCH)
    gl = jnp.pad(gl, ((0, 0), (0, 0), (0, LANES - N_BRANCH)))
    per_seq = lambda *shape: pl.BlockSpec((None,) + shape, lambda b, j, pt: (b,) + (0,) * len(shape))
    const = lambda *shape: pl.BlockSpec(shape, lambda b, j, pt: (0,) * len(shape))
    page_spec = lambda k: pl.BlockSpec((None, page_rows, 2 * width),
                                       lambda b, j, pt: (pt[b, j * PAGES_PER_STEP + k], 0, 1))
    out = pl.pallas_call(
        functools.partial(_nsa_sample_kernel, n_q=n_q, past_len=past_len, n_sel=side['n_sel']),
        grid_spec=pltpu.PrefetchScalarGridSpec(
            num_scalar_prefetch=1,
            grid=(n_seq, n_steps),
            in_specs=[per_seq(rows, HEAD_DIM), per_seq(rows, LANES), per_seq(LANES, width), per_seq(LANES, width),
                      per_seq(w_buf, 2 * width), per_seq(n_q, 4 * width)]
            + [page_spec(k) for k in range(PAGES_PER_STEP)]
            + [const(rows, LANES), const(rows, past_len + LANES), const(rows, w_buf + LANES),
               const(LANES, past_len + LANES), const(LANES, LANES)],
            out_specs=per_seq(rows, HEAD_DIM),
            scratch_shapes=[pltpu.VMEM((rows, past_len + LANES), _F32), pltpu.VMEM((rows, 1), _F32),
                            pltpu.VMEM((rows, 1), _F32), pltpu.VMEM((rows, width), _F32),
                            pltpu.VMEM((rows, HEAD_DIM), _F32), pltpu.VMEM((rows, HEAD_DIM), _F32)],
        ),
        out_shape=jax.ShapeDtypeStruct((n_seq, rows, HEAD_DIM), _F32),
        compiler_params=pltpu.CompilerParams(dimension_semantics=("parallel", "arbitrary"),
                                             vmem_limit_bytes=VMEM_LIMIT),
        name="nsa_sample",
    )(page_table, q_rows, gl, side['kc'], side['vc'], side['win'], side['new'],
      *([side['cache2d']] * PAGES_PER_STEP), side['t_cmp'], side['t_sel'], side['t_win'], side['expand'], side['ov'])
    return jnp.transpose(out.reshape(n_seq, N_HEADS, n_q, HEAD_DIM), (0, 2, 1, 3)).reshape(n_seq, n_q, N_HEADS * HEAD_DIM)


def _nsa_sample_mixer(h, side, w_qg, w_o):
    qg = h @ w_qg
    mixed = _nsa_sample_attention(qg[..., :N_HEADS * HEAD_DIM], qg[..., N_HEADS * HEAD_DIM:], side)
    return mixed @ w_o


def _run_trunk(x, p, s0_re, s0_im, paged, win_buf, prm):
    bsz, t_len, _ = x.shape
    ssm_re, ssm_im = [], []
    side = None
    kv_rows_new = None
    win_state = None
    for i in range(DEPTH):
        h = _rmsnorm(x, prm['g_mix'][i])
        if i < N_A_LAYERS:
            mix, s_re, s_im = _s5_mixer(h, s0_re[i], s0_im[i], prm['ssm_a_re'][i], prm['ssm_a_im'][i],
                                        prm['ssm_log_dt'][i], prm['ssm_b_re'][i], prm['ssm_b_im'][i],
                                        prm['ssm_c_re'][i], prm['ssm_c_im'][i], prm['ssm_d'][i], prm['w_glu'][i])
            ssm_re.append(s_re)
            ssm_im.append(s_im)
        else:
            j = i - N_A_LAYERS
            mixer = _nsa_prompt_mixer if paged is None else _nsa_sample_mixer
            mix = mixer(h, side, prm['w_qg'][j], prm['w_o'][j])
        x = x + mix
        x = _moe_layer(x.reshape(-1, D_MODEL), prm['g_ffn'][i], prm['w_route_group'][i], prm['b_route_group'][i],
                       prm['w_route_expert'][i], prm['b_route_expert'][i], prm['w_exp_up'][i],
                       prm['w_exp_down'][i]).reshape(bsz, t_len, D_MODEL)
        gate = jax.nn.sigmoid(_rmsnorm(x, prm['g_ple'][i]) @ prm['w_ple_gate'][i])
        x = x + (p[i] @ prm['w_ple_proj'][i]) * gate
        if i == N_A_LAYERS - 1:
            kv = (_rmsnorm(x, prm['g_kv']) @ prm['w_kv']).reshape(
                bsz, t_len, KV_SLOTS_PAGED + KV_SLOTS_WIN, N_KV_HEADS, HEAD_DIM)
            kv_rows_new, win_new = kv[:, :, :KV_SLOTS_PAGED], kv[:, :, KV_SLOTS_PAGED:]
            if paged is None:
                win_state = win_new[:, -min(WINDOW, t_len):]
                side = _nsa_prompt_side(kv, prm)
            else:
                w_buf = win_buf.shape[1]
                win_state = jnp.concatenate([win_buf, win_new], axis=1)[:, -w_buf:]
                side = _nsa_sample_side(paged[0], paged[1], win_buf, kv, prm['rel_bias'], prm)
    y = _rmsnorm(x, prm['g_final'])
    return y, kv_rows_new, win_state, jnp.stack(ssm_re), jnp.stack(ssm_im)


def kernel(x_prompt, x_sample, p_prompt, p_sample, cache_kv, cache_win, state_ssm_re, state_ssm_im, page_table,
           g_mix, g_ffn, g_ple, g_kv, g_final,
           ssm_a_re, ssm_a_im, ssm_log_dt, ssm_b_re, ssm_b_im, ssm_c_re, ssm_c_im, ssm_d, w_glu,
           w_kv, cmp_pos_k, cmp_pos_v, cmp_w1_k, cmp_w2_k, cmp_w1_v, cmp_w2_v, w_qg, w_o, rel_bias,
           w_route_group, b_route_group, w_route_expert, b_route_expert, w_exp_up, w_exp_down,
           w_ple_proj, w_ple_gate):
    prm = {
        'g_mix': g_mix, 'g_ffn': g_ffn, 'g_ple': g_ple, 'g_kv': g_kv, 'g_final': g_final,
        'ssm_a_re': ssm_a_re, 'ssm_a_im': ssm_a_im, 'ssm_log_dt': ssm_log_dt,
        'ssm_b_re': ssm_b_re, 'ssm_b_im': ssm_b_im, 'ssm_c_re': ssm_c_re, 'ssm_c_im': ssm_c_im,
        'ssm_d': ssm_d, 'w_glu': w_glu,
        'w_kv': w_kv, 'cmp_pos_k': cmp_pos_k, 'cmp_pos_v': cmp_pos_v,
        'cmp_w1_k': cmp_w1_k, 'cmp_w2_k': cmp_w2_k, 'cmp_w1_v': cmp_w1_v, 'cmp_w2_v': cmp_w2_v,
        'w_qg': w_qg, 'w_o': w_o, 'rel_bias': rel_bias,
        'w_route_group': w_route_group, 'b_route_group': b_route_group,
        'w_route_expert': w_route_expert, 'b_route_expert': b_route_expert,
        'w_exp_up': w_exp_up, 'w_exp_down': w_exp_down,
        'w_ple_proj': w_ple_proj, 'w_ple_gate': w_ple_gate,
    }
    zero_state = jnp.zeros((N_A_LAYERS, x_prompt.shape[0], SSM_GROUPS, SSM_STATE), _F32)
    y_prompt, kv_prompt, win_prompt, ssm_re_prompt, ssm_im_prompt = _run_trunk(
        x_prompt, p_prompt, zero_state, zero_state, None, None, prm)
    y_sample, kv_sample, win_sample, ssm_re_sample, ssm_im_sample = _run_trunk(
        x_sample, p_sample, state_ssm_re, state_ssm_im, (cache_kv, page_table), cache_win, prm)
    return (y_prompt, y_sample, kv_prompt, win_prompt, ssm_re_prompt, ssm_im_prompt,
            kv_sample, win_sample, ssm_re_sample, ssm_im_sample)
```

```python
import functools
import math

import jax
import jax.numpy as jnp
from jax import lax
from jax.experimental import pallas as pl
from jax.experimental.pallas import tpu as pltpu

D_MODEL = 1024
DEPTH = 4
N_A_LAYERS = DEPTH // 2
SSM_GROUP = 16
SSM_GROUPS = D_MODEL // SSM_GROUP
SSM_STATE = 64
N_HEADS = 16
HEAD_DIM = D_MODEL // N_HEADS
N_KV_HEADS = 4
Q_PER_KV = N_HEADS // N_KV_HEADS
CMP_LEN = 32
CMP_STRIDE = 16
SEL_BLOCK = 64
SEL_TOPK = 16
WINDOW = 512
N_BRANCH = 3
KV_SLOTS_PAGED = 4
KV_SLOTS_WIN = 2
Q_BLOCK = 64
N_BUCKETS = 32
MAX_DISTANCE = 128
N_EXPERT_GROUPS = 4
EXPERTS_PER_GROUP = 4
N_EXPERTS = N_EXPERT_GROUPS * EXPERTS_PER_GROUP
EXPERT_TOPK = 2
D_EXPERT = 256
RMS_EPS = 1e-6
NEG_INF = -1e30
FORCE = 1e4

LANES = 128
VMEM_LIMIT = 48 * 1024 * 1024

_F32 = jnp.float32
_BF16 = jnp.bfloat16


def _moe_kernel(x_ref, g_ref, wr_ref, br_ref, wup_ref, wdn_ref, o_ref, h_scr, comb_scr, acc_scr):
    e = pl.program_id(1)
    lane = lax.broadcasted_iota(jnp.int32, comb_scr.shape, 1)

    @pl.when(e == 0)
    def _route():
        x = x_ref[...]
        h = x * lax.rsqrt(jnp.mean(x * x, axis=-1, keepdims=True) + RMS_EPS) * g_ref[...]
        h_scr[...] = h.astype(_BF16)
        logits = jnp.dot(h, wr_ref[...], preferred_element_type=_F32,
                         precision=lax.Precision.HIGHEST) + br_ref[...]
        is_grp = (lane >= N_EXPERTS) & (lane < N_EXPERTS + N_EXPERT_GROUPS)
        lg = jnp.where(is_grp, logits, -jnp.inf)
        gmax = jnp.max(lg, axis=-1, keepdims=True)
        gi = jnp.min(jnp.where(lg == gmax, lane, LANES), axis=-1, keepdims=True) - N_EXPERTS
        gp = 1.0 / jnp.sum(jnp.where(is_grp, jnp.exp(lg - gmax), 0.0), axis=-1, keepdims=True)
        in_grp = (lane < N_EXPERTS) & ((lane // EXPERTS_PER_GROUP) == gi)
        le = jnp.where(in_grp, logits, -jnp.inf)
        m1 = jnp.max(le, axis=-1, keepdims=True)
        i1 = jnp.min(jnp.where(le == m1, lane, LANES), axis=-1, keepdims=True)
        le2 = jnp.where(lane == i1, -jnp.inf, le)
        m2 = jnp.max(le2, axis=-1, keepdims=True)
        i2 = jnp.min(jnp.where(le2 == m2, lane, LANES), axis=-1, keepdims=True)
        e2 = jnp.exp(m2 - m1)
        den = 1.0 + e2
        comb_scr[...] = jnp.where(lane == i1, gp / den, jnp.where(lane == i2, gp * e2 / den, 0.0))
        acc_scr[...] = x

    up = jnp.dot(h_scr[...], wup_ref[0], preferred_element_type=_F32)
    a = up[:, :D_EXPERT]
    b = up[:, D_EXPERT:]
    c = jnp.sum(jnp.where(lane == e, comb_scr[...], 0.0), axis=-1, keepdims=True)
    act = (a * jax.nn.sigmoid(a)) * b * c
    acc_scr[...] += jnp.dot(act.astype(_BF16), wdn_ref[0], preferred_element_type=_F32)

    @pl.when(e == N_EXPERTS - 1)
    def _store():
        o_ref[...] = acc_scr[...]


def _moe_layer(x2, g, w_rg, b_rg, w_re, b_re, w_up, w_dn):
    n_tok, d = x2.shape
    tm = min(n_tok, 1024)
    wr = jnp.zeros((d, LANES), _F32).at[:, :N_EXPERTS].set(w_re).at[:, N_EXPERTS:N_EXPERTS + N_EXPERT_GROUPS].set(w_rg)
    br = jnp.zeros((1, LANES), _F32).at[0, :N_EXPERTS].set(b_re).at[0, N_EXPERTS:N_EXPERTS + N_EXPERT_GROUPS].set(b_rg)
    return pl.pallas_call(
        _moe_kernel,
        grid=(n_tok // tm, N_EXPERTS),
        in_specs=[
            pl.BlockSpec((tm, d), lambda i, e: (i, 0)),
            pl.BlockSpec((1, d), lambda i, e: (0, 0)),
            pl.BlockSpec((d, LANES), lambda i, e: (0, 0)),
            pl.BlockSpec((1, LANES), lambda i, e: (0, 0)),
            pl.BlockSpec((1, d, 2 * D_EXPERT), lambda i, e: (e, 0, 0)),
            pl.BlockSpec((1, D_EXPERT, d), lambda i, e: (e, 0, 0)),
        ],
        out_specs=pl.BlockSpec((tm, d), lambda i, e: (i, 0)),
        out_shape=jax.ShapeDtypeStruct((n_tok, d), _F32),
        scratch_shapes=[
            pltpu.VMEM((tm, d), _BF16),
            pltpu.VMEM((tm, LANES), _F32),
            pltpu.VMEM((tm, d), _F32),
        ],
        compiler_params=pltpu.CompilerParams(
            dimension_semantics=("parallel", "arbitrary"), vmem_limit_bytes=VMEM_LIMIT),
        name="hmoe",
    )(x2, g.reshape(1, d), wr, br, w_up.astype(_BF16), w_dn.astype(_BF16))


def _rmsnorm(x, g):
    xf = x.astype(_F32)
    y = xf * lax.rsqrt(jnp.mean(xf * xf, axis=-1, keepdims=True) + RMS_EPS)
    return (y * g.astype(_F32)).astype(x.dtype)


def _cmul(ar, ai, br, bi):
    return ar * br - ai * bi, ar * bi + ai * br


def _s5_operators(a_re, a_im, log_dt, b_re, b_im, c_re, c_im, chunk):
    hp = lax.Precision.HIGHEST
    dt = jnp.exp(log_dt)[:, None]
    decay = jnp.exp(a_re * dt)
    ab_re, ab_im = decay * jnp.cos(a_im * dt), decay * jnp.sin(a_im * dt)
    den = a_re * a_re + a_im * a_im
    num_re = ab_re - 1.0
    q_re = (num_re * a_re + ab_im * a_im) / den
    q_im = (ab_im * a_re - num_re * a_im) / den
    bb_re, bb_im = _cmul(q_re[..., None], q_im[..., None], b_re, b_im)
    pw_re, pw_im = [jnp.ones_like(ab_re)], [jnp.zeros_like(ab_im)]
    for _ in range(chunk):
        nr, ni = _cmul(pw_re[-1], pw_im[-1], ab_re, ab_im)
        pw_re.append(nr)
        pw_im.append(ni)
    pw_re, pw_im = jnp.stack(pw_re), jnp.stack(pw_im)
    w_re, w_im = _cmul(pw_re[:chunk, :, :, None], pw_im[:chunk, :, :, None], bb_re[None], bb_im[None])
    k_lag = (jnp.einsum('gcp,jgpd->jgcd', c_re, w_re, precision=hp)
             - jnp.einsum('gcp,jgpd->jgcd', c_im, w_im, precision=hp))
    tt = jnp.arange(chunk)
    lag = tt[None, :] - tt[:, None]
    m = jnp.where((lag >= 0)[:, :, None, None, None], k_lag[jnp.clip(lag, 0, chunk - 1)], 0.0)
    m = jnp.transpose(m, (2, 0, 4, 1, 3)).reshape(SSM_GROUPS, chunk * SSM_GROUP, chunk * SSM_GROUP)
    h_re = jnp.transpose(w_re[::-1], (1, 0, 3, 2)).reshape(SSM_GROUPS, chunk * SSM_GROUP, SSM_STATE)
    h_im = jnp.transpose(w_im[::-1], (1, 0, 3, 2)).reshape(SSM_GROUPS, chunk * SSM_GROUP, SSM_STATE)
    ca_re, ca_im = _cmul(c_re[None], c_im[None], pw_re[1:, :, None, :], pw_im[1:, :, None, :])
    gm_re = jnp.transpose(ca_re, (1, 3, 0, 2)).reshape(SSM_GROUPS, SSM_STATE, chunk * SSM_GROUP)
    gm_im = -jnp.transpose(ca_im, (1, 3, 0, 2)).reshape(SSM_GROUPS, SSM_STATE, chunk * SSM_GROUP)
    return m, h_re, h_im, gm_re, gm_im, pw_re[chunk][:, None, :], pw_im[chunk][:, None, :]


def _s5_kernel(u_ref, m_ref, hre_ref, him_ref, gre_ref, gim_ref, alre_ref, alim_ref, d_ref, s0re_ref, s0im_ref,
               y_ref, sre_ref, sim_ref, ere, eim, sinre, sinim, *, n_chunks, rows, groups):
    hp = lax.Precision.HIGHEST
    for g in range(groups):
        u = u_ref[g]
        ere[g] = jnp.dot(u, hre_ref[g], preferred_element_type=_F32, precision=hp)
        eim[g] = jnp.dot(u, him_ref[g], preferred_element_type=_F32, precision=hp)

    def step(j, carry):
        row = pl.ds(pl.multiple_of(j * rows, rows), rows)
        out = []
        for g in range(groups):
            sr, si = carry[2 * g], carry[2 * g + 1]
            sinre[g, row, :] = sr
            sinim[g, row, :] = si
            ar, ai = alre_ref[g], alim_ref[g]
            out.append(ar * sr - ai * si + ere[g, row, :])
            out.append(ar * si + ai * sr + eim[g, row, :])
        return tuple(out)

    init = []
    for g in range(groups):
        init += [s0re_ref[g], s0im_ref[g]]
    fin = lax.fori_loop(0, n_chunks, step, tuple(init))
    for g in range(groups):
        sre_ref[g] = fin[2 * g]
        sim_ref[g] = fin[2 * g + 1]
        u = u_ref[g]
        y = jnp.dot(u.astype(_BF16), m_ref[g], preferred_element_type=_F32)
        y += jnp.dot(sinre[g].astype(_BF16), gre_ref[g], preferred_element_type=_F32)
        y += jnp.dot(sinim[g].astype(_BF16), gim_ref[g], preferred_element_type=_F32)
        y_ref[g] = y + d_ref[g] * u


def _s5_scan(u, s0_re, s0_im, a_re, a_im, log_dt, b_re, b_im, c_re, c_im, d_skip):
    bsz, t_len, _ = u.shape
    chunk = min(t_len, 16)
    n_chunks = t_len // chunk
    rows = -(-bsz // 8) * 8
    width = chunk * SSM_GROUP
    groups = 2
    m, h_re, h_im, gm_re, gm_im, al_re, al_im = _s5_operators(a_re, a_im, log_dt, b_re, b_im, c_re, c_im, chunk)
    uf = u.reshape(bsz, n_chunks, chunk, SSM_GROUPS, SSM_GROUP)
    uf = jnp.transpose(uf, (3, 1, 0, 2, 4))
    uf = jnp.pad(uf, ((0, 0), (0, 0), (0, rows - bsz), (0, 0), (0, 0))).reshape(SSM_GROUPS, n_chunks * rows, width)
    pad_state = lambda s: jnp.pad(jnp.transpose(s, (1, 0, 2)), ((0, 0), (0, rows - bsz), (0, 0)))
    d_tile = jnp.tile(d_skip, (1, chunk))[:, None, :]
    gspec = lambda *shape: pl.BlockSpec((groups,) + shape, lambda i: (i,) + (0,) * len(shape))
    y, s_re, s_im = pl.pallas_call(
        functools.partial(_s5_kernel, n_chunks=n_chunks, rows=rows, groups=groups),
        grid=(SSM_GROUPS // groups,),
        in_specs=[gspec(n_chunks * rows, width), gspec(width, width), gspec(width, SSM_STATE), gspec(width, SSM_STATE),
                  gspec(SSM_STATE, width), gspec(SSM_STATE, width), gspec(1, SSM_STATE), gspec(1, SSM_STATE),
                  gspec(1, width), gspec(rows, SSM_STATE), gspec(rows, SSM_STATE)],
        out_specs=[gspec(n_chunks * rows, width), gspec(rows, SSM_STATE), gspec(rows, SSM_STATE)],
        out_shape=[jax.ShapeDtypeStruct((SSM_GROUPS, n_chunks * rows, width), _F32),
                   jax.ShapeDtypeStruct((SSM_GROUPS, rows, SSM_STATE), _F32),
                   jax.ShapeDtypeStruct((SSM_GROUPS, rows, SSM_STATE), _F32)],
        scratch_shapes=[pltpu.VMEM((groups, n_chunks * rows, SSM_STATE), _F32) for _ in range(4)],
        compiler_params=pltpu.CompilerParams(dimension_semantics=("parallel",), vmem_limit_bytes=VMEM_LIMIT),
        name="s5_scan",
    )(uf, m.astype(_BF16), h_re, h_im, gm_re.astype(_BF16), gm_im.astype(_BF16), al_re, al_im, d_tile,
      pad_state(s0_re), pad_state(s0_im))
    y = y.reshape(SSM_GROUPS, n_chunks, rows, chunk, SSM_GROUP)[:, :, :bsz]
    y = jnp.transpose(y, (2, 1, 3, 0, 4)).reshape(bsz, t_len, D_MODEL)
    unpad = lambda s: jnp.transpose(s[:, :bsz], (1, 0, 2))
    return y, unpad(s_re), unpad(s_im)


def _s5_mixer(u, s0_re, s0_im, a_re, a_im, log_dt, b_re, b_im, c_re, c_im, d_skip, w_glu):
    y, s_re, s_im = _s5_scan(u, s0_re, s0_im, a_re, a_im, log_dt, b_re, b_im, c_re, c_im, d_skip)
    z = jax.nn.gelu(y)
    a, g = jnp.split(z @ w_glu, 2, axis=-1)
    return a * jax.nn.sigmoid(g), s_re, s_im


TQ = 128
TK = 128
NEAR_BUCKET_DIST = 113


def _bucket_table(dist):
    import numpy as np
    max_exact = N_BUCKETS // 2
    d = np.maximum(np.asarray(dist, np.int64), 0)
    df = np.maximum(d, 1).astype(np.float64)
    large = max_exact + (np.log(df / max_exact) / math.log(MAX_DISTANCE / max_exact)
                         * (N_BUCKETS - max_exact)).astype(np.int64)
    return np.where(d < max_exact, d, np.minimum(large, N_BUCKETS - 1)).astype(np.int32)


def _bias_by_distance(rel_bias, dist):
    import numpy as np
    onehot = _bucket_table(dist)[None] == np.arange(N_BUCKETS)[:, None, None]
    return jnp.einsum('kqc,kh->hqc', jnp.asarray(onehot, _BF16).astype(_F32), rel_bias,
                      precision=lax.Precision.HIGHEST)


def _bias_tiles(rel_bias):
    import numpy as np
    rel_bias = rel_bias - rel_bias[N_BUCKETS - 1]
    r = np.arange(TQ)[:, None]
    c = np.arange(2 * LANES)[None, :]
    d_cmp = np.where(c < 16, r - (CMP_LEN - 1) + CMP_STRIDE * (8 - c), 10 ** 6)
    g_cmp = _bias_by_distance(rel_bias, d_cmp).reshape(N_KV_HEADS, Q_PER_KV * TQ, 2 * LANES)
    k = np.arange(2 * TK)[None, :]
    d_near = TK + r - k
    near = _bias_by_distance(rel_bias, d_near) + jnp.asarray(np.where(d_near >= 0, 0.0, NEG_INF), _F32)
    near = near.reshape(N_KV_HEADS, Q_PER_KV * TQ, 2 * TK)
    return g_cmp, near


def _stack_heads(q):
    return jnp.concatenate([q[:, g * HEAD_DIM:(g + 1) * HEAD_DIM] for g in range(Q_PER_KV)], axis=0)


def _unstack_heads(o):
    return jnp.concatenate([o[g] for g in range(Q_PER_KV)], axis=1)


def _nsa_cmp_kernel(q_ref, kct_ref, vc_ref, g_ref, ovt_ref, oc_ref, selt_ref):
    i = pl.program_id(2)
    t0 = i * TQ
    n_cmp_pad = kct_ref.shape[-1]
    qs = _stack_heads(q_ref[...] * (HEAD_DIM ** -0.5)).astype(_BF16)
    s = jnp.dot(qs, kct_ref[...], preferred_element_type=_F32)
    shift = (i * (TQ // CMP_STRIDE) + n_cmp_pad - 8) % n_cmp_pad
    bias = pltpu.roll(g_ref[...], shift, 1)
    s3 = (s + bias).reshape(Q_PER_KV, TQ, n_cmp_pad)
    r = lax.broadcasted_iota(jnp.int32, (TQ, n_cmp_pad), 0)
    n = lax.broadcasted_iota(jnp.int32, (TQ, n_cmp_pad), 1)
    valid = (t0 + r - CMP_STRIDE * n - (CMP_LEN - 1)) >= 0
    sm = jnp.where(valid, s3, NEG_INF)
    m = jnp.max(sm, axis=-1, keepdims=True)
    p = jnp.where(valid, jnp.exp(sm - m), 0.0)
    l = jnp.sum(p, axis=-1, keepdims=True)
    pb = (p * jnp.where(l > 0.0, 1.0 / l, 0.0)).astype(_BF16)
    oc = jnp.dot(pb.reshape(Q_PER_KV * TQ, n_cmp_pad), vc_ref[...], preferred_element_type=_F32)
    oc_ref[...] = _unstack_heads(oc.reshape(Q_PER_KV, TQ, HEAD_DIM))
    imp = jnp.zeros((ovt_ref.shape[0], TQ), _F32)
    for g in range(Q_PER_KV):
        imp += lax.dot_general(ovt_ref[...], pb[g], (((1,), (1,)), ((), ())), preferred_element_type=_F32)
    n_sel = imp.shape[0]
    blk = lax.broadcasted_iota(jnp.int32, (n_sel, TQ), 0)
    t = t0 + lax.broadcasted_iota(jnp.int32, (n_sel, TQ), 1)
    cur = t // SEL_BLOCK
    forced = (blk == 0) | (blk == cur) | (blk == cur - 1)
    score = jnp.where(forced, FORCE, jnp.where(blk * SEL_BLOCK <= t, imp, -FORCE))
    sub = 8
    tiles = [score[v * sub:(v + 1) * sub] for v in range(n_sel // sub)]
    ranks = [jnp.zeros((sub, TQ), _F32) for _ in tiles]
    for sp in range(n_sel):
        row = jnp.broadcast_to(score[sp:sp + 1, :], (sub, TQ))
        for v, tile in enumerate(tiles):
            if v > sp // sub:
                beats = row >= tile
            elif v < sp // sub:
                beats = row > tile
            else:
                beats = (row > tile) | ((row == tile) & (blk[:sub] > sp % sub))
            ranks[v] += jnp.where(beats, 1.0, 0.0)
    selt_ref[...] = (jnp.concatenate(ranks, axis=0) < SEL_TOPK).astype(_F32)


def _flash_step(qs, kt, v, add, m_scr, acc_scr):
    s3 = _scores(qs, kt, add)
    _softmax_pv(lambda k: s3[..., k * LANES:(k + 1) * LANES], kt.shape[-1] // LANES, v, m_scr, acc_scr)


def _scores(qs, kt, add):
    s3 = jnp.dot(qs, kt, preferred_element_type=_F32).reshape(Q_PER_KV, TQ, kt.shape[-1])
    return s3 if add is None else s3 + add


def _softmax_pv(chunk, n_chunks, v, m_scr, acc_scr):
    m_old = m_scr[...]
    m_tile = functools.reduce(jnp.maximum, [chunk(k) for k in range(n_chunks)])
    m_new = jnp.maximum(m_old, jnp.max(m_tile, axis=-1, keepdims=True))
    alpha = jnp.exp(m_old - m_new)
    m_scr[...] = m_new
    pv = None
    for k in range(n_chunks):
        p = jnp.exp(chunk(k) - m_new).astype(_BF16).reshape(Q_PER_KV * TQ, LANES)
        d = jnp.dot(p, v[k * LANES:(k + 1) * LANES], preferred_element_type=_F32)
        pv = d if pv is None else pv + d
    acc_scr[...] = alpha * acc_scr[...] + pv.reshape(Q_PER_KV, TQ, 2 * HEAD_DIM)


FAR_TILES = 4


def _nsa_sel_win_kernel(q_ref, oc_ref, glog_ref, selt_ref, kst_ref, vs_ref, kwt_ref, vw_ref, near_ref,
                        exp_ref, gexp_ref, o_ref, seladd, m_s, acc_s, m_w, acc_w):
    i = pl.program_id(2)
    qs = _stack_heads(q_ref[...] * (HEAD_DIM ** -0.5)).astype(_BF16)
    sel = jnp.dot(selt_ref[...].T.astype(_BF16), exp_ref[...], preferred_element_type=_F32)
    seladd[...] = (sel - 1.0) * (-NEG_INF)
    for m_scr, acc_scr in ((m_s, acc_s), (m_w, acc_w)):
        m_scr[...] = jnp.full(m_scr.shape, NEG_INF, _F32)
        acc_scr[...] = jnp.zeros(acc_scr.shape, _F32)

    def keys(j, n_tiles):
        return pl.ds(pl.multiple_of(j * TK, TK), n_tiles * TK)

    def sel_step(j, n_tiles, bias):
        cols = keys(j, n_tiles)
        add = seladd[:, cols]
        if bias is not None:
            add = add + bias
        _flash_step(qs, kst_ref[:, cols], vs_ref[cols, :], add, m_s, acc_s)

    def win_step(j, n_tiles, add):
        cols = keys(j, n_tiles)
        _flash_step(qs, kwt_ref[:, cols], vw_ref[cols, :], add, m_w, acc_w)

    near = lambda lo: near_ref[:, lo * TK:].reshape(Q_PER_KV, TQ, (2 - lo) * TK)

    n_far = jnp.maximum(i - 1, 0)

    def far_many(j, carry):
        sel_step(j * FAR_TILES, FAR_TILES, None)
        return carry

    def far_one(j, carry):
        sel_step(j, 1, None)
        return carry

    lax.fori_loop(0, n_far // FAR_TILES, far_many, 0)
    lax.fori_loop((n_far // FAR_TILES) * FAR_TILES, n_far, far_one, 0)

    @pl.when(i >= 1)
    def _():
        sel_step(i - 1, 2, near(0))
        win_step(i - 1, 2, near(0))

    @pl.when(i == 0)
    def _():
        sel_step(0, 1, near(1))
        win_step(0, 1, near(1))

    n_lag = WINDOW // TK

    @pl.when(i >= n_lag - 1)
    def _():
        win_step(i - (n_lag - 1), n_lag - 2, None)

    for lag in range(2, n_lag - 1):
        @pl.when(i == lag)
        def _():
            win_step(0, lag - 1, None)

    @pl.when(i >= n_lag)
    def _():
        r = lax.broadcasted_iota(jnp.int32, (TQ, TK), 0)
        c = lax.broadcasted_iota(jnp.int32, (TQ, TK), 1)
        win_step(i - n_lag, 1, jnp.where(c >= r, 0.0, NEG_INF))

    def finish(acc_scr):
        acc = acc_scr[...]
        return _unstack_heads(acc[..., :HEAD_DIM] / acc[..., HEAD_DIM:])

    gates = jnp.dot(jax.nn.sigmoid(glog_ref[...]), gexp_ref[...], preferred_element_type=_F32,
                    precision=lax.Precision.HIGHEST)
    width = Q_PER_KV * HEAD_DIM
    o_ref[...] = (gates[:, :width] * oc_ref[...] + gates[:, width:2 * width] * finish(acc_s)
                  + gates[:, 2 * width:] * finish(acc_w))


def _nsa_prompt_side(kv, prm):
    import numpy as np
    bsz, t_len = kv.shape[:2]
    n_cmp = (t_len - CMP_LEN) // CMP_STRIDE + 1
    n_cmp_pad = t_len // CMP_STRIDE
    assert n_cmp == n_cmp_pad - 1 and n_cmp_pad % LANES == 0 and t_len % TQ == 0
    n_sel = t_len // SEL_BLOCK
    grouped = lambda slot: kv[:, :, slot].reshape(bsz, n_cmp_pad, CMP_STRIDE * N_KV_HEADS * HEAD_DIM)
    one_page = jnp.arange(bsz, dtype=jnp.int32)[:, None]
    k_cmp = _cmp_pages(grouped(0), one_page, prm['cmp_pos_k'], prm['cmp_w1_k'], prm['cmp_w2_k'])
    v_cmp = _cmp_pages(grouped(1), one_page, prm['cmp_pos_v'], prm['cmp_w1_v'], prm['cmp_w2_v'])
    kvb = kv.astype(_BF16)
    t_last = lambda a: jnp.transpose(a, (0, 2, 3, 1))
    t_rows = lambda a: jnp.transpose(a, (0, 2, 1, 3))
    pad_c = lambda a: a.reshape(bsz, n_cmp_pad, N_KV_HEADS, HEAD_DIM)
    rel_bias = prm['rel_bias']
    c_start = np.arange(n_cmp_pad)[None, :] * CMP_STRIDE
    s_start = np.arange(n_sel)[:, None] * SEL_BLOCK
    ovt = ((c_start < s_start + SEL_BLOCK) & (c_start + CMP_LEN > s_start) & (np.arange(n_cmp_pad)[None, :] < n_cmp))
    expand = np.arange(n_sel)[:, None] == (np.arange(t_len)[None, :] // SEL_BLOCK)
    lanes = np.arange(LANES)[:, None]
    cols = np.arange(N_BRANCH * Q_PER_KV * HEAD_DIM)[None, :]
    gexp = lanes == (cols // HEAD_DIM % Q_PER_KV) * N_BRANCH + cols // (Q_PER_KV * HEAD_DIM)
    g_cmp, near = _bias_tiles(rel_bias)
    with_ones = lambda a: jnp.concatenate([t_rows(a), jnp.ones((bsz, N_KV_HEADS, t_len, HEAD_DIM), _BF16)], axis=-1)
    return {
        'kct': t_last(pad_c(k_cmp)), 'vc': t_rows(pad_c(v_cmp)),
        'kst': t_last(kvb[:, :, 2]), 'vs': with_ones(kvb[:, :, 3]),
        'kwt': t_last(kvb[:, :, 4]), 'vw': with_ones(kvb[:, :, 5]),
        'ovt': jnp.asarray(ovt, _BF16), 'expand': jnp.asarray(expand, _BF16), 'gexp': jnp.asarray(gexp, _F32),
        'g_cmp': g_cmp, 'near': near,
    }


def _nsa_prompt_attention(q, glog, side):
    bsz, t_len, _ = q.shape
    n_t = t_len // TQ
    width = Q_PER_KV * HEAD_DIM
    n_cmp_pad = side['kct'].shape[-1]
    n_sel = side['ovt'].shape[0]
    grid = (bsz, N_KV_HEADS, n_t)
    params = pltpu.CompilerParams(dimension_semantics=("parallel", "parallel", "arbitrary"),
                                  vmem_limit_bytes=VMEM_LIMIT)
    q_spec = pl.BlockSpec((None, TQ, width), lambda b, h, i: (b, i, h))
    per_bh = lambda *shape: pl.BlockSpec((None, None) + shape, lambda b, h, i: (b, h, 0, 0))
    per_h = lambda *shape: pl.BlockSpec((None,) + shape, lambda b, h, i: (h,) + (0,) * len(shape))
    const = lambda *shape: pl.BlockSpec(shape, lambda b, h, i: (0,) * len(shape))
    selt_spec = pl.BlockSpec((None, None, n_sel, TQ), lambda b, h, i: (b, h, 0, i))
    o_c, selt = pl.pallas_call(
        _nsa_cmp_kernel,
        grid=grid,
        in_specs=[q_spec, per_bh(HEAD_DIM, n_cmp_pad), per_bh(n_cmp_pad, HEAD_DIM),
                  per_h(Q_PER_KV * TQ, n_cmp_pad), const(n_sel, n_cmp_pad)],
        out_specs=[q_spec, selt_spec],
        out_shape=[jax.ShapeDtypeStruct((bsz, t_len, N_HEADS * HEAD_DIM), _F32),
                   jax.ShapeDtypeStruct((bsz, N_KV_HEADS, n_sel, t_len), _F32)],
        compiler_params=params,
        name="nsa_cmp_select",
    )(q, side['kct'], side['vc'], side['g_cmp'], side['ovt'])
    gl = jnp.transpose(glog.reshape(bsz, t_len, N_KV_HEADS, Q_PER_KV * N_BRANCH), (0, 2, 1, 3))
    gl = jnp.pad(gl, ((0, 0), (0, 0), (0, 0), (0, LANES - Q_PER_KV * N_BRANCH)))
    stat = pltpu.VMEM((Q_PER_KV, TQ, 2 * HEAD_DIM), _F32)
    return pl.pallas_call(
        _nsa_sel_win_kernel,
        grid=grid,
        in_specs=[q_spec, q_spec, pl.BlockSpec((None, None, TQ, LANES), lambda b, h, i: (b, h, i, 0)), selt_spec,
                  per_bh(HEAD_DIM, t_len), per_bh(t_len, 2 * HEAD_DIM), per_bh(HEAD_DIM, t_len),
                  per_bh(t_len, 2 * HEAD_DIM), per_h(Q_PER_KV * TQ, 2 * TK), const(n_sel, t_len),
                  const(LANES, N_BRANCH * width)],
        out_specs=q_spec,
        out_shape=jax.ShapeDtypeStruct((bsz, t_len, N_HEADS * HEAD_DIM), _F32),
        scratch_shapes=[pltpu.VMEM((TQ, t_len), _F32), stat, stat, stat, stat],
        compiler_params=params,
        name="nsa_select_window",
    )(q, o_c, gl, selt, side['kst'], side['vs'], side['kwt'], side['vw'], side['near'],
      side['expand'], side['gexp'])


def _nsa_prompt_mixer(h, side, w_qg, w_o):
    qg = h @ w_qg
    mixed = _nsa_prompt_attention(qg[..., :N_HEADS * HEAD_DIM], qg[..., N_HEADS * HEAD_DIM:], side)
    return mixed @ w_o


PAGES_PER_STEP = 4
GROUPS_PER_PAGE_ROWS = CMP_STRIDE


def _dot_nt(a, b):
    return lax.dot_general(a, b, (((1,), (1,)), ((), ())), preferred_element_type=_F32)


def _cmp_pages_kernel(pt_ref, *refs, n_pages):
    del pt_ref
    pages = refs[:n_pages]
    posa_ref, posb_ref, w1a_ref, w1b_ref, w2_ref, o_ref = refs[n_pages:]
    x = jnp.concatenate([pg[...] for pg in pages], axis=0)
    z0 = jnp.dot((x + posa_ref[...]).astype(_BF16), w1a_ref[...], preferred_element_type=_F32)
    z1 = jnp.dot((x + posb_ref[...]).astype(_BF16), w1b_ref[...], preferred_element_type=_F32)
    pre = z0 + pltpu.roll(z1, z1.shape[0] - 1, 0)
    o_ref[...] = jnp.dot(jax.nn.gelu(pre).astype(_BF16), w2_ref[...], preferred_element_type=_F32).astype(o_ref.dtype)


def _cmp_pages(grouped, page_table, pos_emb, w1, w2):
    n_seq, n_pages = page_table.shape
    groups_per_page, flat = grouped.shape[1:]
    width = N_KV_HEADS * HEAD_DIM
    hidden = w1.shape[1]
    eye = jnp.eye(N_KV_HEADS, dtype=_F32)
    w1r = w1.reshape(2, CMP_STRIDE, HEAD_DIM, hidden)
    bd1 = jnp.einsum('hk,abdf->abhdkf', eye, w1r).reshape(2, flat, N_KV_HEADS * hidden).astype(_BF16)
    bd2 = jnp.einsum('hk,fd->hfkd', eye, w2).reshape(N_KV_HEADS * hidden, width).astype(_BF16)
    pos_t = jnp.broadcast_to(pos_emb.reshape(2, CMP_STRIDE, 1, HEAD_DIM),
                             (2, CMP_STRIDE, N_KV_HEADS, HEAD_DIM)).reshape(2, 1, flat)
    n_rows = n_pages * groups_per_page
    page_spec = lambda k: pl.BlockSpec((None, groups_per_page, flat), lambda b, pt: (pt[b, k], 0, 0))
    const = lambda *shape: pl.BlockSpec(shape, lambda b, pt: (0,) * len(shape))
    return pl.pallas_call(
        functools.partial(_cmp_pages_kernel, n_pages=n_pages),
        grid_spec=pltpu.PrefetchScalarGridSpec(
            num_scalar_prefetch=1,
            grid=(n_seq,),
            in_specs=[page_spec(k) for k in range(n_pages)] + [
                const(1, flat), const(1, flat), const(flat, N_KV_HEADS * hidden), const(flat, N_KV_HEADS * hidden),
                const(N_KV_HEADS * hidden, width)],
            out_specs=pl.BlockSpec((None, n_rows, width), lambda b, pt: (b, 0, 0)),
        ),
        out_shape=jax.ShapeDtypeStruct((n_seq, n_rows, width), _BF16),
        compiler_params=pltpu.CompilerParams(dimension_semantics=("parallel",), vmem_limit_bytes=VMEM_LIMIT),
        name="cmp_pages",
    )(page_table, *([grouped] * n_pages), pos_t[0], pos_t[1], bd1[0], bd1[1], bd2)


def _pick_own_kv_head(x, n_q):
    rows_per = Q_PER_KV * n_q
    return jnp.concatenate([x[h * rows_per:(h + 1) * rows_per, h * HEAD_DIM:(h + 1) * HEAD_DIM]
                            for h in range(N_KV_HEADS)], axis=0)


def _nsa_sample_kernel(pt_ref, q_ref, gl_ref, kc_ref, vc_ref, win_ref, new_ref, *rest, n_q, past_len, n_sel):
    del pt_ref
    pages = rest[:PAGES_PER_STEP]
    (tcmp_ref, tsel_ref, twin_ref, exp_ref, ov_ref, o_ref,
     addsel, m_scr, l_scr, acc_scr, oc_scr, ow_scr) = rest[PAGES_PER_STEP:]
    j = pl.program_id(1)
    rows = N_HEADS * n_q
    width = N_KV_HEADS * HEAD_DIM
    w_buf = win_ref.shape[1]
    step_keys = PAGES_PER_STEP * pages[0].shape[1]
    rr = lax.broadcasted_iota(jnp.int32, (rows, width), 0)
    cc = lax.broadcasted_iota(jnp.int32, (rows, width), 1)
    q4 = jnp.concatenate([q_ref[...] * (HEAD_DIM ** -0.5)] * N_KV_HEADS, axis=1)
    qbd = jnp.where(rr // (Q_PER_KV * n_q) == cc // HEAD_DIM, q4, 0.0).astype(_BF16)
    new_rows = jnp.concatenate([new_ref[...], jnp.zeros((LANES - n_q, new_ref.shape[1]), _F32)], axis=0).astype(_BF16)

    @pl.when(j == 0)
    def _first():
        s = _dot_nt(qbd, kc_ref[...]) + tcmp_ref[...]
        p = jnp.exp(s - jnp.max(s, axis=-1, keepdims=True))
        pb = (p / jnp.sum(p, axis=-1, keepdims=True)).astype(_BF16)
        oc_scr[...] = _pick_own_kv_head(jnp.dot(pb, vc_ref[...], preferred_element_type=_F32), n_q)
        pm = jnp.dot(pb, ov_ref[...], preferred_element_type=_F32)
        imp = jnp.concatenate(
            [sum(pm[(h * Q_PER_KV + g) * n_q:(h * Q_PER_KV + g + 1) * n_q] for g in range(Q_PER_KV))
             for h in range(N_KV_HEADS)], axis=0)
        blk = lax.broadcasted_iota(jnp.int32, imp.shape, 1)
        t = past_len + lax.broadcasted_iota(jnp.int32, imp.shape, 0) % n_q
        cur = t // SEL_BLOCK
        forced = (blk == 0) | (blk == cur) | (blk == cur - 1)
        score = jnp.where(forced, FORCE, jnp.where(blk * SEL_BLOCK <= t, imp, -FORCE))
        score = jnp.where(blk < n_sel, score, -jnp.inf)
        rank = jnp.zeros(imp.shape, jnp.int32)
        for sp in range(n_sel):
            col = score[:, sp:sp + 1]
            rank += ((col > score) | ((col == score) & (blk > sp))).astype(jnp.int32)
        sel = (rank < min(SEL_TOPK, n_sel)).astype(_BF16)
        sel_rows = jnp.concatenate([sel[h * n_q:(h + 1) * n_q] for h in range(N_KV_HEADS) for _ in range(Q_PER_KV)],
                                   axis=0)
        addsel[...] = (jnp.dot(sel_rows, exp_ref[...], preferred_element_type=_F32) - 1.0) * (-NEG_INF) + tsel_ref[...]
        kw_t = win_ref[:width, :].astype(_BF16)
        vw_t = win_ref[width:, :].astype(_BF16)
        s1 = jnp.dot(qbd, kw_t, preferred_element_type=_F32) + twin_ref[:, :w_buf]
        s2 = _dot_nt(qbd, new_rows[:, 2 * width:3 * width]) + twin_ref[:, w_buf:]
        m = jnp.maximum(jnp.max(s1, axis=-1, keepdims=True), jnp.max(s2, axis=-1, keepdims=True))
        p1 = jnp.exp(s1 - m)
        p2 = jnp.exp(s2 - m)
        l = jnp.sum(p1, axis=-1, keepdims=True) + jnp.sum(p2, axis=-1, keepdims=True)
        ow = (_dot_nt(p1.astype(_BF16), vw_t)
              + jnp.dot(p2.astype(_BF16), new_rows[:, 3 * width:], preferred_element_type=_F32))
        ow_scr[...] = _pick_own_kv_head(ow / l, n_q)
        m_scr[...] = jnp.full(m_scr.shape, NEG_INF, _F32)
        l_scr[...] = jnp.zeros(l_scr.shape, _F32)
        acc_scr[...] = jnp.zeros(acc_scr.shape, _F32)

    def flash(s, pv):
        m_old = m_scr[...]
        m_new = jnp.maximum(m_old, jnp.max(s, axis=-1, keepdims=True))
        alpha = jnp.exp(m_old - m_new)
        p = jnp.exp(s - m_new)
        m_scr[...] = m_new
        l_scr[...] = alpha * l_scr[...] + jnp.sum(p, axis=-1, keepdims=True)
        acc_scr[...] = alpha * acc_scr[...] + pv(p.astype(_BF16))

    k_t = jnp.concatenate([pg[:width, :] for pg in pages], axis=1).astype(_BF16)
    v_t = jnp.concatenate([pg[width:, :] for pg in pages], axis=1).astype(_BF16)
    flash(jnp.dot(qbd, k_t, preferred_element_type=_F32)
          + addsel[:, pl.ds(pl.multiple_of(j * step_keys, step_keys), step_keys)], lambda p: _dot_nt(p, v_t))

    @pl.when(j == pl.num_programs(1) - 1)
    def _last():
        flash(_dot_nt(qbd, new_rows[:, :width]) + addsel[:, past_len:],
              lambda p: jnp.dot(p, new_rows[:, width:2 * width], preferred_element_type=_F32))
        o_s = _pick_own_kv_head(acc_scr[...] / l_scr[...], n_q)
        gates = jax.nn.sigmoid(gl_ref[...])
        o_ref[...] = gates[:, 0:1] * oc_scr[...] + gates[:, 1:2] * o_s + gates[:, 2:3] * ow_scr[...]


def _nsa_sample_side(cache_kv, page_table, cache_win, kv_new, rel_bias, prm):
    import numpy as np
    n_seq, n_pages = page_table.shape
    n_phys, page_rows = cache_kv.shape[:2]
    n_q = kv_new.shape[1]
    width = N_KV_HEADS * HEAD_DIM
    past_len = n_pages * page_rows
    w_buf = cache_win.shape[1]
    tk = past_len + n_q
    n_cmp = (tk - CMP_LEN) // CMP_STRIDE + 1
    n_cmp_pad = past_len // CMP_STRIDE
    n_sel = -(-tk // SEL_BLOCK)
    assert n_cmp == n_cmp_pad - 1 and n_cmp_pad == LANES and n_sel <= LANES and n_q <= LANES
    assert page_rows % SEL_BLOCK == 0 and n_pages % PAGES_PER_STEP == 0 and w_buf == min(WINDOW, past_len)
    cache_t = jnp.transpose(cache_kv, (0, 2, 3, 4, 1)).reshape(n_phys, KV_SLOTS_PAGED * width, page_rows)
    win_t = jnp.transpose(cache_win, (0, 2, 3, 4, 1)).reshape(n_seq, KV_SLOTS_WIN * width, w_buf)
    gpp = page_rows // CMP_STRIDE
    grouped = cache_kv[:, :, :2].reshape(n_phys, gpp, CMP_STRIDE, 2, width)
    grouped = jnp.transpose(grouped, (3, 0, 1, 2, 4)).reshape(2, n_phys, gpp, CMP_STRIDE * width)
    kc = _cmp_pages(grouped[0], page_table, prm['cmp_pos_k'], prm['cmp_w1_k'], prm['cmp_w2_k'])
    vc = _cmp_pages(grouped[1], page_table, prm['cmp_pos_v'], prm['cmp_w1_v'], prm['cmp_w2_v'])
    rows = N_HEADS * n_q
    t = past_len + np.arange(n_q)[:, None]
    masked = lambda ok, d: jnp.where(jnp.asarray(ok), _bias_by_distance(rel_bias, d), NEG_INF).reshape(rows, -1)
    n = np.arange(n_cmp_pad)[None, :]
    d_c = t - (n * CMP_STRIDE + CMP_LEN - 1)
    t_cmp = masked((d_c >= 0) & (n < n_cmp), d_c)
    pos = np.arange(past_len + LANES)[None, :]
    t_sel = masked((t - pos >= 0) & (pos < tk), t - pos)
    l = np.arange(w_buf + LANES)[None, :]
    d_w = t - (past_len - w_buf + l)
    t_win = masked((d_w >= 0) & (d_w <= WINDOW) & (l < w_buf + n_q), d_w)
    expand = (np.arange(LANES)[:, None] == pos // SEL_BLOCK) & (np.arange(LANES)[:, None] < n_sel)
    c_start = np.arange(n_cmp_pad)[:, None] * CMP_STRIDE
    s_start = np.arange(LANES)[None, :] * SEL_BLOCK
    ov = ((c_start < s_start + SEL_BLOCK) & (c_start + CMP_LEN > s_start) & (np.arange(n_cmp_pad)[:, None] < n_cmp)
          & (np.arange(LANES)[None, :] < n_sel))
    return {
        'cache_t': cache_t, 'page_table': page_table, 'kc': kc, 'vc': vc, 'win_t': win_t,
        'new': kv_new.reshape(n_seq, n_q, -1)[:, :, 2 * width:],
        't_cmp': t_cmp, 't_sel': t_sel, 't_win': t_win,
        'expand': jnp.asarray(expand, _BF16), 'ov': jnp.asarray(ov, _BF16), 'n_sel': n_sel, 'past_len': past_len,
    }


def _nsa_sample_attention(q, glog, side):
    n_seq, n_q, _ = q.shape
    rows = N_HEADS * n_q
    width = N_KV_HEADS * HEAD_DIM
    page_table = side['page_table']
    n_pages = page_table.shape[1]
    page_rows = side['cache_t'].shape[2]
    past_len = side['past_len']
    w_buf = side['win_t'].shape[2]
    n_steps = n_pages // PAGES_PER_STEP
    q_rows = jnp.transpose(q.reshape(n_seq, n_q, N_HEADS, HEAD_DIM), (0, 2, 1, 3)).reshape(n_seq, rows, HEAD_DIM)
    gl = jnp.transpose(glog.reshape(n_seq, n_q, N_HEADS, N_BRANCH), (0, 2, 1, 3)).reshape(n_seq, rows, N_BRANCH)
    gl = jnp.pad(gl, ((0, 0), (0, 0), (0, LANES - N_BRANCH)))
    per_seq = lambda *shape: pl.BlockSpec((None,) + shape, lambda b, j, pt: (b,) + (0,) * len(shape))
    const = lambda *shape: pl.BlockSpec(shape, lambda b, j, pt: (0,) * len(shape))
    page_spec = lambda k: pl.BlockSpec((None, 2 * width, page_rows),
                                       lambda b, j, pt: (pt[b, j * PAGES_PER_STEP + k], 1, 0))
    out = pl.pallas_call(
        functools.partial(_nsa_sample_kernel, n_q=n_q, past_len=past_len, n_sel=side['n_sel']),
        grid_spec=pltpu.PrefetchScalarGridSpec(
            num_scalar_prefetch=1,
            grid=(n_seq, n_steps),
            in_specs=[per_seq(rows, HEAD_DIM), per_seq(rows, LANES), per_seq(LANES, width), per_seq(LANES, width),
                      per_seq(2 * width, w_buf), per_seq(n_q, 4 * width)]
            + [page_spec(k) for k in range(PAGES_PER_STEP)]
            + [const(rows, LANES), const(rows, past_len + LANES), const(rows, w_buf + LANES),
               const(LANES, past_len + LANES), const(LANES, LANES)],
            out_specs=per_seq(rows, HEAD_DIM),
            scratch_shapes=[pltpu.VMEM((rows, past_len + LANES), _F32), pltpu.VMEM((rows, 1), _F32),
                            pltpu.VMEM((rows, 1), _F32), pltpu.VMEM((rows, width), _F32),
                            pltpu.VMEM((rows, HEAD_DIM), _F32), pltpu.VMEM((rows, HEAD_DIM), _F32)],
        ),
        out_shape=jax.ShapeDtypeStruct((n_seq, rows, HEAD_DIM), _F32),
        compiler_params=pltpu.CompilerParams(dimension_semantics=("parallel", "arbitrary"),
                                             vmem_limit_bytes=VMEM_LIMIT),
        name="nsa_sample",
    )(page_table, q_rows, gl, side['kc'], side['vc'], side['win_t'], side['new'],
      *([side['cache_t']] * PAGES_PER_STEP), side['t_cmp'], side['t_sel'], side['t_win'], side['expand'], side['ov'])
    return jnp.transpose(out.reshape(n_seq, N_HEADS, n_q, HEAD_DIM), (0, 2, 1, 3)).reshape(n_seq, n_q, N_HEADS * HEAD_DIM)


def _nsa_sample_mixer(h, side, w_qg, w_o):
    qg = h @ w_qg
    mixed = _nsa_sample_attention(qg[..., :N_HEADS * HEAD_DIM], qg[..., N_HEADS * HEAD_DIM:], side)
    return mixed @ w_o


def _run_trunk(x, p, s0_re, s0_im, paged, win_buf, prm):
    bsz, t_len, _ = x.shape
    ssm_re, ssm_im = [], []
    side = None
    kv_rows_new = None
    win_state = None
    for i in range(DEPTH):
        h = _rmsnorm(x, prm['g_mix'][i])
        if i < N_A_LAYERS:
            mix, s_re, s_im = _s5_mixer(h, s0_re[i], s0_im[i], prm['ssm_a_re'][i], prm['ssm_a_im'][i],
                                        prm['ssm_log_dt'][i], prm['ssm_b_re'][i], prm['ssm_b_im'][i],
                                        prm['ssm_c_re'][i], prm['ssm_c_im'][i], prm['ssm_d'][i], prm['w_glu'][i])
            ssm_re.append(s_re)
            ssm_im.append(s_im)
        else:
            j = i - N_A_LAYERS
            mixer = _nsa_prompt_mixer if paged is None else _nsa_sample_mixer
            mix = mixer(h, side, prm['w_qg'][j], prm['w_o'][j])
        x = x + mix
        x = _moe_layer(x.reshape(-1, D_MODEL), prm['g_ffn'][i], prm['w_route_group'][i], prm['b_route_group'][i],
                       prm['w_route_expert'][i], prm['b_route_expert'][i], prm['w_exp_up'][i],
                       prm['w_exp_down'][i]).reshape(bsz, t_len, D_MODEL)
        gate = jax.nn.sigmoid(_rmsnorm(x, prm['g_ple'][i]) @ prm['w_ple_gate'][i])
        x = x + (p[i] @ prm['w_ple_proj'][i]) * gate
        if i == N_A_LAYERS - 1:
            kv = (_rmsnorm(x, prm['g_kv']) @ prm['w_kv']).reshape(
                bsz, t_len, KV_SLOTS_PAGED + KV_SLOTS_WIN, N_KV_HEADS, HEAD_DIM)
            kv_rows_new, win_new = kv[:, :, :KV_SLOTS_PAGED], kv[:, :, KV_SLOTS_PAGED:]
            if paged is None:
                win_state = win_new[:, -min(WINDOW, t_len):]
                side = _nsa_prompt_side(kv, prm)
            else:
                w_buf = win_buf.shape[1]
                win_state = jnp.concatenate([win_buf, win_new], axis=1)[:, -w_buf:]
                side = _nsa_sample_side(paged[0], paged[1], win_buf, kv, prm['rel_bias'], prm)
    y = _rmsnorm(x, prm['g_final'])
    return y, kv_rows_new, win_state, jnp.stack(ssm_re), jnp.stack(ssm_im)


def kernel(x_prompt, x_sample, p_prompt, p_sample, cache_kv, cache_win, state_ssm_re, state_ssm_im, page_table,
           g_mix, g_ffn, g_ple, g_kv, g_final,
           ssm_a_re, ssm_a_im, ssm_log_dt, ssm_b_re, ssm_b_im, ssm_c_re, ssm_c_im, ssm_d, w_glu,
           w_kv, cmp_pos_k, cmp_pos_v, cmp_w1_k, cmp_w2_k, cmp_w1_v, cmp_w2_v, w_qg, w_o, rel_bias,
           w_route_group, b_route_group, w_route_expert, b_route_expert, w_exp_up, w_exp_down,
           w_ple_proj, w_ple_gate):
    prm = {
        'g_mix': g_mix, 'g_ffn': g_ffn, 'g_ple': g_ple, 'g_kv': g_kv, 'g_final': g_final,
        'ssm_a_re': ssm_a_re, 'ssm_a_im': ssm_a_im, 'ssm_log_dt': ssm_log_dt,
        'ssm_b_re': ssm_b_re, 'ssm_b_im': ssm_b_im, 'ssm_c_re': ssm_c_re, 'ssm_c_im': ssm_c_im,
        'ssm_d': ssm_d, 'w_glu': w_glu,
        'w_kv': w_kv, 'cmp_pos_k': cmp_pos_k, 'cmp_pos_v': cmp_pos_v,
        'cmp_w1_k': cmp_w1_k, 'cmp_w2_k': cmp_w2_k, 'cmp_w1_v': cmp_w1_v, 'cmp_w2_v': cmp_w2_v,
        'w_qg': w_qg, 'w_o': w_o, 'rel_bias': rel_bias,
        'w_route_group': w_route_group, 'b_route_group': b_route_group,
        'w_route_expert': w_route_expert, 'b_route_expert': b_route_expert,
        'w_exp_up': w_exp_up, 'w_exp_down': w_exp_down,
        'w_ple_proj': w_ple_proj, 'w_ple_gate': w_ple_gate,
    }
    zero_state = jnp.zeros((N_A_LAYERS, x_prompt.shape[0], SSM_GROUPS, SSM_STATE), _F32)
    y_prompt, kv_prompt, win_prompt, ssm_re_prompt, ssm_im_prompt = _run_trunk(
        x_prompt, p_prompt, zero_state, zero_state, None, None, prm)
    y_sample, kv_sample, win_sample, ssm_re_sample, ssm_im_sample = _run_trunk(
        x_sample, p_sample, state_ssm_re, state_ssm_im, (cache_kv, page_table), cache_win, prm)
    return (y_prompt, y_sample, kv_prompt, win_prompt, ssm_re_prompt, ssm_im_prompt,
            kv_sample, win_sample, ssm_re_sample, ssm_im_sample)
```

```python
import functools
import math

import jax
import jax.numpy as jnp
from jax import lax
from jax.experimental import pallas as pl
from jax.experimental.pallas import tpu as pltpu

D_MODEL = 1024
DEPTH = 4
N_A_LAYERS = DEPTH // 2
SSM_GROUP = 16
SSM_GROUPS = D_MODEL // SSM_GROUP
SSM_STATE = 64
N_HEADS = 16
HEAD_DIM = D_MODEL // N_HEADS
N_KV_HEADS = 4
Q_PER_KV = N_HEADS // N_KV_HEADS
CMP_LEN = 32
CMP_STRIDE = 16
SEL_BLOCK = 64
SEL_TOPK = 16
WINDOW = 512
N_BRANCH = 3
KV_SLOTS_PAGED = 4
KV_SLOTS_WIN = 2
Q_BLOCK = 64
N_BUCKETS = 32
MAX_DISTANCE = 128
N_EXPERT_GROUPS = 4
EXPERTS_PER_GROUP = 4
N_EXPERTS = N_EXPERT_GROUPS * EXPERTS_PER_GROUP
EXPERT_TOPK = 2
D_EXPERT = 256
RMS_EPS = 1e-6
NEG_INF = -1e30
FORCE = 1e4

LANES = 128
VMEM_LIMIT = 48 * 1024 * 1024

_F32 = jnp.float32
_BF16 = jnp.bfloat16


def _moe_kernel(x_ref, g_ref, wr_ref, br_ref, wup_ref, wdn_ref, o_ref, h_scr, comb_scr, acc_scr):
    e = pl.program_id(1)
    lane = lax.broadcasted_iota(jnp.int32, comb_scr.shape, 1)

    @pl.when(e == 0)
    def _route():
        x = x_ref[...]
        h = x * lax.rsqrt(jnp.mean(x * x, axis=-1, keepdims=True) + RMS_EPS) * g_ref[...]
        h_scr[...] = h.astype(_BF16)
        logits = jnp.dot(h, wr_ref[...], preferred_element_type=_F32,
                         precision=lax.Precision.HIGHEST) + br_ref[...]
        is_grp = (lane >= N_EXPERTS) & (lane < N_EXPERTS + N_EXPERT_GROUPS)
        lg = jnp.where(is_grp, logits, -jnp.inf)
        gmax = jnp.max(lg, axis=-1, keepdims=True)
        gi = jnp.min(jnp.where(lg == gmax, lane, LANES), axis=-1, keepdims=True) - N_EXPERTS
        gp = 1.0 / jnp.sum(jnp.where(is_grp, jnp.exp(lg - gmax), 0.0), axis=-1, keepdims=True)
        in_grp = (lane < N_EXPERTS) & ((lane // EXPERTS_PER_GROUP) == gi)
        le = jnp.where(in_grp, logits, -jnp.inf)
        m1 = jnp.max(le, axis=-1, keepdims=True)
        i1 = jnp.min(jnp.where(le == m1, lane, LANES), axis=-1, keepdims=True)
        le2 = jnp.where(lane == i1, -jnp.inf, le)
        m2 = jnp.max(le2, axis=-1, keepdims=True)
        i2 = jnp.min(jnp.where(le2 == m2, lane, LANES), axis=-1, keepdims=True)
        e2 = jnp.exp(m2 - m1)
        den = 1.0 + e2
        comb_scr[...] = jnp.where(lane == i1, gp / den, jnp.where(lane == i2, gp * e2 / den, 0.0))
        acc_scr[...] = x

    up = jnp.dot(h_scr[...], wup_ref[0], preferred_element_type=_F32)
    a = up[:, :D_EXPERT]
    b = up[:, D_EXPERT:]
    c = jnp.sum(jnp.where(lane == e, comb_scr[...], 0.0), axis=-1, keepdims=True)
    act = (a * jax.nn.sigmoid(a)) * b * c
    acc_scr[...] += jnp.dot(act.astype(_BF16), wdn_ref[0], preferred_element_type=_F32)

    @pl.when(e == N_EXPERTS - 1)
    def _store():
        o_ref[...] = acc_scr[...]


def _moe_layer(x2, g, w_rg, b_rg, w_re, b_re, w_up, w_dn):
    n_tok, d = x2.shape
    tm = min(n_tok, 1024)
    wr = jnp.zeros((d, LANES), _F32).at[:, :N_EXPERTS].set(w_re).at[:, N_EXPERTS:N_EXPERTS + N_EXPERT_GROUPS].set(w_rg)
    br = jnp.zeros((1, LANES), _F32).at[0, :N_EXPERTS].set(b_re).at[0, N_EXPERTS:N_EXPERTS + N_EXPERT_GROUPS].set(b_rg)
    return pl.pallas_call(
        _moe_kernel,
        grid=(n_tok // tm, N_EXPERTS),
        in_specs=[
            pl.BlockSpec((tm, d), lambda i, e: (i, 0)),
            pl.BlockSpec((1, d), lambda i, e: (0, 0)),
            pl.BlockSpec((d, LANES), lambda i, e: (0, 0)),
            pl.BlockSpec((1, LANES), lambda i, e: (0, 0)),
            pl.BlockSpec((1, d, 2 * D_EXPERT), lambda i, e: (e, 0, 0)),
            pl.BlockSpec((1, D_EXPERT, d), lambda i, e: (e, 0, 0)),
        ],
        out_specs=pl.BlockSpec((tm, d), lambda i, e: (i, 0)),
        out_shape=jax.ShapeDtypeStruct((n_tok, d), _F32),
        scratch_shapes=[
            pltpu.VMEM((tm, d), _BF16),
            pltpu.VMEM((tm, LANES), _F32),
            pltpu.VMEM((tm, d), _F32),
        ],
        compiler_params=pltpu.CompilerParams(
            dimension_semantics=("parallel", "arbitrary"), vmem_limit_bytes=VMEM_LIMIT),
        name="hmoe",
    )(x2, g.reshape(1, d), wr, br, w_up.astype(_BF16), w_dn.astype(_BF16))


ROW_TILE = 512


def _rows_kernel(*refs, body, n_in):
    outs = body(*[r[...] for r in refs[:n_in]])
    for o_ref, o in zip(refs[n_in:], outs):
        o_ref[...] = o


def _rows_call(body, rows, consts, out_widths, name):
    n_rows = rows[0].shape[0]
    tm = min(n_rows, ROW_TILE)
    row_spec = lambda width: pl.BlockSpec((tm, width), lambda i: (i, 0))
    const_spec = lambda c: pl.BlockSpec(c.shape, lambda i: (0,) * c.ndim)
    return pl.pallas_call(
        functools.partial(_rows_kernel, body=body, n_in=len(rows) + len(consts)),
        grid=(n_rows // tm,),
        in_specs=[row_spec(r.shape[1]) for r in rows] + [const_spec(c) for c in consts],
        out_specs=[row_spec(w) for w in out_widths],
        out_shape=[jax.ShapeDtypeStruct((n_rows, w), _F32) for w in out_widths],
        compiler_params=pltpu.CompilerParams(dimension_semantics=("parallel",), vmem_limit_bytes=VMEM_LIMIT),
        name=name,
    )(*rows, *consts)


def _norm(x, g):
    return x * lax.rsqrt(jnp.mean(x * x, axis=-1, keepdims=True) + RMS_EPS) * g


def _mm(x, w):
    return jnp.dot(x.astype(_BF16), w, preferred_element_type=_F32)


def _glu_residual(x2, y2, w_glu):
    def body(x, y, w):
        ag = _mm(jax.nn.gelu(y), w)
        return (x + ag[:, :D_MODEL] * jax.nn.sigmoid(ag[:, D_MODEL:]),)
    return _rows_call(body, [x2, y2], [w_glu.astype(_BF16)], [D_MODEL], "glu_residual")[0]


def _qg_project(x2, g, w_qg):
    per_kv = Q_PER_KV * N_BRANCH
    w_g = w_qg[:, N_HEADS * HEAD_DIM:].reshape(D_MODEL, N_KV_HEADS, per_kv)
    w_g = jnp.pad(w_g, ((0, 0), (0, 0), (0, LANES - per_kv))).reshape(D_MODEL, N_KV_HEADS * LANES)
    w = jnp.concatenate([w_qg[:, :N_HEADS * HEAD_DIM], w_g], axis=1).astype(_BF16)

    def body(x, gain, w):
        qg = _mm(_norm(x, gain), w)
        return qg[:, :N_HEADS * HEAD_DIM], qg[:, N_HEADS * HEAD_DIM:]
    return _rows_call(body, [x2], [g.reshape(1, -1), w], [N_HEADS * HEAD_DIM, N_KV_HEADS * LANES], "qg_project")


def _wo_residual(x2, mixed2, w_o):
    body = lambda x, m, w: (x + _mm(m, w),)
    return _rows_call(body, [x2, mixed2], [w_o.astype(_BF16)], [D_MODEL], "wo_residual")[0]


def _ple_residual(x2, p2, g_ple, w_gate, w_proj, g_next=None, w_next=None):
    consts = [g_ple.reshape(1, -1), w_gate.astype(_BF16), w_proj.astype(_BF16)]
    widths = [D_MODEL]
    if g_next is not None:
        consts.append(g_next.reshape(1, -1))
        widths.append(D_MODEL if w_next is None else w_next.shape[1])
    if w_next is not None:
        consts.append(w_next.astype(_BF16))

    def body(x, p, gain, wg, wp, *nxt):
        out = x + _mm(p, wp) * jax.nn.sigmoid(_mm(_norm(x, gain), wg))
        if not nxt:
            return (out,)
        normed = _norm(out, nxt[0])
        return out, (normed if len(nxt) == 1 else _mm(normed, nxt[1]))
    return _rows_call(body, [x2, p2], consts, widths, "ple_residual")


def _rmsnorm(x, g):
    xf = x.astype(_F32)
    y = xf * lax.rsqrt(jnp.mean(xf * xf, axis=-1, keepdims=True) + RMS_EPS)
    return (y * g.astype(_F32)).astype(x.dtype)


def _cmul(ar, ai, br, bi):
    return ar * br - ai * bi, ar * bi + ai * br


def _s5_operators(a_re, a_im, log_dt, b_re, b_im, c_re, c_im, chunk):
    hp = lax.Precision.HIGHEST
    dt = jnp.exp(log_dt)[:, None]
    decay = jnp.exp(a_re * dt)
    ab_re, ab_im = decay * jnp.cos(a_im * dt), decay * jnp.sin(a_im * dt)
    den = a_re * a_re + a_im * a_im
    num_re = ab_re - 1.0
    q_re = (num_re * a_re + ab_im * a_im) / den
    q_im = (ab_im * a_re - num_re * a_im) / den
    bb_re, bb_im = _cmul(q_re[..., None], q_im[..., None], b_re, b_im)
    pw_re, pw_im = [jnp.ones_like(ab_re)], [jnp.zeros_like(ab_im)]
    for _ in range(chunk):
        nr, ni = _cmul(pw_re[-1], pw_im[-1], ab_re, ab_im)
        pw_re.append(nr)
        pw_im.append(ni)
    pw_re, pw_im = jnp.stack(pw_re), jnp.stack(pw_im)
    w_re, w_im = _cmul(pw_re[:chunk, :, :, None], pw_im[:chunk, :, :, None], bb_re[None], bb_im[None])
    k_lag = (jnp.einsum('gcp,jgpd->jgcd', c_re, w_re, precision=hp)
             - jnp.einsum('gcp,jgpd->jgcd', c_im, w_im, precision=hp))
    tt = jnp.arange(chunk)
    lag = tt[None, :] - tt[:, None]
    m = jnp.where((lag >= 0)[:, :, None, None, None], k_lag[jnp.clip(lag, 0, chunk - 1)], 0.0)
    m = jnp.transpose(m, (2, 0, 4, 1, 3)).reshape(SSM_GROUPS, chunk * SSM_GROUP, chunk * SSM_GROUP)
    h_re = jnp.transpose(w_re[::-1], (1, 0, 3, 2)).reshape(SSM_GROUPS, chunk * SSM_GROUP, SSM_STATE)
    h_im = jnp.transpose(w_im[::-1], (1, 0, 3, 2)).reshape(SSM_GROUPS, chunk * SSM_GROUP, SSM_STATE)
    ca_re, ca_im = _cmul(c_re[None], c_im[None], pw_re[1:, :, None, :], pw_im[1:, :, None, :])
    gm_re = jnp.transpose(ca_re, (1, 3, 0, 2)).reshape(SSM_GROUPS, SSM_STATE, chunk * SSM_GROUP)
    gm_im = -jnp.transpose(ca_im, (1, 3, 0, 2)).reshape(SSM_GROUPS, SSM_STATE, chunk * SSM_GROUP)
    return m, h_re, h_im, gm_re, gm_im, pw_re[chunk][:, None, :], pw_im[chunk][:, None, :]


def _s5_kernel(u_ref, m_ref, hre_ref, him_ref, gre_ref, gim_ref, alre_ref, alim_ref, d_ref, s0re_ref, s0im_ref,
               y_ref, sre_ref, sim_ref, ere, eim, sinre, sinim, *, n_chunks, rows, groups):
    hp = lax.Precision.HIGHEST
    for g in range(groups):
        u = u_ref[g]
        ere[g] = jnp.dot(u, hre_ref[g], preferred_element_type=_F32, precision=hp)
        eim[g] = jnp.dot(u, him_ref[g], preferred_element_type=_F32, precision=hp)

    def step(j, carry):
        row = pl.ds(pl.multiple_of(j * rows, rows), rows)
        out = []
        for g in range(groups):
            sr, si = carry[2 * g], carry[2 * g + 1]
            sinre[g, row, :] = sr
            sinim[g, row, :] = si
            ar, ai = alre_ref[g], alim_ref[g]
            out.append(ar * sr - ai * si + ere[g, row, :])
            out.append(ar * si + ai * sr + eim[g, row, :])
        return tuple(out)

    init = []
    for g in range(groups):
        init += [s0re_ref[g], s0im_ref[g]]
    fin = lax.fori_loop(0, n_chunks, step, tuple(init))
    for g in range(groups):
        sre_ref[g] = fin[2 * g]
        sim_ref[g] = fin[2 * g + 1]
        u = u_ref[g]
        y = jnp.dot(u.astype(_BF16), m_ref[g], preferred_element_type=_F32)
        y += jnp.dot(sinre[g].astype(_BF16), gre_ref[g], preferred_element_type=_F32)
        y += jnp.dot(sinim[g].astype(_BF16), gim_ref[g], preferred_element_type=_F32)
        y_ref[g] = y + d_ref[g] * u


def _s5_scan(u, s0_re, s0_im, a_re, a_im, log_dt, b_re, b_im, c_re, c_im, d_skip):
    bsz, t_len, _ = u.shape
    chunk = min(t_len, 16)
    n_chunks = t_len // chunk
    rows = -(-bsz // 8) * 8
    width = chunk * SSM_GROUP
    groups = 2
    m, h_re, h_im, gm_re, gm_im, al_re, al_im = _s5_operators(a_re, a_im, log_dt, b_re, b_im, c_re, c_im, chunk)
    uf = u.reshape(bsz, n_chunks, chunk, SSM_GROUPS, SSM_GROUP)
    uf = jnp.transpose(uf, (3, 1, 0, 2, 4))
    uf = jnp.pad(uf, ((0, 0), (0, 0), (0, rows - bsz), (0, 0), (0, 0))).reshape(SSM_GROUPS, n_chunks * rows, width)
    pad_state = lambda s: jnp.pad(jnp.transpose(s, (1, 0, 2)), ((0, 0), (0, rows - bsz), (0, 0)))
    d_tile = jnp.tile(d_skip, (1, chunk))[:, None, :]
    gspec = lambda *shape: pl.BlockSpec((groups,) + shape, lambda i: (i,) + (0,) * len(shape))
    y, s_re, s_im = pl.pallas_call(
        functools.partial(_s5_kernel, n_chunks=n_chunks, rows=rows, groups=groups),
        grid=(SSM_GROUPS // groups,),
        in_specs=[gspec(n_chunks * rows, width), gspec(width, width), gspec(width, SSM_STATE), gspec(width, SSM_STATE),
                  gspec(SSM_STATE, width), gspec(SSM_STATE, width), gspec(1, SSM_STATE), gspec(1, SSM_STATE),
                  gspec(1, width), gspec(rows, SSM_STATE), gspec(rows, SSM_STATE)],
        out_specs=[gspec(n_chunks * rows, width), gspec(rows, SSM_STATE), gspec(rows, SSM_STATE)],
        out_shape=[jax.ShapeDtypeStruct((SSM_GROUPS, n_chunks * rows, width), _F32),
                   jax.ShapeDtypeStruct((SSM_GROUPS, rows, SSM_STATE), _F32),
                   jax.ShapeDtypeStruct((SSM_GROUPS, rows, SSM_STATE), _F32)],
        scratch_shapes=[pltpu.VMEM((groups, n_chunks * rows, SSM_STATE), _F32) for _ in range(4)],
        compiler_params=pltpu.CompilerParams(dimension_semantics=("parallel",), vmem_limit_bytes=VMEM_LIMIT),
        name="s5_scan",
    )(uf, m.astype(_BF16), h_re, h_im, gm_re.astype(_BF16), gm_im.astype(_BF16), al_re, al_im, d_tile,
      pad_state(s0_re), pad_state(s0_im))
    y = y.reshape(SSM_GROUPS, n_chunks, rows, chunk, SSM_GROUP)[:, :, :bsz]
    y = jnp.transpose(y, (2, 1, 3, 0, 4)).reshape(bsz, t_len, D_MODEL)
    unpad = lambda s: jnp.transpose(s[:, :bsz], (1, 0, 2))
    return y, unpad(s_re), unpad(s_im)


TQ = 256
TK = 128
NEAR_TILES = TQ // TK + 1
NEAR_BUCKET_DIST = 113


def _bucket_table(dist):
    import numpy as np
    max_exact = N_BUCKETS // 2
    d = np.maximum(np.asarray(dist, np.int64), 0)
    df = np.maximum(d, 1).astype(np.float64)
    large = max_exact + (np.log(df / max_exact) / math.log(MAX_DISTANCE / max_exact)
                         * (N_BUCKETS - max_exact)).astype(np.int64)
    return np.where(d < max_exact, d, np.minimum(large, N_BUCKETS - 1)).astype(np.int32)


def _bias_by_distance(rel_bias, dist):
    import numpy as np
    onehot = _bucket_table(dist)[None] == np.arange(N_BUCKETS)[:, None, None]
    return jnp.einsum('kqc,kh->hqc', jnp.asarray(onehot, _BF16).astype(_F32), rel_bias,
                      precision=lax.Precision.HIGHEST)


def _bias_tiles(rel_bias, n_cmp_pad):
    import numpy as np
    rel_bias = rel_bias - rel_bias[N_BUCKETS - 1]
    r = np.arange(TQ)[:, None]
    c = np.arange(n_cmp_pad)[None, :]
    d_cmp = np.where(c < 8 + TQ // CMP_STRIDE, r - (CMP_LEN - 1) + CMP_STRIDE * (8 - c), 10 ** 6)
    g_cmp = _bias_by_distance(rel_bias, d_cmp).reshape(N_KV_HEADS, Q_PER_KV * TQ, n_cmp_pad)
    k = np.arange(NEAR_TILES * TK)[None, :]
    d_near = TK + r - k
    near = _bias_by_distance(rel_bias, d_near) + jnp.asarray(np.where(d_near >= 0, 0.0, NEG_INF), _F32)
    near = near.reshape(N_KV_HEADS, Q_PER_KV * TQ, NEAR_TILES * TK)
    return g_cmp, near


def _stack_heads(q):
    return jnp.concatenate([q[:, g * HEAD_DIM:(g + 1) * HEAD_DIM] for g in range(Q_PER_KV)], axis=0)


def _unstack_heads(o):
    return jnp.concatenate([o[g] for g in range(Q_PER_KV)], axis=1)


def _nsa_cmp_kernel(q_ref, kct_ref, vc_ref, g_ref, ovt_ref, oc_ref, selt_ref):
    i = pl.program_id(2)
    t0 = i * TQ
    n_cmp_pad = kct_ref.shape[-1]
    qs = _stack_heads(q_ref[...] * (HEAD_DIM ** -0.5)).astype(_BF16)
    s = jnp.dot(qs, kct_ref[...], preferred_element_type=_F32)
    shift = (i * (TQ // CMP_STRIDE) + n_cmp_pad - 8) % n_cmp_pad
    bias = pltpu.roll(g_ref[...], shift, 1)
    s3 = (s + bias).reshape(Q_PER_KV, TQ, n_cmp_pad)
    r = lax.broadcasted_iota(jnp.int32, (TQ, n_cmp_pad), 0)
    n = lax.broadcasted_iota(jnp.int32, (TQ, n_cmp_pad), 1)
    valid = (t0 + r - CMP_STRIDE * n - (CMP_LEN - 1)) >= 0
    sm = jnp.where(valid, s3, NEG_INF)
    m = jnp.max(sm, axis=-1, keepdims=True)
    p = jnp.where(valid, jnp.exp(sm - m), 0.0)
    l = jnp.sum(p, axis=-1, keepdims=True)
    pb = (p * jnp.where(l > 0.0, 1.0 / l, 0.0)).astype(_BF16)
    oc = jnp.dot(pb.reshape(Q_PER_KV * TQ, n_cmp_pad), vc_ref[...], preferred_element_type=_F32)
    oc_ref[...] = _unstack_heads(oc.reshape(Q_PER_KV, TQ, HEAD_DIM))
    imp = jnp.zeros((ovt_ref.shape[0], TQ), _F32)
    for g in range(Q_PER_KV):
        imp += lax.dot_general(ovt_ref[...], pb[g], (((1,), (1,)), ((), ())), preferred_element_type=_F32)
    n_sel = imp.shape[0]
    blk = lax.broadcasted_iota(jnp.int32, (n_sel, TQ), 0)
    t = t0 + lax.broadcasted_iota(jnp.int32, (n_sel, TQ), 1)
    cur = t // SEL_BLOCK
    forced = (blk == 0) | (blk == cur) | (blk == cur - 1)
    score = jnp.where(forced, FORCE, jnp.where(blk * SEL_BLOCK <= t, imp, -FORCE))
    rank = jnp.zeros((n_sel, TQ), _F32)
    for sp in range(n_sel):
        row = score[sp:sp + 1, :]
        beats = (row > score) | ((row == score) & (blk > sp))
        rank += jnp.where(beats, 1.0, 0.0)
    selt_ref[...] = (rank < SEL_TOPK).astype(_F32)


def _flash_step(qs, kt, v, add, m_scr, acc_scr):
    s3 = _scores(qs, kt, add)
    _softmax_pv(lambda k: s3[..., k * LANES:(k + 1) * LANES], kt.shape[-1] // LANES, v, m_scr, acc_scr)


def _scores(qs, kt, add):
    s3 = jnp.dot(qs, kt, preferred_element_type=_F32).reshape(Q_PER_KV, TQ, kt.shape[-1])
    return s3 if add is None else s3 + add


def _softmax_pv(chunk, n_chunks, v, m_scr, acc_scr):
    m_old = m_scr[...]
    m_tile = functools.reduce(jnp.maximum, [chunk(k) for k in range(n_chunks)])
    m_new = jnp.maximum(m_old, jnp.max(m_tile, axis=-1, keepdims=True))
    alpha = jnp.exp(m_old - m_new)
    m_scr[...] = m_new
    pv = None
    for k in range(n_chunks):
        p = jnp.exp(chunk(k) - m_new).astype(_BF16).reshape(Q_PER_KV * TQ, LANES)
        d = jnp.dot(p, v[k * LANES:(k + 1) * LANES], preferred_element_type=_F32)
        pv = d if pv is None else pv + d
    acc_scr[...] = alpha * acc_scr[...] + pv.reshape(Q_PER_KV, TQ, 2 * HEAD_DIM)


FAR_TILES = 4


def _nsa_sel_win_kernel(q_ref, oc_ref, glog_ref, selt_ref, kst_ref, vs_ref, kwt_ref, vw_ref, near_ref,
                        exp_ref, gexp_ref, o_ref, seladd, m_s, acc_s, m_w, acc_w):
    i = pl.program_id(2)
    qs = _stack_heads(q_ref[...] * (HEAD_DIM ** -0.5)).astype(_BF16)
    sel = jnp.dot(selt_ref[...].T.astype(_BF16), exp_ref[...], preferred_element_type=_F32)
    seladd[...] = (sel - 1.0) * (-NEG_INF)
    for m_scr, acc_scr in ((m_s, acc_s), (m_w, acc_w)):
        m_scr[...] = jnp.full(m_scr.shape, NEG_INF, _F32)
        acc_scr[...] = jnp.zeros(acc_scr.shape, _F32)

    def keys(j, n_tiles):
        return pl.ds(pl.multiple_of(j * TK, TK), n_tiles * TK)

    def sel_step(j, n_tiles, bias):
        cols = keys(j, n_tiles)
        add = seladd[:, cols]
        if bias is not None:
            add = add + bias
        _flash_step(qs, kst_ref[:, cols], vs_ref[cols, :], add, m_s, acc_s)

    def win_step(j, n_tiles, add):
        cols = keys(j, n_tiles)
        _flash_step(qs, kwt_ref[:, cols], vw_ref[cols, :], add, m_w, acc_w)

    near = lambda lo: near_ref[:, lo * TK:].reshape(Q_PER_KV, TQ, (NEAR_TILES - lo) * TK)
    own = TQ // TK
    first = i * own

    n_far = jnp.maximum(first - 1, 0)

    def far_many(j, carry):
        sel_step(j * FAR_TILES, FAR_TILES, None)
        return carry

    def far_one(j, carry):
        sel_step(j, 1, None)
        return carry

    lax.fori_loop(0, n_far // FAR_TILES, far_many, 0)
    lax.fori_loop((n_far // FAR_TILES) * FAR_TILES, n_far, far_one, 0)

    @pl.when(i >= 1)
    def _():
        sel_step(first - 1, NEAR_TILES, near(0))
        win_step(first - 1, NEAR_TILES, near(0))

    @pl.when(i == 0)
    def _():
        sel_step(0, own, near(1))
        win_step(0, own, near(1))

    n_back = WINDOW // TK

    def back_step(start_tile, n_tiles, offset):
        add = None
        if offset < TQ - 1:
            r = lax.broadcasted_iota(jnp.int32, (TQ, n_tiles * TK), 0)
            kk = offset + lax.broadcasted_iota(jnp.int32, (TQ, n_tiles * TK), 1)
            add = jnp.where(kk >= r, 0.0, NEG_INF)
        win_step(start_tile, n_tiles, add)

    @pl.when(first >= n_back)
    def _():
        back_step(first - n_back, n_back - 1, 0)

    for i_small in range(1, -(-n_back // own)):
        if i_small * own > 1:
            @pl.when(i == i_small)
            def _():
                back_step(0, i_small * own - 1, (n_back - i_small * own) * TK)

    def finish(acc_scr):
        acc = acc_scr[...]
        return _unstack_heads(acc[..., :HEAD_DIM] / acc[..., HEAD_DIM:])

    gates = jnp.dot(jax.nn.sigmoid(glog_ref[...]), gexp_ref[...], preferred_element_type=_F32,
                    precision=lax.Precision.HIGHEST)
    width = Q_PER_KV * HEAD_DIM
    o_ref[...] = (gates[:, :width] * oc_ref[...] + gates[:, width:2 * width] * finish(acc_s)
                  + gates[:, 2 * width:] * finish(acc_w))


def _nsa_prompt_side(kv, prm):
    import numpy as np
    bsz, t_len = kv.shape[:2]
    n_cmp = (t_len - CMP_LEN) // CMP_STRIDE + 1
    n_cmp_pad = t_len // CMP_STRIDE
    assert n_cmp == n_cmp_pad - 1 and n_cmp_pad % LANES == 0 and t_len % TQ == 0
    n_sel = t_len // SEL_BLOCK
    grouped = lambda slot: kv[:, :, slot].reshape(bsz, n_cmp_pad, CMP_STRIDE * N_KV_HEADS * HEAD_DIM)
    one_page = jnp.arange(bsz, dtype=jnp.int32)[:, None]
    k_cmp = _cmp_pages(grouped(0), one_page, prm['cmp_pos_k'], prm['cmp_w1_k'], prm['cmp_w2_k'])
    v_cmp = _cmp_pages(grouped(1), one_page, prm['cmp_pos_v'], prm['cmp_w1_v'], prm['cmp_w2_v'])
    kvb = kv.astype(_BF16)
    t_last = lambda a: jnp.transpose(a, (0, 2, 3, 1))
    t_rows = lambda a: jnp.transpose(a, (0, 2, 1, 3))
    pad_c = lambda a: a.reshape(bsz, n_cmp_pad, N_KV_HEADS, HEAD_DIM)
    rel_bias = prm['rel_bias']
    c_start = np.arange(n_cmp_pad)[None, :] * CMP_STRIDE
    s_start = np.arange(n_sel)[:, None] * SEL_BLOCK
    ovt = ((c_start < s_start + SEL_BLOCK) & (c_start + CMP_LEN > s_start) & (np.arange(n_cmp_pad)[None, :] < n_cmp))
    expand = np.arange(n_sel)[:, None] == (np.arange(t_len)[None, :] // SEL_BLOCK)
    lanes = np.arange(LANES)[:, None]
    cols = np.arange(N_BRANCH * Q_PER_KV * HEAD_DIM)[None, :]
    gexp = lanes == (cols // HEAD_DIM % Q_PER_KV) * N_BRANCH + cols // (Q_PER_KV * HEAD_DIM)
    g_cmp, near = _bias_tiles(rel_bias, n_cmp_pad)
    with_ones = lambda a: jnp.concatenate([t_rows(a), jnp.ones((bsz, N_KV_HEADS, t_len, HEAD_DIM), _BF16)], axis=-1)
    return {
        'kct': t_last(pad_c(k_cmp)), 'vc': t_rows(pad_c(v_cmp)),
        'kst': t_last(kvb[:, :, 2]), 'vs': with_ones(kvb[:, :, 3]),
        'kwt': t_last(kvb[:, :, 4]), 'vw': with_ones(kvb[:, :, 5]),
        'ovt': jnp.asarray(ovt, _BF16), 'expand': jnp.asarray(expand, _BF16), 'gexp': jnp.asarray(gexp, _F32),
        'g_cmp': g_cmp, 'near': near,
    }


def _nsa_prompt_attention(q, glog, side):
    bsz, t_len, _ = q.shape
    n_t = t_len // TQ
    width = Q_PER_KV * HEAD_DIM
    n_cmp_pad = side['kct'].shape[-1]
    n_sel = side['ovt'].shape[0]
    grid = (bsz, N_KV_HEADS, n_t)
    params = pltpu.CompilerParams(dimension_semantics=("parallel", "parallel", "arbitrary"),
                                  vmem_limit_bytes=VMEM_LIMIT)
    q_spec = pl.BlockSpec((None, TQ, width), lambda b, h, i: (b, i, h))
    per_bh = lambda *shape: pl.BlockSpec((None, None) + shape, lambda b, h, i: (b, h, 0, 0))
    per_h = lambda *shape: pl.BlockSpec((None,) + shape, lambda b, h, i: (h,) + (0,) * len(shape))
    const = lambda *shape: pl.BlockSpec(shape, lambda b, h, i: (0,) * len(shape))
    selt_spec = pl.BlockSpec((None, None, n_sel, TQ), lambda b, h, i: (b, h, 0, i))
    o_c, selt = pl.pallas_call(
        _nsa_cmp_kernel,
        grid=grid,
        in_specs=[q_spec, per_bh(HEAD_DIM, n_cmp_pad), per_bh(n_cmp_pad, HEAD_DIM),
                  per_h(Q_PER_KV * TQ, n_cmp_pad), const(n_sel, n_cmp_pad)],
        out_specs=[q_spec, selt_spec],
        out_shape=[jax.ShapeDtypeStruct((bsz, t_len, N_HEADS * HEAD_DIM), _F32),
                   jax.ShapeDtypeStruct((bsz, N_KV_HEADS, n_sel, t_len), _F32)],
        compiler_params=params,
        name="nsa_cmp_select",
    )(q, side['kct'], side['vc'], side['g_cmp'], side['ovt'])
    stat = pltpu.VMEM((Q_PER_KV, TQ, 2 * HEAD_DIM), _F32)
    return pl.pallas_call(
        _nsa_sel_win_kernel,
        grid=grid,
        in_specs=[q_spec, q_spec, pl.BlockSpec((None, TQ, LANES), lambda b, h, i: (b, i, h)), selt_spec,
                  per_bh(HEAD_DIM, t_len), per_bh(t_len, 2 * HEAD_DIM), per_bh(HEAD_DIM, t_len),
                  per_bh(t_len, 2 * HEAD_DIM), per_h(Q_PER_KV * TQ, NEAR_TILES * TK), const(n_sel, t_len),
                  const(LANES, N_BRANCH * width)],
        out_specs=q_spec,
        out_shape=jax.ShapeDtypeStruct((bsz, t_len, N_HEADS * HEAD_DIM), _F32),
        scratch_shapes=[pltpu.VMEM((TQ, t_len), _F32), stat, stat, stat, stat],
        compiler_params=params,
        name="nsa_select_window",
    )(q, o_c, glog, selt, side['kst'], side['vs'], side['kwt'], side['vw'], side['near'],
      side['expand'], side['gexp'])


PAGES_PER_STEP = 4
GROUPS_PER_PAGE_ROWS = CMP_STRIDE


def _dot_nt(a, b):
    return lax.dot_general(a, b, (((1,), (1,)), ((), ())), preferred_element_type=_F32)


def _cmp_pages_kernel(pt_ref, *refs, n_pages):
    del pt_ref
    pages = refs[:n_pages]
    posa_ref, posb_ref, w1a_ref, w1b_ref, w2_ref, o_ref = refs[n_pages:]
    x = jnp.concatenate([pg[...] for pg in pages], axis=0)
    z0 = jnp.dot((x + posa_ref[...]).astype(_BF16), w1a_ref[...], preferred_element_type=_F32)
    z1 = jnp.dot((x + posb_ref[...]).astype(_BF16), w1b_ref[...], preferred_element_type=_F32)
    pre = z0 + pltpu.roll(z1, z1.shape[0] - 1, 0)
    o_ref[...] = jnp.dot(jax.nn.gelu(pre).astype(_BF16), w2_ref[...], preferred_element_type=_F32).astype(o_ref.dtype)


def _cmp_pages(grouped, page_table, pos_emb, w1, w2):
    n_seq, n_pages = page_table.shape
    groups_per_page, flat = grouped.shape[1:]
    width = N_KV_HEADS * HEAD_DIM
    hidden = w1.shape[1]
    eye = jnp.eye(N_KV_HEADS, dtype=_F32)
    w1r = w1.reshape(2, CMP_STRIDE, HEAD_DIM, hidden)
    bd1 = jnp.einsum('hk,abdf->abhdkf', eye, w1r).reshape(2, flat, N_KV_HEADS * hidden).astype(_BF16)
    bd2 = jnp.einsum('hk,fd->hfkd', eye, w2).reshape(N_KV_HEADS * hidden, width).astype(_BF16)
    pos_t = jnp.broadcast_to(pos_emb.reshape(2, CMP_STRIDE, 1, HEAD_DIM),
                             (2, CMP_STRIDE, N_KV_HEADS, HEAD_DIM)).reshape(2, 1, flat)
    n_rows = n_pages * groups_per_page
    page_spec = lambda k: pl.BlockSpec((None, groups_per_page, flat), lambda b, pt: (pt[b, k], 0, 0))
    const = lambda *shape: pl.BlockSpec(shape, lambda b, pt: (0,) * len(shape))
    return pl.pallas_call(
        functools.partial(_cmp_pages_kernel, n_pages=n_pages),
        grid_spec=pltpu.PrefetchScalarGridSpec(
            num_scalar_prefetch=1,
            grid=(n_seq,),
            in_specs=[page_spec(k) for k in range(n_pages)] + [
                const(1, flat), const(1, flat), const(flat, N_KV_HEADS * hidden), const(flat, N_KV_HEADS * hidden),
                const(N_KV_HEADS * hidden, width)],
            out_specs=pl.BlockSpec((None, n_rows, width), lambda b, pt: (b, 0, 0)),
        ),
        out_shape=jax.ShapeDtypeStruct((n_seq, n_rows, width), _BF16),
        compiler_params=pltpu.CompilerParams(dimension_semantics=("parallel",), vmem_limit_bytes=VMEM_LIMIT),
        name="cmp_pages",
    )(page_table, *([grouped] * n_pages), pos_t[0], pos_t[1], bd1[0], bd1[1], bd2)


def _pick_own_kv_head(x, n_q):
    rows_per = Q_PER_KV * n_q
    return jnp.concatenate([x[h * rows_per:(h + 1) * rows_per, h * HEAD_DIM:(h + 1) * HEAD_DIM]
                            for h in range(N_KV_HEADS)], axis=0)


def _nsa_sample_kernel(pt_ref, q_ref, gl_ref, kc_ref, vc_ref, win_ref, new_ref, *rest, n_q, past_len, n_sel):
    del pt_ref
    pages = rest[:PAGES_PER_STEP]
    (tcmp_ref, tsel_ref, twin_ref, exp_ref, ov_ref, o_ref,
     addsel, m_scr, l_scr, acc_scr, oc_scr, ow_scr) = rest[PAGES_PER_STEP:]
    j = pl.program_id(1)
    rows = N_HEADS * n_q
    width = N_KV_HEADS * HEAD_DIM
    w_buf = win_ref.shape[1]
    step_keys = PAGES_PER_STEP * pages[0].shape[1]
    rr = lax.broadcasted_iota(jnp.int32, (rows, width), 0)
    cc = lax.broadcasted_iota(jnp.int32, (rows, width), 1)
    q4 = jnp.concatenate([q_ref[...] * (HEAD_DIM ** -0.5)] * N_KV_HEADS, axis=1)
    qbd = jnp.where(rr // (Q_PER_KV * n_q) == cc // HEAD_DIM, q4, 0.0).astype(_BF16)
    new_rows = jnp.concatenate([new_ref[...], jnp.zeros((LANES - n_q, new_ref.shape[1]), _F32)], axis=0).astype(_BF16)

    @pl.when(j == 0)
    def _first():
        s = _dot_nt(qbd, kc_ref[...]) + tcmp_ref[...]
        p = jnp.exp(s - jnp.max(s, axis=-1, keepdims=True))
        pb = (p / jnp.sum(p, axis=-1, keepdims=True)).astype(_BF16)
        oc_scr[...] = _pick_own_kv_head(jnp.dot(pb, vc_ref[...], preferred_element_type=_F32), n_q)
        pm = jnp.dot(pb, ov_ref[...], preferred_element_type=_F32)
        imp = jnp.concatenate(
            [sum(pm[(h * Q_PER_KV + g) * n_q:(h * Q_PER_KV + g + 1) * n_q] for g in range(Q_PER_KV))
             for h in range(N_KV_HEADS)], axis=0)
        blk = lax.broadcasted_iota(jnp.int32, imp.shape, 1)
        t = past_len + lax.broadcasted_iota(jnp.int32, imp.shape, 0) % n_q
        cur = t // SEL_BLOCK
        forced = (blk == 0) | (blk == cur) | (blk == cur - 1)
        score = jnp.where(forced, FORCE, jnp.where(blk * SEL_BLOCK <= t, imp, -FORCE))
        score = jnp.where(blk < n_sel, score, -jnp.inf)
        rank = jnp.zeros(imp.shape, jnp.int32)
        for sp in range(n_sel):
            col = score[:, sp:sp + 1]
            rank += ((col > score) | ((col == score) & (blk > sp))).astype(jnp.int32)
        sel = (rank < min(SEL_TOPK, n_sel)).astype(_BF16)
        sel_rows = jnp.concatenate([sel[h * n_q:(h + 1) * n_q] for h in range(N_KV_HEADS) for _ in range(Q_PER_KV)],
                                   axis=0)
        addsel[...] = (jnp.dot(sel_rows, exp_ref[...], preferred_element_type=_F32) - 1.0) * (-NEG_INF) + tsel_ref[...]
        kw_t = win_ref[:width, :].astype(_BF16)
        vw_t = win_ref[width:, :].astype(_BF16)
        s1 = jnp.dot(qbd, kw_t, preferred_element_type=_F32) + twin_ref[:, :w_buf]
        s2 = _dot_nt(qbd, new_rows[:, 2 * width:3 * width]) + twin_ref[:, w_buf:]
        m = jnp.maximum(jnp.max(s1, axis=-1, keepdims=True), jnp.max(s2, axis=-1, keepdims=True))
        p1 = jnp.exp(s1 - m)
        p2 = jnp.exp(s2 - m)
        l = jnp.sum(p1, axis=-1, keepdims=True) + jnp.sum(p2, axis=-1, keepdims=True)
        ow = (_dot_nt(p1.astype(_BF16), vw_t)
              + jnp.dot(p2.astype(_BF16), new_rows[:, 3 * width:], preferred_element_type=_F32))
        ow_scr[...] = _pick_own_kv_head(ow / l, n_q)
        m_scr[...] = jnp.full(m_scr.shape, NEG_INF, _F32)
        l_scr[...] = jnp.zeros(l_scr.shape, _F32)
        acc_scr[...] = jnp.zeros(acc_scr.shape, _F32)

    def flash(s, pv):
        m_old = m_scr[...]
        m_new = jnp.maximum(m_old, jnp.max(s, axis=-1, keepdims=True))
        alpha = jnp.exp(m_old - m_new)
        p = jnp.exp(s - m_new)
        m_scr[...] = m_new
        l_scr[...] = alpha * l_scr[...] + jnp.sum(p, axis=-1, keepdims=True)
        acc_scr[...] = alpha * acc_scr[...] + pv(p.astype(_BF16))

    k_t = jnp.concatenate([pg[:width, :] for pg in pages], axis=1).astype(_BF16)
    v_t = jnp.concatenate([pg[width:, :] for pg in pages], axis=1).astype(_BF16)
    flash(jnp.dot(qbd, k_t, preferred_element_type=_F32)
          + addsel[:, pl.ds(pl.multiple_of(j * step_keys, step_keys), step_keys)], lambda p: _dot_nt(p, v_t))

    @pl.when(j == pl.num_programs(1) - 1)
    def _last():
        flash(_dot_nt(qbd, new_rows[:, :width]) + addsel[:, past_len:],
              lambda p: jnp.dot(p, new_rows[:, width:2 * width], preferred_element_type=_F32))
        o_s = _pick_own_kv_head(acc_scr[...] / l_scr[...], n_q)
        gates = jax.nn.sigmoid(gl_ref[...])
        o_ref[...] = gates[:, 0:1] * oc_scr[...] + gates[:, 1:2] * o_s + gates[:, 2:3] * ow_scr[...]


def _nsa_sample_side(cache_kv, page_table, cache_win, kv_new, rel_bias, prm):
    import numpy as np
    n_seq, n_pages = page_table.shape
    n_phys, page_rows = cache_kv.shape[:2]
    n_q = kv_new.shape[1]
    width = N_KV_HEADS * HEAD_DIM
    past_len = n_pages * page_rows
    w_buf = cache_win.shape[1]
    tk = past_len + n_q
    n_cmp = (tk - CMP_LEN) // CMP_STRIDE + 1
    n_cmp_pad = past_len // CMP_STRIDE
    n_sel = -(-tk // SEL_BLOCK)
    assert n_cmp == n_cmp_pad - 1 and n_cmp_pad == LANES and n_sel <= LANES and n_q <= LANES
    assert page_rows % SEL_BLOCK == 0 and n_pages % PAGES_PER_STEP == 0 and w_buf == min(WINDOW, past_len)
    cache_t = jnp.transpose(cache_kv, (0, 2, 3, 4, 1)).reshape(n_phys, KV_SLOTS_PAGED * width, page_rows)
    win_t = jnp.transpose(cache_win, (0, 2, 3, 4, 1)).reshape(n_seq, KV_SLOTS_WIN * width, w_buf)
    gpp = page_rows // CMP_STRIDE
    grouped = cache_kv[:, :, :2].reshape(n_phys, gpp, CMP_STRIDE, 2, width)
    grouped = jnp.transpose(grouped, (3, 0, 1, 2, 4)).reshape(2, n_phys, gpp, CMP_STRIDE * width)
    kc = _cmp_pages(grouped[0], page_table, prm['cmp_pos_k'], prm['cmp_w1_k'], prm['cmp_w2_k'])
    vc = _cmp_pages(grouped[1], page_table, prm['cmp_pos_v'], prm['cmp_w1_v'], prm['cmp_w2_v'])
    rows = N_HEADS * n_q
    t = past_len + np.arange(n_q)[:, None]
    masked = lambda ok, d: jnp.where(jnp.asarray(ok), _bias_by_distance(rel_bias, d), NEG_INF).reshape(rows, -1)
    n = np.arange(n_cmp_pad)[None, :]
    d_c = t - (n * CMP_STRIDE + CMP_LEN - 1)
    t_cmp = masked((d_c >= 0) & (n < n_cmp), d_c)
    pos = np.arange(past_len + LANES)[None, :]
    t_sel = masked((t - pos >= 0) & (pos < tk), t - pos)
    l = np.arange(w_buf + LANES)[None, :]
    d_w = t - (past_len - w_buf + l)
    t_win = masked((d_w >= 0) & (d_w <= WINDOW) & (l < w_buf + n_q), d_w)
    expand = (np.arange(LANES)[:, None] == pos // SEL_BLOCK) & (np.arange(LANES)[:, None] < n_sel)
    c_start = np.arange(n_cmp_pad)[:, None] * CMP_STRIDE
    s_start = np.arange(LANES)[None, :] * SEL_BLOCK
    ov = ((c_start < s_start + SEL_BLOCK) & (c_start + CMP_LEN > s_start) & (np.arange(n_cmp_pad)[:, None] < n_cmp)
          & (np.arange(LANES)[None, :] < n_sel))
    return {
        'cache_t': cache_t, 'page_table': page_table, 'kc': kc, 'vc': vc, 'win_t': win_t,
        'new': kv_new.reshape(n_seq, n_q, -1)[:, :, 2 * width:],
        't_cmp': t_cmp, 't_sel': t_sel, 't_win': t_win,
        'expand': jnp.asarray(expand, _BF16), 'ov': jnp.asarray(ov, _BF16), 'n_sel': n_sel, 'past_len': past_len,
    }


def _nsa_sample_attention(q, glog, side):
    n_seq, n_q, _ = q.shape
    glog = glog.reshape(n_seq, n_q, N_KV_HEADS, LANES)[..., :Q_PER_KV * N_BRANCH]
    rows = N_HEADS * n_q
    width = N_KV_HEADS * HEAD_DIM
    page_table = side['page_table']
    n_pages = page_table.shape[1]
    page_rows = side['cache_t'].shape[2]
    past_len = side['past_len']
    w_buf = side['win_t'].shape[2]
    n_steps = n_pages // PAGES_PER_STEP
    q_rows = jnp.transpose(q.reshape(n_seq, n_q, N_HEADS, HEAD_DIM), (0, 2, 1, 3)).reshape(n_seq, rows, HEAD_DIM)
    gl = jnp.transpose(glog.reshape(n_seq, n_q, N_HEADS, N_BRANCH), (0, 2, 1, 3)).reshape(n_seq, rows, N_BRANCH)
    gl = jnp.pad(gl, ((0, 0), (0, 0), (0, LANES - N_BRANCH)))
    per_seq = lambda *shape: pl.BlockSpec((None,) + shape, lambda b, j, pt: (b,) + (0,) * len(shape))
    const = lambda *shape: pl.BlockSpec(shape, lambda b, j, pt: (0,) * len(shape))
    page_spec = lambda k: pl.BlockSpec((None, 2 * width, page_rows),
                                       lambda b, j, pt: (pt[b, j * PAGES_PER_STEP + k], 1, 0))
    out = pl.pallas_call(
        functools.partial(_nsa_sample_kernel, n_q=n_q, past_len=past_len, n_sel=side['n_sel']),
        grid_spec=pltpu.PrefetchScalarGridSpec(
            num_scalar_prefetch=1,
            grid=(n_seq, n_steps),
            in_specs=[per_seq(rows, HEAD_DIM), per_seq(rows, LANES), per_seq(LANES, width), per_seq(LANES, width),
                      per_seq(2 * width, w_buf), per_seq(n_q, 4 * width)]
            + [page_spec(k) for k in range(PAGES_PER_STEP)]
            + [const(rows, LANES), const(rows, past_len + LANES), const(rows, w_buf + LANES),
               const(LANES, past_len + LANES), const(LANES, LANES)],
            out_specs=per_seq(rows, HEAD_DIM),
            scratch_shapes=[pltpu.VMEM((rows, past_len + LANES), _F32), pltpu.VMEM((rows, 1), _F32),
                            pltpu.VMEM((rows, 1), _F32), pltpu.VMEM((rows, width), _F32),
                            pltpu.VMEM((rows, HEAD_DIM), _F32), pltpu.VMEM((rows, HEAD_DIM), _F32)],
        ),
        out_shape=jax.ShapeDtypeStruct((n_seq, rows, HEAD_DIM), _F32),
        compiler_params=pltpu.CompilerParams(dimension_semantics=("parallel", "arbitrary"),
                                             vmem_limit_bytes=VMEM_LIMIT),
        name="nsa_sample",
    )(page_table, q_rows, gl, side['kc'], side['vc'], side['win_t'], side['new'],
      *([side['cache_t']] * PAGES_PER_STEP), side['t_cmp'], side['t_sel'], side['t_win'], side['expand'], side['ov'])
    return jnp.transpose(out.reshape(n_seq, N_HEADS, n_q, HEAD_DIM), (0, 2, 1, 3)).reshape(n_seq, n_q, N_HEADS * HEAD_DIM)


def _run_trunk(x, p, s0_re, s0_im, paged, win_buf, prm):
    bsz, t_len, _ = x.shape
    ssm_re, ssm_im = [], []
    side = None
    kv_rows_new = None
    win_state = None
    y = None
    x2 = x.reshape(-1, D_MODEL)
    as_seq = lambda a: a.reshape(bsz, t_len, a.shape[-1])
    for i in range(DEPTH):
        if i < N_A_LAYERS:
            u = as_seq(_rmsnorm(x2, prm['g_mix'][i]))
            ys, s_re, s_im = _s5_scan(u, s0_re[i], s0_im[i], prm['ssm_a_re'][i], prm['ssm_a_im'][i],
                                      prm['ssm_log_dt'][i], prm['ssm_b_re'][i], prm['ssm_b_im'][i],
                                      prm['ssm_c_re'][i], prm['ssm_c_im'][i], prm['ssm_d'][i])
            ssm_re.append(s_re)
            ssm_im.append(s_im)
            x2 = _glu_residual(x2, ys.reshape(-1, D_MODEL), prm['w_glu'][i])
        else:
            j = i - N_A_LAYERS
            q, glog = _qg_project(x2, prm['g_mix'][i], prm['w_qg'][j])
            attend = _nsa_prompt_attention if paged is None else _nsa_sample_attention
            mixed = attend(as_seq(q), as_seq(glog), side)
            x2 = _wo_residual(x2, mixed.reshape(-1, D_MODEL), prm['w_o'][j])
        x2 = _moe_layer(x2, prm['g_ffn'][i], prm['w_route_group'][i], prm['b_route_group'][i],
                        prm['w_route_expert'][i], prm['b_route_expert'][i], prm['w_exp_up'][i],
                        prm['w_exp_down'][i])
        ple = (x2, p[i].reshape(-1, p.shape[-1]), prm['g_ple'][i], prm['w_ple_gate'][i], prm['w_ple_proj'][i])
        if i == N_A_LAYERS - 1:
            x2, kv2 = _ple_residual(*ple, g_next=prm['g_kv'], w_next=prm['w_kv'])
            kv = kv2.reshape(bsz, t_len, KV_SLOTS_PAGED + KV_SLOTS_WIN, N_KV_HEADS, HEAD_DIM)
            kv_rows_new, win_new = kv[:, :, :KV_SLOTS_PAGED], kv[:, :, KV_SLOTS_PAGED:]
            if paged is None:
                win_state = win_new[:, -min(WINDOW, t_len):]
                side = _nsa_prompt_side(kv, prm)
            else:
                w_buf = win_buf.shape[1]
                win_state = jnp.concatenate([win_buf, win_new], axis=1)[:, -w_buf:]
                side = _nsa_sample_side(paged[0], paged[1], win_buf, kv, prm['rel_bias'], prm)
        elif i == DEPTH - 1:
            x2, y = _ple_residual(*ple, g_next=prm['g_final'])
        else:
            x2, = _ple_residual(*ple)
    return as_seq(y), kv_rows_new, win_state, jnp.stack(ssm_re), jnp.stack(ssm_im)


def kernel(x_prompt, x_sample, p_prompt, p_sample, cache_kv, cache_win, state_ssm_re, state_ssm_im, page_table,
           g_mix, g_ffn, g_ple, g_kv, g_final,
           ssm_a_re, ssm_a_im, ssm_log_dt, ssm_b_re, ssm_b_im, ssm_c_re, ssm_c_im, ssm_d, w_glu,
           w_kv, cmp_pos_k, cmp_pos_v, cmp_w1_k, cmp_w2_k, cmp_w1_v, cmp_w2_v, w_qg, w_o, rel_bias,
           w_route_group, b_route_group, w_route_expert, b_route_expert, w_exp_up, w_exp_down,
           w_ple_proj, w_ple_gate):
    prm = {
        'g_mix': g_mix, 'g_ffn': g_ffn, 'g_ple': g_ple, 'g_kv': g_kv, 'g_final': g_final,
        'ssm_a_re': ssm_a_re, 'ssm_a_im': ssm_a_im, 'ssm_log_dt': ssm_log_dt,
        'ssm_b_re': ssm_b_re, 'ssm_b_im': ssm_b_im, 'ssm_c_re': ssm_c_re, 'ssm_c_im': ssm_c_im,
        'ssm_d': ssm_d, 'w_glu': w_glu,
        'w_kv': w_kv, 'cmp_pos_k': cmp_pos_k, 'cmp_pos_v': cmp_pos_v,
        'cmp_w1_k': cmp_w1_k, 'cmp_w2_k': cmp_w2_k, 'cmp_w1_v': cmp_w1_v, 'cmp_w2_v': cmp_w2_v,
        'w_qg': w_qg, 'w_o': w_o, 'rel_bias': rel_bias,
        'w_route_group': w_route_group, 'b_route_group': b_route_group,
        'w_route_expert': w_route_expert, 'b_route_expert': b_route_expert,
        'w_exp_up': w_exp_up, 'w_exp_down': w_exp_down,
        'w_ple_proj': w_ple_proj, 'w_ple_gate': w_ple_gate,
    }
    zero_state = jnp.zeros((N_A_LAYERS, x_prompt.shape[0], SSM_GROUPS, SSM_STATE), _F32)
    y_prompt, kv_prompt, win_prompt, ssm_re_prompt, ssm_im_prompt = _run_trunk(
        x_prompt, p_prompt, zero_state, zero_state, None, None, prm)
    y_sample, kv_sample, win_sample, ssm_re_sample, ssm_im_sample = _run_trunk(
        x_sample, p_sample, state_ssm_re, state_ssm_im, (cache_kv, page_table), cache_win, prm)
    return (y_prompt, y_sample, kv_prompt, win_prompt, ssm_re_prompt, ssm_im_prompt,
            kv_sample, win_sample, ssm_re_sample, ssm_im_sample)
```

```python
import functools
import math

import jax
import jax.numpy as jnp
from jax import lax
from jax.experimental import pallas as pl
from jax.experimental.pallas import tpu as pltpu

D_MODEL = 1024
DEPTH = 4
N_A_LAYERS = DEPTH // 2
SSM_GROUP = 16
SSM_GROUPS = D_MODEL // SSM_GROUP
SSM_STATE = 64
N_HEADS = 16
HEAD_DIM = D_MODEL // N_HEADS
N_KV_HEADS = 4
Q_PER_KV = N_HEADS // N_KV_HEADS
CMP_LEN = 32
CMP_STRIDE = 16
SEL_BLOCK = 64
SEL_TOPK = 16
WINDOW = 512
N_BRANCH = 3
KV_SLOTS_PAGED = 4
KV_SLOTS_WIN = 2
Q_BLOCK = 64
N_BUCKETS = 32
MAX_DISTANCE = 128
N_EXPERT_GROUPS = 4
EXPERTS_PER_GROUP = 4
N_EXPERTS = N_EXPERT_GROUPS * EXPERTS_PER_GROUP
EXPERT_TOPK = 2
D_EXPERT = 256
RMS_EPS = 1e-6
NEG_INF = -1e30
FORCE = 1e4

LANES = 128
VMEM_LIMIT = 48 * 1024 * 1024

_F32 = jnp.float32
_BF16 = jnp.bfloat16


def _moe_kernel(x_ref, g_ref, wr_ref, br_ref, wup_ref, wdn_ref, o_ref, h_scr, comb_scr, acc_scr):
    e = pl.program_id(1)
    lane = lax.broadcasted_iota(jnp.int32, comb_scr.shape, 1)

    @pl.when(e == 0)
    def _route():
        x = x_ref[...]
        h = x * lax.rsqrt(jnp.mean(x * x, axis=-1, keepdims=True) + RMS_EPS) * g_ref[...]
        h_scr[...] = h.astype(_BF16)
        logits = jnp.dot(h, wr_ref[...], preferred_element_type=_F32,
                         precision=lax.Precision.HIGHEST) + br_ref[...]
        is_grp = (lane >= N_EXPERTS) & (lane < N_EXPERTS + N_EXPERT_GROUPS)
        lg = jnp.where(is_grp, logits, -jnp.inf)
        gmax = jnp.max(lg, axis=-1, keepdims=True)
        gi = jnp.min(jnp.where(lg == gmax, lane, LANES), axis=-1, keepdims=True) - N_EXPERTS
        gp = 1.0 / jnp.sum(jnp.where(is_grp, jnp.exp(lg - gmax), 0.0), axis=-1, keepdims=True)
        in_grp = (lane < N_EXPERTS) & ((lane // EXPERTS_PER_GROUP) == gi)
        le = jnp.where(in_grp, logits, -jnp.inf)
        m1 = jnp.max(le, axis=-1, keepdims=True)
        i1 = jnp.min(jnp.where(le == m1, lane, LANES), axis=-1, keepdims=True)
        le2 = jnp.where(lane == i1, -jnp.inf, le)
        m2 = jnp.max(le2, axis=-1, keepdims=True)
        i2 = jnp.min(jnp.where(le2 == m2, lane, LANES), axis=-1, keepdims=True)
        e2 = jnp.exp(m2 - m1)
        den = 1.0 + e2
        comb_scr[...] = jnp.where(lane == i1, gp / den, jnp.where(lane == i2, gp * e2 / den, 0.0))
        acc_scr[...] = x

    up = jnp.dot(h_scr[...], wup_ref[0].astype(_BF16), preferred_element_type=_F32)
    a = up[:, :D_EXPERT]
    b = up[:, D_EXPERT:]
    c = jnp.sum(jnp.where(lane == e, comb_scr[...], 0.0), axis=-1, keepdims=True)
    act = (a * jax.nn.sigmoid(a)) * b * c
    acc_scr[...] += jnp.dot(act.astype(_BF16), wdn_ref[0].astype(_BF16), preferred_element_type=_F32)

    @pl.when(e == N_EXPERTS - 1)
    def _store():
        o_ref[...] = acc_scr[...]


def _moe_layer(x2, g, w_rg, b_rg, w_re, b_re, w_up, w_dn):
    n_tok, d = x2.shape
    tm = min(n_tok, 1024)
    wr = jnp.zeros((d, LANES), _F32).at[:, :N_EXPERTS].set(w_re).at[:, N_EXPERTS:N_EXPERTS + N_EXPERT_GROUPS].set(w_rg)
    br = jnp.zeros((1, LANES), _F32).at[0, :N_EXPERTS].set(b_re).at[0, N_EXPERTS:N_EXPERTS + N_EXPERT_GROUPS].set(b_rg)
    return pl.pallas_call(
        _moe_kernel,
        grid=(n_tok // tm, N_EXPERTS),
        in_specs=[
            pl.BlockSpec((tm, d), lambda i, e: (i, 0)),
            pl.BlockSpec((1, d), lambda i, e: (0, 0)),
            pl.BlockSpec((d, LANES), lambda i, e: (0, 0)),
            pl.BlockSpec((1, LANES), lambda i, e: (0, 0)),
            pl.BlockSpec((1, d, 2 * D_EXPERT), lambda i, e: (e, 0, 0)),
            pl.BlockSpec((1, D_EXPERT, d), lambda i, e: (e, 0, 0)),
        ],
        out_specs=pl.BlockSpec((tm, d), lambda i, e: (i, 0)),
        out_shape=jax.ShapeDtypeStruct((n_tok, d), _F32),
        scratch_shapes=[
            pltpu.VMEM((tm, d), _BF16),
            pltpu.VMEM((tm, LANES), _F32),
            pltpu.VMEM((tm, d), _F32),
        ],
        compiler_params=pltpu.CompilerParams(
            dimension_semantics=("parallel", "arbitrary"), vmem_limit_bytes=VMEM_LIMIT),
        name="hmoe",
    )(x2, g.reshape(1, d), wr, br, w_up, w_dn)


ROW_TILE = 512


def _rows_kernel(*refs, body, n_in):
    outs = body(*[r[...] for r in refs[:n_in]])
    for o_ref, o in zip(refs[n_in:], outs):
        o_ref[...] = o


def _rows_call(body, rows, consts, out_widths, name):
    n_rows = rows[0].shape[0]
    tm = min(n_rows, ROW_TILE)
    row_spec = lambda width: pl.BlockSpec((tm, width), lambda i: (i, 0))
    const_spec = lambda c: pl.BlockSpec(c.shape, lambda i: (0,) * c.ndim)
    return pl.pallas_call(
        functools.partial(_rows_kernel, body=body, n_in=len(rows) + len(consts)),
        grid=(n_rows // tm,),
        in_specs=[row_spec(r.shape[1]) for r in rows] + [const_spec(c) for c in consts],
        out_specs=[row_spec(w) for w in out_widths],
        out_shape=[jax.ShapeDtypeStruct((n_rows, w), _F32) for w in out_widths],
        compiler_params=pltpu.CompilerParams(dimension_semantics=("parallel",), vmem_limit_bytes=VMEM_LIMIT),
        name=name,
    )(*rows, *consts)


def _norm(x, g):
    return x * lax.rsqrt(jnp.mean(x * x, axis=-1, keepdims=True) + RMS_EPS) * g


def _mm(x, w):
    return jnp.dot(x.astype(_BF16), w, preferred_element_type=_F32)


def _glu_residual(x2, y2, w_glu):
    def body(x, y, w):
        ag = _mm(jax.nn.gelu(y), w)
        return (x + ag[:, :D_MODEL] * jax.nn.sigmoid(ag[:, D_MODEL:]),)
    return _rows_call(body, [x2, y2], [w_glu.astype(_BF16)], [D_MODEL], "glu_residual")[0]


def _qg_project(x2, g, w_qg):
    per_kv = Q_PER_KV * N_BRANCH
    w_g = w_qg[:, N_HEADS * HEAD_DIM:].reshape(D_MODEL, N_KV_HEADS, per_kv)
    w_g = jnp.pad(w_g, ((0, 0), (0, 0), (0, LANES - per_kv))).reshape(D_MODEL, N_KV_HEADS * LANES)
    w = jnp.concatenate([w_qg[:, :N_HEADS * HEAD_DIM], w_g], axis=1).astype(_BF16)

    def body(x, gain, w):
        qg = _mm(_norm(x, gain), w)
        return qg[:, :N_HEADS * HEAD_DIM], qg[:, N_HEADS * HEAD_DIM:]
    return _rows_call(body, [x2], [g.reshape(1, -1), w], [N_HEADS * HEAD_DIM, N_KV_HEADS * LANES], "qg_project")


def _wo_residual(x2, mixed2, w_o):
    body = lambda x, m, w: (x + _mm(m, w),)
    return _rows_call(body, [x2, mixed2], [w_o.astype(_BF16)], [D_MODEL], "wo_residual")[0]


def _ple_residual(x2, p2, g_ple, w_gate, w_proj, g_next=None, w_next=None):
    consts = [g_ple.reshape(1, -1), w_gate.astype(_BF16), w_proj.astype(_BF16)]
    widths = [D_MODEL]
    if g_next is not None:
        consts.append(g_next.reshape(1, -1))
        widths.append(D_MODEL if w_next is None else w_next.shape[1])
    if w_next is not None:
        consts.append(w_next.astype(_BF16))

    def body(x, p, gain, wg, wp, *nxt):
        out = x + _mm(p, wp) * jax.nn.sigmoid(_mm(_norm(x, gain), wg))
        if not nxt:
            return (out,)
        normed = _norm(out, nxt[0])
        return out, (normed if len(nxt) == 1 else _mm(normed, nxt[1]))
    return _rows_call(body, [x2, p2], consts, widths, "ple_residual")


def _cmul(ar, ai, br, bi):
    return ar * br - ai * bi, ar * bi + ai * br


S5_TILE_GROUPS = LANES // SSM_GROUP
S5_TILES = D_MODEL // LANES
S5_STATE_LANES = S5_TILE_GROUPS * SSM_STATE


def _s5_operators(a_re, a_im, log_dt, b_re, b_im, c_re, c_im, chunk):
    hp = lax.Precision.HIGHEST
    dt = jnp.exp(log_dt)[:, None]
    decay = jnp.exp(a_re * dt)
    ab_re, ab_im = decay * jnp.cos(a_im * dt), decay * jnp.sin(a_im * dt)
    den = a_re * a_re + a_im * a_im
    num_re = ab_re - 1.0
    q_re = (num_re * a_re + ab_im * a_im) / den
    q_im = (ab_im * a_re - num_re * a_im) / den
    bb_re, bb_im = _cmul(q_re[..., None], q_im[..., None], b_re, b_im)
    pw_re, pw_im = [jnp.ones_like(ab_re)], [jnp.zeros_like(ab_im)]
    for _ in range(chunk):
        nr, ni = _cmul(pw_re[-1], pw_im[-1], ab_re, ab_im)
        pw_re.append(nr)
        pw_im.append(ni)
    pw_re, pw_im = jnp.stack(pw_re), jnp.stack(pw_im)
    w_re, w_im = _cmul(pw_re[:chunk, :, :, None], pw_im[:chunk, :, :, None], bb_re[None], bb_im[None])
    k_lag = (jnp.einsum('gcp,jgpd->jgcd', c_re, w_re, precision=hp)
             - jnp.einsum('gcp,jgpd->jgcd', c_im, w_im, precision=hp))
    ca_re, ca_im = _cmul(c_re[None], c_im[None], pw_re[1:, :, None, :], pw_im[1:, :, None, :])
    eye = jnp.eye(S5_TILE_GROUPS, dtype=_F32)
    tiled = lambda a: a.reshape((chunk, S5_TILES, S5_TILE_GROUPS) + a.shape[2:])
    k_bd = jnp.einsum('lthcd,hk->tlhdkc', tiled(k_lag), eye).reshape(S5_TILES, chunk, LANES, LANES)
    h_bd = jnp.concatenate(
        [jnp.einsum('lthpd,hk->tlhdkp', tiled(w), eye).reshape(S5_TILES, chunk, LANES, S5_STATE_LANES)
         for w in (w_re, w_im)], axis=-1)
    g_bd = jnp.concatenate(
        [jnp.einsum('lthcp,hk->tlhpkc', tiled(c), eye).reshape(S5_TILES, chunk, S5_STATE_LANES, LANES)
         for c in (ca_re, -ca_im)], axis=2)
    a_pow = jnp.concatenate([pw_re[chunk].reshape(S5_TILES, 1, S5_STATE_LANES),
                             pw_im[chunk].reshape(S5_TILES, 1, S5_STATE_LANES)], axis=-1)
    return k_bd.astype(_BF16), h_bd, g_bd.astype(_BF16), a_pow


def _advance(a_pow, s, e):
    half = S5_STATE_LANES
    ar, ai, sr, si = a_pow[:, :half], a_pow[:, half:], s[:, :half], s[:, half:]
    return jnp.concatenate([ar * sr - ai * si + e[:, :half], ar * si + ai * sr + e[:, half:]], axis=1)


def _s5_kernel(u_ref, k_ref, h_ref, g_ref, a_ref, d_ref, s0_ref, y_ref, sfin_ref, upad, e_scr, sin_scr, *,
               chunk, carry):
    n_rows = u_ref.shape[0]
    n_chunks = n_rows // chunk
    u = u_ref[...]
    upad[:chunk] = jnp.zeros((chunk, LANES), _F32)
    upad[chunk:] = u
    step = lax.broadcasted_iota(jnp.int32, (n_rows, LANES), 0) % chunk
    y = d_ref[...] * u
    for lag in range(chunk):
        part = jnp.dot(upad[chunk - lag:chunk - lag + n_rows].astype(_BF16), k_ref[lag], preferred_element_type=_F32)
        y += part if lag == 0 else jnp.where(step >= lag, part, 0.0)
    y_ref[...] = y
    e = None
    for lag in range(chunk):
        rows = u_ref[pl.ds(chunk - 1 - lag, n_chunks, stride=chunk), :]
        if h_ref.dtype == _F32:
            d = jnp.dot(rows, h_ref[lag], preferred_element_type=_F32, precision=lax.Precision.HIGHEST)
        else:
            d = jnp.dot(rows.astype(_BF16), h_ref[lag], preferred_element_type=_F32)
        e = d if e is None else e + d
    a_pow = a_ref[...]
    if carry:
        e_scr[...] = e
        sub = 8

        def body(jj, s):
            rows = pl.ds(pl.multiple_of(jj * sub, sub), sub)
            e_tile = e_scr[rows, :]
            starts = []
            for r in range(sub):
                starts.append(s)
                s = _advance(a_pow, s, e_tile[r:r + 1])
            sin_scr[rows, :] = jnp.concatenate(starts, axis=0)
            return s
        sfin_ref[...] = lax.fori_loop(0, n_chunks // sub, body, s0_ref[...])
    else:
        sin_scr[...] = s0_ref[...]
        sfin_ref[...] = _advance(a_pow, s0_ref[...], e)
    s_in = sin_scr[...].astype(_BF16)
    for k in range(chunk):
        rows = pl.ds(k, n_chunks, stride=chunk)
        y_ref[rows, :] = y_ref[rows, :] + jnp.dot(s_in, g_ref[k], preferred_element_type=_F32)


def _s5_scan(u, s0, carry, a_re, a_im, log_dt, b_re, b_im, c_re, c_im, d_skip):
    n_b, n_rows, _ = u.shape
    chunk = 16 if carry else 8
    n_chunks = n_rows // chunk
    n_state = s0.shape[2]
    k_bd, h_bd, g_bd, a_pow = _s5_operators(a_re, a_im, log_dt, b_re, b_im, c_re, c_im, chunk)
    if carry:
        h_bd = h_bd.astype(_BF16)
    per_tile = lambda *shape: pl.BlockSpec((None,) + shape, lambda t, b: (t,) + (0,) * len(shape))
    rows_spec = pl.BlockSpec((None, n_rows, LANES), lambda t, b: (b, 0, t))
    state_spec = pl.BlockSpec((None, None, n_state, 2 * S5_STATE_LANES), lambda t, b: (b, t, 0, 0))
    return pl.pallas_call(
        functools.partial(_s5_kernel, chunk=chunk, carry=carry),
        grid=(S5_TILES, n_b),
        in_specs=[rows_spec, per_tile(chunk, LANES, LANES), per_tile(chunk, LANES, 2 * S5_STATE_LANES),
                  per_tile(chunk, 2 * S5_STATE_LANES, LANES), per_tile(1, 2 * S5_STATE_LANES), per_tile(1, LANES),
                  state_spec],
        out_specs=[rows_spec, state_spec],
        out_shape=[jax.ShapeDtypeStruct(u.shape, _F32), jax.ShapeDtypeStruct(s0.shape, _F32)],
        scratch_shapes=[pltpu.VMEM((n_rows + chunk, LANES), _F32), pltpu.VMEM((n_chunks, 2 * S5_STATE_LANES), _F32),
                        pltpu.VMEM((n_chunks, 2 * S5_STATE_LANES), _F32)],
        compiler_params=pltpu.CompilerParams(dimension_semantics=("parallel", "parallel"),
                                             vmem_limit_bytes=VMEM_LIMIT),
        name="s5_scan",
    )(u, k_bd, h_bd, g_bd, a_pow, d_skip.reshape(S5_TILES, 1, LANES), s0)


def _state_to_tiles(s_re, s_im):
    bsz = s_re.shape[0]
    return jnp.concatenate([s_re.reshape(bsz, S5_TILES, S5_STATE_LANES), s_im.reshape(bsz, S5_TILES, S5_STATE_LANES)],
                           axis=-1)


def _state_from_tiles(s):
    bsz = s.shape[0]
    return (s[..., :S5_STATE_LANES].reshape(bsz, SSM_GROUPS, SSM_STATE),
            s[..., S5_STATE_LANES:].reshape(bsz, SSM_GROUPS, SSM_STATE))


TQ = 256
TK = 128
NEAR_TILES = TQ // TK + 1
NEAR_BUCKET_DIST = 113


def _bucket_table(dist):
    import numpy as np
    max_exact = N_BUCKETS // 2
    d = np.maximum(np.asarray(dist, np.int64), 0)
    df = np.maximum(d, 1).astype(np.float64)
    large = max_exact + (np.log(df / max_exact) / math.log(MAX_DISTANCE / max_exact)
                         * (N_BUCKETS - max_exact)).astype(np.int64)
    return np.where(d < max_exact, d, np.minimum(large, N_BUCKETS - 1)).astype(np.int32)


def _bias_by_distance(rel_bias, dist):
    import numpy as np
    onehot = _bucket_table(dist)[None] == np.arange(N_BUCKETS)[:, None, None]
    return jnp.einsum('kqc,kh->hqc', jnp.asarray(onehot, _BF16).astype(_F32), rel_bias,
                      precision=lax.Precision.HIGHEST)


def _bias_tiles(rel_bias, n_cmp_pad):
    import numpy as np
    rel_bias = rel_bias - rel_bias[N_BUCKETS - 1]
    r = np.arange(TQ)[:, None]
    c = np.arange(n_cmp_pad)[None, :]
    d_cmp = np.where(c < 8 + TQ // CMP_STRIDE, r - (CMP_LEN - 1) + CMP_STRIDE * (8 - c), 10 ** 6)
    g_cmp = _bias_by_distance(rel_bias, d_cmp).reshape(N_KV_HEADS, Q_PER_KV * TQ, n_cmp_pad)
    k = np.arange(NEAR_TILES * TK)[None, :]
    d_near = TK + r - k
    near = _bias_by_distance(rel_bias, d_near) + jnp.asarray(np.where(d_near >= 0, 0.0, NEG_INF), _F32)
    near = near.reshape(N_KV_HEADS, Q_PER_KV * TQ, NEAR_TILES * TK)
    return g_cmp, near


def _stack_heads(q):
    return jnp.concatenate([q[:, g * HEAD_DIM:(g + 1) * HEAD_DIM] for g in range(Q_PER_KV)], axis=0)


def _unstack_heads(o):
    return jnp.concatenate([o[g] for g in range(Q_PER_KV)], axis=1)


def _nsa_cmp_kernel(q_ref, kct_ref, vc_ref, g_ref, ovt_ref, oc_ref, selt_ref):
    i = pl.program_id(2)
    t0 = i * TQ
    n_cmp_pad = kct_ref.shape[-1]
    qs = _stack_heads(q_ref[...] * (HEAD_DIM ** -0.5)).astype(_BF16)
    s = jnp.dot(qs, kct_ref[...], preferred_element_type=_F32)
    shift = (i * (TQ // CMP_STRIDE) + n_cmp_pad - 8) % n_cmp_pad
    bias = pltpu.roll(g_ref[...], shift, 1)
    s3 = (s + bias).reshape(Q_PER_KV, TQ, n_cmp_pad)
    r = lax.broadcasted_iota(jnp.int32, (TQ, n_cmp_pad), 0)
    n = lax.broadcasted_iota(jnp.int32, (TQ, n_cmp_pad), 1)
    valid = (t0 + r - CMP_STRIDE * n - (CMP_LEN - 1)) >= 0
    sm = jnp.where(valid, s3, NEG_INF)
    m = jnp.max(sm, axis=-1, keepdims=True)
    p = jnp.where(valid, jnp.exp(sm - m), 0.0)
    l = jnp.sum(p, axis=-1, keepdims=True)
    pb = (p * jnp.where(l > 0.0, 1.0 / l, 0.0)).astype(_BF16)
    oc = jnp.dot(pb.reshape(Q_PER_KV * TQ, n_cmp_pad), vc_ref[...], preferred_element_type=_F32)
    oc_ref[...] = _unstack_heads(oc.reshape(Q_PER_KV, TQ, HEAD_DIM))
    imp = jnp.zeros((ovt_ref.shape[0], TQ), _F32)
    for g in range(Q_PER_KV):
        imp += lax.dot_general(ovt_ref[...], pb[g], (((1,), (1,)), ((), ())), preferred_element_type=_F32)
    n_sel = imp.shape[0]
    blk = lax.broadcasted_iota(jnp.int32, (n_sel, TQ), 0)
    t = t0 + lax.broadcasted_iota(jnp.int32, (n_sel, TQ), 1)
    cur = t // SEL_BLOCK
    forced = (blk == 0) | (blk == cur) | (blk == cur - 1)
    score = jnp.where(forced, FORCE, jnp.where(blk * SEL_BLOCK <= t, imp, -FORCE))
    rank = jnp.zeros((n_sel, TQ), _F32)
    for sp in range(n_sel):
        row = score[sp:sp + 1, :]
        beats = (row > score) | ((row == score) & (blk > sp))
        rank += jnp.where(beats, 1.0, 0.0)
    selt_ref[...] = (rank < SEL_TOPK).astype(_F32)


ROW_BLOCK = 128


def _flash_step(qs, kt, v, add_rows, m_scr, acc_scr):
    n_chunks = kt.shape[-1] // LANES
    for g in range(Q_PER_KV):
        for r0 in range(0, TQ, ROW_BLOCK):
            s = jnp.dot(qs[g * TQ + r0:g * TQ + r0 + ROW_BLOCK], kt, preferred_element_type=_F32)
            add = add_rows(g, r0)
            if add is not None:
                s = s + add
            chunks = [s[:, k * LANES:(k + 1) * LANES] for k in range(n_chunks)]
            m_old = m_scr[g, r0:r0 + ROW_BLOCK]
            m_new = jnp.maximum(m_old, jnp.max(functools.reduce(jnp.maximum, chunks), axis=-1, keepdims=True))
            alpha = jnp.exp(m_old - m_new)
            m_scr[g, r0:r0 + ROW_BLOCK] = m_new
            pv = None
            for k, sk in enumerate(chunks):
                d = jnp.dot(jnp.exp(sk - m_new).astype(_BF16), v[k * LANES:(k + 1) * LANES],
                            preferred_element_type=_F32)
                pv = d if pv is None else pv + d
            acc_scr[g, r0:r0 + ROW_BLOCK] = alpha * acc_scr[g, r0:r0 + ROW_BLOCK] + pv


FAR_TILES = 4


def _nsa_sel_win_kernel(q_ref, oc_ref, glog_ref, selt_ref, kst_ref, vs_ref, kwt_ref, vw_ref, near_ref,
                        exp_ref, gexp_ref, o_ref, seladd, m_s, acc_s, m_w, acc_w):
    i = pl.program_id(2)
    qs = _stack_heads(q_ref[...] * (HEAD_DIM ** -0.5)).astype(_BF16)
    sel = jnp.dot(selt_ref[...].T.astype(_BF16), exp_ref[...], preferred_element_type=_F32)
    seladd[...] = (sel - 1.0) * (-NEG_INF)
    for m_scr, acc_scr in ((m_s, acc_s), (m_w, acc_w)):
        m_scr[...] = jnp.full(m_scr.shape, NEG_INF, _F32)
        acc_scr[...] = jnp.zeros(acc_scr.shape, _F32)

    def keys(j, n_tiles):
        return pl.ds(pl.multiple_of(j * TK, TK), n_tiles * TK)

    def sel_step(j, n_tiles, bias):
        cols = keys(j, n_tiles)

        def add_rows(g, r0):
            add = seladd[r0:r0 + ROW_BLOCK, cols]
            return add if bias is None else add + bias(g, r0)
        _flash_step(qs, kst_ref[:, cols], vs_ref[cols, :], add_rows, m_s, acc_s)

    def win_step(j, n_tiles, add_rows):
        cols = keys(j, n_tiles)
        _flash_step(qs, kwt_ref[:, cols], vw_ref[cols, :], add_rows, m_w, acc_w)

    near = lambda lo: (lambda g, r0: near_ref[g * TQ + r0:g * TQ + r0 + ROW_BLOCK, lo * TK:])
    no_add = lambda g, r0: None
    own = TQ // TK
    first = i * own

    n_far = jnp.maximum(first - 1, 0)

    def far_many(j, carry):
        sel_step(j * FAR_TILES, FAR_TILES, None)
        return carry

    def far_one(j, carry):
        sel_step(j, 1, None)
        return carry

    lax.fori_loop(0, n_far // FAR_TILES, far_many, 0)
    lax.fori_loop((n_far // FAR_TILES) * FAR_TILES, n_far, far_one, 0)

    @pl.when(i >= 1)
    def _():
        sel_step(first - 1, NEAR_TILES, near(0))
        win_step(first - 1, NEAR_TILES, near(0))

    @pl.when(i == 0)
    def _():
        sel_step(0, own, near(1))
        win_step(0, own, near(1))

    n_back = WINDOW // TK

    def back_step(start_tile, n_tiles, offset):
        masks = {}

        def add_rows(g, r0):
            if offset >= r0 + ROW_BLOCK - 1:
                return None
            if r0 not in masks:
                r = r0 + lax.broadcasted_iota(jnp.int32, (ROW_BLOCK, n_tiles * TK), 0)
                kk = offset + lax.broadcasted_iota(jnp.int32, (ROW_BLOCK, n_tiles * TK), 1)
                masks[r0] = jnp.where(kk >= r, 0.0, NEG_INF)
            return masks[r0]
        win_step(start_tile, n_tiles, add_rows)

    @pl.when(first >= n_back)
    def _():
        back_step(first - n_back, n_back - 1, 0)

    for i_small in range(1, -(-n_back // own)):
        if i_small * own > 1:
            @pl.when(i == i_small)
            def _():
                back_step(0, i_small * own - 1, (n_back - i_small * own) * TK)

    def finish(acc_scr):
        acc = acc_scr[...]
        return _unstack_heads(acc[..., :HEAD_DIM] / acc[..., HEAD_DIM:])

    gates = jnp.dot(jax.nn.sigmoid(glog_ref[...]), gexp_ref[...], preferred_element_type=_F32,
                    precision=lax.Precision.HIGHEST)
    width = Q_PER_KV * HEAD_DIM
    o_ref[...] = (gates[:, :width] * oc_ref[...] + gates[:, width:2 * width] * finish(acc_s)
                  + gates[:, 2 * width:] * finish(acc_w))


def _nsa_prompt_side(kv, prm):
    import numpy as np
    bsz, t_len = kv.shape[:2]
    n_cmp = (t_len - CMP_LEN) // CMP_STRIDE + 1
    n_cmp_pad = t_len // CMP_STRIDE
    assert n_cmp == n_cmp_pad - 1 and n_cmp_pad % LANES == 0 and t_len % TQ == 0
    n_sel = t_len // SEL_BLOCK
    grouped = lambda slot: kv[:, :, slot].reshape(bsz, n_cmp_pad, CMP_STRIDE * N_KV_HEADS * HEAD_DIM)
    one_page = jnp.arange(bsz, dtype=jnp.int32)[:, None]
    k_cmp = _cmp_pages(grouped(0), one_page, prm['cmp_pos_k'], prm['cmp_w1_k'], prm['cmp_w2_k'])
    v_cmp = _cmp_pages(grouped(1), one_page, prm['cmp_pos_v'], prm['cmp_w1_v'], prm['cmp_w2_v'])
    kvb = kv.astype(_BF16)
    t_last = lambda a: jnp.transpose(a, (0, 2, 3, 1))
    t_rows = lambda a: jnp.transpose(a, (0, 2, 1, 3))
    pad_c = lambda a: a.reshape(bsz, n_cmp_pad, N_KV_HEADS, HEAD_DIM)
    rel_bias = prm['rel_bias']
    c_start = np.arange(n_cmp_pad)[None, :] * CMP_STRIDE
    s_start = np.arange(n_sel)[:, None] * SEL_BLOCK
    ovt = ((c_start < s_start + SEL_BLOCK) & (c_start + CMP_LEN > s_start) & (np.arange(n_cmp_pad)[None, :] < n_cmp))
    expand = np.arange(n_sel)[:, None] == (np.arange(t_len)[None, :] // SEL_BLOCK)
    lanes = np.arange(LANES)[:, None]
    cols = np.arange(N_BRANCH * Q_PER_KV * HEAD_DIM)[None, :]
    gexp = lanes == (cols // HEAD_DIM % Q_PER_KV) * N_BRANCH + cols // (Q_PER_KV * HEAD_DIM)
    g_cmp, near = _bias_tiles(rel_bias, n_cmp_pad)
    with_ones = lambda a: jnp.concatenate([t_rows(a), jnp.ones((bsz, N_KV_HEADS, t_len, HEAD_DIM), _BF16)], axis=-1)
    return {
        'kct': t_last(pad_c(k_cmp)), 'vc': t_rows(pad_c(v_cmp)),
        'kst': t_last(kvb[:, :, 2]), 'vs': with_ones(kvb[:, :, 3]),
        'kwt': t_last(kvb[:, :, 4]), 'vw': with_ones(kvb[:, :, 5]),
        'ovt': jnp.asarray(ovt, _BF16), 'expand': jnp.asarray(expand, _BF16), 'gexp': jnp.asarray(gexp, _F32),
        'g_cmp': g_cmp, 'near': near,
    }


def _nsa_prompt_attention(q, glog, side):
    bsz, t_len, _ = q.shape
    n_t = t_len // TQ
    width = Q_PER_KV * HEAD_DIM
    n_cmp_pad = side['kct'].shape[-1]
    n_sel = side['ovt'].shape[0]
    grid = (bsz, N_KV_HEADS, n_t)
    params = pltpu.CompilerParams(dimension_semantics=("parallel", "parallel", "arbitrary"),
                                  vmem_limit_bytes=VMEM_LIMIT)
    q_spec = pl.BlockSpec((None, TQ, width), lambda b, h, i: (b, i, h))
    per_bh = lambda *shape: pl.BlockSpec((None, None) + shape, lambda b, h, i: (b, h, 0, 0))
    per_h = lambda *shape: pl.BlockSpec((None,) + shape, lambda b, h, i: (h,) + (0,) * len(shape))
    const = lambda *shape: pl.BlockSpec(shape, lambda b, h, i: (0,) * len(shape))
    selt_spec = pl.BlockSpec((None, None, n_sel, TQ), lambda b, h, i: (b, h, 0, i))
    o_c, selt = pl.pallas_call(
        _nsa_cmp_kernel,
        grid=grid,
        in_specs=[q_spec, per_bh(HEAD_DIM, n_cmp_pad), per_bh(n_cmp_pad, HEAD_DIM),
                  per_h(Q_PER_KV * TQ, n_cmp_pad), const(n_sel, n_cmp_pad)],
        out_specs=[q_spec, selt_spec],
        out_shape=[jax.ShapeDtypeStruct((bsz, t_len, N_HEADS * HEAD_DIM), _F32),
                   jax.ShapeDtypeStruct((bsz, N_KV_HEADS, n_sel, t_len), _F32)],
        compiler_params=params,
        name="nsa_cmp_select",
    )(q, side['kct'], side['vc'], side['g_cmp'], side['ovt'])
    stat = pltpu.VMEM((Q_PER_KV, TQ, 2 * HEAD_DIM), _F32)
    return pl.pallas_call(
        _nsa_sel_win_kernel,
        grid=grid,
        in_specs=[q_spec, q_spec, pl.BlockSpec((None, TQ, LANES), lambda b, h, i: (b, i, h)), selt_spec,
                  per_bh(HEAD_DIM, t_len), per_bh(t_len, 2 * HEAD_DIM), per_bh(HEAD_DIM, t_len),
                  per_bh(t_len, 2 * HEAD_DIM), per_h(Q_PER_KV * TQ, NEAR_TILES * TK), const(n_sel, t_len),
                  const(LANES, N_BRANCH * width)],
        out_specs=q_spec,
        out_shape=jax.ShapeDtypeStruct((bsz, t_len, N_HEADS * HEAD_DIM), _F32),
        scratch_shapes=[pltpu.VMEM((TQ, t_len), _F32), stat, stat, stat, stat],
        compiler_params=params,
        name="nsa_select_window",
    )(q, o_c, glog, selt, side['kst'], side['vs'], side['kwt'], side['vw'], side['near'],
      side['expand'], side['gexp'])


PAGES_PER_STEP = 4
GROUPS_PER_PAGE_ROWS = CMP_STRIDE


def _dot_nt(a, b):
    return lax.dot_general(a, b, (((1,), (1,)), ((), ())), preferred_element_type=_F32)


def _cmp_pages_kernel(pt_ref, *refs, n_pages):
    del pt_ref
    pages = refs[:n_pages]
    posa_ref, posb_ref, w1a_ref, w1b_ref, w2_ref, o_ref = refs[n_pages:]
    x = jnp.concatenate([pg[...] for pg in pages], axis=0)
    z0 = jnp.dot((x + posa_ref[...]).astype(_BF16), w1a_ref[...], preferred_element_type=_F32)
    z1 = jnp.dot((x + posb_ref[...]).astype(_BF16), w1b_ref[...], preferred_element_type=_F32)
    pre = z0 + pltpu.roll(z1, z1.shape[0] - 1, 0)
    o_ref[...] = jnp.dot(jax.nn.gelu(pre).astype(_BF16), w2_ref[...], preferred_element_type=_F32).astype(o_ref.dtype)


def _cmp_pages(grouped, page_table, pos_emb, w1, w2):
    n_seq, n_pages = page_table.shape
    groups_per_page, flat = grouped.shape[1:]
    width = N_KV_HEADS * HEAD_DIM
    hidden = w1.shape[1]
    eye = jnp.eye(N_KV_HEADS, dtype=_F32)
    w1r = w1.reshape(2, CMP_STRIDE, HEAD_DIM, hidden)
    bd1 = jnp.einsum('hk,abdf->abhdkf', eye, w1r).reshape(2, flat, N_KV_HEADS * hidden).astype(_BF16)
    bd2 = jnp.einsum('hk,fd->hfkd', eye, w2).reshape(N_KV_HEADS * hidden, width).astype(_BF16)
    pos_t = jnp.broadcast_to(pos_emb.reshape(2, CMP_STRIDE, 1, HEAD_DIM),
                             (2, CMP_STRIDE, N_KV_HEADS, HEAD_DIM)).reshape(2, 1, flat)
    n_rows = n_pages * groups_per_page
    page_spec = lambda k: pl.BlockSpec((None, groups_per_page, flat), lambda b, pt: (pt[b, k], 0, 0))
    const = lambda *shape: pl.BlockSpec(shape, lambda b, pt: (0,) * len(shape))
    return pl.pallas_call(
        functools.partial(_cmp_pages_kernel, n_pages=n_pages),
        grid_spec=pltpu.PrefetchScalarGridSpec(
            num_scalar_prefetch=1,
            grid=(n_seq,),
            in_specs=[page_spec(k) for k in range(n_pages)] + [
                const(1, flat), const(1, flat), const(flat, N_KV_HEADS * hidden), const(flat, N_KV_HEADS * hidden),
                const(N_KV_HEADS * hidden, width)],
            out_specs=pl.BlockSpec((None, n_rows, width), lambda b, pt: (b, 0, 0)),
        ),
        out_shape=jax.ShapeDtypeStruct((n_seq, n_rows, width), _BF16),
        compiler_params=pltpu.CompilerParams(dimension_semantics=("parallel",), vmem_limit_bytes=VMEM_LIMIT),
        name="cmp_pages",
    )(page_table, *([grouped] * n_pages), pos_t[0], pos_t[1], bd1[0], bd1[1], bd2)


def _pick_own_kv_head(x, n_q):
    rows_per = Q_PER_KV * n_q
    return jnp.concatenate([x[h * rows_per:(h + 1) * rows_per, h * HEAD_DIM:(h + 1) * HEAD_DIM]
                            for h in range(N_KV_HEADS)], axis=0)


def _nsa_sample_kernel(pt_ref, q_ref, gl_ref, kc_ref, vc_ref, win_ref, new_ref, *rest, n_q, past_len, n_sel):
    del pt_ref
    pages = rest[:PAGES_PER_STEP]
    (tcmp_ref, tsel_ref, twin_ref, exp_ref, ov_ref, o_ref,
     addsel, m_scr, l_scr, acc_scr, oc_scr, ow_scr) = rest[PAGES_PER_STEP:]
    j = pl.program_id(1)
    rows = N_HEADS * n_q
    width = N_KV_HEADS * HEAD_DIM
    w_buf = win_ref.shape[1]
    step_keys = PAGES_PER_STEP * pages[0].shape[1]
    rr = lax.broadcasted_iota(jnp.int32, (rows, width), 0)
    cc = lax.broadcasted_iota(jnp.int32, (rows, width), 1)
    q4 = jnp.concatenate([q_ref[...] * (HEAD_DIM ** -0.5)] * N_KV_HEADS, axis=1)
    qbd = jnp.where(rr // (Q_PER_KV * n_q) == cc // HEAD_DIM, q4, 0.0).astype(_BF16)
    new_rows = jnp.concatenate([new_ref[...], jnp.zeros((LANES - n_q, new_ref.shape[1]), _F32)], axis=0).astype(_BF16)

    @pl.when(j == 0)
    def _first():
        s = _dot_nt(qbd, kc_ref[...]) + tcmp_ref[...]
        p = jnp.exp(s - jnp.max(s, axis=-1, keepdims=True))
        pb = (p / jnp.sum(p, axis=-1, keepdims=True)).astype(_BF16)
        oc_scr[...] = _pick_own_kv_head(jnp.dot(pb, vc_ref[...], preferred_element_type=_F32), n_q)
        pm = jnp.dot(pb, ov_ref[...], preferred_element_type=_F32)
        imp = jnp.concatenate(
            [sum(pm[(h * Q_PER_KV + g) * n_q:(h * Q_PER_KV + g + 1) * n_q] for g in range(Q_PER_KV))
             for h in range(N_KV_HEADS)], axis=0)
        blk = lax.broadcasted_iota(jnp.int32, imp.shape, 1)
        t = past_len + lax.broadcasted_iota(jnp.int32, imp.shape, 0) % n_q
        cur = t // SEL_BLOCK
        forced = (blk == 0) | (blk == cur) | (blk == cur - 1)
        score = jnp.where(forced, FORCE, jnp.where(blk * SEL_BLOCK <= t, imp, -FORCE))
        score = jnp.where(blk < n_sel, score, -jnp.inf)
        rank = jnp.zeros(imp.shape, jnp.int32)
        for sp in range(n_sel):
            col = score[:, sp:sp + 1]
            rank += ((col > score) | ((col == score) & (blk > sp))).astype(jnp.int32)
        sel = (rank < min(SEL_TOPK, n_sel)).astype(_BF16)
        sel_rows = jnp.concatenate([sel[h * n_q:(h + 1) * n_q] for h in range(N_KV_HEADS) for _ in range(Q_PER_KV)],
                                   axis=0)
        addsel[...] = (jnp.dot(sel_rows, exp_ref[...], preferred_element_type=_F32) - 1.0) * (-NEG_INF) + tsel_ref[...]
        kw_t = win_ref[:width, :].astype(_BF16)
        vw_t = win_ref[width:, :].astype(_BF16)
        s1 = jnp.dot(qbd, kw_t, preferred_element_type=_F32) + twin_ref[:, :w_buf]
        s2 = _dot_nt(qbd, new_rows[:, 2 * width:3 * width]) + twin_ref[:, w_buf:]
        m = jnp.maximum(jnp.max(s1, axis=-1, keepdims=True), jnp.max(s2, axis=-1, keepdims=True))
        p1 = jnp.exp(s1 - m)
        p2 = jnp.exp(s2 - m)
        l = jnp.sum(p1, axis=-1, keepdims=True) + jnp.sum(p2, axis=-1, keepdims=True)
        ow = (_dot_nt(p1.astype(_BF16), vw_t)
              + jnp.dot(p2.astype(_BF16), new_rows[:, 3 * width:], preferred_element_type=_F32))
        ow_scr[...] = _pick_own_kv_head(ow / l, n_q)
        m_scr[...] = jnp.full(m_scr.shape, NEG_INF, _F32)
        l_scr[...] = jnp.zeros(l_scr.shape, _F32)
        acc_scr[...] = jnp.zeros(acc_scr.shape, _F32)

    def flash(s, pv):
        m_old = m_scr[...]
        m_new = jnp.maximum(m_old, jnp.max(s, axis=-1, keepdims=True))
        alpha = jnp.exp(m_old - m_new)
        p = jnp.exp(s - m_new)
        m_scr[...] = m_new
        l_scr[...] = alpha * l_scr[...] + jnp.sum(p, axis=-1, keepdims=True)
        acc_scr[...] = alpha * acc_scr[...] + pv(p.astype(_BF16))

    k_t = jnp.concatenate([pg[:width, :] for pg in pages], axis=1).astype(_BF16)
    v_t = jnp.concatenate([pg[width:, :] for pg in pages], axis=1).astype(_BF16)
    flash(jnp.dot(qbd, k_t, preferred_element_type=_F32)
          + addsel[:, pl.ds(pl.multiple_of(j * step_keys, step_keys), step_keys)], lambda p: _dot_nt(p, v_t))

    @pl.when(j == pl.num_programs(1) - 1)
    def _last():
        flash(_dot_nt(qbd, new_rows[:, :width]) + addsel[:, past_len:],
              lambda p: jnp.dot(p, new_rows[:, width:2 * width], preferred_element_type=_F32))
        o_s = _pick_own_kv_head(acc_scr[...] / l_scr[...], n_q)
        gates = jax.nn.sigmoid(gl_ref[...])
        o_ref[...] = gates[:, 0:1] * oc_scr[...] + gates[:, 1:2] * o_s + gates[:, 2:3] * ow_scr[...]


def _nsa_sample_side(cache_kv, page_table, cache_win, kv_new, rel_bias, prm):
    import numpy as np
    n_seq, n_pages = page_table.shape
    n_phys, page_rows = cache_kv.shape[:2]
    n_q = kv_new.shape[1]
    width = N_KV_HEADS * HEAD_DIM
    past_len = n_pages * page_rows
    w_buf = cache_win.shape[1]
    tk = past_len + n_q
    n_cmp = (tk - CMP_LEN) // CMP_STRIDE + 1
    n_cmp_pad = past_len // CMP_STRIDE
    n_sel = -(-tk // SEL_BLOCK)
    assert n_cmp == n_cmp_pad - 1 and n_cmp_pad == LANES and n_sel <= LANES and n_q <= LANES
    assert page_rows % SEL_BLOCK == 0 and n_pages % PAGES_PER_STEP == 0 and w_buf == min(WINDOW, past_len)
    cache_t = jnp.transpose(cache_kv, (0, 2, 3, 4, 1)).reshape(n_phys, KV_SLOTS_PAGED * width, page_rows)
    win_t = jnp.transpose(cache_win, (0, 2, 3, 4, 1)).reshape(n_seq, KV_SLOTS_WIN * width, w_buf)
    gpp = page_rows // CMP_STRIDE
    grouped = cache_kv[:, :, :2].reshape(n_phys, gpp, CMP_STRIDE, 2, width)
    grouped = jnp.transpose(grouped, (3, 0, 1, 2, 4)).reshape(2, n_phys, gpp, CMP_STRIDE * width)
    kc = _cmp_pages(grouped[0], page_table, prm['cmp_pos_k'], prm['cmp_w1_k'], prm['cmp_w2_k'])
    vc = _cmp_pages(grouped[1], page_table, prm['cmp_pos_v'], prm['cmp_w1_v'], prm['cmp_w2_v'])
    rows = N_HEADS * n_q
    t = past_len + np.arange(n_q)[:, None]
    masked = lambda ok, d: jnp.where(jnp.asarray(ok), _bias_by_distance(rel_bias, d), NEG_INF).reshape(rows, -1)
    n = np.arange(n_cmp_pad)[None, :]
    d_c = t - (n * CMP_STRIDE + CMP_LEN - 1)
    t_cmp = masked((d_c >= 0) & (n < n_cmp), d_c)
    pos = np.arange(past_len + LANES)[None, :]
    t_sel = masked((t - pos >= 0) & (pos < tk), t - pos)
    l = np.arange(w_buf + LANES)[None, :]
    d_w = t - (past_len - w_buf + l)
    t_win = masked((d_w >= 0) & (d_w <= WINDOW) & (l < w_buf + n_q), d_w)
    expand = (np.arange(LANES)[:, None] == pos // SEL_BLOCK) & (np.arange(LANES)[:, None] < n_sel)
    c_start = np.arange(n_cmp_pad)[:, None] * CMP_STRIDE
    s_start = np.arange(LANES)[None, :] * SEL_BLOCK
    ov = ((c_start < s_start + SEL_BLOCK) & (c_start + CMP_LEN > s_start) & (np.arange(n_cmp_pad)[:, None] < n_cmp)
          & (np.arange(LANES)[None, :] < n_sel))
    return {
        'cache_t': cache_t, 'page_table': page_table, 'kc': kc, 'vc': vc, 'win_t': win_t,
        'new': kv_new.reshape(n_seq, n_q, -1)[:, :, 2 * width:],
        't_cmp': t_cmp, 't_sel': t_sel, 't_win': t_win,
        'expand': jnp.asarray(expand, _BF16), 'ov': jnp.asarray(ov, _BF16), 'n_sel': n_sel, 'past_len': past_len,
    }


def _nsa_sample_attention(q, glog, side):
    n_seq, n_q, _ = q.shape
    glog = glog.reshape(n_seq, n_q, N_KV_HEADS, LANES)[..., :Q_PER_KV * N_BRANCH]
    rows = N_HEADS * n_q
    width = N_KV_HEADS * HEAD_DIM
    page_table = side['page_table']
    n_pages = page_table.shape[1]
    page_rows = side['cache_t'].shape[2]
    past_len = side['past_len']
    w_buf = side['win_t'].shape[2]
    n_steps = n_pages // PAGES_PER_STEP
    q_rows = jnp.transpose(q.reshape(n_seq, n_q, N_HEADS, HEAD_DIM), (0, 2, 1, 3)).reshape(n_seq, rows, HEAD_DIM)
    gl = jnp.transpose(glog.reshape(n_seq, n_q, N_HEADS, N_BRANCH), (0, 2, 1, 3)).reshape(n_seq, rows, N_BRANCH)
    gl = jnp.pad(gl, ((0, 0), (0, 0), (0, LANES - N_BRANCH)))
    per_seq = lambda *shape: pl.BlockSpec((None,) + shape, lambda b, j, pt: (b,) + (0,) * len(shape))
    const = lambda *shape: pl.BlockSpec(shape, lambda b, j, pt: (0,) * len(shape))
    page_spec = lambda k: pl.BlockSpec((None, 2 * width, page_rows),
                                       lambda b, j, pt: (pt[b, j * PAGES_PER_STEP + k], 1, 0))
    out = pl.pallas_call(
        functools.partial(_nsa_sample_kernel, n_q=n_q, past_len=past_len, n_sel=side['n_sel']),
        grid_spec=pltpu.PrefetchScalarGridSpec(
            num_scalar_prefetch=1,
            grid=(n_seq, n_steps),
            in_specs=[per_seq(rows, HEAD_DIM), per_seq(rows, LANES), per_seq(LANES, width), per_seq(LANES, width),
                      per_seq(2 * width, w_buf), per_seq(n_q, 4 * width)]
            + [page_spec(k) for k in range(PAGES_PER_STEP)]
            + [const(rows, LANES), const(rows, past_len + LANES), const(rows, w_buf + LANES),
               const(LANES, past_len + LANES), const(LANES, LANES)],
            out_specs=per_seq(rows, HEAD_DIM),
            scratch_shapes=[pltpu.VMEM((rows, past_len + LANES), _F32), pltpu.VMEM((rows, 1), _F32),
                            pltpu.VMEM((rows, 1), _F32), pltpu.VMEM((rows, width), _F32),
                            pltpu.VMEM((rows, HEAD_DIM), _F32), pltpu.VMEM((rows, HEAD_DIM), _F32)],
        ),
        out_shape=jax.ShapeDtypeStruct((n_seq, rows, HEAD_DIM), _F32),
        compiler_params=pltpu.CompilerParams(dimension_semantics=("parallel", "arbitrary"),
                                             vmem_limit_bytes=VMEM_LIMIT),
        name="nsa_sample",
    )(page_table, q_rows, gl, side['kc'], side['vc'], side['win_t'], side['new'],
      *([side['cache_t']] * PAGES_PER_STEP), side['t_cmp'], side['t_sel'], side['t_win'], side['expand'], side['ov'])
    return jnp.transpose(out.reshape(n_seq, N_HEADS, n_q, HEAD_DIM), (0, 2, 1, 3)).reshape(n_seq, n_q, N_HEADS * HEAD_DIM)


def _run_trunk(x, p, s0_re, s0_im, paged, win_buf, prm):
    bsz, t_len, _ = x.shape
    ssm_re, ssm_im = [], []
    side = None
    kv_rows_new = None
    win_state = None
    y = None
    x2 = x.reshape(-1, D_MODEL)
    as_seq = lambda a: a.reshape(bsz, t_len, a.shape[-1])
    for i in range(DEPTH):
        if i < N_A_LAYERS:
            u2, = _rows_call(lambda xt, gain: (_norm(xt, gain),), [x2], [prm['g_mix'][i].reshape(1, -1)],
                             [D_MODEL], "s5_norm")
            ssm = (prm['ssm_a_re'][i], prm['ssm_a_im'][i], prm['ssm_log_dt'][i], prm['ssm_b_re'][i],
                   prm['ssm_b_im'][i], prm['ssm_c_re'][i], prm['ssm_c_im'][i], prm['ssm_d'][i])
            s0 = _state_to_tiles(s0_re[i], s0_im[i])
            if t_len == 8:
                ys, s_fin = _s5_scan(u2[None], jnp.transpose(s0, (1, 0, 2))[None], False, *ssm)
                s_re, s_im = _state_from_tiles(jnp.transpose(s_fin[0], (1, 0, 2)))
            else:
                assert t_len % (16 * 8) == 0
                ys, s_fin = _s5_scan(as_seq(u2), s0[:, :, None, :], True, *ssm)
                s_re, s_im = _state_from_tiles(s_fin[:, :, 0, :])
            ssm_re.append(s_re)
            ssm_im.append(s_im)
            x2 = _glu_residual(x2, ys.reshape(-1, D_MODEL), prm['w_glu'][i])
        else:
            j = i - N_A_LAYERS
            q, glog = _qg_project(x2, prm['g_mix'][i], prm['w_qg'][j])
            attend = _nsa_prompt_attention if paged is None else _nsa_sample_attention
            mixed = attend(as_seq(q), as_seq(glog), side)
            x2 = _wo_residual(x2, mixed.reshape(-1, D_MODEL), prm['w_o'][j])
        x2 = _moe_layer(x2, prm['g_ffn'][i], prm['w_route_group'][i], prm['b_route_group'][i],
                        prm['w_route_expert'][i], prm['b_route_expert'][i], prm['w_exp_up'][i],
                        prm['w_exp_down'][i])
        ple = (x2, p[i].reshape(-1, p.shape[-1]), prm['g_ple'][i], prm['w_ple_gate'][i], prm['w_ple_proj'][i])
        if i == N_A_LAYERS - 1:
            x2, kv2 = _ple_residual(*ple, g_next=prm['g_kv'], w_next=prm['w_kv'])
            kv = kv2.reshape(bsz, t_len, KV_SLOTS_PAGED + KV_SLOTS_WIN, N_KV_HEADS, HEAD_DIM)
            kv_rows_new, win_new = kv[:, :, :KV_SLOTS_PAGED], kv[:, :, KV_SLOTS_PAGED:]
            if paged is None:
                win_state = win_new[:, -min(WINDOW, t_len):]
                side = _nsa_prompt_side(kv, prm)
            else:
                w_buf = win_buf.shape[1]
                win_state = jnp.concatenate([win_buf, win_new], axis=1)[:, -w_buf:]
                side = _nsa_sample_side(paged[0], paged[1], win_buf, kv, prm['rel_bias'], prm)
        elif i == DEPTH - 1:
            x2, y = _ple_residual(*ple, g_next=prm['g_final'])
        else:
            x2, = _ple_residual(*ple)
    return as_seq(y), kv_rows_new, win_state, jnp.stack(ssm_re), jnp.stack(ssm_im)


def kernel(x_prompt, x_sample, p_prompt, p_sample, cache_kv, cache_win, state_ssm_re, state_ssm_im, page_table,
           g_mix, g_ffn, g_ple, g_kv, g_final,
           ssm_a_re, ssm_a_im, ssm_log_dt, ssm_b_re, ssm_b_im, ssm_c_re, ssm_c_im, ssm_d, w_glu,
           w_kv, cmp_pos_k, cmp_pos_v, cmp_w1_k, cmp_w2_k, cmp_w1_v, cmp_w2_v, w_qg, w_o, rel_bias,
           w_route_group, b_route_group, w_route_expert, b_route_expert, w_exp_up, w_exp_down,
           w_ple_proj, w_ple_gate):
    prm = {
        'g_mix': g_mix, 'g_ffn': g_ffn, 'g_ple': g_ple, 'g_kv': g_kv, 'g_final': g_final,
        'ssm_a_re': ssm_a_re, 'ssm_a_im': ssm_a_im, 'ssm_log_dt': ssm_log_dt,
        'ssm_b_re': ssm_b_re, 'ssm_b_im': ssm_b_im, 'ssm_c_re': ssm_c_re, 'ssm_c_im': ssm_c_im,
        'ssm_d': ssm_d, 'w_glu': w_glu,
        'w_kv': w_kv, 'cmp_pos_k': cmp_pos_k, 'cmp_pos_v': cmp_pos_v,
        'cmp_w1_k': cmp_w1_k, 'cmp_w2_k': cmp_w2_k, 'cmp_w1_v': cmp_w1_v, 'cmp_w2_v': cmp_w2_v,
        'w_qg': w_qg, 'w_o': w_o, 'rel_bias': rel_bias,
        'w_route_group': w_route_group, 'b_route_group': b_route_group,
        'w_route_expert': w_route_expert, 'b_route_expert': b_route_expert,
        'w_exp_up': w_exp_up, 'w_exp_down': w_exp_down,
        'w_ple_proj': w_ple_proj, 'w_ple_gate': w_ple_gate,
    }
    zero_state = jnp.zeros((N_A_LAYERS, x_prompt.shape[0], SSM_GROUPS, SSM_STATE), _F32)
    y_prompt, kv_prompt, win_prompt, ssm_re_prompt, ssm_im_prompt = _run_trunk(
        x_prompt, p_prompt, zero_state, zero_state, None, None, prm)
    y_sample, kv_sample, win_sample, ssm_re_sample, ssm_im_sample = _run_trunk(
        x_sample, p_sample, state_ssm_re, state_ssm_im, (cache_kv, page_table), cache_win, prm)
    return (y_prompt, y_sample, kv_prompt, win_prompt, ssm_re_prompt, ssm_im_prompt,
            kv_sample, win_sample, ssm_re_sample, ssm_im_sample)
```

```python
import functools
import math

import jax
import jax.numpy as jnp
from jax import lax
from jax.experimental import pallas as pl
from jax.experimental.pallas import tpu as pltpu

D_MODEL = 1024
DEPTH = 4
N_A_LAYERS = DEPTH // 2
SSM_GROUP = 16
SSM_GROUPS = D_MODEL // SSM_GROUP
SSM_STATE = 64
N_HEADS = 16
HEAD_DIM = D_MODEL // N_HEADS
N_KV_HEADS = 4
Q_PER_KV = N_HEADS // N_KV_HEADS
CMP_LEN = 32
CMP_STRIDE = 16
SEL_BLOCK = 64
SEL_TOPK = 16
WINDOW = 512
N_BRANCH = 3
KV_SLOTS_PAGED = 4
KV_SLOTS_WIN = 2
Q_BLOCK = 64
N_BUCKETS = 32
MAX_DISTANCE = 128
N_EXPERT_GROUPS = 4
EXPERTS_PER_GROUP = 4
N_EXPERTS = N_EXPERT_GROUPS * EXPERTS_PER_GROUP
EXPERT_TOPK = 2
D_EXPERT = 256
RMS_EPS = 1e-6
NEG_INF = -1e30
FORCE = 1e4

LANES = 128
VMEM_LIMIT = 48 * 1024 * 1024

_F32 = jnp.float32
_BF16 = jnp.bfloat16


def _moe_kernel(x_ref, g_ref, wr_ref, br_ref, wup_ref, wdn_ref, o_ref, h_scr, comb_scr, acc_scr):
    e = pl.program_id(1)
    lane = lax.broadcasted_iota(jnp.int32, comb_scr.shape, 1)

    @pl.when(e == 0)
    def _route():
        x = x_ref[...]
        h = x * lax.rsqrt(jnp.mean(x * x, axis=-1, keepdims=True) + RMS_EPS) * g_ref[...]
        h_scr[...] = h.astype(_BF16)
        logits = jnp.dot(h, wr_ref[...], preferred_element_type=_F32,
                         precision=lax.Precision.HIGHEST) + br_ref[...]
        is_grp = (lane >= N_EXPERTS) & (lane < N_EXPERTS + N_EXPERT_GROUPS)
        lg = jnp.where(is_grp, logits, -jnp.inf)
        gmax = jnp.max(lg, axis=-1, keepdims=True)
        gi = jnp.min(jnp.where(lg == gmax, lane, LANES), axis=-1, keepdims=True) - N_EXPERTS
        gp = 1.0 / jnp.sum(jnp.where(is_grp, jnp.exp(lg - gmax), 0.0), axis=-1, keepdims=True)
        in_grp = (lane < N_EXPERTS) & ((lane // EXPERTS_PER_GROUP) == gi)
        le = jnp.where(in_grp, logits, -jnp.inf)
        m1 = jnp.max(le, axis=-1, keepdims=True)
        i1 = jnp.min(jnp.where(le == m1, lane, LANES), axis=-1, keepdims=True)
        le2 = jnp.where(lane == i1, -jnp.inf, le)
        m2 = jnp.max(le2, axis=-1, keepdims=True)
        i2 = jnp.min(jnp.where(le2 == m2, lane, LANES), axis=-1, keepdims=True)
        e2 = jnp.exp(m2 - m1)
        den = 1.0 + e2
        comb_scr[...] = jnp.where(lane == i1, gp / den, jnp.where(lane == i2, gp * e2 / den, 0.0))
        acc_scr[...] = x

    up = jnp.dot(h_scr[...], wup_ref[0].astype(_BF16), preferred_element_type=_F32)
    a = up[:, :D_EXPERT]
    b = up[:, D_EXPERT:]
    c = jnp.sum(jnp.where(lane == e, comb_scr[...], 0.0), axis=-1, keepdims=True)
    act = (a * jax.nn.sigmoid(a)) * b * c
    acc_scr[...] += jnp.dot(act.astype(_BF16), wdn_ref[0].astype(_BF16), preferred_element_type=_F32)

    @pl.when(e == N_EXPERTS - 1)
    def _store():
        o_ref[...] = acc_scr[...]


def _moe_layer(x2, g, w_rg, b_rg, w_re, b_re, w_up, w_dn):
    n_tok, d = x2.shape
    tm = min(n_tok, 1024)
    wr = jnp.zeros((d, LANES), _F32).at[:, :N_EXPERTS].set(w_re).at[:, N_EXPERTS:N_EXPERTS + N_EXPERT_GROUPS].set(w_rg)
    br = jnp.zeros((1, LANES), _F32).at[0, :N_EXPERTS].set(b_re).at[0, N_EXPERTS:N_EXPERTS + N_EXPERT_GROUPS].set(b_rg)
    return pl.pallas_call(
        _moe_kernel,
        grid=(n_tok // tm, N_EXPERTS),
        in_specs=[
            pl.BlockSpec((tm, d), lambda i, e: (i, 0)),
            pl.BlockSpec((1, d), lambda i, e: (0, 0)),
            pl.BlockSpec((d, LANES), lambda i, e: (0, 0)),
            pl.BlockSpec((1, LANES), lambda i, e: (0, 0)),
            pl.BlockSpec((1, d, 2 * D_EXPERT), lambda i, e: (e, 0, 0)),
            pl.BlockSpec((1, D_EXPERT, d), lambda i, e: (e, 0, 0)),
        ],
        out_specs=pl.BlockSpec((tm, d), lambda i, e: (i, 0)),
        out_shape=jax.ShapeDtypeStruct((n_tok, d), _F32),
        scratch_shapes=[
            pltpu.VMEM((tm, d), _BF16),
            pltpu.VMEM((tm, LANES), _F32),
            pltpu.VMEM((tm, d), _F32),
        ],
        compiler_params=pltpu.CompilerParams(
            dimension_semantics=("parallel", "arbitrary"), vmem_limit_bytes=VMEM_LIMIT),
        name="hmoe",
    )(x2, g.reshape(1, d), wr, br, w_up, w_dn)


ROW_TILE = 512


def _rows_kernel(*refs, body, n_in):
    outs = body(*[r[...] for r in refs[:n_in]])
    for o_ref, o in zip(refs[n_in:], outs):
        o_ref[...] = o


def _rows_call(body, rows, consts, outs, name):
    n_rows = rows[0].shape[0]
    tm = min(n_rows, ROW_TILE)
    row_spec = lambda width: pl.BlockSpec((tm, width), lambda i: (i, 0))
    const_spec = lambda c: pl.BlockSpec(c.shape, lambda i: (0,) * c.ndim)
    out_specs, out_shape = [], []
    for o in outs:
        if isinstance(o, int):
            o = (o, _F32)
        if o[0] == 'T':
            out_specs.append(pl.BlockSpec((o[1], tm), lambda i: (0, i)))
            out_shape.append(jax.ShapeDtypeStruct((o[1], n_rows), o[2]))
        else:
            out_specs.append(row_spec(o[0]))
            out_shape.append(jax.ShapeDtypeStruct((n_rows, o[0]), o[1]))
    return pl.pallas_call(
        functools.partial(_rows_kernel, body=body, n_in=len(rows) + len(consts)),
        grid=(n_rows // tm,),
        in_specs=[row_spec(r.shape[1]) for r in rows] + [const_spec(c) for c in consts],
        out_specs=out_specs,
        out_shape=out_shape,
        compiler_params=pltpu.CompilerParams(dimension_semantics=("parallel",), vmem_limit_bytes=VMEM_LIMIT),
        name=name,
    )(*rows, *consts)


def _norm(x, g):
    return x * lax.rsqrt(jnp.mean(x * x, axis=-1, keepdims=True) + RMS_EPS) * g


def _mm(x, w):
    return jnp.dot(x.astype(_BF16), w, preferred_element_type=_F32)


def _glu_residual(x2, y2, w_glu):
    def body(x, y, w):
        ag = _mm(jax.nn.gelu(y), w)
        return (x + ag[:, :D_MODEL] * jax.nn.sigmoid(ag[:, D_MODEL:]),)
    return _rows_call(body, [x2, y2], [w_glu.astype(_BF16)], [D_MODEL], "glu_residual")[0]


def _qg_project(x2, g, w_qg):
    per_kv = Q_PER_KV * N_BRANCH
    w_g = w_qg[:, N_HEADS * HEAD_DIM:].reshape(D_MODEL, N_KV_HEADS, per_kv)
    w_g = jnp.pad(w_g, ((0, 0), (0, 0), (0, LANES - per_kv))).reshape(D_MODEL, N_KV_HEADS * LANES)
    w = jnp.concatenate([w_qg[:, :N_HEADS * HEAD_DIM], w_g], axis=1).astype(_BF16)

    def body(x, gain, w):
        qg = _mm(_norm(x, gain), w)
        return qg[:, :N_HEADS * HEAD_DIM], qg[:, N_HEADS * HEAD_DIM:]
    return _rows_call(body, [x2], [g.reshape(1, -1), w], [N_HEADS * HEAD_DIM, N_KV_HEADS * LANES], "qg_project")


def _wo_residual(x2, mixed2, w_o):
    body = lambda x, m, w: (x + _mm(m, w),)
    return _rows_call(body, [x2, mixed2], [w_o.astype(_BF16)], [D_MODEL], "wo_residual")[0]


def _ple_residual(x2, p2, g_ple, w_gate, w_proj, g_next=None, w_next=None, attention_operands=False):
    consts = [g_ple.reshape(1, -1), w_gate.astype(_BF16), w_proj.astype(_BF16)]
    outs = [D_MODEL]
    width = N_KV_HEADS * HEAD_DIM
    if g_next is not None:
        consts.append(g_next.reshape(1, -1))
        outs.append(D_MODEL if w_next is None else w_next.shape[1])
    if w_next is not None:
        consts.append(w_next.astype(_BF16))
    if attention_operands:
        outs += [('T', 2 * width, _BF16), (4 * width, _BF16)]

    def body(x, p, gain, wg, wp, *nxt):
        out = x + _mm(p, wp) * jax.nn.sigmoid(_mm(_norm(x, gain), wg))
        if not nxt:
            return (out,)
        normed = _norm(out, nxt[0])
        if len(nxt) == 1:
            return out, normed
        kv = _mm(normed, nxt[1])
        if not attention_operands:
            return out, kv
        slot = lambda s: kv[:, s * width:(s + 1) * width]
        keys_t = jnp.concatenate([slot(2), slot(4)], axis=1).T.astype(_BF16)
        ones = jnp.ones((kv.shape[0], HEAD_DIM), _F32)
        vals = jnp.concatenate([piece for s in (3, 5) for h in range(N_KV_HEADS)
                                for piece in (slot(s)[:, h * HEAD_DIM:(h + 1) * HEAD_DIM], ones)], axis=1)
        return out, kv, keys_t, vals.astype(_BF16)
    return _rows_call(body, [x2, p2], consts, outs, "ple_residual")


def _cmul(ar, ai, br, bi):
    return ar * br - ai * bi, ar * bi + ai * br


S5_TILE_GROUPS = LANES // SSM_GROUP
S5_TILES = D_MODEL // LANES
S5_STATE_LANES = S5_TILE_GROUPS * SSM_STATE


def _s5_operators(a_re, a_im, log_dt, b_re, b_im, c_re, c_im, chunk):
    hp = lax.Precision.HIGHEST
    dt = jnp.exp(log_dt)[:, None]
    decay = jnp.exp(a_re * dt)
    ab_re, ab_im = decay * jnp.cos(a_im * dt), decay * jnp.sin(a_im * dt)
    den = a_re * a_re + a_im * a_im
    num_re = ab_re - 1.0
    q_re = (num_re * a_re + ab_im * a_im) / den
    q_im = (ab_im * a_re - num_re * a_im) / den
    bb_re, bb_im = _cmul(q_re[..., None], q_im[..., None], b_re, b_im)
    pw_re, pw_im = [jnp.ones_like(ab_re)], [jnp.zeros_like(ab_im)]
    for _ in range(chunk):
        nr, ni = _cmul(pw_re[-1], pw_im[-1], ab_re, ab_im)
        pw_re.append(nr)
        pw_im.append(ni)
    pw_re, pw_im = jnp.stack(pw_re), jnp.stack(pw_im)
    w_re, w_im = _cmul(pw_re[:chunk, :, :, None], pw_im[:chunk, :, :, None], bb_re[None], bb_im[None])
    k_lag = (jnp.einsum('gcp,jgpd->jgcd', c_re, w_re, precision=hp)
             - jnp.einsum('gcp,jgpd->jgcd', c_im, w_im, precision=hp))
    ca_re, ca_im = _cmul(c_re[None], c_im[None], pw_re[1:, :, None, :], pw_im[1:, :, None, :])
    eye = jnp.eye(S5_TILE_GROUPS, dtype=_F32)
    tiled = lambda a: a.reshape((chunk, S5_TILES, S5_TILE_GROUPS) + a.shape[2:])
    k_bd = jnp.einsum('lthcd,hk->tlhdkc', tiled(k_lag), eye).reshape(S5_TILES, chunk, LANES, LANES)
    h_bd = jnp.concatenate(
        [jnp.einsum('lthpd,hk->tlhdkp', tiled(w), eye).reshape(S5_TILES, chunk, LANES, S5_STATE_LANES)
         for w in (w_re, w_im)], axis=-1)
    g_bd = jnp.concatenate(
        [jnp.einsum('lthcp,hk->tlhpkc', tiled(c), eye).reshape(S5_TILES, chunk, S5_STATE_LANES, LANES)
         for c in (ca_re, -ca_im)], axis=2)
    a_pow = jnp.concatenate([pw_re[chunk].reshape(S5_TILES, 1, S5_STATE_LANES),
                             pw_im[chunk].reshape(S5_TILES, 1, S5_STATE_LANES)], axis=-1)
    return k_bd.astype(_BF16), h_bd, g_bd.astype(_BF16), a_pow


def _advance(a_pow, s, e):
    half = S5_STATE_LANES
    ar, ai, sr, si = a_pow[:, :half], a_pow[:, half:], s[:, :half], s[:, half:]
    return jnp.concatenate([ar * sr - ai * si + e[:, :half], ar * si + ai * sr + e[:, half:]], axis=1)


def _s5_kernel(u_ref, k_ref, h_ref, g_ref, a_ref, d_ref, s0_ref, y_ref, sfin_ref, upad, e_scr, sin_scr, *,
               chunk, carry):
    n_rows = u_ref.shape[0]
    n_chunks = n_rows // chunk
    u = u_ref[...]
    upad[:chunk] = jnp.zeros((chunk, LANES), _F32)
    upad[chunk:] = u
    step = lax.broadcasted_iota(jnp.int32, (n_rows, LANES), 0) % chunk
    y = d_ref[...] * u
    for lag in range(chunk):
        part = jnp.dot(upad[chunk - lag:chunk - lag + n_rows].astype(_BF16), k_ref[lag], preferred_element_type=_F32)
        y += part if lag == 0 else jnp.where(step >= lag, part, 0.0)
    y_ref[...] = y
    e = None
    for lag in range(chunk):
        rows = u_ref[pl.ds(chunk - 1 - lag, n_chunks, stride=chunk), :]
        if h_ref.dtype == _F32:
            d = jnp.dot(rows, h_ref[lag], preferred_element_type=_F32, precision=lax.Precision.HIGHEST)
        else:
            d = jnp.dot(rows.astype(_BF16), h_ref[lag], preferred_element_type=_F32)
        e = d if e is None else e + d
    a_pow = a_ref[...]
    if carry:
        e_scr[...] = e
        sub = 8

        def body(jj, s):
            rows = pl.ds(pl.multiple_of(jj * sub, sub), sub)
            e_tile = e_scr[rows, :]
            starts = []
            for r in range(sub):
                starts.append(s)
                s = _advance(a_pow, s, e_tile[r:r + 1])
            sin_scr[rows, :] = jnp.concatenate(starts, axis=0)
            return s
        sfin_ref[...] = lax.fori_loop(0, n_chunks // sub, body, s0_ref[...])
    else:
        sin_scr[...] = s0_ref[...]
        sfin_ref[...] = _advance(a_pow, s0_ref[...], e)
    s_in = sin_scr[...].astype(_BF16)
    for k in range(chunk):
        rows = pl.ds(k, n_chunks, stride=chunk)
        y_ref[rows, :] = y_ref[rows, :] + jnp.dot(s_in, g_ref[k], preferred_element_type=_F32)


def _s5_scan(u, s0, carry, a_re, a_im, log_dt, b_re, b_im, c_re, c_im, d_skip):
    n_b, n_rows, _ = u.shape
    chunk = 16 if carry else 8
    n_chunks = n_rows // chunk
    n_state = s0.shape[2]
    k_bd, h_bd, g_bd, a_pow = _s5_operators(a_re, a_im, log_dt, b_re, b_im, c_re, c_im, chunk)
    if carry:
        h_bd = h_bd.astype(_BF16)
    per_tile = lambda *shape: pl.BlockSpec((None,) + shape, lambda t, b: (t,) + (0,) * len(shape))
    rows_spec = pl.BlockSpec((None, n_rows, LANES), lambda t, b: (b, 0, t))
    state_spec = pl.BlockSpec((None, None, n_state, 2 * S5_STATE_LANES), lambda t, b: (b, t, 0, 0))
    return pl.pallas_call(
        functools.partial(_s5_kernel, chunk=chunk, carry=carry),
        grid=(S5_TILES, n_b),
        in_specs=[rows_spec, per_tile(chunk, LANES, LANES), per_tile(chunk, LANES, 2 * S5_STATE_LANES),
                  per_tile(chunk, 2 * S5_STATE_LANES, LANES), per_tile(1, 2 * S5_STATE_LANES), per_tile(1, LANES),
                  state_spec],
        out_specs=[rows_spec, state_spec],
        out_shape=[jax.ShapeDtypeStruct(u.shape, _F32), jax.ShapeDtypeStruct(s0.shape, _F32)],
        scratch_shapes=[pltpu.VMEM((n_rows + chunk, LANES), _F32), pltpu.VMEM((n_chunks, 2 * S5_STATE_LANES), _F32),
                        pltpu.VMEM((n_chunks, 2 * S5_STATE_LANES), _F32)],
        compiler_params=pltpu.CompilerParams(dimension_semantics=("parallel", "parallel"),
                                             vmem_limit_bytes=VMEM_LIMIT),
        name="s5_scan",
    )(u, k_bd, h_bd, g_bd, a_pow, d_skip.reshape(S5_TILES, 1, LANES), s0)


def _state_to_tiles(s_re, s_im):
    bsz = s_re.shape[0]
    return jnp.concatenate([s_re.reshape(bsz, S5_TILES, S5_STATE_LANES), s_im.reshape(bsz, S5_TILES, S5_STATE_LANES)],
                           axis=-1)


def _state_from_tiles(s):
    bsz = s.shape[0]
    return (s[..., :S5_STATE_LANES].reshape(bsz, SSM_GROUPS, SSM_STATE),
            s[..., S5_STATE_LANES:].reshape(bsz, SSM_GROUPS, SSM_STATE))


TQ = 256
TK = 128
NEAR_TILES = TQ // TK + 1
NEAR_BUCKET_DIST = 113


def _bucket_table(dist):
    import numpy as np
    max_exact = N_BUCKETS // 2
    d = np.maximum(np.asarray(dist, np.int64), 0)
    df = np.maximum(d, 1).astype(np.float64)
    large = max_exact + (np.log(df / max_exact) / math.log(MAX_DISTANCE / max_exact)
                         * (N_BUCKETS - max_exact)).astype(np.int64)
    return np.where(d < max_exact, d, np.minimum(large, N_BUCKETS - 1)).astype(np.int32)


def _bias_by_distance(rel_bias, dist):
    import numpy as np
    onehot = _bucket_table(dist)[None] == np.arange(N_BUCKETS)[:, None, None]
    return jnp.einsum('kqc,kh->hqc', jnp.asarray(onehot, _BF16).astype(_F32), rel_bias,
                      precision=lax.Precision.HIGHEST)


def _bias_tiles(rel_bias, n_cmp_pad):
    import numpy as np
    rel_bias = rel_bias - rel_bias[N_BUCKETS - 1]
    r = np.arange(TQ)[:, None]
    c = np.arange(n_cmp_pad)[None, :]
    d_cmp = np.where(c < 8 + TQ // CMP_STRIDE, r - (CMP_LEN - 1) + CMP_STRIDE * (8 - c), 10 ** 6)
    g_cmp = _bias_by_distance(rel_bias, d_cmp).reshape(N_KV_HEADS, Q_PER_KV * TQ, n_cmp_pad)
    k = np.arange(NEAR_TILES * TK)[None, :]
    d_near = TK + r - k
    near = _bias_by_distance(rel_bias, d_near) + jnp.asarray(np.where(d_near >= 0, 0.0, NEG_INF), _F32)
    near = near.reshape(N_KV_HEADS, Q_PER_KV * TQ, NEAR_TILES * TK)
    return g_cmp, near


def _stack_heads(q):
    return jnp.concatenate([q[:, g * HEAD_DIM:(g + 1) * HEAD_DIM] for g in range(Q_PER_KV)], axis=0)


def _unstack_heads(o):
    return jnp.concatenate([o[g] for g in range(Q_PER_KV)], axis=1)


def _nsa_cmp_kernel(q_ref, kct_ref, vc_ref, g_ref, ovt_ref, oc_ref, selt_ref):
    i = pl.program_id(2)
    t0 = i * TQ
    n_cmp_pad = kct_ref.shape[-1]
    qs = _stack_heads(q_ref[...] * (HEAD_DIM ** -0.5)).astype(_BF16)
    s = jnp.dot(qs, kct_ref[...], preferred_element_type=_F32)
    shift = (i * (TQ // CMP_STRIDE) + n_cmp_pad - 8) % n_cmp_pad
    bias = pltpu.roll(g_ref[...], shift, 1)
    s3 = (s + bias).reshape(Q_PER_KV, TQ, n_cmp_pad)
    r = lax.broadcasted_iota(jnp.int32, (TQ, n_cmp_pad), 0)
    n = lax.broadcasted_iota(jnp.int32, (TQ, n_cmp_pad), 1)
    valid = (t0 + r - CMP_STRIDE * n - (CMP_LEN - 1)) >= 0
    sm = jnp.where(valid, s3, NEG_INF)
    m = jnp.max(sm, axis=-1, keepdims=True)
    p = jnp.where(valid, jnp.exp(sm - m), 0.0)
    l = jnp.sum(p, axis=-1, keepdims=True)
    pb = (p * jnp.where(l > 0.0, 1.0 / l, 0.0)).astype(_BF16)
    oc = jnp.dot(pb.reshape(Q_PER_KV * TQ, n_cmp_pad), vc_ref[...], preferred_element_type=_F32)
    oc_ref[...] = _unstack_heads(oc.reshape(Q_PER_KV, TQ, HEAD_DIM))
    imp = jnp.zeros((ovt_ref.shape[0], TQ), _F32)
    for g in range(Q_PER_KV):
        imp += lax.dot_general(ovt_ref[...], pb[g], (((1,), (1,)), ((), ())), preferred_element_type=_F32)
    n_sel = imp.shape[0]
    blk = lax.broadcasted_iota(jnp.int32, (n_sel, TQ), 0)
    t = t0 + lax.broadcasted_iota(jnp.int32, (n_sel, TQ), 1)
    cur = t // SEL_BLOCK
    forced = (blk == 0) | (blk == cur) | (blk == cur - 1)
    score = jnp.where(forced, FORCE, jnp.where(blk * SEL_BLOCK <= t, imp, -FORCE))
    rank = jnp.zeros((n_sel, TQ), _F32)
    for sp in range(n_sel):
        row = score[sp:sp + 1, :]
        beats = (row > score) | ((row == score) & (blk > sp))
        rank += jnp.where(beats, 1.0, 0.0)
    selt_ref[...] = (rank < SEL_TOPK).astype(_F32)


ROW_BLOCK = 128


def _flash_step(qs, kt, v, add_rows, m_scr, acc_scr):
    n_chunks = kt.shape[-1] // LANES
    for r0 in range(0, TQ, ROW_BLOCK):
        for g in range(Q_PER_KV):
            s = jnp.dot(qs[g * TQ + r0:g * TQ + r0 + ROW_BLOCK], kt, preferred_element_type=_F32)
            add = add_rows(g, r0)
            if add is not None:
                s = s + add
            chunks = [s[:, k * LANES:(k + 1) * LANES] for k in range(n_chunks)]
            m_old = m_scr[g, r0:r0 + ROW_BLOCK]
            m_new = jnp.maximum(m_old, jnp.max(functools.reduce(jnp.maximum, chunks), axis=-1, keepdims=True))
            alpha = jnp.exp(m_old - m_new)
            m_scr[g, r0:r0 + ROW_BLOCK] = m_new
            pv = None
            for k, sk in enumerate(chunks):
                d = jnp.dot(jnp.exp(sk - m_new).astype(_BF16), v[k * LANES:(k + 1) * LANES],
                            preferred_element_type=_F32)
                pv = d if pv is None else pv + d
            acc_scr[g, r0:r0 + ROW_BLOCK] = alpha * acc_scr[g, r0:r0 + ROW_BLOCK] + pv


FAR_TILES = 4


def _nsa_sel_win_kernel(q_ref, oc_ref, glog_ref, selt_ref, kst_ref, vs_ref, kwt_ref, vw_ref, near_ref,
                        exp_ref, gexp_ref, o_ref, m_s, acc_s, m_w, acc_w):
    i = pl.program_id(2)
    qs = _stack_heads(q_ref[...] * (HEAD_DIM ** -0.5)).astype(_BF16)
    sel_rows = selt_ref[...].T.astype(_BF16)
    for m_scr, acc_scr in ((m_s, acc_s), (m_w, acc_w)):
        m_scr[...] = jnp.full(m_scr.shape, NEG_INF, _F32)
        acc_scr[...] = jnp.zeros(acc_scr.shape, _F32)

    def keys(j, n_tiles):
        return pl.ds(pl.multiple_of(j * TK, TK), n_tiles * TK)

    def sel_step(j, n_tiles, bias):
        cols = keys(j, n_tiles)
        masks = {}

        def add_rows(g, r0):
            if r0 not in masks:
                hit = jnp.dot(sel_rows[r0:r0 + ROW_BLOCK], exp_ref[:, cols], preferred_element_type=_F32)
                masks[r0] = (hit - 1.0) * (-NEG_INF)
            add = masks[r0]
            return add if bias is None else add + bias(g, r0)
        _flash_step(qs, kst_ref[:, cols], vs_ref[cols, :], add_rows, m_s, acc_s)

    def win_step(j, n_tiles, add_rows):
        cols = keys(j, n_tiles)
        _flash_step(qs, kwt_ref[:, cols], vw_ref[cols, :], add_rows, m_w, acc_w)

    near = lambda lo: (lambda g, r0: near_ref[g * TQ + r0:g * TQ + r0 + ROW_BLOCK, lo * TK:])
    no_add = lambda g, r0: None
    own = TQ // TK
    first = i * own

    n_far = jnp.maximum(first - 1, 0)

    def far_many(j, carry):
        sel_step(j * FAR_TILES, FAR_TILES, None)
        return carry

    def far_one(j, carry):
        sel_step(j, 1, None)
        return carry

    lax.fori_loop(0, n_far // FAR_TILES, far_many, 0)
    lax.fori_loop((n_far // FAR_TILES) * FAR_TILES, n_far, far_one, 0)

    @pl.when(i >= 1)
    def _():
        sel_step(first - 1, NEAR_TILES, near(0))
        win_step(first - 1, NEAR_TILES, near(0))

    @pl.when(i == 0)
    def _():
        sel_step(0, own, near(1))
        win_step(0, own, near(1))

    n_back = WINDOW // TK

    def back_step(start_tile, n_tiles, offset):
        masks = {}

        def add_rows(g, r0):
            if offset >= r0 + ROW_BLOCK - 1:
                return None
            if r0 not in masks:
                r = r0 + lax.broadcasted_iota(jnp.int32, (ROW_BLOCK, n_tiles * TK), 0)
                kk = offset + lax.broadcasted_iota(jnp.int32, (ROW_BLOCK, n_tiles * TK), 1)
                masks[r0] = jnp.where(kk >= r, 0.0, NEG_INF)
            return masks[r0]
        win_step(start_tile, n_tiles, add_rows)

    @pl.when(first >= n_back)
    def _():
        back_step(first - n_back, n_back - 1, 0)

    for i_small in range(1, -(-n_back // own)):
        if i_small * own > 1:
            @pl.when(i == i_small)
            def _():
                back_step(0, i_small * own - 1, (n_back - i_small * own) * TK)

    def finish(acc_scr):
        acc = acc_scr[...]
        return _unstack_heads(acc[..., :HEAD_DIM] / acc[..., HEAD_DIM:])

    gates = jnp.dot(jax.nn.sigmoid(glog_ref[...]), gexp_ref[...], preferred_element_type=_F32,
                    precision=lax.Precision.HIGHEST)
    width = Q_PER_KV * HEAD_DIM
    o_ref[...] = (gates[:, :width] * oc_ref[...] + gates[:, width:2 * width] * finish(acc_s)
                  + gates[:, 2 * width:] * finish(acc_w))


def _nsa_prompt_side(kv, keys_t, vals, prm):
    import numpy as np
    bsz, t_len = kv.shape[:2]
    n_cmp = (t_len - CMP_LEN) // CMP_STRIDE + 1
    n_cmp_pad = t_len // CMP_STRIDE
    assert n_cmp == n_cmp_pad - 1 and n_cmp_pad % LANES == 0 and t_len % TQ == 0
    n_sel = t_len // SEL_BLOCK
    grouped = lambda slot: kv[:, :, slot].reshape(bsz, n_cmp_pad, CMP_STRIDE * N_KV_HEADS * HEAD_DIM)
    one_page = jnp.arange(bsz, dtype=jnp.int32)[:, None]
    k_cmp = _cmp_pages(grouped(0), one_page, prm['cmp_pos_k'], prm['cmp_w1_k'], prm['cmp_w2_k'])
    v_cmp = _cmp_pages(grouped(1), one_page, prm['cmp_pos_v'], prm['cmp_w1_v'], prm['cmp_w2_v'])
    t_last = lambda a: jnp.transpose(a, (0, 2, 3, 1))
    t_rows = lambda a: jnp.transpose(a, (0, 2, 1, 3))
    pad_c = lambda a: a.reshape(bsz, n_cmp_pad, N_KV_HEADS, HEAD_DIM)
    rel_bias = prm['rel_bias']
    c_start = np.arange(n_cmp_pad)[None, :] * CMP_STRIDE
    s_start = np.arange(n_sel)[:, None] * SEL_BLOCK
    ovt = ((c_start < s_start + SEL_BLOCK) & (c_start + CMP_LEN > s_start) & (np.arange(n_cmp_pad)[None, :] < n_cmp))
    expand = np.arange(n_sel)[:, None] == (np.arange(t_len)[None, :] // SEL_BLOCK)
    lanes = np.arange(LANES)[:, None]
    cols = np.arange(N_BRANCH * Q_PER_KV * HEAD_DIM)[None, :]
    gexp = lanes == (cols // HEAD_DIM % Q_PER_KV) * N_BRANCH + cols // (Q_PER_KV * HEAD_DIM)
    g_cmp, near = _bias_tiles(rel_bias, n_cmp_pad)
    return {
        'kct': t_last(pad_c(k_cmp)), 'vc': t_rows(pad_c(v_cmp)),
        'keys_t': keys_t, 'vals': vals.reshape(bsz, t_len, vals.shape[-1]),
        'ovt': jnp.asarray(ovt, _BF16), 'expand': jnp.asarray(expand, _BF16), 'gexp': jnp.asarray(gexp, _F32),
        'g_cmp': g_cmp, 'near': near,
    }


def _nsa_prompt_attention(q, glog, side):
    bsz, t_len, _ = q.shape
    n_t = t_len // TQ
    width = Q_PER_KV * HEAD_DIM
    n_cmp_pad = side['kct'].shape[-1]
    n_sel = side['ovt'].shape[0]
    grid = (bsz, N_KV_HEADS, n_t)
    params = pltpu.CompilerParams(dimension_semantics=("parallel", "parallel", "arbitrary"),
                                  vmem_limit_bytes=VMEM_LIMIT)
    q_spec = pl.BlockSpec((None, TQ, width), lambda b, h, i: (b, i, h))
    per_bh = lambda *shape: pl.BlockSpec((None, None) + shape, lambda b, h, i: (b, h, 0, 0))
    per_h = lambda *shape: pl.BlockSpec((None,) + shape, lambda b, h, i: (h,) + (0,) * len(shape))
    const = lambda *shape: pl.BlockSpec(shape, lambda b, h, i: (0,) * len(shape))
    selt_spec = pl.BlockSpec((None, None, n_sel, TQ), lambda b, h, i: (b, h, 0, i))
    o_c, selt = pl.pallas_call(
        _nsa_cmp_kernel,
        grid=grid,
        in_specs=[q_spec, per_bh(HEAD_DIM, n_cmp_pad), per_bh(n_cmp_pad, HEAD_DIM),
                  per_h(Q_PER_KV * TQ, n_cmp_pad), const(n_sel, n_cmp_pad)],
        out_specs=[q_spec, selt_spec],
        out_shape=[jax.ShapeDtypeStruct((bsz, t_len, N_HEADS * HEAD_DIM), _F32),
                   jax.ShapeDtypeStruct((bsz, N_KV_HEADS, n_sel, t_len), _F32)],
        compiler_params=params,
        name="nsa_cmp_select",
    )(q, side['kct'], side['vc'], side['g_cmp'], side['ovt'])
    stat = pltpu.VMEM((Q_PER_KV, TQ, 2 * HEAD_DIM), _F32)
    keys_spec = lambda branch: pl.BlockSpec((HEAD_DIM, t_len), lambda b, h, i: (branch * N_KV_HEADS + h, b))
    vals_spec = lambda branch: pl.BlockSpec((None, t_len, 2 * HEAD_DIM),
                                            lambda b, h, i: (b, 0, branch * N_KV_HEADS + h))
    return pl.pallas_call(
        _nsa_sel_win_kernel,
        grid=grid,
        in_specs=[q_spec, q_spec, pl.BlockSpec((None, TQ, LANES), lambda b, h, i: (b, i, h)), selt_spec,
                  keys_spec(0), vals_spec(0), keys_spec(1), vals_spec(1),
                  per_h(Q_PER_KV * TQ, NEAR_TILES * TK), const(n_sel, t_len),
                  const(LANES, N_BRANCH * width)],
        out_specs=q_spec,
        out_shape=jax.ShapeDtypeStruct((bsz, t_len, N_HEADS * HEAD_DIM), _F32),
        scratch_shapes=[stat, stat, stat, stat],
        compiler_params=params,
        name="nsa_select_window",
    )(q, o_c, glog, selt, side['keys_t'], side['vals'], side['keys_t'], side['vals'], side['near'],
      side['expand'], side['gexp'])


PAGES_PER_STEP = 4
GROUPS_PER_PAGE_ROWS = CMP_STRIDE


def _dot_nt(a, b):
    return lax.dot_general(a, b, (((1,), (1,)), ((), ())), preferred_element_type=_F32)


def _cmp_pages_kernel(pt_ref, *refs, n_pages):
    del pt_ref
    pages = refs[:n_pages]
    posa_ref, posb_ref, w1a_ref, w1b_ref, w2_ref, o_ref = refs[n_pages:]
    x = jnp.concatenate([pg[...] for pg in pages], axis=0)
    z0 = jnp.dot((x + posa_ref[...]).astype(_BF16), w1a_ref[...], preferred_element_type=_F32)
    z1 = jnp.dot((x + posb_ref[...]).astype(_BF16), w1b_ref[...], preferred_element_type=_F32)
    pre = z0 + pltpu.roll(z1, z1.shape[0] - 1, 0)
    o_ref[...] = jnp.dot(jax.nn.gelu(pre).astype(_BF16), w2_ref[...], preferred_element_type=_F32).astype(o_ref.dtype)


def _cmp_pages(grouped, page_table, pos_emb, w1, w2):
    n_seq, n_pages = page_table.shape
    groups_per_page, flat = grouped.shape[1:]
    width = N_KV_HEADS * HEAD_DIM
    hidden = w1.shape[1]
    eye = jnp.eye(N_KV_HEADS, dtype=_F32)
    w1r = w1.reshape(2, CMP_STRIDE, HEAD_DIM, hidden)
    bd1 = jnp.einsum('hk,abdf->abhdkf', eye, w1r).reshape(2, flat, N_KV_HEADS * hidden).astype(_BF16)
    bd2 = jnp.einsum('hk,fd->hfkd', eye, w2).reshape(N_KV_HEADS * hidden, width).astype(_BF16)
    pos_t = jnp.broadcast_to(pos_emb.reshape(2, CMP_STRIDE, 1, HEAD_DIM),
                             (2, CMP_STRIDE, N_KV_HEADS, HEAD_DIM)).reshape(2, 1, flat)
    n_rows = n_pages * groups_per_page
    page_spec = lambda k: pl.BlockSpec((None, groups_per_page, flat), lambda b, pt: (pt[b, k], 0, 0))
    const = lambda *shape: pl.BlockSpec(shape, lambda b, pt: (0,) * len(shape))
    return pl.pallas_call(
        functools.partial(_cmp_pages_kernel, n_pages=n_pages),
        grid_spec=pltpu.PrefetchScalarGridSpec(
            num_scalar_prefetch=1,
            grid=(n_seq,),
            in_specs=[page_spec(k) for k in range(n_pages)] + [
                const(1, flat), const(1, flat), const(flat, N_KV_HEADS * hidden), const(flat, N_KV_HEADS * hidden),
                const(N_KV_HEADS * hidden, width)],
            out_specs=pl.BlockSpec((None, n_rows, width), lambda b, pt: (b, 0, 0)),
        ),
        out_shape=jax.ShapeDtypeStruct((n_seq, n_rows, width), _BF16),
        compiler_params=pltpu.CompilerParams(dimension_semantics=("parallel",), vmem_limit_bytes=VMEM_LIMIT),
        name="cmp_pages",
    )(page_table, *([grouped] * n_pages), pos_t[0], pos_t[1], bd1[0], bd1[1], bd2)


def _pick_own_kv_head(x, n_q):
    rows_per = Q_PER_KV * n_q
    return jnp.concatenate([x[h * rows_per:(h + 1) * rows_per, h * HEAD_DIM:(h + 1) * HEAD_DIM]
                            for h in range(N_KV_HEADS)], axis=0)


def _nsa_sample_kernel(pt_ref, q_ref, gl_ref, kc_ref, vc_ref, win_ref, new_ref, *rest, n_q, past_len, n_sel):
    del pt_ref
    pages = rest[:PAGES_PER_STEP]
    (tcmp_ref, tsel_ref, twin_ref, exp_ref, ov_ref, o_ref,
     addsel, m_scr, l_scr, acc_scr, oc_scr, ow_scr) = rest[PAGES_PER_STEP:]
    j = pl.program_id(1)
    rows = N_HEADS * n_q
    width = N_KV_HEADS * HEAD_DIM
    w_buf = win_ref.shape[1]
    step_keys = PAGES_PER_STEP * pages[0].shape[1]
    rr = lax.broadcasted_iota(jnp.int32, (rows, width), 0)
    cc = lax.broadcasted_iota(jnp.int32, (rows, width), 1)
    q4 = jnp.concatenate([q_ref[...] * (HEAD_DIM ** -0.5)] * N_KV_HEADS, axis=1)
    qbd = jnp.where(rr // (Q_PER_KV * n_q) == cc // HEAD_DIM, q4, 0.0).astype(_BF16)
    new_rows = jnp.concatenate([new_ref[...], jnp.zeros((LANES - n_q, new_ref.shape[1]), _F32)], axis=0).astype(_BF16)

    @pl.when(j == 0)
    def _first():
        s = _dot_nt(qbd, kc_ref[...]) + tcmp_ref[...]
        p = jnp.exp(s - jnp.max(s, axis=-1, keepdims=True))
        pb = (p / jnp.sum(p, axis=-1, keepdims=True)).astype(_BF16)
        oc_scr[...] = _pick_own_kv_head(jnp.dot(pb, vc_ref[...], preferred_element_type=_F32), n_q)
        pm = jnp.dot(pb, ov_ref[...], preferred_element_type=_F32)
        imp = jnp.concatenate(
            [sum(pm[(h * Q_PER_KV + g) * n_q:(h * Q_PER_KV + g + 1) * n_q] for g in range(Q_PER_KV))
             for h in range(N_KV_HEADS)], axis=0)
        blk = lax.broadcasted_iota(jnp.int32, imp.shape, 1)
        t = past_len + lax.broadcasted_iota(jnp.int32, imp.shape, 0) % n_q
        cur = t // SEL_BLOCK
        forced = (blk == 0) | (blk == cur) | (blk == cur - 1)
        score = jnp.where(forced, FORCE, jnp.where(blk * SEL_BLOCK <= t, imp, -FORCE))
        score = jnp.where(blk < n_sel, score, -jnp.inf)
        rank = jnp.zeros(imp.shape, jnp.int32)
        for sp in range(n_sel):
            col = score[:, sp:sp + 1]
            rank += ((col > score) | ((col == score) & (blk > sp))).astype(jnp.int32)
        sel = (rank < min(SEL_TOPK, n_sel)).astype(_BF16)
        sel_rows = jnp.concatenate([sel[h * n_q:(h + 1) * n_q] for h in range(N_KV_HEADS) for _ in range(Q_PER_KV)],
                                   axis=0)
        addsel[...] = (jnp.dot(sel_rows, exp_ref[...], preferred_element_type=_F32) - 1.0) * (-NEG_INF) + tsel_ref[...]
        kw_t = win_ref[:width, :].astype(_BF16)
        vw_t = win_ref[width:, :].astype(_BF16)
        s1 = jnp.dot(qbd, kw_t, preferred_element_type=_F32) + twin_ref[:, :w_buf]
        s2 = _dot_nt(qbd, new_rows[:, 2 * width:3 * width]) + twin_ref[:, w_buf:]
        m = jnp.maximum(jnp.max(s1, axis=-1, keepdims=True), jnp.max(s2, axis=-1, keepdims=True))
        p1 = jnp.exp(s1 - m)
        p2 = jnp.exp(s2 - m)
        l = jnp.sum(p1, axis=-1, keepdims=True) + jnp.sum(p2, axis=-1, keepdims=True)
        ow = (_dot_nt(p1.astype(_BF16), vw_t)
              + jnp.dot(p2.astype(_BF16), new_rows[:, 3 * width:], preferred_element_type=_F32))
        ow_scr[...] = _pick_own_kv_head(ow / l, n_q)
        m_scr[...] = jnp.full(m_scr.shape, NEG_INF, _F32)
        l_scr[...] = jnp.zeros(l_scr.shape, _F32)
        acc_scr[...] = jnp.zeros(acc_scr.shape, _F32)

    def flash(s, pv):
        m_old = m_scr[...]
        m_new = jnp.maximum(m_old, jnp.max(s, axis=-1, keepdims=True))
        alpha = jnp.exp(m_old - m_new)
        p = jnp.exp(s - m_new)
        m_scr[...] = m_new
        l_scr[...] = alpha * l_scr[...] + jnp.sum(p, axis=-1, keepdims=True)
        acc_scr[...] = alpha * acc_scr[...] + pv(p.astype(_BF16))

    k_t = jnp.concatenate([pg[:width, :] for pg in pages], axis=1).astype(_BF16)
    v_t = jnp.concatenate([pg[width:, :] for pg in pages], axis=1).astype(_BF16)
    flash(jnp.dot(qbd, k_t, preferred_element_type=_F32)
          + addsel[:, pl.ds(pl.multiple_of(j * step_keys, step_keys), step_keys)], lambda p: _dot_nt(p, v_t))

    @pl.when(j == pl.num_programs(1) - 1)
    def _last():
        flash(_dot_nt(qbd, new_rows[:, :width]) + addsel[:, past_len:],
              lambda p: jnp.dot(p, new_rows[:, width:2 * width], preferred_element_type=_F32))
        o_s = _pick_own_kv_head(acc_scr[...] / l_scr[...], n_q)
        gates = jax.nn.sigmoid(gl_ref[...])
        o_ref[...] = gates[:, 0:1] * oc_scr[...] + gates[:, 1:2] * o_s + gates[:, 2:3] * ow_scr[...]


def _nsa_sample_side(cache_kv, page_table, cache_win, kv_new, rel_bias, prm):
    import numpy as np
    n_seq, n_pages = page_table.shape
    n_phys, page_rows = cache_kv.shape[:2]
    n_q = kv_new.shape[1]
    width = N_KV_HEADS * HEAD_DIM
    past_len = n_pages * page_rows
    w_buf = cache_win.shape[1]
    tk = past_len + n_q
    n_cmp = (tk - CMP_LEN) // CMP_STRIDE + 1
    n_cmp_pad = past_len // CMP_STRIDE
    n_sel = -(-tk // SEL_BLOCK)
    assert n_cmp == n_cmp_pad - 1 and n_cmp_pad == LANES and n_sel <= LANES and n_q <= LANES
    assert page_rows % SEL_BLOCK == 0 and n_pages % PAGES_PER_STEP == 0 and w_buf == min(WINDOW, past_len)
    cache_t = jnp.transpose(cache_kv, (0, 2, 3, 4, 1)).reshape(n_phys, KV_SLOTS_PAGED * width, page_rows)
    win_t = jnp.transpose(cache_win, (0, 2, 3, 4, 1)).reshape(n_seq, KV_SLOTS_WIN * width, w_buf)
    gpp = page_rows // CMP_STRIDE
    grouped = cache_kv[:, :, :2].reshape(n_phys, gpp, CMP_STRIDE, 2, width)
    grouped = jnp.transpose(grouped, (3, 0, 1, 2, 4)).reshape(2, n_phys, gpp, CMP_STRIDE * width)
    kc = _cmp_pages(grouped[0], page_table, prm['cmp_pos_k'], prm['cmp_w1_k'], prm['cmp_w2_k'])
    vc = _cmp_pages(grouped[1], page_table, prm['cmp_pos_v'], prm['cmp_w1_v'], prm['cmp_w2_v'])
    rows = N_HEADS * n_q
    t = past_len + np.arange(n_q)[:, None]
    masked = lambda ok, d: jnp.where(jnp.asarray(ok), _bias_by_distance(rel_bias, d), NEG_INF).reshape(rows, -1)
    n = np.arange(n_cmp_pad)[None, :]
    d_c = t - (n * CMP_STRIDE + CMP_LEN - 1)
    t_cmp = masked((d_c >= 0) & (n < n_cmp), d_c)
    pos = np.arange(past_len + LANES)[None, :]
    t_sel = masked((t - pos >= 0) & (pos < tk), t - pos)
    l = np.arange(w_buf + LANES)[None, :]
    d_w = t - (past_len - w_buf + l)
    t_win = masked((d_w >= 0) & (d_w <= WINDOW) & (l < w_buf + n_q), d_w)
    expand = (np.arange(LANES)[:, None] == pos // SEL_BLOCK) & (np.arange(LANES)[:, None] < n_sel)
    c_start = np.arange(n_cmp_pad)[:, None] * CMP_STRIDE
    s_start = np.arange(LANES)[None, :] * SEL_BLOCK
    ov = ((c_start < s_start + SEL_BLOCK) & (c_start + CMP_LEN > s_start) & (np.arange(n_cmp_pad)[:, None] < n_cmp)
          & (np.arange(LANES)[None, :] < n_sel))
    return {
        'cache_t': cache_t, 'page_table': page_table, 'kc': kc, 'vc': vc, 'win_t': win_t,
        'new': kv_new.reshape(n_seq, n_q, -1)[:, :, 2 * width:],
        't_cmp': t_cmp, 't_sel': t_sel, 't_win': t_win,
        'expand': jnp.asarray(expand, _BF16), 'ov': jnp.asarray(ov, _BF16), 'n_sel': n_sel, 'past_len': past_len,
    }


def _nsa_sample_attention(q, glog, side):
    n_seq, n_q, _ = q.shape
    glog = glog.reshape(n_seq, n_q, N_KV_HEADS, LANES)[..., :Q_PER_KV * N_BRANCH]
    rows = N_HEADS * n_q
    width = N_KV_HEADS * HEAD_DIM
    page_table = side['page_table']
    n_pages = page_table.shape[1]
    page_rows = side['cache_t'].shape[2]
    past_len = side['past_len']
    w_buf = side['win_t'].shape[2]
    n_steps = n_pages // PAGES_PER_STEP
    q_rows = jnp.transpose(q.reshape(n_seq, n_q, N_HEADS, HEAD_DIM), (0, 2, 1, 3)).reshape(n_seq, rows, HEAD_DIM)
    gl = jnp.transpose(glog.reshape(n_seq, n_q, N_HEADS, N_BRANCH), (0, 2, 1, 3)).reshape(n_seq, rows, N_BRANCH)
    gl = jnp.pad(gl, ((0, 0), (0, 0), (0, LANES - N_BRANCH)))
    per_seq = lambda *shape: pl.BlockSpec((None,) + shape, lambda b, j, pt: (b,) + (0,) * len(shape))
    const = lambda *shape: pl.BlockSpec(shape, lambda b, j, pt: (0,) * len(shape))
    page_spec = lambda k: pl.BlockSpec((None, 2 * width, page_rows),
                                       lambda b, j, pt: (pt[b, j * PAGES_PER_STEP + k], 1, 0))
    out = pl.pallas_call(
        functools.partial(_nsa_sample_kernel, n_q=n_q, past_len=past_len, n_sel=side['n_sel']),
        grid_spec=pltpu.PrefetchScalarGridSpec(
            num_scalar_prefetch=1,
            grid=(n_seq, n_steps),
            in_specs=[per_seq(rows, HEAD_DIM), per_seq(rows, LANES), per_seq(LANES, width), per_seq(LANES, width),
                      per_seq(2 * width, w_buf), per_seq(n_q, 4 * width)]
            + [page_spec(k) for k in range(PAGES_PER_STEP)]
            + [const(rows, LANES), const(rows, past_len + LANES), const(rows, w_buf + LANES),
               const(LANES, past_len + LANES), const(LANES, LANES)],
            out_specs=per_seq(rows, HEAD_DIM),
            scratch_shapes=[pltpu.VMEM((rows, past_len + LANES), _F32), pltpu.VMEM((rows, 1), _F32),
                            pltpu.VMEM((rows, 1), _F32), pltpu.VMEM((rows, width), _F32),
                            pltpu.VMEM((rows, HEAD_DIM), _F32), pltpu.VMEM((rows, HEAD_DIM), _F32)],
        ),
        out_shape=jax.ShapeDtypeStruct((n_seq, rows, HEAD_DIM), _F32),
        compiler_params=pltpu.CompilerParams(dimension_semantics=("parallel", "arbitrary"),
                                             vmem_limit_bytes=VMEM_LIMIT),
        name="nsa_sample",
    )(page_table, q_rows, gl, side['kc'], side['vc'], side['win_t'], side['new'],
      *([side['cache_t']] * PAGES_PER_STEP), side['t_cmp'], side['t_sel'], side['t_win'], side['expand'], side['ov'])
    return jnp.transpose(out.reshape(n_seq, N_HEADS, n_q, HEAD_DIM), (0, 2, 1, 3)).reshape(n_seq, n_q, N_HEADS * HEAD_DIM)


def _run_trunk(x, p, s0_re, s0_im, paged, win_buf, prm):
    bsz, t_len, _ = x.shape
    ssm_re, ssm_im = [], []
    side = None
    kv_rows_new = None
    win_state = None
    y = None
    x2 = x.reshape(-1, D_MODEL)
    as_seq = lambda a: a.reshape(bsz, t_len, a.shape[-1])
    for i in range(DEPTH):
        if i < N_A_LAYERS:
            u2, = _rows_call(lambda xt, gain: (_norm(xt, gain),), [x2], [prm['g_mix'][i].reshape(1, -1)],
                             [D_MODEL], "s5_norm")
            ssm = (prm['ssm_a_re'][i], prm['ssm_a_im'][i], prm['ssm_log_dt'][i], prm['ssm_b_re'][i],
                   prm['ssm_b_im'][i], prm['ssm_c_re'][i], prm['ssm_c_im'][i], prm['ssm_d'][i])
            s0 = _state_to_tiles(s0_re[i], s0_im[i])
            if t_len == 8:
                ys, s_fin = _s5_scan(u2[None], jnp.transpose(s0, (1, 0, 2))[None], False, *ssm)
                s_re, s_im = _state_from_tiles(jnp.transpose(s_fin[0], (1, 0, 2)))
            else:
                assert t_len % (16 * 8) == 0
                ys, s_fin = _s5_scan(as_seq(u2), s0[:, :, None, :], True, *ssm)
                s_re, s_im = _state_from_tiles(s_fin[:, :, 0, :])
            ssm_re.append(s_re)
            ssm_im.append(s_im)
            x2 = _glu_residual(x2, ys.reshape(-1, D_MODEL), prm['w_glu'][i])
        else:
            j = i - N_A_LAYERS
            q, glog = _qg_project(x2, prm['g_mix'][i], prm['w_qg'][j])
            attend = _nsa_prompt_attention if paged is None else _nsa_sample_attention
            mixed = attend(as_seq(q), as_seq(glog), side)
            x2 = _wo_residual(x2, mixed.reshape(-1, D_MODEL), prm['w_o'][j])
        x2 = _moe_layer(x2, prm['g_ffn'][i], prm['w_route_group'][i], prm['b_route_group'][i],
                        prm['w_route_expert'][i], prm['b_route_expert'][i], prm['w_exp_up'][i],
                        prm['w_exp_down'][i])
        ple = (x2, p[i].reshape(-1, p.shape[-1]), prm['g_ple'][i], prm['w_ple_gate'][i], prm['w_ple_proj'][i])
        if i == N_A_LAYERS - 1:
            x2, kv2, *operands = _ple_residual(*ple, g_next=prm['g_kv'], w_next=prm['w_kv'],
                                               attention_operands=paged is None)
            kv = kv2.reshape(bsz, t_len, KV_SLOTS_PAGED + KV_SLOTS_WIN, N_KV_HEADS, HEAD_DIM)
            kv_rows_new, win_new = kv[:, :, :KV_SLOTS_PAGED], kv[:, :, KV_SLOTS_PAGED:]
            if paged is None:
                win_state = win_new[:, -min(WINDOW, t_len):]
                side = _nsa_prompt_side(kv, *operands, prm)
            else:
                w_buf = win_buf.shape[1]
                win_state = jnp.concatenate([win_buf, win_new], axis=1)[:, -w_buf:]
                side = _nsa_sample_side(paged[0], paged[1], win_buf, kv, prm['rel_bias'], prm)
        elif i == DEPTH - 1:
            x2, y = _ple_residual(*ple, g_next=prm['g_final'])
        else:
            x2, = _ple_residual(*ple)
    return as_seq(y), kv_rows_new, win_state, jnp.stack(ssm_re), jnp.stack(ssm_im)


def kernel(x_prompt, x_sample, p_prompt, p_sample, cache_kv, cache_win, state_ssm_re, state_ssm_im, page_table,
           g_mix, g_ffn, g_ple, g_kv, g_final,
           ssm_a_re, ssm_a_im, ssm_log_dt, ssm_b_re, ssm_b_im, ssm_c_re, ssm_c_im, ssm_d, w_glu,
           w_kv, cmp_pos_k, cmp_pos_v, cmp_w1_k, cmp_w2_k, cmp_w1_v, cmp_w2_v, w_qg, w_o, rel_bias,
           w_route_group, b_route_group, w_route_expert, b_route_expert, w_exp_up, w_exp_down,
           w_ple_proj, w_ple_gate):
    prm = {
        'g_mix': g_mix, 'g_ffn': g_ffn, 'g_ple': g_ple, 'g_kv': g_kv, 'g_final': g_final,
        'ssm_a_re': ssm_a_re, 'ssm_a_im': ssm_a_im, 'ssm_log_dt': ssm_log_dt,
        'ssm_b_re': ssm_b_re, 'ssm_b_im': ssm_b_im, 'ssm_c_re': ssm_c_re, 'ssm_c_im': ssm_c_im,
        'ssm_d': ssm_d, 'w_glu': w_glu,
        'w_kv': w_kv, 'cmp_pos_k': cmp_pos_k, 'cmp_pos_v': cmp_pos_v,
        'cmp_w1_k': cmp_w1_k, 'cmp_w2_k': cmp_w2_k, 'cmp_w1_v': cmp_w1_v, 'cmp_w2_v': cmp_w2_v,
        'w_qg': w_qg, 'w_o': w_o, 'rel_bias': rel_bias,
        'w_route_group': w_route_group, 'b_route_group': b_route_group,
        'w_route_expert': w_route_expert, 'b_route_expert': b_route_expert,
        'w_exp_up': w_exp_up, 'w_exp_down': w_exp_down,
        'w_ple_proj': w_ple_proj, 'w_ple_gate': w_ple_gate,
    }
    zero_state = jnp.zeros((N_A_LAYERS, x_prompt.shape[0], SSM_GROUPS, SSM_STATE), _F32)
    y_prompt, kv_prompt, win_prompt, ssm_re_prompt, ssm_im_prompt = _run_trunk(
        x_prompt, p_prompt, zero_state, zero_state, None, None, prm)
    y_sample, kv_sample, win_sample, ssm_re_sample, ssm_im_sample = _run_trunk(
        x_sample, p_sample, state_ssm_re, state_ssm_im, (cache_kv, page_table), cache_win, prm)
    return (y_prompt, y_sample, kv_prompt, win_prompt, ssm_re_prompt, ssm_im_prompt,
            kv_sample, win_sample, ssm_re_sample, ssm_im_sample)
```

```python
import functools
import math

import jax
import jax.numpy as jnp
from jax import lax
from jax.experimental import pallas as pl
from jax.experimental.pallas import tpu as pltpu

D_MODEL = 1024
DEPTH = 4
N_A_LAYERS = DEPTH // 2
SSM_GROUP = 16
SSM_GROUPS = D_MODEL // SSM_GROUP
SSM_STATE = 64
N_HEADS = 16
HEAD_DIM = D_MODEL // N_HEADS
N_KV_HEADS = 4
Q_PER_KV = N_HEADS // N_KV_HEADS
CMP_LEN = 32
CMP_STRIDE = 16
SEL_BLOCK = 64
SEL_TOPK = 16
WINDOW = 512
N_BRANCH = 3
KV_SLOTS_PAGED = 4
KV_SLOTS_WIN = 2
Q_BLOCK = 64
N_BUCKETS = 32
MAX_DISTANCE = 128
N_EXPERT_GROUPS = 4
EXPERTS_PER_GROUP = 4
N_EXPERTS = N_EXPERT_GROUPS * EXPERTS_PER_GROUP
EXPERT_TOPK = 2
D_EXPERT = 256
RMS_EPS = 1e-6
NEG_INF = -1e30
FORCE = 1e4

LANES = 128
VMEM_LIMIT = 48 * 1024 * 1024

_F32 = jnp.float32
_BF16 = jnp.bfloat16


def _moe_kernel(x_ref, g_ref, wr_ref, br_ref, wup_ref, wdn_ref, o_ref, h_scr, comb_scr, acc_scr):
    e = pl.program_id(1)
    lane = lax.broadcasted_iota(jnp.int32, comb_scr.shape, 1)

    @pl.when(e == 0)
    def _route():
        x = x_ref[...]
        h = x * lax.rsqrt(jnp.mean(x * x, axis=-1, keepdims=True) + RMS_EPS) * g_ref[...]
        h_scr[...] = h.astype(_BF16)
        logits = jnp.dot(h, wr_ref[...], preferred_element_type=_F32,
                         precision=lax.Precision.HIGHEST) + br_ref[...]
        is_grp = (lane >= N_EXPERTS) & (lane < N_EXPERTS + N_EXPERT_GROUPS)
        lg = jnp.where(is_grp, logits, -jnp.inf)
        gmax = jnp.max(lg, axis=-1, keepdims=True)
        gi = jnp.min(jnp.where(lg == gmax, lane, LANES), axis=-1, keepdims=True) - N_EXPERTS
        gp = 1.0 / jnp.sum(jnp.where(is_grp, jnp.exp(lg - gmax), 0.0), axis=-1, keepdims=True)
        in_grp = (lane < N_EXPERTS) & ((lane // EXPERTS_PER_GROUP) == gi)
        le = jnp.where(in_grp, logits, -jnp.inf)
        m1 = jnp.max(le, axis=-1, keepdims=True)
        i1 = jnp.min(jnp.where(le == m1, lane, LANES), axis=-1, keepdims=True)
        le2 = jnp.where(lane == i1, -jnp.inf, le)
        m2 = jnp.max(le2, axis=-1, keepdims=True)
        i2 = jnp.min(jnp.where(le2 == m2, lane, LANES), axis=-1, keepdims=True)
        e2 = jnp.exp(m2 - m1)
        den = 1.0 + e2
        comb_scr[...] = jnp.where(lane == i1, gp / den, jnp.where(lane == i2, gp * e2 / den, 0.0))
        acc_scr[...] = x

    up = jnp.dot(h_scr[...], wup_ref[0].astype(_BF16), preferred_element_type=_F32)
    a = up[:, :D_EXPERT]
    b = up[:, D_EXPERT:]
    c = jnp.sum(jnp.where(lane == e, comb_scr[...], 0.0), axis=-1, keepdims=True)
    act = (a * jax.nn.sigmoid(a)) * b * c
    acc_scr[...] += jnp.dot(act.astype(_BF16), wdn_ref[0].astype(_BF16), preferred_element_type=_F32)

    @pl.when(e == N_EXPERTS - 1)
    def _store():
        o_ref[...] = acc_scr[...]


def _moe_layer(x2, g, w_rg, b_rg, w_re, b_re, w_up, w_dn, layer):
    n_tok, d = x2.shape
    tm = min(n_tok, 1024)
    wr = jnp.zeros((d, LANES), _F32).at[:, :N_EXPERTS].set(w_re).at[:, N_EXPERTS:N_EXPERTS + N_EXPERT_GROUPS].set(w_rg)
    br = jnp.zeros((1, LANES), _F32).at[0, :N_EXPERTS].set(b_re).at[0, N_EXPERTS:N_EXPERTS + N_EXPERT_GROUPS].set(b_rg)
    return pl.pallas_call(
        _moe_kernel,
        grid=(n_tok // tm, N_EXPERTS),
        in_specs=[
            pl.BlockSpec((tm, d), lambda i, e: (i, 0)),
            pl.BlockSpec((1, d), lambda i, e: (0, 0)),
            pl.BlockSpec((d, LANES), lambda i, e: (0, 0)),
            pl.BlockSpec((1, LANES), lambda i, e: (0, 0)),
            pl.BlockSpec((None, 1, d, 2 * D_EXPERT), lambda i, e: (layer, e, 0, 0)),
            pl.BlockSpec((None, 1, D_EXPERT, d), lambda i, e: (layer, e, 0, 0)),
        ],
        out_specs=pl.BlockSpec((tm, d), lambda i, e: (i, 0)),
        out_shape=jax.ShapeDtypeStruct((n_tok, d), _F32),
        scratch_shapes=[
            pltpu.VMEM((tm, d), _BF16),
            pltpu.VMEM((tm, LANES), _F32),
            pltpu.VMEM((tm, d), _F32),
        ],
        compiler_params=pltpu.CompilerParams(
            dimension_semantics=("parallel", "arbitrary"), vmem_limit_bytes=VMEM_LIMIT),
        name="hmoe",
    )(x2, g.reshape(1, d), wr, br, w_up, w_dn)


ROW_TILE = 512


def _rows_kernel(*refs, body, n_in):
    outs = body(*[r[...] for r in refs[:n_in]])
    for o_ref, o in zip(refs[n_in:], outs):
        o_ref[...] = o


def _rows_call(body, rows, consts, outs, name):
    n_rows = rows[0].shape[0]
    tm = min(n_rows, ROW_TILE)
    row_spec = lambda width: pl.BlockSpec((tm, width), lambda i: (i, 0))
    const_spec = lambda c: pl.BlockSpec(c.shape, lambda i: (0,) * c.ndim)
    out_specs, out_shape = [], []
    for o in outs:
        if isinstance(o, int):
            o = (o, _F32)
        if o[0] == 'T':
            out_specs.append(pl.BlockSpec((o[1], tm), lambda i: (0, i)))
            out_shape.append(jax.ShapeDtypeStruct((o[1], n_rows), o[2]))
        else:
            out_specs.append(row_spec(o[0]))
            out_shape.append(jax.ShapeDtypeStruct((n_rows, o[0]), o[1]))
    return pl.pallas_call(
        functools.partial(_rows_kernel, body=body, n_in=len(rows) + len(consts)),
        grid=(n_rows // tm,),
        in_specs=[row_spec(r.shape[1]) for r in rows] + [const_spec(c) for c in consts],
        out_specs=out_specs,
        out_shape=out_shape,
        compiler_params=pltpu.CompilerParams(dimension_semantics=("parallel",), vmem_limit_bytes=VMEM_LIMIT),
        name=name,
    )(*rows, *consts)


def _norm(x, g):
    return x * lax.rsqrt(jnp.mean(x * x, axis=-1, keepdims=True) + RMS_EPS) * g


def _mm(x, w):
    return jnp.dot(x.astype(_BF16), w, preferred_element_type=_F32)


def _glu_residual(x2, y2, w_glu):
    def body(x, y, w):
        ag = _mm(jax.nn.gelu(y), w)
        return (x + ag[:, :D_MODEL] * jax.nn.sigmoid(ag[:, D_MODEL:]),)
    return _rows_call(body, [x2, y2], [w_glu.astype(_BF16)], [D_MODEL], "glu_residual")[0]


def _qg_project(x2, g, w_qg):
    per_kv = Q_PER_KV * N_BRANCH
    w_g = w_qg[:, N_HEADS * HEAD_DIM:].reshape(D_MODEL, N_KV_HEADS, per_kv)
    w_g = jnp.pad(w_g, ((0, 0), (0, 0), (0, LANES - per_kv))).reshape(D_MODEL, N_KV_HEADS * LANES)
    w = jnp.concatenate([w_qg[:, :N_HEADS * HEAD_DIM], w_g], axis=1).astype(_BF16)

    def body(x, gain, w):
        qg = _mm(_norm(x, gain), w)
        return qg[:, :N_HEADS * HEAD_DIM], qg[:, N_HEADS * HEAD_DIM:]
    return _rows_call(body, [x2], [g.reshape(1, -1), w], [N_HEADS * HEAD_DIM, N_KV_HEADS * LANES], "qg_project")


def _wo_residual(x2, mixed2, w_o):
    body = lambda x, m, w: (x + _mm(m, w),)
    return _rows_call(body, [x2, mixed2], [w_o.astype(_BF16)], [D_MODEL], "wo_residual")[0]


def _ple_residual(x2, p2, g_ple, w_gate, w_proj, g_next=None, w_next=None, attention_operands=False):
    consts = [g_ple.reshape(1, -1), w_gate.astype(_BF16), w_proj.astype(_BF16)]
    outs = [D_MODEL]
    width = N_KV_HEADS * HEAD_DIM
    if g_next is not None:
        consts.append(g_next.reshape(1, -1))
        outs.append(D_MODEL if w_next is None else w_next.shape[1])
    if w_next is not None:
        consts.append(w_next.astype(_BF16))
    if attention_operands:
        outs += [('T', 2 * width, _BF16), (4 * width, _BF16)]

    def body(x, p, gain, wg, wp, *nxt):
        out = x + _mm(p, wp) * jax.nn.sigmoid(_mm(_norm(x, gain), wg))
        if not nxt:
            return (out,)
        normed = _norm(out, nxt[0])
        if len(nxt) == 1:
            return out, normed
        kv = _mm(normed, nxt[1])
        if not attention_operands:
            return out, kv
        slot = lambda s: kv[:, s * width:(s + 1) * width]
        keys_t = jnp.concatenate([slot(2), slot(4)], axis=1).T.astype(_BF16)
        ones = jnp.ones((kv.shape[0], HEAD_DIM), _F32)
        vals = jnp.concatenate([piece for s in (3, 5) for h in range(N_KV_HEADS)
                                for piece in (slot(s)[:, h * HEAD_DIM:(h + 1) * HEAD_DIM], ones)], axis=1)
        return out, kv, keys_t, vals.astype(_BF16)
    return _rows_call(body, [x2, p2], consts, outs, "ple_residual")


def _cmul(ar, ai, br, bi):
    return ar * br - ai * bi, ar * bi + ai * br


S5_TILE_GROUPS = LANES // SSM_GROUP
S5_TILES = D_MODEL // LANES
S5_STATE_LANES = S5_TILE_GROUPS * SSM_STATE


def _s5_operators(a_re, a_im, log_dt, b_re, b_im, c_re, c_im, chunk):
    hp = lax.Precision.HIGHEST
    dt = jnp.exp(log_dt)[:, None]
    decay = jnp.exp(a_re * dt)
    ab_re, ab_im = decay * jnp.cos(a_im * dt), decay * jnp.sin(a_im * dt)
    den = a_re * a_re + a_im * a_im
    num_re = ab_re - 1.0
    q_re = (num_re * a_re + ab_im * a_im) / den
    q_im = (ab_im * a_re - num_re * a_im) / den
    bb_re, bb_im = _cmul(q_re[..., None], q_im[..., None], b_re, b_im)
    pw_re, pw_im = [jnp.ones_like(ab_re)], [jnp.zeros_like(ab_im)]
    for _ in range(chunk):
        nr, ni = _cmul(pw_re[-1], pw_im[-1], ab_re, ab_im)
        pw_re.append(nr)
        pw_im.append(ni)
    pw_re, pw_im = jnp.stack(pw_re), jnp.stack(pw_im)
    w_re, w_im = _cmul(pw_re[:chunk, :, :, None], pw_im[:chunk, :, :, None], bb_re[None], bb_im[None])
    k_lag = (jnp.einsum('gcp,jgpd->jgcd', c_re, w_re, precision=hp)
             - jnp.einsum('gcp,jgpd->jgcd', c_im, w_im, precision=hp))
    ca_re, ca_im = _cmul(c_re[None], c_im[None], pw_re[1:, :, None, :], pw_im[1:, :, None, :])
    import numpy as np

    def block_diag(per_group):
        a = per_group.reshape((chunk, S5_TILES, S5_TILE_GROUPS) + per_group.shape[2:])
        a = jnp.transpose(a, (1, 0, 2, 4, 3))
        n_r, n_c = a.shape[3:]
        wide = jnp.tile(a.reshape(S5_TILES, chunk, S5_TILE_GROUPS * n_r, n_c), (1, 1, 1, S5_TILE_GROUPS))
        same = (np.arange(S5_TILE_GROUPS * n_r)[:, None] // n_r) == (np.arange(S5_TILE_GROUPS * n_c)[None, :] // n_c)
        return jnp.where(same, wide, 0.0)

    k_bd = block_diag(k_lag)
    h_bd = jnp.concatenate([block_diag(w_re), block_diag(w_im)], axis=-1)
    g_bd = jnp.concatenate([block_diag(ca_re), block_diag(-ca_im)], axis=2)
    a_pow = jnp.concatenate([pw_re[chunk].reshape(S5_TILES, 1, S5_STATE_LANES),
                             pw_im[chunk].reshape(S5_TILES, 1, S5_STATE_LANES)], axis=-1)
    return k_bd.astype(_BF16), h_bd, g_bd.astype(_BF16), a_pow


def _advance(a_pow, s, e):
    half = S5_STATE_LANES
    ar, ai, sr, si = a_pow[:, :half], a_pow[:, half:], s[:, :half], s[:, half:]
    return jnp.concatenate([ar * sr - ai * si + e[:, :half], ar * si + ai * sr + e[:, half:]], axis=1)


def _s5_kernel(u_ref, k_ref, h_ref, g_ref, a_ref, d_ref, s0_ref, y_ref, sfin_ref, upad, e_scr, sin_scr, *,
               chunk, carry):
    n_rows = u_ref.shape[0]
    n_chunks = n_rows // chunk
    u = u_ref[...]
    upad[:chunk] = jnp.zeros((chunk, LANES), _F32)
    upad[chunk:] = u
    step = lax.broadcasted_iota(jnp.int32, (n_rows, LANES), 0) % chunk
    y = d_ref[...] * u
    for lag in range(chunk):
        part = jnp.dot(upad[chunk - lag:chunk - lag + n_rows].astype(_BF16), k_ref[lag], preferred_element_type=_F32)
        y += part if lag == 0 else jnp.where(step >= lag, part, 0.0)
    y_ref[...] = y
    e = None
    for lag in range(chunk):
        rows = u_ref[pl.ds(chunk - 1 - lag, n_chunks, stride=chunk), :]
        if h_ref.dtype == _F32:
            d = jnp.dot(rows, h_ref[lag], preferred_element_type=_F32, precision=lax.Precision.HIGHEST)
        else:
            d = jnp.dot(rows.astype(_BF16), h_ref[lag], preferred_element_type=_F32)
        e = d if e is None else e + d
    a_pow = a_ref[...]
    if carry:
        e_scr[...] = e
        sub = 8

        def body(jj, s):
            rows = pl.ds(pl.multiple_of(jj * sub, sub), sub)
            e_tile = e_scr[rows, :]
            starts = []
            for r in range(sub):
                starts.append(s)
                s = _advance(a_pow, s, e_tile[r:r + 1])
            sin_scr[rows, :] = jnp.concatenate(starts, axis=0)
            return s
        sfin_ref[...] = lax.fori_loop(0, n_chunks // sub, body, s0_ref[...])
    else:
        sin_scr[...] = s0_ref[...]
        sfin_ref[...] = _advance(a_pow, s0_ref[...], e)
    s_in = sin_scr[...].astype(_BF16)
    for k in range(chunk):
        rows = pl.ds(k, n_chunks, stride=chunk)
        y_ref[rows, :] = y_ref[rows, :] + jnp.dot(s_in, g_ref[k], preferred_element_type=_F32)


def _s5_scan(u, s0, carry, a_re, a_im, log_dt, b_re, b_im, c_re, c_im, d_skip):
    n_b, n_rows, _ = u.shape
    chunk = 16 if carry else 8
    n_chunks = n_rows // chunk
    n_state = s0.shape[2]
    k_bd, h_bd, g_bd, a_pow = _s5_operators(a_re, a_im, log_dt, b_re, b_im, c_re, c_im, chunk)
    if carry:
        h_bd = h_bd.astype(_BF16)
    per_tile = lambda *shape: pl.BlockSpec((None,) + shape, lambda t, b: (t,) + (0,) * len(shape))
    rows_spec = pl.BlockSpec((None, n_rows, LANES), lambda t, b: (b, 0, t))
    state_spec = pl.BlockSpec((None, None, n_state, 2 * S5_STATE_LANES), lambda t, b: (b, t, 0, 0))
    return pl.pallas_call(
        functools.partial(_s5_kernel, chunk=chunk, carry=carry),
        grid=(S5_TILES, n_b),
        in_specs=[rows_spec, per_tile(chunk, LANES, LANES), per_tile(chunk, LANES, 2 * S5_STATE_LANES),
                  per_tile(chunk, 2 * S5_STATE_LANES, LANES), per_tile(1, 2 * S5_STATE_LANES), per_tile(1, LANES),
                  state_spec],
        out_specs=[rows_spec, state_spec],
        out_shape=[jax.ShapeDtypeStruct(u.shape, _F32), jax.ShapeDtypeStruct(s0.shape, _F32)],
        scratch_shapes=[pltpu.VMEM((n_rows + chunk, LANES), _F32), pltpu.VMEM((n_chunks, 2 * S5_STATE_LANES), _F32),
                        pltpu.VMEM((n_chunks, 2 * S5_STATE_LANES), _F32)],
        compiler_params=pltpu.CompilerParams(dimension_semantics=("parallel", "parallel"),
                                             vmem_limit_bytes=VMEM_LIMIT),
        name="s5_scan",
    )(u, k_bd, h_bd, g_bd, a_pow, d_skip.reshape(S5_TILES, 1, LANES), s0)


def _state_to_tiles(s_re, s_im):
    bsz = s_re.shape[0]
    return jnp.concatenate([s_re.reshape(bsz, S5_TILES, S5_STATE_LANES), s_im.reshape(bsz, S5_TILES, S5_STATE_LANES)],
                           axis=-1)


def _state_from_tiles(s):
    bsz = s.shape[0]
    return (s[..., :S5_STATE_LANES].reshape(bsz, SSM_GROUPS, SSM_STATE),
            s[..., S5_STATE_LANES:].reshape(bsz, SSM_GROUPS, SSM_STATE))


TQ = 256
TK = 128
NEAR_TILES = TQ // TK + 1
NEAR_BUCKET_DIST = 113


def _bucket_table(dist):
    import numpy as np
    max_exact = N_BUCKETS // 2
    d = np.maximum(np.asarray(dist, np.int64), 0)
    df = np.maximum(d, 1).astype(np.float64)
    large = max_exact + (np.log(df / max_exact) / math.log(MAX_DISTANCE / max_exact)
                         * (N_BUCKETS - max_exact)).astype(np.int64)
    return np.where(d < max_exact, d, np.minimum(large, N_BUCKETS - 1)).astype(np.int32)


def _bias_by_distance(rel_bias, dist):
    import numpy as np
    onehot = _bucket_table(dist)[None] == np.arange(N_BUCKETS)[:, None, None]
    return jnp.einsum('kqc,kh->hqc', jnp.asarray(onehot, _BF16).astype(_F32), rel_bias,
                      precision=lax.Precision.HIGHEST)


def _bias_tiles(rel_bias, n_cmp_pad):
    import numpy as np
    rel_bias = rel_bias - rel_bias[N_BUCKETS - 1]
    r = np.arange(TQ)[:, None]
    c = np.arange(n_cmp_pad)[None, :]
    d_cmp = np.where(c < 8 + TQ // CMP_STRIDE, r - (CMP_LEN - 1) + CMP_STRIDE * (8 - c), 10 ** 6)
    g_cmp = _bias_by_distance(rel_bias, d_cmp).reshape(N_KV_HEADS, Q_PER_KV * TQ, n_cmp_pad)
    k = np.arange(NEAR_TILES * TK)[None, :]
    d_near = TK + r - k
    near = _bias_by_distance(rel_bias, d_near) + jnp.asarray(np.where(d_near >= 0, 0.0, NEG_INF), _F32)
    near = near.reshape(N_KV_HEADS, Q_PER_KV * TQ, NEAR_TILES * TK)
    return g_cmp, near


def _stack_heads(q):
    return jnp.concatenate([q[:, g * HEAD_DIM:(g + 1) * HEAD_DIM] for g in range(Q_PER_KV)], axis=0)


def _unstack_heads(o):
    return jnp.concatenate([o[g] for g in range(Q_PER_KV)], axis=1)


def _nsa_cmp_kernel(q_ref, kct_ref, vc_ref, g_ref, ovt_ref, oc_ref, selt_ref):
    i = pl.program_id(2)
    t0 = i * TQ
    n_cmp_pad = kct_ref.shape[-1]
    qs = _stack_heads(q_ref[...] * (HEAD_DIM ** -0.5)).astype(_BF16)
    s = jnp.dot(qs, kct_ref[...], preferred_element_type=_F32)
    shift = (i * (TQ // CMP_STRIDE) + n_cmp_pad - 8) % n_cmp_pad
    bias = pltpu.roll(g_ref[...], shift, 1)
    s3 = (s + bias).reshape(Q_PER_KV, TQ, n_cmp_pad)
    r = lax.broadcasted_iota(jnp.int32, (TQ, n_cmp_pad), 0)
    n = lax.broadcasted_iota(jnp.int32, (TQ, n_cmp_pad), 1)
    valid = (t0 + r - CMP_STRIDE * n - (CMP_LEN - 1)) >= 0
    sm = jnp.where(valid, s3, NEG_INF)
    m = jnp.max(sm, axis=-1, keepdims=True)
    p = jnp.where(valid, jnp.exp(sm - m), 0.0)
    l = jnp.sum(p, axis=-1, keepdims=True)
    pb = (p * jnp.where(l > 0.0, 1.0 / l, 0.0)).astype(_BF16)
    oc = jnp.dot(pb.reshape(Q_PER_KV * TQ, n_cmp_pad), vc_ref[...], preferred_element_type=_F32)
    oc_ref[...] = _unstack_heads(oc.reshape(Q_PER_KV, TQ, HEAD_DIM))
    imp = jnp.zeros((ovt_ref.shape[0], TQ), _F32)
    for g in range(Q_PER_KV):
        imp += lax.dot_general(ovt_ref[...], pb[g], (((1,), (1,)), ((), ())), preferred_element_type=_F32)
    n_sel = imp.shape[0]
    blk = lax.broadcasted_iota(jnp.int32, (n_sel, TQ), 0)
    t = t0 + lax.broadcasted_iota(jnp.int32, (n_sel, TQ), 1)
    cur = t // SEL_BLOCK
    forced = (blk == 0) | (blk == cur) | (blk == cur - 1)
    score = jnp.where(forced, FORCE, jnp.where(blk * SEL_BLOCK <= t, imp, -FORCE))
    rank = jnp.zeros((n_sel, TQ), _F32)
    for sp in range(n_sel):
        row = score[sp:sp + 1, :]
        beats = (row > score) | ((row == score) & (blk > sp))
        rank += jnp.where(beats, 1.0, 0.0)
    selt_ref[...] = (rank < SEL_TOPK).astype(_F32)


ROW_BLOCK = 128


def _flash_step(qs, kt, v, add_rows, m_scr, acc_scr):
    n_chunks = kt.shape[-1] // LANES
    for r0 in range(0, TQ, ROW_BLOCK):
        for g in range(Q_PER_KV):
            s = jnp.dot(qs[g * TQ + r0:g * TQ + r0 + ROW_BLOCK], kt, preferred_element_type=_F32)
            add = add_rows(g, r0)
            if add is not None:
                s = s + add
            chunks = [s[:, k * LANES:(k + 1) * LANES] for k in range(n_chunks)]
            m_old = m_scr[g, r0:r0 + ROW_BLOCK]
            m_new = jnp.maximum(m_old, jnp.max(functools.reduce(jnp.maximum, chunks), axis=-1, keepdims=True))
            alpha = jnp.exp(m_old - m_new)
            m_scr[g, r0:r0 + ROW_BLOCK] = m_new
            pv = None
            for k, sk in enumerate(chunks):
                d = jnp.dot(jnp.exp(sk - m_new).astype(_BF16), v[k * LANES:(k + 1) * LANES],
                            preferred_element_type=_F32)
                pv = d if pv is None else pv + d
            acc_scr[g, r0:r0 + ROW_BLOCK] = alpha * acc_scr[g, r0:r0 + ROW_BLOCK] + pv


FAR_TILES = 4


def _nsa_sel_win_kernel(q_ref, oc_ref, glog_ref, selt_ref, kst_ref, vs_ref, kwt_ref, vw_ref, near_ref,
                        exp_ref, gexp_ref, o_ref, seladd, m_s, acc_s, m_w, acc_w):
    i = pl.program_id(2)
    qs = _stack_heads(q_ref[...] * (HEAD_DIM ** -0.5)).astype(_BF16)
    sel = jnp.dot(selt_ref[...].T.astype(_BF16), exp_ref[...], preferred_element_type=_F32)
    seladd[...] = (sel - 1.0) * (-NEG_INF)
    for m_scr, acc_scr in ((m_s, acc_s), (m_w, acc_w)):
        m_scr[...] = jnp.full(m_scr.shape, NEG_INF, _F32)
        acc_scr[...] = jnp.zeros(acc_scr.shape, _F32)

    def keys(j, n_tiles):
        return pl.ds(pl.multiple_of(j * TK, TK), n_tiles * TK)

    def sel_step(j, n_tiles, bias):
        cols = keys(j, n_tiles)

        def add_rows(g, r0):
            add = seladd[r0:r0 + ROW_BLOCK, cols]
            return add if bias is None else add + bias(g, r0)
        _flash_step(qs, kst_ref[:, cols], vs_ref[cols, :], add_rows, m_s, acc_s)

    def win_step(j, n_tiles, add_rows):
        cols = keys(j, n_tiles)
        _flash_step(qs, kwt_ref[:, cols], vw_ref[cols, :], add_rows, m_w, acc_w)

    near = lambda lo: (lambda g, r0: near_ref[g * TQ + r0:g * TQ + r0 + ROW_BLOCK, lo * TK:])
    no_add = lambda g, r0: None
    own = TQ // TK
    first = i * own

    n_far = jnp.maximum(first - 1, 0)

    def far_many(j, carry):
        sel_step(j * FAR_TILES, FAR_TILES, None)
        return carry

    def far_one(j, carry):
        sel_step(j, 1, None)
        return carry

    lax.fori_loop(0, n_far // FAR_TILES, far_many, 0)
    lax.fori_loop((n_far // FAR_TILES) * FAR_TILES, n_far, far_one, 0)

    @pl.when(i >= 1)
    def _():
        sel_step(first - 1, NEAR_TILES, near(0))
        win_step(first - 1, NEAR_TILES, near(0))

    @pl.when(i == 0)
    def _():
        sel_step(0, own, near(1))
        win_step(0, own, near(1))

    n_back = WINDOW // TK

    def back_step(start_tile, n_tiles, offset):
        masks = {}

        def add_rows(g, r0):
            if offset >= r0 + ROW_BLOCK - 1:
                return None
            if r0 not in masks:
                r = r0 + lax.broadcasted_iota(jnp.int32, (ROW_BLOCK, n_tiles * TK), 0)
                kk = offset + lax.broadcasted_iota(jnp.int32, (ROW_BLOCK, n_tiles * TK), 1)
                masks[r0] = jnp.where(kk >= r, 0.0, NEG_INF)
            return masks[r0]
        win_step(start_tile, n_tiles, add_rows)

    @pl.when(first >= n_back)
    def _():
        back_step(first - n_back, n_back - 1, 0)

    for i_small in range(1, -(-n_back // own)):
        if i_small * own > 1:
            @pl.when(i == i_small)
            def _():
                back_step(0, i_small * own - 1, (n_back - i_small * own) * TK)

    def finish(acc_scr):
        acc = acc_scr[...]
        return _unstack_heads(acc[..., :HEAD_DIM] / acc[..., HEAD_DIM:])

    gates = jnp.dot(jax.nn.sigmoid(glog_ref[...]), gexp_ref[...], preferred_element_type=_F32,
                    precision=lax.Precision.HIGHEST)
    width = Q_PER_KV * HEAD_DIM
    o_ref[...] = (gates[:, :width] * oc_ref[...] + gates[:, width:2 * width] * finish(acc_s)
                  + gates[:, 2 * width:] * finish(acc_w))


def _nsa_prompt_side(kv, keys_t, vals, prm):
    import numpy as np
    bsz, t_len = kv.shape[:2]
    n_cmp = (t_len - CMP_LEN) // CMP_STRIDE + 1
    n_cmp_pad = t_len // CMP_STRIDE
    assert n_cmp == n_cmp_pad - 1 and n_cmp_pad % LANES == 0 and t_len % TQ == 0
    n_sel = t_len // SEL_BLOCK
    grouped = lambda slot: kv[:, :, slot].reshape(bsz, n_cmp_pad, CMP_STRIDE * N_KV_HEADS * HEAD_DIM)
    one_page = jnp.arange(bsz, dtype=jnp.int32)[:, None]
    k_cmp = _cmp_pages(grouped(0), one_page, prm['cmp_pos_k'], prm['cmp_w1_k'], prm['cmp_w2_k'])
    v_cmp = _cmp_pages(grouped(1), one_page, prm['cmp_pos_v'], prm['cmp_w1_v'], prm['cmp_w2_v'])
    t_last = lambda a: jnp.transpose(a, (0, 2, 3, 1))
    t_rows = lambda a: jnp.transpose(a, (0, 2, 1, 3))
    pad_c = lambda a: a.reshape(bsz, n_cmp_pad, N_KV_HEADS, HEAD_DIM)
    rel_bias = prm['rel_bias']
    c_start = np.arange(n_cmp_pad)[None, :] * CMP_STRIDE
    s_start = np.arange(n_sel)[:, None] * SEL_BLOCK
    ovt = ((c_start < s_start + SEL_BLOCK) & (c_start + CMP_LEN > s_start) & (np.arange(n_cmp_pad)[None, :] < n_cmp))
    expand = np.arange(n_sel)[:, None] == (np.arange(t_len)[None, :] // SEL_BLOCK)
    lanes = np.arange(LANES)[:, None]
    cols = np.arange(N_BRANCH * Q_PER_KV * HEAD_DIM)[None, :]
    gexp = lanes == (cols // HEAD_DIM % Q_PER_KV) * N_BRANCH + cols // (Q_PER_KV * HEAD_DIM)
    g_cmp, near = _bias_tiles(rel_bias, n_cmp_pad)
    return {
        'kct': t_last(pad_c(k_cmp)), 'vc': t_rows(pad_c(v_cmp)),
        'keys_t': keys_t, 'vals': vals.reshape(bsz, t_len, vals.shape[-1]),
        'ovt': jnp.asarray(ovt, _BF16), 'expand': jnp.asarray(expand, _BF16), 'gexp': jnp.asarray(gexp, _F32),
        'g_cmp': g_cmp, 'near': near,
    }


def _nsa_prompt_attention(q, glog, side):
    bsz, t_len, _ = q.shape
    n_t = t_len // TQ
    width = Q_PER_KV * HEAD_DIM
    n_cmp_pad = side['kct'].shape[-1]
    n_sel = side['ovt'].shape[0]
    grid = (bsz, N_KV_HEADS, n_t)
    params = pltpu.CompilerParams(dimension_semantics=("parallel", "parallel", "arbitrary"),
                                  vmem_limit_bytes=VMEM_LIMIT)
    q_spec = pl.BlockSpec((None, TQ, width), lambda b, h, i: (b, i, h))
    per_bh = lambda *shape: pl.BlockSpec((None, None) + shape, lambda b, h, i: (b, h, 0, 0))
    per_h = lambda *shape: pl.BlockSpec((None,) + shape, lambda b, h, i: (h,) + (0,) * len(shape))
    const = lambda *shape: pl.BlockSpec(shape, lambda b, h, i: (0,) * len(shape))
    selt_spec = pl.BlockSpec((None, None, n_sel, TQ), lambda b, h, i: (b, h, 0, i))
    o_c, selt = pl.pallas_call(
        _nsa_cmp_kernel,
        grid=grid,
        in_specs=[q_spec, per_bh(HEAD_DIM, n_cmp_pad), per_bh(n_cmp_pad, HEAD_DIM),
                  per_h(Q_PER_KV * TQ, n_cmp_pad), const(n_sel, n_cmp_pad)],
        out_specs=[q_spec, selt_spec],
        out_shape=[jax.ShapeDtypeStruct((bsz, t_len, N_HEADS * HEAD_DIM), _F32),
                   jax.ShapeDtypeStruct((bsz, N_KV_HEADS, n_sel, t_len), _F32)],
        compiler_params=params,
        name="nsa_cmp_select",
    )(q, side['kct'], side['vc'], side['g_cmp'], side['ovt'])
    stat = pltpu.VMEM((Q_PER_KV, TQ, 2 * HEAD_DIM), _F32)
    keys_spec = lambda branch: pl.BlockSpec((HEAD_DIM, t_len), lambda b, h, i: (branch * N_KV_HEADS + h, b))
    vals_spec = lambda branch: pl.BlockSpec((None, t_len, 2 * HEAD_DIM),
                                            lambda b, h, i: (b, 0, branch * N_KV_HEADS + h))
    return pl.pallas_call(
        _nsa_sel_win_kernel,
        grid=grid,
        in_specs=[q_spec, q_spec, pl.BlockSpec((None, TQ, LANES), lambda b, h, i: (b, i, h)), selt_spec,
                  keys_spec(0), vals_spec(0), keys_spec(1), vals_spec(1),
                  per_h(Q_PER_KV * TQ, NEAR_TILES * TK), const(n_sel, t_len),
                  const(LANES, N_BRANCH * width)],
        out_specs=q_spec,
        out_shape=jax.ShapeDtypeStruct((bsz, t_len, N_HEADS * HEAD_DIM), _F32),
        scratch_shapes=[pltpu.VMEM((TQ, t_len), _F32), stat, stat, stat, stat],
        compiler_params=params,
        name="nsa_select_window",
    )(q, o_c, glog, selt, side['keys_t'], side['vals'], side['keys_t'], side['vals'], side['near'],
      side['expand'], side['gexp'])


PAGES_PER_STEP = 16


def _dot_nt(a, b):
    return lax.dot_general(a, b, (((1,), (1,)), ((), ())), preferred_element_type=_F32)


def _cmp_pages_kernel(pt_ref, *refs, n_pages):
    del pt_ref
    pages = refs[:n_pages]
    posa_ref, posb_ref, w1a_ref, w1b_ref, w2_ref, o_ref = refs[n_pages:]
    x = jnp.concatenate([pg[...] for pg in pages], axis=0)
    z0 = jnp.dot((x + posa_ref[...]).astype(_BF16), w1a_ref[...], preferred_element_type=_F32)
    z1 = jnp.dot((x + posb_ref[...]).astype(_BF16), w1b_ref[...], preferred_element_type=_F32)
    pre = z0 + pltpu.roll(z1, z1.shape[0] - 1, 0)
    o_ref[...] = jnp.dot(jax.nn.gelu(pre).astype(_BF16), w2_ref[...], preferred_element_type=_F32).astype(o_ref.dtype)


def _cmp_pages(grouped, page_table, pos_emb, w1, w2):
    n_seq, n_pages = page_table.shape
    groups_per_page, flat = grouped.shape[1:]
    width = N_KV_HEADS * HEAD_DIM
    hidden = w1.shape[1]
    eye = jnp.eye(N_KV_HEADS, dtype=_F32)
    w1r = w1.reshape(2, CMP_STRIDE, HEAD_DIM, hidden)
    bd1 = jnp.einsum('hk,abdf->abhdkf', eye, w1r).reshape(2, flat, N_KV_HEADS * hidden).astype(_BF16)
    bd2 = jnp.einsum('hk,fd->hfkd', eye, w2).reshape(N_KV_HEADS * hidden, width).astype(_BF16)
    pos_t = jnp.broadcast_to(pos_emb.reshape(2, CMP_STRIDE, 1, HEAD_DIM),
                             (2, CMP_STRIDE, N_KV_HEADS, HEAD_DIM)).reshape(2, 1, flat)
    n_rows = n_pages * groups_per_page
    page_spec = lambda k: pl.BlockSpec((None, groups_per_page, flat), lambda b, pt: (pt[b, k], 0, 0))
    const = lambda *shape: pl.BlockSpec(shape, lambda b, pt: (0,) * len(shape))
    return pl.pallas_call(
        functools.partial(_cmp_pages_kernel, n_pages=n_pages),
        grid_spec=pltpu.PrefetchScalarGridSpec(
            num_scalar_prefetch=1,
            grid=(n_seq,),
            in_specs=[page_spec(k) for k in range(n_pages)] + [
                const(1, flat), const(1, flat), const(flat, N_KV_HEADS * hidden), const(flat, N_KV_HEADS * hidden),
                const(N_KV_HEADS * hidden, width)],
            out_specs=pl.BlockSpec((None, n_rows, width), lambda b, pt: (b, 0, 0)),
        ),
        out_shape=jax.ShapeDtypeStruct((n_seq, n_rows, width), _BF16),
        compiler_params=pltpu.CompilerParams(dimension_semantics=("parallel",), vmem_limit_bytes=VMEM_LIMIT),
        name="cmp_pages",
    )(page_table, *([grouped] * n_pages), pos_t[0], pos_t[1], bd1[0], bd1[1], bd2)


def _pick_own_kv_head(x, n_q):
    rows_per = Q_PER_KV * n_q
    return jnp.concatenate([x[h * rows_per:(h + 1) * rows_per, h * HEAD_DIM:(h + 1) * HEAD_DIM]
                            for h in range(N_KV_HEADS)], axis=0)


def _nsa_sample_kernel(pt_ref, q_ref, gl_ref, kc_ref, vc_ref, win_ref, new_ref, *rest, n_q, past_len, n_sel):
    del pt_ref
    pages = rest[:PAGES_PER_STEP]
    (tcmp_ref, tsel_ref, twin_ref, exp_ref, ov_ref, o_ref,
     addsel, m_scr, l_scr, acc_scr, oc_scr, ow_scr) = rest[PAGES_PER_STEP:]
    j = pl.program_id(1)
    rows = N_HEADS * n_q
    width = N_KV_HEADS * HEAD_DIM
    w_buf = win_ref.shape[1]
    step_keys = PAGES_PER_STEP * pages[0].shape[1]
    rr = lax.broadcasted_iota(jnp.int32, (rows, width), 0)
    cc = lax.broadcasted_iota(jnp.int32, (rows, width), 1)
    q4 = jnp.concatenate([q_ref[...] * (HEAD_DIM ** -0.5)] * N_KV_HEADS, axis=1)
    qbd = jnp.where(rr // (Q_PER_KV * n_q) == cc // HEAD_DIM, q4, 0.0).astype(_BF16)
    new_rows = jnp.concatenate([new_ref[...], jnp.zeros((LANES - n_q, new_ref.shape[1]), _F32)], axis=0).astype(_BF16)

    @pl.when(j == 0)
    def _first():
        s = _dot_nt(qbd, kc_ref[...]) + tcmp_ref[...]
        p = jnp.exp(s - jnp.max(s, axis=-1, keepdims=True))
        pb = (p / jnp.sum(p, axis=-1, keepdims=True)).astype(_BF16)
        oc_scr[...] = _pick_own_kv_head(jnp.dot(pb, vc_ref[...], preferred_element_type=_F32), n_q)
        pm = jnp.dot(pb, ov_ref[...], preferred_element_type=_F32)
        imp = jnp.concatenate(
            [sum(pm[(h * Q_PER_KV + g) * n_q:(h * Q_PER_KV + g + 1) * n_q] for g in range(Q_PER_KV))
             for h in range(N_KV_HEADS)], axis=0)
        blk = lax.broadcasted_iota(jnp.int32, imp.shape, 1)
        t = past_len + lax.broadcasted_iota(jnp.int32, imp.shape, 0) % n_q
        cur = t // SEL_BLOCK
        forced = (blk == 0) | (blk == cur) | (blk == cur - 1)
        score = jnp.where(forced, FORCE, jnp.where(blk * SEL_BLOCK <= t, imp, -FORCE))
        score = jnp.where(blk < n_sel, score, -jnp.inf)
        rank = jnp.zeros(imp.shape, jnp.int32)
        for sp in range(n_sel):
            col = score[:, sp:sp + 1]
            rank += ((col > score) | ((col == score) & (blk > sp))).astype(jnp.int32)
        sel = (rank < min(SEL_TOPK, n_sel)).astype(_BF16)
        sel_rows = jnp.concatenate([sel[h * n_q:(h + 1) * n_q] for h in range(N_KV_HEADS) for _ in range(Q_PER_KV)],
                                   axis=0)
        addsel[...] = (jnp.dot(sel_rows, exp_ref[...], preferred_element_type=_F32) - 1.0) * (-NEG_INF) + tsel_ref[...]
        kw_t = win_ref[:width, :].astype(_BF16)
        vw_t = win_ref[width:, :].astype(_BF16)
        s1 = jnp.dot(qbd, kw_t, preferred_element_type=_F32) + twin_ref[:, :w_buf]
        s2 = _dot_nt(qbd, new_rows[:, 2 * width:3 * width]) + twin_ref[:, w_buf:]
        m = jnp.maximum(jnp.max(s1, axis=-1, keepdims=True), jnp.max(s2, axis=-1, keepdims=True))
        p1 = jnp.exp(s1 - m)
        p2 = jnp.exp(s2 - m)
        l = jnp.sum(p1, axis=-1, keepdims=True) + jnp.sum(p2, axis=-1, keepdims=True)
        ow = (_dot_nt(p1.astype(_BF16), vw_t)
              + jnp.dot(p2.astype(_BF16), new_rows[:, 3 * width:], preferred_element_type=_F32))
        ow_scr[...] = _pick_own_kv_head(ow / l, n_q)
        m_scr[...] = jnp.full(m_scr.shape, NEG_INF, _F32)
        l_scr[...] = jnp.zeros(l_scr.shape, _F32)
        acc_scr[...] = jnp.zeros(acc_scr.shape, _F32)

    def flash(s, pv):
        m_old = m_scr[...]
        m_new = jnp.maximum(m_old, jnp.max(s, axis=-1, keepdims=True))
        alpha = jnp.exp(m_old - m_new)
        p = jnp.exp(s - m_new)
        m_scr[...] = m_new
        l_scr[...] = alpha * l_scr[...] + jnp.sum(p, axis=-1, keepdims=True)
        acc_scr[...] = alpha * acc_scr[...] + pv(p.astype(_BF16))

    k_t = jnp.concatenate([pg[:width, :] for pg in pages], axis=1).astype(_BF16)
    v_t = jnp.concatenate([pg[width:, :] for pg in pages], axis=1).astype(_BF16)
    flash(jnp.dot(qbd, k_t, preferred_element_type=_F32)
          + addsel[:, pl.ds(pl.multiple_of(j * step_keys, step_keys), step_keys)], lambda p: _dot_nt(p, v_t))

    @pl.when(j == pl.num_programs(1) - 1)
    def _last():
        flash(_dot_nt(qbd, new_rows[:, :width]) + addsel[:, past_len:],
              lambda p: jnp.dot(p, new_rows[:, width:2 * width], preferred_element_type=_F32))
        o_s = _pick_own_kv_head(acc_scr[...] / l_scr[...], n_q)
        gates = jax.nn.sigmoid(gl_ref[...])
        o_ref[...] = gates[:, 0:1] * oc_scr[...] + gates[:, 1:2] * o_s + gates[:, 2:3] * ow_scr[...]


def _nsa_sample_side(cache_kv, page_table, cache_win, kv_new, rel_bias, prm):
    import numpy as np
    n_seq, n_pages = page_table.shape
    n_phys, page_rows = cache_kv.shape[:2]
    n_q = kv_new.shape[1]
    width = N_KV_HEADS * HEAD_DIM
    past_len = n_pages * page_rows
    w_buf = cache_win.shape[1]
    tk = past_len + n_q
    n_cmp = (tk - CMP_LEN) // CMP_STRIDE + 1
    n_cmp_pad = past_len // CMP_STRIDE
    n_sel = -(-tk // SEL_BLOCK)
    assert n_cmp == n_cmp_pad - 1 and n_cmp_pad == LANES and n_sel <= LANES and n_q <= LANES
    assert page_rows % SEL_BLOCK == 0 and n_pages % PAGES_PER_STEP == 0 and w_buf == min(WINDOW, past_len)
    cache_t = jnp.transpose(cache_kv, (0, 2, 3, 4, 1)).reshape(n_phys, KV_SLOTS_PAGED * width, page_rows)
    win_t = jnp.transpose(cache_win, (0, 2, 3, 4, 1)).reshape(n_seq, KV_SLOTS_WIN * width, w_buf)
    gpp = page_rows // CMP_STRIDE
    grouped = cache_kv[:, :, :2].reshape(n_phys, gpp, CMP_STRIDE, 2, width)
    grouped = jnp.transpose(grouped, (3, 0, 1, 2, 4)).reshape(2, n_phys, gpp, CMP_STRIDE * width)
    kc = _cmp_pages(grouped[0], page_table, prm['cmp_pos_k'], prm['cmp_w1_k'], prm['cmp_w2_k'])
    vc = _cmp_pages(grouped[1], page_table, prm['cmp_pos_v'], prm['cmp_w1_v'], prm['cmp_w2_v'])
    rows = N_HEADS * n_q
    t = past_len + np.arange(n_q)[:, None]
    masked = lambda ok, d: jnp.where(jnp.asarray(ok), _bias_by_distance(rel_bias, d), NEG_INF).reshape(rows, -1)
    n = np.arange(n_cmp_pad)[None, :]
    d_c = t - (n * CMP_STRIDE + CMP_LEN - 1)
    t_cmp = masked((d_c >= 0) & (n < n_cmp), d_c)
    pos = np.arange(past_len + LANES)[None, :]
    t_sel = masked((t - pos >= 0) & (pos < tk), t - pos)
    l = np.arange(w_buf + LANES)[None, :]
    d_w = t - (past_len - w_buf + l)
    t_win = masked((d_w >= 0) & (d_w <= WINDOW) & (l < w_buf + n_q), d_w)
    expand = (np.arange(LANES)[:, None] == pos // SEL_BLOCK) & (np.arange(LANES)[:, None] < n_sel)
    c_start = np.arange(n_cmp_pad)[:, None] * CMP_STRIDE
    s_start = np.arange(LANES)[None, :] * SEL_BLOCK
    ov = ((c_start < s_start + SEL_BLOCK) & (c_start + CMP_LEN > s_start) & (np.arange(n_cmp_pad)[:, None] < n_cmp)
          & (np.arange(LANES)[None, :] < n_sel))
    return {
        'cache_t': cache_t, 'page_table': page_table, 'kc': kc, 'vc': vc, 'win_t': win_t,
        'new': kv_new.reshape(n_seq, n_q, -1)[:, :, 2 * width:],
        't_cmp': t_cmp, 't_sel': t_sel, 't_win': t_win,
        'expand': jnp.asarray(expand, _BF16), 'ov': jnp.asarray(ov, _BF16), 'n_sel': n_sel, 'past_len': past_len,
    }


def _nsa_sample_attention(q, glog, side):
    n_seq, n_q, _ = q.shape
    glog = glog.reshape(n_seq, n_q, N_KV_HEADS, LANES)[..., :Q_PER_KV * N_BRANCH]
    rows = N_HEADS * n_q
    width = N_KV_HEADS * HEAD_DIM
    page_table = side['page_table']
    n_pages = page_table.shape[1]
    page_rows = side['cache_t'].shape[2]
    past_len = side['past_len']
    w_buf = side['win_t'].shape[2]
    n_steps = n_pages // PAGES_PER_STEP
    q_rows = jnp.transpose(q.reshape(n_seq, n_q, N_HEADS, HEAD_DIM), (0, 2, 1, 3)).reshape(n_seq, rows, HEAD_DIM)
    gl = jnp.transpose(glog.reshape(n_seq, n_q, N_HEADS, N_BRANCH), (0, 2, 1, 3)).reshape(n_seq, rows, N_BRANCH)
    gl = jnp.pad(gl, ((0, 0), (0, 0), (0, LANES - N_BRANCH)))
    per_seq = lambda *shape: pl.BlockSpec((None,) + shape, lambda b, j, pt: (b,) + (0,) * len(shape))
    const = lambda *shape: pl.BlockSpec(shape, lambda b, j, pt: (0,) * len(shape))
    page_spec = lambda k: pl.BlockSpec((None, 2 * width, page_rows),
                                       lambda b, j, pt: (pt[b, j * PAGES_PER_STEP + k], 1, 0))
    out = pl.pallas_call(
        functools.partial(_nsa_sample_kernel, n_q=n_q, past_len=past_len, n_sel=side['n_sel']),
        grid_spec=pltpu.PrefetchScalarGridSpec(
            num_scalar_prefetch=1,
            grid=(n_seq, n_steps),
            in_specs=[per_seq(rows, HEAD_DIM), per_seq(rows, LANES), per_seq(LANES, width), per_seq(LANES, width),
                      per_seq(2 * width, w_buf), per_seq(n_q, 4 * width)]
            + [page_spec(k) for k in range(PAGES_PER_STEP)]
            + [const(rows, LANES), const(rows, past_len + LANES), const(rows, w_buf + LANES),
               const(LANES, past_len + LANES), const(LANES, LANES)],
            out_specs=per_seq(rows, HEAD_DIM),
            scratch_shapes=[pltpu.VMEM((rows, past_len + LANES), _F32), pltpu.VMEM((rows, 1), _F32),
                            pltpu.VMEM((rows, 1), _F32), pltpu.VMEM((rows, width), _F32),
                            pltpu.VMEM((rows, HEAD_DIM), _F32), pltpu.VMEM((rows, HEAD_DIM), _F32)],
        ),
        out_shape=jax.ShapeDtypeStruct((n_seq, rows, HEAD_DIM), _F32),
        compiler_params=pltpu.CompilerParams(dimension_semantics=("parallel", "arbitrary"),
                                             vmem_limit_bytes=VMEM_LIMIT),
        name="nsa_sample",
    )(page_table, q_rows, gl, side['kc'], side['vc'], side['win_t'], side['new'],
      *([side['cache_t']] * PAGES_PER_STEP), side['t_cmp'], side['t_sel'], side['t_win'], side['expand'], side['ov'])
    return jnp.transpose(out.reshape(n_seq, N_HEADS, n_q, HEAD_DIM), (0, 2, 1, 3)).reshape(n_seq, n_q, N_HEADS * HEAD_DIM)


def _run_trunk(x, p, s0_re, s0_im, paged, win_buf, prm):
    bsz, t_len, _ = x.shape
    ssm_re, ssm_im = [], []
    side = None
    kv_rows_new = None
    win_state = None
    y = None
    x2 = x.reshape(-1, D_MODEL)
    as_seq = lambda a: a.reshape(bsz, t_len, a.shape[-1])
    for i in range(DEPTH):
        if i < N_A_LAYERS:
            u2, = _rows_call(lambda xt, gain: (_norm(xt, gain),), [x2], [prm['g_mix'][i].reshape(1, -1)],
                             [D_MODEL], "s5_norm")
            ssm = (prm['ssm_a_re'][i], prm['ssm_a_im'][i], prm['ssm_log_dt'][i], prm['ssm_b_re'][i],
                   prm['ssm_b_im'][i], prm['ssm_c_re'][i], prm['ssm_c_im'][i], prm['ssm_d'][i])
            s0 = _state_to_tiles(s0_re[i], s0_im[i])
            if t_len == 8:
                ys, s_fin = _s5_scan(u2[None], jnp.transpose(s0, (1, 0, 2))[None], False, *ssm)
                s_re, s_im = _state_from_tiles(jnp.transpose(s_fin[0], (1, 0, 2)))
            else:
                assert t_len % (16 * 8) == 0
                ys, s_fin = _s5_scan(as_seq(u2), s0[:, :, None, :], True, *ssm)
                s_re, s_im = _state_from_tiles(s_fin[:, :, 0, :])
            ssm_re.append(s_re)
            ssm_im.append(s_im)
            x2 = _glu_residual(x2, ys.reshape(-1, D_MODEL), prm['w_glu'][i])
        else:
            j = i - N_A_LAYERS
            q, glog = _qg_project(x2, prm['g_mix'][i], prm['w_qg'][j])
            attend = _nsa_prompt_attention if paged is None else _nsa_sample_attention
            mixed = attend(as_seq(q), as_seq(glog), side)
            x2 = _wo_residual(x2, mixed.reshape(-1, D_MODEL), prm['w_o'][j])
        x2 = _moe_layer(x2, prm['g_ffn'][i], prm['w_route_group'][i], prm['b_route_group'][i],
                        prm['w_route_expert'][i], prm['b_route_expert'][i], prm['w_exp_up'], prm['w_exp_down'], i)
        ple = (x2, p[i].reshape(-1, p.shape[-1]), prm['g_ple'][i], prm['w_ple_gate'][i], prm['w_ple_proj'][i])
        if i == N_A_LAYERS - 1:
            x2, kv2, *operands = _ple_residual(*ple, g_next=prm['g_kv'], w_next=prm['w_kv'],
                                               attention_operands=paged is None)
            kv = kv2.reshape(bsz, t_len, KV_SLOTS_PAGED + KV_SLOTS_WIN, N_KV_HEADS, HEAD_DIM)
            kv_rows_new, win_new = kv[:, :, :KV_SLOTS_PAGED], kv[:, :, KV_SLOTS_PAGED:]
            if paged is None:
                win_state = win_new[:, -min(WINDOW, t_len):]
                side = _nsa_prompt_side(kv, *operands, prm)
            else:
                w_buf = win_buf.shape[1]
                win_state = jnp.concatenate([win_buf, win_new], axis=1)[:, -w_buf:]
                side = _nsa_sample_side(paged[0], paged[1], win_buf, kv, prm['rel_bias'], prm)
        elif i == DEPTH - 1:
            x2, y = _ple_residual(*ple, g_next=prm['g_final'])
        else:
            x2, = _ple_residual(*ple)
    return as_seq(y), kv_rows_new, win_state, jnp.stack(ssm_re), jnp.stack(ssm_im)


def kernel(x_prompt, x_sample, p_prompt, p_sample, cache_kv, cache_win, state_ssm_re, state_ssm_im, page_table,
           g_mix, g_ffn, g_ple, g_kv, g_final,
           ssm_a_re, ssm_a_im, ssm_log_dt, ssm_b_re, ssm_b_im, ssm_c_re, ssm_c_im, ssm_d, w_glu,
           w_kv, cmp_pos_k, cmp_pos_v, cmp_w1_k, cmp_w2_k, cmp_w1_v, cmp_w2_v, w_qg, w_o, rel_bias,
           w_route_group, b_route_group, w_route_expert, b_route_expert, w_exp_up, w_exp_down,
           w_ple_proj, w_ple_gate):
    prm = {
        'g_mix': g_mix, 'g_ffn': g_ffn, 'g_ple': g_ple, 'g_kv': g_kv, 'g_final': g_final,
        'ssm_a_re': ssm_a_re, 'ssm_a_im': ssm_a_im, 'ssm_log_dt': ssm_log_dt,
        'ssm_b_re': ssm_b_re, 'ssm_b_im': ssm_b_im, 'ssm_c_re': ssm_c_re, 'ssm_c_im': ssm_c_im,
        'ssm_d': ssm_d, 'w_glu': w_glu,
        'w_kv': w_kv, 'cmp_pos_k': cmp_pos_k, 'cmp_pos_v': cmp_pos_v,
        'cmp_w1_k': cmp_w1_k, 'cmp_w2_k': cmp_w2_k, 'cmp_w1_v': cmp_w1_v, 'cmp_w2_v': cmp_w2_v,
        'w_qg': w_qg, 'w_o': w_o, 'rel_bias': rel_bias,
        'w_route_group': w_route_group, 'b_route_group': b_route_group,
        'w_route_expert': w_route_expert, 'b_route_expert': b_route_expert,
        'w_exp_up': w_exp_up, 'w_exp_down': w_exp_down,
        'w_ple_proj': w_ple_proj, 'w_ple_gate': w_ple_gate,
    }
    zero_state = jnp.zeros((N_A_LAYERS, x_prompt.shape[0], SSM_GROUPS, SSM_STATE), _F32)
    y_prompt, kv_prompt, win_prompt, ssm_re_prompt, ssm_im_prompt = _run_trunk(
        x_prompt, p_prompt, zero_state, zero_state, None, None, prm)
    y_sample, kv_sample, win_sample, ssm_re_sample, ssm_im_sample = _run_trunk(
        x_sample, p_sample, state_ssm_re, state_ssm_im, (cache_kv, page_table), cache_win, prm)
    return (y_prompt, y_sample, kv_prompt, win_prompt, ssm_re_prompt, ssm_im_prompt,
            kv_sample, win_sample, ssm_re_sample, ssm_im_sample)
```

```python
import functools
import math

import jax
import jax.numpy as jnp
from jax import lax
from jax.experimental import pallas as pl
from jax.experimental.pallas import tpu as pltpu

D_MODEL = 1024
DEPTH = 4
N_A_LAYERS = DEPTH // 2
SSM_GROUP = 16
SSM_GROUPS = D_MODEL // SSM_GROUP
SSM_STATE = 64
N_HEADS = 16
HEAD_DIM = D_MODEL // N_HEADS
N_KV_HEADS = 4
Q_PER_KV = N_HEADS // N_KV_HEADS
CMP_LEN = 32
CMP_STRIDE = 16
SEL_BLOCK = 64
SEL_TOPK = 16
WINDOW = 512
N_BRANCH = 3
KV_SLOTS_PAGED = 4
KV_SLOTS_WIN = 2
Q_BLOCK = 64
N_BUCKETS = 32
MAX_DISTANCE = 128
N_EXPERT_GROUPS = 4
EXPERTS_PER_GROUP = 4
N_EXPERTS = N_EXPERT_GROUPS * EXPERTS_PER_GROUP
EXPERT_TOPK = 2
D_EXPERT = 256
RMS_EPS = 1e-6
NEG_INF = -1e30
FORCE = 1e4

LANES = 128
VMEM_LIMIT = 48 * 1024 * 1024

_F32 = jnp.float32
_BF16 = jnp.bfloat16


def _moe_kernel(x_ref, g_ref, wr_ref, br_ref, wup_ref, wdn_ref, o_ref, h_scr, comb_scr, acc_scr):
    e = pl.program_id(1)
    lane = lax.broadcasted_iota(jnp.int32, comb_scr.shape, 1)

    @pl.when(e == 0)
    def _route():
        x = x_ref[...]
        h = x * lax.rsqrt(jnp.mean(x * x, axis=-1, keepdims=True) + RMS_EPS) * g_ref[...]
        h_scr[...] = h.astype(_BF16)
        logits = jnp.dot(h, wr_ref[...], preferred_element_type=_F32,
                         precision=lax.Precision.HIGHEST) + br_ref[...]
        is_grp = (lane >= N_EXPERTS) & (lane < N_EXPERTS + N_EXPERT_GROUPS)
        lg = jnp.where(is_grp, logits, -jnp.inf)
        gmax = jnp.max(lg, axis=-1, keepdims=True)
        gi = jnp.min(jnp.where(lg == gmax, lane, LANES), axis=-1, keepdims=True) - N_EXPERTS
        gp = 1.0 / jnp.sum(jnp.where(is_grp, jnp.exp(lg - gmax), 0.0), axis=-1, keepdims=True)
        in_grp = (lane < N_EXPERTS) & ((lane // EXPERTS_PER_GROUP) == gi)
        le = jnp.where(in_grp, logits, -jnp.inf)
        m1 = jnp.max(le, axis=-1, keepdims=True)
        i1 = jnp.min(jnp.where(le == m1, lane, LANES), axis=-1, keepdims=True)
        le2 = jnp.where(lane == i1, -jnp.inf, le)
        m2 = jnp.max(le2, axis=-1, keepdims=True)
        i2 = jnp.min(jnp.where(le2 == m2, lane, LANES), axis=-1, keepdims=True)
        e2 = jnp.exp(m2 - m1)
        den = 1.0 + e2
        comb_scr[...] = jnp.where(lane == i1, gp / den, jnp.where(lane == i2, gp * e2 / den, 0.0))
        acc_scr[...] = x

    up = jnp.dot(h_scr[...], wup_ref[0].astype(_BF16), preferred_element_type=_F32)
    a = up[:, :D_EXPERT]
    b = up[:, D_EXPERT:]
    c = jnp.sum(jnp.where(lane == e, comb_scr[...], 0.0), axis=-1, keepdims=True)
    act = (a * jax.nn.sigmoid(a)) * b * c
    acc_scr[...] += jnp.dot(act.astype(_BF16), wdn_ref[0].astype(_BF16), preferred_element_type=_F32)

    @pl.when(e == N_EXPERTS - 1)
    def _store():
        o_ref[...] = acc_scr[...]


def _moe_layer(x2, g, w_rg, b_rg, w_re, b_re, w_up, w_dn, layer):
    n_tok, d = x2.shape
    tm = min(n_tok, 1024)
    wr = jnp.zeros((d, LANES), _F32).at[:, :N_EXPERTS].set(w_re).at[:, N_EXPERTS:N_EXPERTS + N_EXPERT_GROUPS].set(w_rg)
    br = jnp.zeros((1, LANES), _F32).at[0, :N_EXPERTS].set(b_re).at[0, N_EXPERTS:N_EXPERTS + N_EXPERT_GROUPS].set(b_rg)
    return pl.pallas_call(
        _moe_kernel,
        grid=(n_tok // tm, N_EXPERTS),
        in_specs=[
            pl.BlockSpec((tm, d), lambda i, e: (i, 0)),
            pl.BlockSpec((1, d), lambda i, e: (0, 0)),
            pl.BlockSpec((d, LANES), lambda i, e: (0, 0)),
            pl.BlockSpec((1, LANES), lambda i, e: (0, 0)),
            pl.BlockSpec((None, 1, d, 2 * D_EXPERT), lambda i, e: (layer, e, 0, 0)),
            pl.BlockSpec((None, 1, D_EXPERT, d), lambda i, e: (layer, e, 0, 0)),
        ],
        out_specs=pl.BlockSpec((tm, d), lambda i, e: (i, 0)),
        out_shape=jax.ShapeDtypeStruct((n_tok, d), _F32),
        scratch_shapes=[
            pltpu.VMEM((tm, d), _BF16),
            pltpu.VMEM((tm, LANES), _F32),
            pltpu.VMEM((tm, d), _F32),
        ],
        compiler_params=pltpu.CompilerParams(
            dimension_semantics=("parallel", "arbitrary"), vmem_limit_bytes=VMEM_LIMIT),
        name="hmoe",
    )(x2, g.reshape(1, d), wr, br, w_up, w_dn)


ROW_TILE = 512


def _rows_kernel(*refs, body, n_in):
    outs = body(*[r[...] for r in refs[:n_in]])
    for o_ref, o in zip(refs[n_in:], outs):
        o_ref[...] = o


def _rows_call(body, rows, consts, outs, name):
    n_rows = rows[0].shape[0]
    tm = min(n_rows, ROW_TILE)
    row_spec = lambda width: pl.BlockSpec((tm, width), lambda i: (i, 0))
    const_spec = lambda c: pl.BlockSpec(c.shape, lambda i: (0,) * c.ndim)
    out_specs, out_shape = [], []
    for o in outs:
        if isinstance(o, int):
            o = (o, _F32)
        if o[0] == 'T':
            out_specs.append(pl.BlockSpec((o[1], tm), lambda i: (0, i)))
            out_shape.append(jax.ShapeDtypeStruct((o[1], n_rows), o[2]))
        else:
            out_specs.append(row_spec(o[0]))
            out_shape.append(jax.ShapeDtypeStruct((n_rows, o[0]), o[1]))
    return pl.pallas_call(
        functools.partial(_rows_kernel, body=body, n_in=len(rows) + len(consts)),
        grid=(n_rows // tm,),
        in_specs=[row_spec(r.shape[1]) for r in rows] + [const_spec(c) for c in consts],
        out_specs=out_specs,
        out_shape=out_shape,
        compiler_params=pltpu.CompilerParams(dimension_semantics=("parallel",), vmem_limit_bytes=VMEM_LIMIT),
        name=name,
    )(*rows, *consts)


def _norm(x, g):
    return x * lax.rsqrt(jnp.mean(x * x, axis=-1, keepdims=True) + RMS_EPS) * g


def _mm(x, w):
    return jnp.dot(x.astype(_BF16), w, preferred_element_type=_F32)


def _glu_residual(x2, y2, w_glu):
    def body(x, y, w):
        ag = _mm(jax.nn.gelu(y), w)
        return (x + ag[:, :D_MODEL] * jax.nn.sigmoid(ag[:, D_MODEL:]),)
    return _rows_call(body, [x2, y2], [w_glu.astype(_BF16)], [D_MODEL], "glu_residual")[0]


def _qg_project(x2, g, w_qg):
    per_kv = Q_PER_KV * N_BRANCH
    w_g = w_qg[:, N_HEADS * HEAD_DIM:].reshape(D_MODEL, N_KV_HEADS, per_kv)
    w_g = jnp.pad(w_g, ((0, 0), (0, 0), (0, LANES - per_kv))).reshape(D_MODEL, N_KV_HEADS * LANES)
    w = jnp.concatenate([w_qg[:, :N_HEADS * HEAD_DIM], w_g], axis=1).astype(_BF16)

    def body(x, gain, w):
        qg = _mm(_norm(x, gain), w)
        return qg[:, :N_HEADS * HEAD_DIM], qg[:, N_HEADS * HEAD_DIM:]
    return _rows_call(body, [x2], [g.reshape(1, -1), w], [N_HEADS * HEAD_DIM, N_KV_HEADS * LANES], "qg_project")


def _wo_residual(x2, mixed2, w_o):
    body = lambda x, m, w: (x + _mm(m, w),)
    return _rows_call(body, [x2, mixed2], [w_o.astype(_BF16)], [D_MODEL], "wo_residual")[0]


def _ple_residual(x2, p2, g_ple, w_gate, w_proj, g_next=None, w_next=None, attention_operands=False):
    consts = [g_ple.reshape(1, -1), w_gate.astype(_BF16), w_proj.astype(_BF16)]
    outs = [D_MODEL]
    width = N_KV_HEADS * HEAD_DIM
    if g_next is not None:
        consts.append(g_next.reshape(1, -1))
        outs.append(D_MODEL if w_next is None else w_next.shape[1])
    if w_next is not None:
        consts.append(w_next.astype(_BF16))
    if attention_operands:
        outs += [('T', 2 * width, _BF16), (4 * width, _BF16)]

    def body(x, p, gain, wg, wp, *nxt):
        out = x + _mm(p, wp) * jax.nn.sigmoid(_mm(_norm(x, gain), wg))
        if not nxt:
            return (out,)
        normed = _norm(out, nxt[0])
        if len(nxt) == 1:
            return out, normed
        kv = _mm(normed, nxt[1])
        if not attention_operands:
            return out, kv
        slot = lambda s: kv[:, s * width:(s + 1) * width]
        keys_t = jnp.concatenate([slot(2), slot(4)], axis=1).T.astype(_BF16)
        ones = jnp.ones((kv.shape[0], HEAD_DIM), _F32)
        vals = jnp.concatenate([piece for s in (3, 5) for h in range(N_KV_HEADS)
                                for piece in (slot(s)[:, h * HEAD_DIM:(h + 1) * HEAD_DIM], ones)], axis=1)
        return out, kv, keys_t, vals.astype(_BF16)
    return _rows_call(body, [x2, p2], consts, outs, "ple_residual")


def _cmul(ar, ai, br, bi):
    return ar * br - ai * bi, ar * bi + ai * br


S5_TILE_GROUPS = LANES // SSM_GROUP
S5_TILES = D_MODEL // LANES
S5_STATE_LANES = S5_TILE_GROUPS * SSM_STATE


def _s5_operators(a_re, a_im, log_dt, b_re, b_im, c_re, c_im, chunk):
    hp = lax.Precision.HIGHEST
    dt = jnp.exp(log_dt)[:, None]
    decay = jnp.exp(a_re * dt)
    ab_re, ab_im = decay * jnp.cos(a_im * dt), decay * jnp.sin(a_im * dt)
    den = a_re * a_re + a_im * a_im
    num_re = ab_re - 1.0
    q_re = (num_re * a_re + ab_im * a_im) / den
    q_im = (ab_im * a_re - num_re * a_im) / den
    bb_re, bb_im = _cmul(q_re[..., None], q_im[..., None], b_re, b_im)
    pw_re, pw_im = [jnp.ones_like(ab_re)], [jnp.zeros_like(ab_im)]
    for _ in range(chunk):
        nr, ni = _cmul(pw_re[-1], pw_im[-1], ab_re, ab_im)
        pw_re.append(nr)
        pw_im.append(ni)
    pw_re, pw_im = jnp.stack(pw_re), jnp.stack(pw_im)
    w_re, w_im = _cmul(pw_re[:chunk, :, :, None], pw_im[:chunk, :, :, None], bb_re[None], bb_im[None])
    k_lag = (jnp.einsum('gcp,jgpd->jgcd', c_re, w_re, precision=hp)
             - jnp.einsum('gcp,jgpd->jgcd', c_im, w_im, precision=hp))
    ca_re, ca_im = _cmul(c_re[None], c_im[None], pw_re[1:, :, None, :], pw_im[1:, :, None, :])
    import numpy as np

    def block_diag(per_group):
        a = per_group.reshape((chunk, S5_TILES, S5_TILE_GROUPS) + per_group.shape[2:])
        a = jnp.transpose(a, (1, 0, 2, 4, 3))
        n_r, n_c = a.shape[3:]
        wide = jnp.tile(a.reshape(S5_TILES, chunk, S5_TILE_GROUPS * n_r, n_c), (1, 1, 1, S5_TILE_GROUPS))
        same = (np.arange(S5_TILE_GROUPS * n_r)[:, None] // n_r) == (np.arange(S5_TILE_GROUPS * n_c)[None, :] // n_c)
        return jnp.where(same, wide, 0.0)

    k_bd = block_diag(k_lag)
    h_bd = jnp.concatenate([block_diag(w_re), block_diag(w_im)], axis=-1)
    g_bd = jnp.concatenate([block_diag(ca_re), block_diag(-ca_im)], axis=2)
    a_pow = jnp.concatenate([pw_re[chunk].reshape(S5_TILES, 1, S5_STATE_LANES),
                             pw_im[chunk].reshape(S5_TILES, 1, S5_STATE_LANES)], axis=-1)
    return k_bd.astype(_BF16), h_bd, g_bd.astype(_BF16), a_pow


def _advance(a_pow, s, e):
    half = S5_STATE_LANES
    ar, ai, sr, si = a_pow[:, :half], a_pow[:, half:], s[:, :half], s[:, half:]
    return jnp.concatenate([ar * sr - ai * si + e[:, :half], ar * si + ai * sr + e[:, half:]], axis=1)


def _s5_kernel(u_ref, k_ref, h_ref, g_ref, a_ref, d_ref, s0_ref, y_ref, sfin_ref, upad, e_scr, sin_scr, *,
               chunk, carry):
    n_rows = u_ref.shape[0]
    n_chunks = n_rows // chunk
    u = u_ref[...]
    upad[:chunk] = jnp.zeros((chunk, LANES), _F32)
    upad[chunk:] = u
    step = lax.broadcasted_iota(jnp.int32, (n_rows, LANES), 0) % chunk
    y = d_ref[...] * u
    for lag in range(chunk):
        part = jnp.dot(upad[chunk - lag:chunk - lag + n_rows].astype(_BF16), k_ref[lag], preferred_element_type=_F32)
        y += part if lag == 0 else jnp.where(step >= lag, part, 0.0)
    y_ref[...] = y
    e = None
    for lag in range(chunk):
        rows = u_ref[pl.ds(chunk - 1 - lag, n_chunks, stride=chunk), :]
        if h_ref.dtype == _F32:
            d = jnp.dot(rows, h_ref[lag], preferred_element_type=_F32, precision=lax.Precision.HIGHEST)
        else:
            d = jnp.dot(rows.astype(_BF16), h_ref[lag], preferred_element_type=_F32)
        e = d if e is None else e + d
    a_pow = a_ref[...]
    if carry:
        e_scr[...] = e
        sub = 8

        def body(jj, s):
            rows = pl.ds(pl.multiple_of(jj * sub, sub), sub)
            e_tile = e_scr[rows, :]
            starts = []
            for r in range(sub):
                starts.append(s)
                s = _advance(a_pow, s, e_tile[r:r + 1])
            sin_scr[rows, :] = jnp.concatenate(starts, axis=0)
            return s
        sfin_ref[...] = lax.fori_loop(0, n_chunks // sub, body, s0_ref[...])
    else:
        sin_scr[...] = s0_ref[...]
        sfin_ref[...] = _advance(a_pow, s0_ref[...], e)
    s_in = sin_scr[...].astype(_BF16)
    for k in range(chunk):
        rows = pl.ds(k, n_chunks, stride=chunk)
        y_ref[rows, :] = y_ref[rows, :] + jnp.dot(s_in, g_ref[k], preferred_element_type=_F32)


def _s5_scan(u, s0, carry, a_re, a_im, log_dt, b_re, b_im, c_re, c_im, d_skip):
    n_b, n_rows, _ = u.shape
    chunk = 16 if carry else 8
    n_chunks = n_rows // chunk
    n_state = s0.shape[2]
    k_bd, h_bd, g_bd, a_pow = _s5_operators(a_re, a_im, log_dt, b_re, b_im, c_re, c_im, chunk)
    if carry:
        h_bd = h_bd.astype(_BF16)
    per_tile = lambda *shape: pl.BlockSpec((None,) + shape, lambda t, b: (t,) + (0,) * len(shape))
    rows_spec = pl.BlockSpec((None, n_rows, LANES), lambda t, b: (b, 0, t))
    state_spec = pl.BlockSpec((None, None, n_state, 2 * S5_STATE_LANES), lambda t, b: (b, t, 0, 0))
    return pl.pallas_call(
        functools.partial(_s5_kernel, chunk=chunk, carry=carry),
        grid=(S5_TILES, n_b),
        in_specs=[rows_spec, per_tile(chunk, LANES, LANES), per_tile(chunk, LANES, 2 * S5_STATE_LANES),
                  per_tile(chunk, 2 * S5_STATE_LANES, LANES), per_tile(1, 2 * S5_STATE_LANES), per_tile(1, LANES),
                  state_spec],
        out_specs=[rows_spec, state_spec],
        out_shape=[jax.ShapeDtypeStruct(u.shape, _F32), jax.ShapeDtypeStruct(s0.shape, _F32)],
        scratch_shapes=[pltpu.VMEM((n_rows + chunk, LANES), _F32), pltpu.VMEM((n_chunks, 2 * S5_STATE_LANES), _F32),
                        pltpu.VMEM((n_chunks, 2 * S5_STATE_LANES), _F32)],
        compiler_params=pltpu.CompilerParams(dimension_semantics=("parallel", "parallel"),
                                             vmem_limit_bytes=VMEM_LIMIT),
        name="s5_scan",
    )(u, k_bd, h_bd, g_bd, a_pow, d_skip.reshape(S5_TILES, 1, LANES), s0)


def _state_to_tiles(s_re, s_im):
    bsz = s_re.shape[0]
    return jnp.concatenate([s_re.reshape(bsz, S5_TILES, S5_STATE_LANES), s_im.reshape(bsz, S5_TILES, S5_STATE_LANES)],
                           axis=-1)


def _state_from_tiles(s):
    bsz = s.shape[0]
    return (s[..., :S5_STATE_LANES].reshape(bsz, SSM_GROUPS, SSM_STATE),
            s[..., S5_STATE_LANES:].reshape(bsz, SSM_GROUPS, SSM_STATE))


TQ = 256
TK = 128
NEAR_TILES = TQ // TK + 1
NEAR_BUCKET_DIST = 113


def _bucket_table(dist):
    import numpy as np
    max_exact = N_BUCKETS // 2
    d = np.maximum(np.asarray(dist, np.int64), 0)
    df = np.maximum(d, 1).astype(np.float64)
    large = max_exact + (np.log(df / max_exact) / math.log(MAX_DISTANCE / max_exact)
                         * (N_BUCKETS - max_exact)).astype(np.int64)
    return np.where(d < max_exact, d, np.minimum(large, N_BUCKETS - 1)).astype(np.int32)


def _bias_by_distance(rel_bias, dist):
    import numpy as np
    onehot = _bucket_table(dist)[None] == np.arange(N_BUCKETS)[:, None, None]
    return jnp.einsum('kqc,kh->hqc', jnp.asarray(onehot, _BF16).astype(_F32), rel_bias,
                      precision=lax.Precision.HIGHEST)


def _bias_tiles(rel_bias, n_cmp_pad):
    import numpy as np
    rel_bias = rel_bias - rel_bias[N_BUCKETS - 1]
    r = np.arange(TQ)[:, None]
    c = np.arange(n_cmp_pad)[None, :]
    d_cmp = np.where(c < 8 + TQ // CMP_STRIDE, r - (CMP_LEN - 1) + CMP_STRIDE * (8 - c), 10 ** 6)
    g_cmp = _bias_by_distance(rel_bias, d_cmp).reshape(N_KV_HEADS, Q_PER_KV * TQ, n_cmp_pad)
    k = np.arange(NEAR_TILES * TK)[None, :]
    d_near = TK + r - k
    near = _bias_by_distance(rel_bias, d_near) + jnp.asarray(np.where(d_near >= 0, 0.0, NEG_INF), _F32)
    near = near.reshape(N_KV_HEADS, Q_PER_KV * TQ, NEAR_TILES * TK)
    return g_cmp, near


def _stack_heads(q):
    return jnp.concatenate([q[:, g * HEAD_DIM:(g + 1) * HEAD_DIM] for g in range(Q_PER_KV)], axis=0)


def _unstack_heads(o):
    return jnp.concatenate([o[g] for g in range(Q_PER_KV)], axis=1)


def _nsa_cmp_kernel(q_ref, kct_ref, vc_ref, g_ref, ovt_ref, oc_ref, selt_ref):
    i = pl.program_id(2)
    t0 = i * TQ
    n_cmp_pad = kct_ref.shape[-1]
    qs = _stack_heads(q_ref[...] * (HEAD_DIM ** -0.5)).astype(_BF16)
    s = jnp.dot(qs, kct_ref[...], preferred_element_type=_F32)
    shift = (i * (TQ // CMP_STRIDE) + n_cmp_pad - 8) % n_cmp_pad
    bias = pltpu.roll(g_ref[...], shift, 1)
    s3 = (s + bias).reshape(Q_PER_KV, TQ, n_cmp_pad)
    r = lax.broadcasted_iota(jnp.int32, (TQ, n_cmp_pad), 0)
    n = lax.broadcasted_iota(jnp.int32, (TQ, n_cmp_pad), 1)
    valid = (t0 + r - CMP_STRIDE * n - (CMP_LEN - 1)) >= 0
    sm = jnp.where(valid, s3, NEG_INF)
    m = jnp.max(sm, axis=-1, keepdims=True)
    p = jnp.where(valid, jnp.exp(sm - m), 0.0)
    l = jnp.sum(p, axis=-1, keepdims=True)
    pb = (p * jnp.where(l > 0.0, 1.0 / l, 0.0)).astype(_BF16)
    oc = jnp.dot(pb.reshape(Q_PER_KV * TQ, n_cmp_pad), vc_ref[...], preferred_element_type=_F32)
    oc_ref[...] = _unstack_heads(oc.reshape(Q_PER_KV, TQ, HEAD_DIM))
    imp = jnp.zeros((ovt_ref.shape[0], TQ), _F32)
    for g in range(Q_PER_KV):
        imp += lax.dot_general(ovt_ref[...], pb[g], (((1,), (1,)), ((), ())), preferred_element_type=_F32)
    n_sel = imp.shape[0]
    blk = lax.broadcasted_iota(jnp.int32, (n_sel, TQ), 0)
    t = t0 + lax.broadcasted_iota(jnp.int32, (n_sel, TQ), 1)
    cur = t // SEL_BLOCK
    forced = (blk == 0) | (blk == cur) | (blk == cur - 1)
    score = jnp.where(forced, FORCE, jnp.where(blk * SEL_BLOCK <= t, imp, -FORCE))
    rank = jnp.zeros((n_sel, TQ), _F32)
    for sp in range(n_sel):
        row = score[sp:sp + 1, :]
        beats = (row > score) | ((row == score) & (blk > sp))
        rank += jnp.where(beats, 1.0, 0.0)
    selt_ref[...] = (rank < SEL_TOPK).astype(_F32)


ROW_BLOCK = 128


def _flash_step(qs, kt, v, add_rows, m_scr, acc_scr):
    n_chunks = kt.shape[-1] // LANES
    for r0 in range(0, TQ, ROW_BLOCK):
        for g in range(Q_PER_KV):
            s = jnp.dot(qs[g * TQ + r0:g * TQ + r0 + ROW_BLOCK], kt, preferred_element_type=_F32)
            add = add_rows(g, r0)
            if add is not None:
                s = s + add
            chunks = [s[:, k * LANES:(k + 1) * LANES] for k in range(n_chunks)]
            m_old = m_scr[g, r0:r0 + ROW_BLOCK]
            m_new = jnp.maximum(m_old, jnp.max(functools.reduce(jnp.maximum, chunks), axis=-1, keepdims=True))
            alpha = jnp.exp(m_old - m_new)
            m_scr[g, r0:r0 + ROW_BLOCK] = m_new
            pv = None
            for k, sk in enumerate(chunks):
                d = jnp.dot(jnp.exp(sk - m_new).astype(_BF16), v[k * LANES:(k + 1) * LANES],
                            preferred_element_type=_F32)
                pv = d if pv is None else pv + d
            acc_scr[g, r0:r0 + ROW_BLOCK] = alpha * acc_scr[g, r0:r0 + ROW_BLOCK] + pv


FAR_TILES = 4


def _nsa_sel_win_kernel(q_ref, oc_ref, glog_ref, selt_ref, kst_ref, vs_ref, kwt_ref, vw_ref, near_ref,
                        exp_ref, gexp_ref, o_ref, seladd, m_s, acc_s, m_w, acc_w):
    i = pl.program_id(2)
    qs = _stack_heads(q_ref[...] * (HEAD_DIM ** -0.5)).astype(_BF16)
    sel = jnp.dot(selt_ref[...].T.astype(_BF16), exp_ref[...], preferred_element_type=_F32)
    seladd[...] = (sel - 1.0) * (-NEG_INF)
    for m_scr, acc_scr in ((m_s, acc_s), (m_w, acc_w)):
        m_scr[...] = jnp.full(m_scr.shape, NEG_INF, _F32)
        acc_scr[...] = jnp.zeros(acc_scr.shape, _F32)

    def keys(j, n_tiles):
        return pl.ds(pl.multiple_of(j * TK, TK), n_tiles * TK)

    def sel_step(j, n_tiles, bias):
        cols = keys(j, n_tiles)

        def add_rows(g, r0):
            add = seladd[r0:r0 + ROW_BLOCK, cols]
            return add if bias is None else add + bias(g, r0)
        _flash_step(qs, kst_ref[:, cols], vs_ref[cols, :], add_rows, m_s, acc_s)

    def win_step(j, n_tiles, add_rows):
        cols = keys(j, n_tiles)
        _flash_step(qs, kwt_ref[:, cols], vw_ref[cols, :], add_rows, m_w, acc_w)

    near = lambda lo: (lambda g, r0: near_ref[g * TQ + r0:g * TQ + r0 + ROW_BLOCK, lo * TK:])
    no_add = lambda g, r0: None
    own = TQ // TK
    first = i * own

    n_far = jnp.maximum(first - 1, 0)

    def far_many(j, carry):
        sel_step(j * FAR_TILES, FAR_TILES, None)
        return carry

    def far_one(j, carry):
        sel_step(j, 1, None)
        return carry

    lax.fori_loop(0, n_far // FAR_TILES, far_many, 0)
    lax.fori_loop((n_far // FAR_TILES) * FAR_TILES, n_far, far_one, 0)

    @pl.when(i >= 1)
    def _():
        sel_step(first - 1, NEAR_TILES, near(0))
        win_step(first - 1, NEAR_TILES, near(0))

    @pl.when(i == 0)
    def _():
        sel_step(0, own, near(1))
        win_step(0, own, near(1))

    n_back = WINDOW // TK

    def back_step(start_tile, n_tiles, offset):
        masks = {}

        def add_rows(g, r0):
            if offset >= r0 + ROW_BLOCK - 1:
                return None
            if r0 not in masks:
                r = r0 + lax.broadcasted_iota(jnp.int32, (ROW_BLOCK, n_tiles * TK), 0)
                kk = offset + lax.broadcasted_iota(jnp.int32, (ROW_BLOCK, n_tiles * TK), 1)
                masks[r0] = jnp.where(kk >= r, 0.0, NEG_INF)
            return masks[r0]
        win_step(start_tile, n_tiles, add_rows)

    @pl.when(first >= n_back)
    def _():
        back_step(first - n_back, n_back - 1, 0)

    for i_small in range(1, -(-n_back // own)):
        if i_small * own > 1:
            @pl.when(i == i_small)
            def _():
                back_step(0, i_small * own - 1, (n_back - i_small * own) * TK)

    def finish(acc_scr):
        acc = acc_scr[...]
        return _unstack_heads(acc[..., :HEAD_DIM] / acc[..., HEAD_DIM:])

    gates = jnp.dot(jax.nn.sigmoid(glog_ref[...]), gexp_ref[...], preferred_element_type=_F32,
                    precision=lax.Precision.HIGHEST)
    width = Q_PER_KV * HEAD_DIM
    o_ref[...] = (gates[:, :width] * oc_ref[...] + gates[:, width:2 * width] * finish(acc_s)
                  + gates[:, 2 * width:] * finish(acc_w))


def _nsa_prompt_side(kv, keys_t, vals, prm):
    import numpy as np
    bsz, t_len = kv.shape[:2]
    n_cmp = (t_len - CMP_LEN) // CMP_STRIDE + 1
    n_cmp_pad = t_len // CMP_STRIDE
    assert n_cmp == n_cmp_pad - 1 and n_cmp_pad % LANES == 0 and t_len % TQ == 0
    n_sel = t_len // SEL_BLOCK
    grouped = lambda slot: kv[:, :, slot].reshape(bsz, n_cmp_pad, CMP_STRIDE * N_KV_HEADS * HEAD_DIM)
    one_page = jnp.arange(bsz, dtype=jnp.int32)[:, None]
    k_cmp = _cmp_pages(grouped(0), one_page, prm['cmp_pos_k'], prm['cmp_w1_k'], prm['cmp_w2_k'])
    v_cmp = _cmp_pages(grouped(1), one_page, prm['cmp_pos_v'], prm['cmp_w1_v'], prm['cmp_w2_v'])
    t_last = lambda a: jnp.transpose(a, (0, 2, 3, 1))
    t_rows = lambda a: jnp.transpose(a, (0, 2, 1, 3))
    pad_c = lambda a: a.reshape(bsz, n_cmp_pad, N_KV_HEADS, HEAD_DIM)
    rel_bias = prm['rel_bias']
    c_start = np.arange(n_cmp_pad)[None, :] * CMP_STRIDE
    s_start = np.arange(n_sel)[:, None] * SEL_BLOCK
    ovt = ((c_start < s_start + SEL_BLOCK) & (c_start + CMP_LEN > s_start) & (np.arange(n_cmp_pad)[None, :] < n_cmp))
    expand = np.arange(n_sel)[:, None] == (np.arange(t_len)[None, :] // SEL_BLOCK)
    lanes = np.arange(LANES)[:, None]
    cols = np.arange(N_BRANCH * Q_PER_KV * HEAD_DIM)[None, :]
    gexp = lanes == (cols // HEAD_DIM % Q_PER_KV) * N_BRANCH + cols // (Q_PER_KV * HEAD_DIM)
    g_cmp, near = _bias_tiles(rel_bias, n_cmp_pad)
    return {
        'kct': t_last(pad_c(k_cmp)), 'vc': t_rows(pad_c(v_cmp)),
        'keys_t': keys_t, 'vals': vals.reshape(bsz, t_len, vals.shape[-1]),
        'ovt': jnp.asarray(ovt, _BF16), 'expand': jnp.asarray(expand, _BF16), 'gexp': jnp.asarray(gexp, _F32),
        'g_cmp': g_cmp, 'near': near,
    }


def _nsa_prompt_attention(q, glog, side):
    bsz, t_len, _ = q.shape
    n_t = t_len // TQ
    width = Q_PER_KV * HEAD_DIM
    n_cmp_pad = side['kct'].shape[-1]
    n_sel = side['ovt'].shape[0]
    grid = (bsz, N_KV_HEADS, n_t)
    params = pltpu.CompilerParams(dimension_semantics=("parallel", "parallel", "arbitrary"),
                                  vmem_limit_bytes=VMEM_LIMIT)
    q_spec = pl.BlockSpec((None, TQ, width), lambda b, h, i: (b, i, h))
    per_bh = lambda *shape: pl.BlockSpec((None, None) + shape, lambda b, h, i: (b, h, 0, 0))
    per_h = lambda *shape: pl.BlockSpec((None,) + shape, lambda b, h, i: (h,) + (0,) * len(shape))
    const = lambda *shape: pl.BlockSpec(shape, lambda b, h, i: (0,) * len(shape))
    selt_spec = pl.BlockSpec((None, None, n_sel, TQ), lambda b, h, i: (b, h, 0, i))
    o_c, selt = pl.pallas_call(
        _nsa_cmp_kernel,
        grid=grid,
        in_specs=[q_spec, per_bh(HEAD_DIM, n_cmp_pad), per_bh(n_cmp_pad, HEAD_DIM),
                  per_h(Q_PER_KV * TQ, n_cmp_pad), const(n_sel, n_cmp_pad)],
        out_specs=[q_spec, selt_spec],
        out_shape=[jax.ShapeDtypeStruct((bsz, t_len, N_HEADS * HEAD_DIM), _F32),
                   jax.ShapeDtypeStruct((bsz, N_KV_HEADS, n_sel, t_len), _F32)],
        compiler_params=params,
        name="nsa_cmp_select",
    )(q, side['kct'], side['vc'], side['g_cmp'], side['ovt'])
    stat = pltpu.VMEM((Q_PER_KV, TQ, 2 * HEAD_DIM), _F32)
    keys_spec = lambda branch: pl.BlockSpec((HEAD_DIM, t_len), lambda b, h, i: (branch * N_KV_HEADS + h, b))
    vals_spec = lambda branch: pl.BlockSpec((None, t_len, 2 * HEAD_DIM),
                                            lambda b, h, i: (b, 0, branch * N_KV_HEADS + h))
    return pl.pallas_call(
        _nsa_sel_win_kernel,
        grid=grid,
        in_specs=[q_spec, q_spec, pl.BlockSpec((None, TQ, LANES), lambda b, h, i: (b, i, h)), selt_spec,
                  keys_spec(0), vals_spec(0), keys_spec(1), vals_spec(1),
                  per_h(Q_PER_KV * TQ, NEAR_TILES * TK), const(n_sel, t_len),
                  const(LANES, N_BRANCH * width)],
        out_specs=q_spec,
        out_shape=jax.ShapeDtypeStruct((bsz, t_len, N_HEADS * HEAD_DIM), _F32),
        scratch_shapes=[pltpu.VMEM((TQ, t_len), _F32), stat, stat, stat, stat],
        compiler_params=params,
        name="nsa_select_window",
    )(q, o_c, glog, selt, side['keys_t'], side['vals'], side['keys_t'], side['vals'], side['near'],
      side['expand'], side['gexp'])


PAGES_PER_STEP = 16


def _dot_nt(a, b):
    return lax.dot_general(a, b, (((1,), (1,)), ((), ())), preferred_element_type=_F32)


def _cmp_pages_kernel(pt_ref, *refs, n_pages):
    del pt_ref
    pages = refs[:n_pages]
    posa_ref, posb_ref, w1a_ref, w1b_ref, w2_ref, o_ref = refs[n_pages:]
    x = jnp.concatenate([pg[...] for pg in pages], axis=0)
    z0 = jnp.dot((x + posa_ref[...]).astype(_BF16), w1a_ref[...], preferred_element_type=_F32)
    z1 = jnp.dot((x + posb_ref[...]).astype(_BF16), w1b_ref[...], preferred_element_type=_F32)
    pre = z0 + pltpu.roll(z1, z1.shape[0] - 1, 0)
    o_ref[...] = jnp.dot(jax.nn.gelu(pre).astype(_BF16), w2_ref[...], preferred_element_type=_F32).astype(o_ref.dtype)


def _cmp_pages(grouped, page_table, pos_emb, w1, w2):
    n_seq, n_pages = page_table.shape
    groups_per_page, flat = grouped.shape[1:]
    width = N_KV_HEADS * HEAD_DIM
    hidden = w1.shape[1]
    eye = jnp.eye(N_KV_HEADS, dtype=_F32)
    w1r = w1.reshape(2, CMP_STRIDE, HEAD_DIM, hidden)
    bd1 = jnp.einsum('hk,abdf->abhdkf', eye, w1r).reshape(2, flat, N_KV_HEADS * hidden).astype(_BF16)
    bd2 = jnp.einsum('hk,fd->hfkd', eye, w2).reshape(N_KV_HEADS * hidden, width).astype(_BF16)
    pos_t = jnp.broadcast_to(pos_emb.reshape(2, CMP_STRIDE, 1, HEAD_DIM),
                             (2, CMP_STRIDE, N_KV_HEADS, HEAD_DIM)).reshape(2, 1, flat)
    n_rows = n_pages * groups_per_page
    page_spec = lambda k: pl.BlockSpec((None, groups_per_page, flat), lambda b, pt: (pt[b, k], 0, 0))
    const = lambda *shape: pl.BlockSpec(shape, lambda b, pt: (0,) * len(shape))
    return pl.pallas_call(
        functools.partial(_cmp_pages_kernel, n_pages=n_pages),
        grid_spec=pltpu.PrefetchScalarGridSpec(
            num_scalar_prefetch=1,
            grid=(n_seq,),
            in_specs=[page_spec(k) for k in range(n_pages)] + [
                const(1, flat), const(1, flat), const(flat, N_KV_HEADS * hidden), const(flat, N_KV_HEADS * hidden),
                const(N_KV_HEADS * hidden, width)],
            out_specs=pl.BlockSpec((None, n_rows, width), lambda b, pt: (b, 0, 0)),
        ),
        out_shape=jax.ShapeDtypeStruct((n_seq, n_rows, width), _BF16),
        compiler_params=pltpu.CompilerParams(dimension_semantics=("parallel",), vmem_limit_bytes=VMEM_LIMIT),
        name="cmp_pages",
    )(page_table, *([grouped] * n_pages), pos_t[0], pos_t[1], bd1[0], bd1[1], bd2)


def _cmp_pages_t_kernel(pt_ref, *refs, n_pages):
    del pt_ref
    pages = refs[:n_pages]
    posa_ref, posb_ref, w1a_ref, w1b_ref, w2_ref, o_ref, rows_scr = refs[n_pages:]
    page_rows = pages[0].shape[1]
    n_groups = n_pages * page_rows // CMP_STRIDE
    for k, pg in enumerate(pages):
        x = pg[...].T
        for half in range(rows_scr.shape[0]):
            rows_scr[half, k * page_rows:(k + 1) * page_rows, :] = x[:, half * LANES:(half + 1) * LANES]
    z0 = z1 = None
    for b in range(CMP_STRIDE):
        xb = jnp.concatenate([rows_scr[half, pl.ds(b, n_groups, stride=CMP_STRIDE), :]
                              for half in range(rows_scr.shape[0])], axis=1)
        d0 = jnp.dot((xb + posa_ref[b]).astype(_BF16), w1a_ref[b], preferred_element_type=_F32)
        d1 = jnp.dot((xb + posb_ref[b]).astype(_BF16), w1b_ref[b], preferred_element_type=_F32)
        z0 = d0 if z0 is None else z0 + d0
        z1 = d1 if z1 is None else z1 + d1
    pre = z0 + pltpu.roll(z1, n_groups - 1, 0)
    o_ref[...] = jnp.dot(jax.nn.gelu(pre).astype(_BF16), w2_ref[...], preferred_element_type=_F32).astype(o_ref.dtype)


def _cmp_pages_t(pages_t, slot, page_table, pos_emb, w1, w2):
    n_seq, n_pages = page_table.shape
    page_rows = pages_t.shape[2]
    width = N_KV_HEADS * HEAD_DIM
    hidden = w1.shape[1]
    eye = jnp.eye(N_KV_HEADS, dtype=_F32)
    w1r = w1.reshape(2, CMP_STRIDE, HEAD_DIM, hidden)
    bd1 = jnp.einsum('hk,abdf->abhdkf', eye, w1r).reshape(2, CMP_STRIDE, width, N_KV_HEADS * hidden).astype(_BF16)
    bd2 = jnp.einsum('hk,fd->hfkd', eye, w2).reshape(N_KV_HEADS * hidden, width).astype(_BF16)
    pos_t = jnp.broadcast_to(pos_emb.reshape(2, CMP_STRIDE, 1, 1, HEAD_DIM),
                             (2, CMP_STRIDE, 1, N_KV_HEADS, HEAD_DIM)).reshape(2, CMP_STRIDE, 1, width)
    n_rows = n_pages * page_rows // CMP_STRIDE
    page_spec = lambda k: pl.BlockSpec((None, width, page_rows), lambda b, pt: (pt[b, k], slot, 0))
    const = lambda *shape: pl.BlockSpec(shape, lambda b, pt: (0,) * len(shape))
    return pl.pallas_call(
        functools.partial(_cmp_pages_t_kernel, n_pages=n_pages),
        grid_spec=pltpu.PrefetchScalarGridSpec(
            num_scalar_prefetch=1,
            grid=(n_seq,),
            in_specs=[page_spec(k) for k in range(n_pages)] + [
                const(CMP_STRIDE, 1, width), const(CMP_STRIDE, 1, width),
                const(CMP_STRIDE, width, N_KV_HEADS * hidden), const(CMP_STRIDE, width, N_KV_HEADS * hidden),
                const(N_KV_HEADS * hidden, width)],
            out_specs=pl.BlockSpec((None, n_rows, width), lambda b, pt: (b, 0, 0)),
            scratch_shapes=[pltpu.VMEM((width // LANES, n_pages * page_rows, LANES), _F32)],
        ),
        out_shape=jax.ShapeDtypeStruct((n_seq, n_rows, width), _BF16),
        compiler_params=pltpu.CompilerParams(dimension_semantics=("parallel",), vmem_limit_bytes=VMEM_LIMIT),
        name="cmp_pages_t",
    )(page_table, *([pages_t] * n_pages), pos_t[0], pos_t[1], bd1[0], bd1[1], bd2)


def _pick_own_kv_head(x, n_q):
    rows_per = Q_PER_KV * n_q
    return jnp.concatenate([x[h * rows_per:(h + 1) * rows_per, h * HEAD_DIM:(h + 1) * HEAD_DIM]
                            for h in range(N_KV_HEADS)], axis=0)


def _nsa_sample_kernel(pt_ref, q_ref, gl_ref, kc_ref, vc_ref, win_ref, new_ref, *rest, n_q, past_len, n_sel):
    del pt_ref
    pages = rest[:PAGES_PER_STEP]
    (tcmp_ref, tsel_ref, twin_ref, exp_ref, ov_ref, o_ref,
     addsel, m_scr, l_scr, acc_scr, oc_scr, ow_scr) = rest[PAGES_PER_STEP:]
    j = pl.program_id(1)
    rows = N_HEADS * n_q
    width = N_KV_HEADS * HEAD_DIM
    w_buf = win_ref.shape[1]
    step_keys = PAGES_PER_STEP * pages[0].shape[1]
    rr = lax.broadcasted_iota(jnp.int32, (rows, width), 0)
    cc = lax.broadcasted_iota(jnp.int32, (rows, width), 1)
    q4 = jnp.concatenate([q_ref[...] * (HEAD_DIM ** -0.5)] * N_KV_HEADS, axis=1)
    qbd = jnp.where(rr // (Q_PER_KV * n_q) == cc // HEAD_DIM, q4, 0.0).astype(_BF16)
    new_rows = jnp.concatenate([new_ref[...], jnp.zeros((LANES - n_q, new_ref.shape[1]), _F32)], axis=0).astype(_BF16)

    @pl.when(j == 0)
    def _first():
        s = _dot_nt(qbd, kc_ref[...]) + tcmp_ref[...]
        p = jnp.exp(s - jnp.max(s, axis=-1, keepdims=True))
        pb = (p / jnp.sum(p, axis=-1, keepdims=True)).astype(_BF16)
        oc_scr[...] = _pick_own_kv_head(jnp.dot(pb, vc_ref[...], preferred_element_type=_F32), n_q)
        pm = jnp.dot(pb, ov_ref[...], preferred_element_type=_F32)
        imp = jnp.concatenate(
            [sum(pm[(h * Q_PER_KV + g) * n_q:(h * Q_PER_KV + g + 1) * n_q] for g in range(Q_PER_KV))
             for h in range(N_KV_HEADS)], axis=0)
        blk = lax.broadcasted_iota(jnp.int32, imp.shape, 1)
        t = past_len + lax.broadcasted_iota(jnp.int32, imp.shape, 0) % n_q
        cur = t // SEL_BLOCK
        forced = (blk == 0) | (blk == cur) | (blk == cur - 1)
        score = jnp.where(forced, FORCE, jnp.where(blk * SEL_BLOCK <= t, imp, -FORCE))
        score = jnp.where(blk < n_sel, score, -jnp.inf)
        rank = jnp.zeros(imp.shape, jnp.int32)
        for sp in range(n_sel):
            col = score[:, sp:sp + 1]
            rank += ((col > score) | ((col == score) & (blk > sp))).astype(jnp.int32)
        sel = (rank < min(SEL_TOPK, n_sel)).astype(_BF16)
        sel_rows = jnp.concatenate([sel[h * n_q:(h + 1) * n_q] for h in range(N_KV_HEADS) for _ in range(Q_PER_KV)],
                                   axis=0)
        addsel[...] = (jnp.dot(sel_rows, exp_ref[...], preferred_element_type=_F32) - 1.0) * (-NEG_INF) + tsel_ref[...]
        kw_t = win_ref[:width, :].astype(_BF16)
        vw_t = win_ref[width:, :].astype(_BF16)
        s1 = jnp.dot(qbd, kw_t, preferred_element_type=_F32) + twin_ref[:, :w_buf]
        s2 = _dot_nt(qbd, new_rows[:, 2 * width:3 * width]) + twin_ref[:, w_buf:]
        m = jnp.maximum(jnp.max(s1, axis=-1, keepdims=True), jnp.max(s2, axis=-1, keepdims=True))
        p1 = jnp.exp(s1 - m)
        p2 = jnp.exp(s2 - m)
        l = jnp.sum(p1, axis=-1, keepdims=True) + jnp.sum(p2, axis=-1, keepdims=True)
        ow = (_dot_nt(p1.astype(_BF16), vw_t)
              + jnp.dot(p2.astype(_BF16), new_rows[:, 3 * width:], preferred_element_type=_F32))
        ow_scr[...] = _pick_own_kv_head(ow / l, n_q)
        m_scr[...] = jnp.full(m_scr.shape, NEG_INF, _F32)
        l_scr[...] = jnp.zeros(l_scr.shape, _F32)
        acc_scr[...] = jnp.zeros(acc_scr.shape, _F32)

    def flash(s, pv):
        m_old = m_scr[...]
        m_new = jnp.maximum(m_old, jnp.max(s, axis=-1, keepdims=True))
        alpha = jnp.exp(m_old - m_new)
        p = jnp.exp(s - m_new)
        m_scr[...] = m_new
        l_scr[...] = alpha * l_scr[...] + jnp.sum(p, axis=-1, keepdims=True)
        acc_scr[...] = alpha * acc_scr[...] + pv(p.astype(_BF16))

    k_t = jnp.concatenate([pg[:width, :] for pg in pages], axis=1).astype(_BF16)
    v_t = jnp.concatenate([pg[width:, :] for pg in pages], axis=1).astype(_BF16)
    flash(jnp.dot(qbd, k_t, preferred_element_type=_F32)
          + addsel[:, pl.ds(pl.multiple_of(j * step_keys, step_keys), step_keys)], lambda p: _dot_nt(p, v_t))

    @pl.when(j == pl.num_programs(1) - 1)
    def _last():
        flash(_dot_nt(qbd, new_rows[:, :width]) + addsel[:, past_len:],
              lambda p: jnp.dot(p, new_rows[:, width:2 * width], preferred_element_type=_F32))
        o_s = _pick_own_kv_head(acc_scr[...] / l_scr[...], n_q)
        gates = jax.nn.sigmoid(gl_ref[...])
        o_ref[...] = gates[:, 0:1] * oc_scr[...] + gates[:, 1:2] * o_s + gates[:, 2:3] * ow_scr[...]


def _nsa_sample_side(cache_kv, page_table, cache_win, kv_new, rel_bias, prm):
    import numpy as np
    n_seq, n_pages = page_table.shape
    n_phys, page_rows = cache_kv.shape[:2]
    n_q = kv_new.shape[1]
    width = N_KV_HEADS * HEAD_DIM
    past_len = n_pages * page_rows
    w_buf = cache_win.shape[1]
    tk = past_len + n_q
    n_cmp = (tk - CMP_LEN) // CMP_STRIDE + 1
    n_cmp_pad = past_len // CMP_STRIDE
    n_sel = -(-tk // SEL_BLOCK)
    assert n_cmp == n_cmp_pad - 1 and n_cmp_pad == LANES and n_sel <= LANES and n_q <= LANES
    assert page_rows % SEL_BLOCK == 0 and n_pages % PAGES_PER_STEP == 0 and w_buf == min(WINDOW, past_len)
    cache_t = jnp.transpose(cache_kv, (0, 2, 3, 4, 1)).reshape(n_phys, KV_SLOTS_PAGED * width, page_rows)
    win_t = jnp.transpose(cache_win, (0, 2, 3, 4, 1)).reshape(n_seq, KV_SLOTS_WIN * width, w_buf)
    kc = _cmp_pages_t(cache_t, 0, page_table, prm['cmp_pos_k'], prm['cmp_w1_k'], prm['cmp_w2_k'])
    vc = _cmp_pages_t(cache_t, 1, page_table, prm['cmp_pos_v'], prm['cmp_w1_v'], prm['cmp_w2_v'])
    rows = N_HEADS * n_q
    t = past_len + np.arange(n_q)[:, None]
    masked = lambda ok, d: jnp.where(jnp.asarray(ok), _bias_by_distance(rel_bias, d), NEG_INF).reshape(rows, -1)
    n = np.arange(n_cmp_pad)[None, :]
    d_c = t - (n * CMP_STRIDE + CMP_LEN - 1)
    t_cmp = masked((d_c >= 0) & (n < n_cmp), d_c)
    pos = np.arange(past_len + LANES)[None, :]
    t_sel = masked((t - pos >= 0) & (pos < tk), t - pos)
    l = np.arange(w_buf + LANES)[None, :]
    d_w = t - (past_len - w_buf + l)
    t_win = masked((d_w >= 0) & (d_w <= WINDOW) & (l < w_buf + n_q), d_w)
    expand = (np.arange(LANES)[:, None] == pos // SEL_BLOCK) & (np.arange(LANES)[:, None] < n_sel)
    c_start = np.arange(n_cmp_pad)[:, None] * CMP_STRIDE
    s_start = np.arange(LANES)[None, :] * SEL_BLOCK
    ov = ((c_start < s_start + SEL_BLOCK) & (c_start + CMP_LEN > s_start) & (np.arange(n_cmp_pad)[:, None] < n_cmp)
          & (np.arange(LANES)[None, :] < n_sel))
    return {
        'cache_t': cache_t, 'page_table': page_table, 'kc': kc, 'vc': vc, 'win_t': win_t,
        'new': kv_new.reshape(n_seq, n_q, -1)[:, :, 2 * width:],
        't_cmp': t_cmp, 't_sel': t_sel, 't_win': t_win,
        'expand': jnp.asarray(expand, _BF16), 'ov': jnp.asarray(ov, _BF16), 'n_sel': n_sel, 'past_len': past_len,
    }


def _nsa_sample_attention(q, glog, side):
    n_seq, n_q, _ = q.shape
    glog = glog.reshape(n_seq, n_q, N_KV_HEADS, LANES)[..., :Q_PER_KV * N_BRANCH]
    rows = N_HEADS * n_q
    width = N_KV_HEADS * HEAD_DIM
    page_table = side['page_table']
    n_pages = page_table.shape[1]
    page_rows = side['cache_t'].shape[2]
    past_len = side['past_len']
    w_buf = side['win_t'].shape[2]
    n_steps = n_pages // PAGES_PER_STEP
    q_rows = jnp.transpose(q.reshape(n_seq, n_q, N_HEADS, HEAD_DIM), (0, 2, 1, 3)).reshape(n_seq, rows, HEAD_DIM)
    gl = jnp.transpose(glog.reshape(n_seq, n_q, N_HEADS, N_BRANCH), (0, 2, 1, 3)).reshape(n_seq, rows, N_BRANCH)
    gl = jnp.pad(gl, ((0, 0), (0, 0), (0, LANES - N_BRANCH)))
    per_seq = lambda *shape: pl.BlockSpec((None,) + shape, lambda b, j, pt: (b,) + (0,) * len(shape))
    const = lambda *shape: pl.BlockSpec(shape, lambda b, j, pt: (0,) * len(shape))
    page_spec = lambda k: pl.BlockSpec((None, 2 * width, page_rows),
                                       lambda b, j, pt: (pt[b, j * PAGES_PER_STEP + k], 1, 0))
    out = pl.pallas_call(
        functools.partial(_nsa_sample_kernel, n_q=n_q, past_len=past_len, n_sel=side['n_sel']),
        grid_spec=pltpu.PrefetchScalarGridSpec(
            num_scalar_prefetch=1,
            grid=(n_seq, n_steps),
            in_specs=[per_seq(rows, HEAD_DIM), per_seq(rows, LANES), per_seq(LANES, width), per_seq(LANES, width),
                      per_seq(2 * width, w_buf), per_seq(n_q, 4 * width)]
            + [page_spec(k) for k in range(PAGES_PER_STEP)]
            + [const(rows, LANES), const(rows, past_len + LANES), const(rows, w_buf + LANES),
               const(LANES, past_len + LANES), const(LANES, LANES)],
            out_specs=per_seq(rows, HEAD_DIM),
            scratch_shapes=[pltpu.VMEM((rows, past_len + LANES), _F32), pltpu.VMEM((rows, 1), _F32),
                            pltpu.VMEM((rows, 1), _F32), pltpu.VMEM((rows, width), _F32),
                            pltpu.VMEM((rows, HEAD_DIM), _F32), pltpu.VMEM((rows, HEAD_DIM), _F32)],
        ),
        out_shape=jax.ShapeDtypeStruct((n_seq, rows, HEAD_DIM), _F32),
        compiler_params=pltpu.CompilerParams(dimension_semantics=("parallel", "arbitrary"),
                                             vmem_limit_bytes=VMEM_LIMIT),
        name="nsa_sample",
    )(page_table, q_rows, gl, side['kc'], side['vc'], side['win_t'], side['new'],
      *([side['cache_t']] * PAGES_PER_STEP), side['t_cmp'], side['t_sel'], side['t_win'], side['expand'], side['ov'])
    return jnp.transpose(out.reshape(n_seq, N_HEADS, n_q, HEAD_DIM), (0, 2, 1, 3)).reshape(n_seq, n_q, N_HEADS * HEAD_DIM)


def _run_trunk(x, p, s0_re, s0_im, paged, win_buf, prm):
    bsz, t_len, _ = x.shape
    ssm_re, ssm_im = [], []
    side = None
    kv_rows_new = None
    win_state = None
    y = None
    x2 = x.reshape(-1, D_MODEL)
    as_seq = lambda a: a.reshape(bsz, t_len, a.shape[-1])
    for i in range(DEPTH):
        if i < N_A_LAYERS:
            u2, = _rows_call(lambda xt, gain: (_norm(xt, gain),), [x2], [prm['g_mix'][i].reshape(1, -1)],
                             [D_MODEL], "s5_norm")
            ssm = (prm['ssm_a_re'][i], prm['ssm_a_im'][i], prm['ssm_log_dt'][i], prm['ssm_b_re'][i],
                   prm['ssm_b_im'][i], prm['ssm_c_re'][i], prm['ssm_c_im'][i], prm['ssm_d'][i])
            s0 = _state_to_tiles(s0_re[i], s0_im[i])
            if t_len == 8:
                ys, s_fin = _s5_scan(u2[None], jnp.transpose(s0, (1, 0, 2))[None], False, *ssm)
                s_re, s_im = _state_from_tiles(jnp.transpose(s_fin[0], (1, 0, 2)))
            else:
                assert t_len % (16 * 8) == 0
                ys, s_fin = _s5_scan(as_seq(u2), s0[:, :, None, :], True, *ssm)
                s_re, s_im = _state_from_tiles(s_fin[:, :, 0, :])
            ssm_re.append(s_re)
            ssm_im.append(s_im)
            x2 = _glu_residual(x2, ys.reshape(-1, D_MODEL), prm['w_glu'][i])
        else:
            j = i - N_A_LAYERS
            q, glog = _qg_project(x2, prm['g_mix'][i], prm['w_qg'][j])
            attend = _nsa_prompt_attention if paged is None else _nsa_sample_attention
            mixed = attend(as_seq(q), as_seq(glog), side)
            x2 = _wo_residual(x2, mixed.reshape(-1, D_MODEL), prm['w_o'][j])
        x2 = _moe_layer(x2, prm['g_ffn'][i], prm['w_route_group'][i], prm['b_route_group'][i],
                        prm['w_route_expert'][i], prm['b_route_expert'][i], prm['w_exp_up'], prm['w_exp_down'], i)
        ple = (x2, p[i].reshape(-1, p.shape[-1]), prm['g_ple'][i], prm['w_ple_gate'][i], prm['w_ple_proj'][i])
        if i == N_A_LAYERS - 1:
            x2, kv2, *operands = _ple_residual(*ple, g_next=prm['g_kv'], w_next=prm['w_kv'],
                                               attention_operands=paged is None)
            kv = kv2.reshape(bsz, t_len, KV_SLOTS_PAGED + KV_SLOTS_WIN, N_KV_HEADS, HEAD_DIM)
            kv_rows_new, win_new = kv[:, :, :KV_SLOTS_PAGED], kv[:, :, KV_SLOTS_PAGED:]
            if paged is None:
                win_state = win_new[:, -min(WINDOW, t_len):]
                side = _nsa_prompt_side(kv, *operands, prm)
            else:
                w_buf = win_buf.shape[1]
                win_state = jnp.concatenate([win_buf, win_new], axis=1)[:, -w_buf:]
                side = _nsa_sample_side(paged[0], paged[1], win_buf, kv, prm['rel_bias'], prm)
        elif i == DEPTH - 1:
            x2, y = _ple_residual(*ple, g_next=prm['g_final'])
        else:
            x2, = _ple_residual(*ple)
    return as_seq(y), kv_rows_new, win_state, jnp.stack(ssm_re), jnp.stack(ssm_im)


def kernel(x_prompt, x_sample, p_prompt, p_sample, cache_kv, cache_win, state_ssm_re, state_ssm_im, page_table,
           g_mix, g_ffn, g_ple, g_kv, g_final,
           ssm_a_re, ssm_a_im, ssm_log_dt, ssm_b_re, ssm_b_im, ssm_c_re, ssm_c_im, ssm_d, w_glu,
           w_kv, cmp_pos_k, cmp_pos_v, cmp_w1_k, cmp_w2_k, cmp_w1_v, cmp_w2_v, w_qg, w_o, rel_bias,
           w_route_group, b_route_group, w_route_expert, b_route_expert, w_exp_up, w_exp_down,
           w_ple_proj, w_ple_gate):
    prm = {
        'g_mix': g_mix, 'g_ffn': g_ffn, 'g_ple': g_ple, 'g_kv': g_kv, 'g_final': g_final,
        'ssm_a_re': ssm_a_re, 'ssm_a_im': ssm_a_im, 'ssm_log_dt': ssm_log_dt,
        'ssm_b_re': ssm_b_re, 'ssm_b_im': ssm_b_im, 'ssm_c_re': ssm_c_re, 'ssm_c_im': ssm_c_im,
        'ssm_d': ssm_d, 'w_glu': w_glu,
        'w_kv': w_kv, 'cmp_pos_k': cmp_pos_k, 'cmp_pos_v': cmp_pos_v,
        'cmp_w1_k': cmp_w1_k, 'cmp_w2_k': cmp_w2_k, 'cmp_w1_v': cmp_w1_v, 'cmp_w2_v': cmp_w2_v,
        'w_qg': w_qg, 'w_o': w_o, 'rel_bias': rel_bias,
        'w_route_group': w_route_group, 'b_route_group': b_route_group,
        'w_route_expert': w_route_expert, 'b_route_expert': b_route_expert,
        'w_exp_up': w_exp_up, 'w_exp_down': w_exp_down,
        'w_ple_proj': w_ple_proj, 'w_ple_gate': w_ple_gate,
    }
    zero_state = jnp.zeros((N_A_LAYERS, x_prompt.shape[0], SSM_GROUPS, SSM_STATE), _F32)
    y_prompt, kv_prompt, win_prompt, ssm_re_prompt, ssm_im_prompt = _run_trunk(
        x_prompt, p_prompt, zero_state, zero_state, None, None, prm)
    y_sample, kv_sample, win_sample, ssm_re_sample, ssm_im_sample = _run_trunk(
        x_sample, p_sample, state_ssm_re, state_ssm_im, (cache_kv, page_table), cache_win, prm)
    return (y_prompt, y_sample, kv_prompt, win_prompt, ssm_re_prompt, ssm_im_prompt,
            kv_sample, win_sample, ssm_re_sample, ssm_im_sample)
```

```python
import functools
import math

import jax
import jax.numpy as jnp
from jax import lax
from jax.experimental import pallas as pl
from jax.experimental.pallas import tpu as pltpu

D_MODEL = 1024
DEPTH = 4
N_A_LAYERS = DEPTH // 2
SSM_GROUP = 16
SSM_GROUPS = D_MODEL // SSM_GROUP
SSM_STATE = 64
N_HEADS = 16
HEAD_DIM = D_MODEL // N_HEADS
N_KV_HEADS = 4
Q_PER_KV = N_HEADS // N_KV_HEADS
CMP_LEN = 32
CMP_STRIDE = 16
SEL_BLOCK = 64
SEL_TOPK = 16
WINDOW = 512
N_BRANCH = 3
KV_SLOTS_PAGED = 4
KV_SLOTS_WIN = 2
Q_BLOCK = 64
N_BUCKETS = 32
MAX_DISTANCE = 128
N_EXPERT_GROUPS = 4
EXPERTS_PER_GROUP = 4
N_EXPERTS = N_EXPERT_GROUPS * EXPERTS_PER_GROUP
EXPERT_TOPK = 2
D_EXPERT = 256
RMS_EPS = 1e-6
NEG_INF = -1e30
FORCE = 1e4

LANES = 128
VMEM_LIMIT = 48 * 1024 * 1024

_F32 = jnp.float32
_BF16 = jnp.bfloat16


def _moe_kernel(x_ref, g_ref, wr_ref, br_ref, wup_ref, wdn_ref, o_ref, h_scr, comb_scr, acc_scr):
    e = pl.program_id(1)
    lane = lax.broadcasted_iota(jnp.int32, comb_scr.shape, 1)

    @pl.when(e == 0)
    def _route():
        x = x_ref[...]
        h = x * lax.rsqrt(jnp.mean(x * x, axis=-1, keepdims=True) + RMS_EPS) * g_ref[...]
        h_scr[...] = h.astype(_BF16)
        logits = jnp.dot(h_scr[...], wr_ref[...].astype(_BF16), preferred_element_type=_F32) + br_ref[...]
        is_grp = (lane >= N_EXPERTS) & (lane < N_EXPERTS + N_EXPERT_GROUPS)
        lg = jnp.where(is_grp, logits, -jnp.inf)
        gmax = jnp.max(lg, axis=-1, keepdims=True)
        gi = jnp.min(jnp.where(lg == gmax, lane, LANES), axis=-1, keepdims=True) - N_EXPERTS
        gp = 1.0 / jnp.sum(jnp.where(is_grp, jnp.exp(lg - gmax), 0.0), axis=-1, keepdims=True)
        in_grp = (lane < N_EXPERTS) & ((lane // EXPERTS_PER_GROUP) == gi)
        le = jnp.where(in_grp, logits, -jnp.inf)
        m1 = jnp.max(le, axis=-1, keepdims=True)
        i1 = jnp.min(jnp.where(le == m1, lane, LANES), axis=-1, keepdims=True)
        le2 = jnp.where(lane == i1, -jnp.inf, le)
        m2 = jnp.max(le2, axis=-1, keepdims=True)
        i2 = jnp.min(jnp.where(le2 == m2, lane, LANES), axis=-1, keepdims=True)
        e2 = jnp.exp(m2 - m1)
        den = 1.0 + e2
        comb_scr[...] = jnp.where(lane == i1, gp / den, jnp.where(lane == i2, gp * e2 / den, 0.0))
        acc_scr[...] = x

    up = jnp.dot(h_scr[...], wup_ref[0].astype(_BF16), preferred_element_type=_F32)
    a = up[:, :D_EXPERT]
    b = up[:, D_EXPERT:]
    c = jnp.sum(jnp.where(lane == e, comb_scr[...], 0.0), axis=-1, keepdims=True)
    act = (a * jax.nn.sigmoid(a)) * b * c
    acc_scr[...] += jnp.dot(act.astype(_BF16), wdn_ref[0].astype(_BF16), preferred_element_type=_F32)

    @pl.when(e == N_EXPERTS - 1)
    def _store():
        o_ref[...] = acc_scr[...]


def _moe_layer(x2, g, w_rg, b_rg, w_re, b_re, w_up, w_dn, layer):
    n_tok, d = x2.shape
    tm = min(n_tok, 1024)
    wr = jnp.zeros((d, LANES), _F32).at[:, :N_EXPERTS].set(w_re).at[:, N_EXPERTS:N_EXPERTS + N_EXPERT_GROUPS].set(w_rg)
    br = jnp.zeros((1, LANES), _F32).at[0, :N_EXPERTS].set(b_re).at[0, N_EXPERTS:N_EXPERTS + N_EXPERT_GROUPS].set(b_rg)
    return pl.pallas_call(
        _moe_kernel,
        grid=(n_tok // tm, N_EXPERTS),
        in_specs=[
            pl.BlockSpec((tm, d), lambda i, e: (i, 0)),
            pl.BlockSpec((1, d), lambda i, e: (0, 0)),
            pl.BlockSpec((d, LANES), lambda i, e: (0, 0)),
            pl.BlockSpec((1, LANES), lambda i, e: (0, 0)),
            pl.BlockSpec((None, 1, d, 2 * D_EXPERT), lambda i, e: (layer, e, 0, 0)),
            pl.BlockSpec((None, 1, D_EXPERT, d), lambda i, e: (layer, e, 0, 0)),
        ],
        out_specs=pl.BlockSpec((tm, d), lambda i, e: (i, 0)),
        out_shape=jax.ShapeDtypeStruct((n_tok, d), _F32),
        scratch_shapes=[
            pltpu.VMEM((tm, d), _BF16),
            pltpu.VMEM((tm, LANES), _F32),
            pltpu.VMEM((tm, d), _F32),
        ],
        compiler_params=pltpu.CompilerParams(
            dimension_semantics=("parallel", "arbitrary"), vmem_limit_bytes=VMEM_LIMIT),
        name="hmoe",
    )(x2, g.reshape(1, d), wr, br, w_up, w_dn)


ROW_TILE = 512


def _rows_kernel(*refs, body, n_in):
    outs = body(*[r[...] for r in refs[:n_in]])
    for o_ref, o in zip(refs[n_in:], outs):
        o_ref[...] = o


def _rows_call(body, rows, consts, outs, name):
    n_rows = rows[0].shape[0]
    tm = min(n_rows, ROW_TILE)
    row_spec = lambda width: pl.BlockSpec((tm, width), lambda i: (i, 0))
    const_spec = lambda c: pl.BlockSpec(c.shape, lambda i: (0,) * c.ndim)
    out_specs, out_shape = [], []
    for o in outs:
        if isinstance(o, int):
            o = (o, _F32)
        if o[0] == 'T':
            out_specs.append(pl.BlockSpec((o[1], tm), lambda i: (0, i)))
            out_shape.append(jax.ShapeDtypeStruct((o[1], n_rows), o[2]))
        else:
            out_specs.append(row_spec(o[0]))
            out_shape.append(jax.ShapeDtypeStruct((n_rows, o[0]), o[1]))
    return pl.pallas_call(
        functools.partial(_rows_kernel, body=body, n_in=len(rows) + len(consts)),
        grid=(n_rows // tm,),
        in_specs=[row_spec(r.shape[1]) for r in rows] + [const_spec(c) for c in consts],
        out_specs=out_specs,
        out_shape=out_shape,
        compiler_params=pltpu.CompilerParams(dimension_semantics=("parallel",), vmem_limit_bytes=VMEM_LIMIT),
        name=name,
    )(*rows, *consts)


def _norm(x, g):
    return x * lax.rsqrt(jnp.mean(x * x, axis=-1, keepdims=True) + RMS_EPS) * g


def _mm(x, w):
    return jnp.dot(x.astype(_BF16), w, preferred_element_type=_F32)


def _glu_residual(x2, y2, w_glu):
    def body(x, y, w):
        ag = _mm(jax.nn.gelu(y), w)
        return (x + ag[:, :D_MODEL] * jax.nn.sigmoid(ag[:, D_MODEL:]),)
    return _rows_call(body, [x2, y2], [w_glu.astype(_BF16)], [D_MODEL], "glu_residual")[0]


def _qg_project(x2, g, w_qg):
    per_kv = Q_PER_KV * N_BRANCH
    w_g = w_qg[:, N_HEADS * HEAD_DIM:].reshape(D_MODEL, N_KV_HEADS, per_kv)
    w_g = jnp.pad(w_g, ((0, 0), (0, 0), (0, LANES - per_kv))).reshape(D_MODEL, N_KV_HEADS * LANES)
    w = jnp.concatenate([w_qg[:, :N_HEADS * HEAD_DIM], w_g], axis=1).astype(_BF16)

    def body(x, gain, w):
        qg = _mm(_norm(x, gain), w)
        return qg[:, :N_HEADS * HEAD_DIM], qg[:, N_HEADS * HEAD_DIM:]
    return _rows_call(body, [x2], [g.reshape(1, -1), w], [N_HEADS * HEAD_DIM, N_KV_HEADS * LANES], "qg_project")


def _wo_residual(x2, mixed2, w_o):
    body = lambda x, m, w: (x + _mm(m, w),)
    return _rows_call(body, [x2, mixed2], [w_o.astype(_BF16)], [D_MODEL], "wo_residual")[0]


def _ple_residual(x2, p2, g_ple, w_gate, w_proj, g_next=None, w_next=None, attention_operands=False):
    consts = [g_ple.reshape(1, -1), w_gate.astype(_BF16), w_proj.astype(_BF16)]
    outs = [D_MODEL]
    width = N_KV_HEADS * HEAD_DIM
    if g_next is not None:
        consts.append(g_next.reshape(1, -1))
        outs.append(D_MODEL if w_next is None else w_next.shape[1])
    if w_next is not None:
        consts.append(w_next.astype(_BF16))
    if attention_operands:
        outs += [('T', 2 * width, _BF16), (4 * width, _BF16)]

    def body(x, p, gain, wg, wp, *nxt):
        out = x + _mm(p, wp) * jax.nn.sigmoid(_mm(_norm(x, gain), wg))
        if not nxt:
            return (out,)
        normed = _norm(out, nxt[0])
        if len(nxt) == 1:
            return out, normed
        kv = _mm(normed, nxt[1])
        if not attention_operands:
            return out, kv
        slot = lambda s: kv[:, s * width:(s + 1) * width]
        keys_t = jnp.concatenate([slot(2), slot(4)], axis=1).T.astype(_BF16)
        ones = jnp.ones((kv.shape[0], HEAD_DIM), _F32)
        vals = jnp.concatenate([piece for s in (3, 5) for h in range(N_KV_HEADS)
                                for piece in (slot(s)[:, h * HEAD_DIM:(h + 1) * HEAD_DIM], ones)], axis=1)
        return out, kv, keys_t, vals.astype(_BF16)
    return _rows_call(body, [x2, p2], consts, outs, "ple_residual")


def _cmul(ar, ai, br, bi):
    return ar * br - ai * bi, ar * bi + ai * br


S5_TILE_GROUPS = LANES // SSM_GROUP
S5_TILES = D_MODEL // LANES
S5_STATE_LANES = S5_TILE_GROUPS * SSM_STATE


def _s5_operators(a_re, a_im, log_dt, b_re, b_im, c_re, c_im, chunk):
    hp = lax.Precision.HIGHEST
    dt = jnp.exp(log_dt)[:, None]
    decay = jnp.exp(a_re * dt)
    ab_re, ab_im = decay * jnp.cos(a_im * dt), decay * jnp.sin(a_im * dt)
    den = a_re * a_re + a_im * a_im
    num_re = ab_re - 1.0
    q_re = (num_re * a_re + ab_im * a_im) / den
    q_im = (ab_im * a_re - num_re * a_im) / den
    bb_re, bb_im = _cmul(q_re[..., None], q_im[..., None], b_re, b_im)
    pw_re, pw_im = [jnp.ones_like(ab_re)], [jnp.zeros_like(ab_im)]
    for _ in range(chunk):
        nr, ni = _cmul(pw_re[-1], pw_im[-1], ab_re, ab_im)
        pw_re.append(nr)
        pw_im.append(ni)
    pw_re, pw_im = jnp.stack(pw_re), jnp.stack(pw_im)
    w_re, w_im = _cmul(pw_re[:chunk, :, :, None], pw_im[:chunk, :, :, None], bb_re[None], bb_im[None])
    k_lag = (jnp.einsum('gcp,jgpd->jgcd', c_re, w_re, precision=hp)
             - jnp.einsum('gcp,jgpd->jgcd', c_im, w_im, precision=hp))
    ca_re, ca_im = _cmul(c_re[None], c_im[None], pw_re[1:, :, None, :], pw_im[1:, :, None, :])
    import numpy as np

    def block_diag(per_group):
        a = per_group.reshape((chunk, S5_TILES, S5_TILE_GROUPS) + per_group.shape[2:])
        a = jnp.transpose(a, (1, 0, 2, 4, 3))
        n_r, n_c = a.shape[3:]
        wide = jnp.tile(a.reshape(S5_TILES, chunk, S5_TILE_GROUPS * n_r, n_c), (1, 1, 1, S5_TILE_GROUPS))
        same = (np.arange(S5_TILE_GROUPS * n_r)[:, None] // n_r) == (np.arange(S5_TILE_GROUPS * n_c)[None, :] // n_c)
        return jnp.where(same, wide, 0.0)

    k_bd = block_diag(k_lag)
    h_bd = jnp.concatenate([block_diag(w_re), block_diag(w_im)], axis=-1)
    g_bd = jnp.concatenate([block_diag(ca_re), block_diag(-ca_im)], axis=2)
    a_pow = jnp.concatenate([pw_re[chunk].reshape(S5_TILES, 1, S5_STATE_LANES),
                             pw_im[chunk].reshape(S5_TILES, 1, S5_STATE_LANES)], axis=-1)
    return k_bd.astype(_BF16), h_bd, g_bd.astype(_BF16), a_pow


def _advance(a_pow, s, e):
    half = S5_STATE_LANES
    ar, ai, sr, si = a_pow[:, :half], a_pow[:, half:], s[:, :half], s[:, half:]
    return jnp.concatenate([ar * sr - ai * si + e[:, :half], ar * si + ai * sr + e[:, half:]], axis=1)


def _s5_kernel(u_ref, k_ref, h_ref, g_ref, a_ref, d_ref, s0_ref, y_ref, sfin_ref, upad, e_scr, sin_scr, *,
               chunk, carry):
    n_rows = u_ref.shape[0]
    n_chunks = n_rows // chunk
    u = u_ref[...]
    upad[:chunk] = jnp.zeros((chunk, LANES), _F32)
    upad[chunk:] = u
    step = lax.broadcasted_iota(jnp.int32, (n_rows, LANES), 0) % chunk
    y = d_ref[...] * u
    for lag in range(chunk):
        part = jnp.dot(upad[chunk - lag:chunk - lag + n_rows].astype(_BF16), k_ref[lag], preferred_element_type=_F32)
        y += part if lag == 0 else jnp.where(step >= lag, part, 0.0)
    y_ref[...] = y
    e = None
    for lag in range(chunk):
        rows = u_ref[pl.ds(chunk - 1 - lag, n_chunks, stride=chunk), :]
        if h_ref.dtype == _F32:
            d = jnp.dot(rows, h_ref[lag], preferred_element_type=_F32, precision=lax.Precision.HIGHEST)
        else:
            d = jnp.dot(rows.astype(_BF16), h_ref[lag], preferred_element_type=_F32)
        e = d if e is None else e + d
    a_pow = a_ref[...]
    if carry:
        e_scr[...] = e
        sub = 8

        def body(jj, s):
            rows = pl.ds(pl.multiple_of(jj * sub, sub), sub)
            e_tile = e_scr[rows, :]
            starts = []
            for r in range(sub):
                starts.append(s)
                s = _advance(a_pow, s, e_tile[r:r + 1])
            sin_scr[rows, :] = jnp.concatenate(starts, axis=0)
            return s
        sfin_ref[...] = lax.fori_loop(0, n_chunks // sub, body, s0_ref[...])
    else:
        sin_scr[...] = s0_ref[...]
        sfin_ref[...] = _advance(a_pow, s0_ref[...], e)
    s_in = sin_scr[...].astype(_BF16)
    for k in range(chunk):
        rows = pl.ds(k, n_chunks, stride=chunk)
        y_ref[rows, :] = y_ref[rows, :] + jnp.dot(s_in, g_ref[k], preferred_element_type=_F32)


def _s5_scan(u, s0, carry, a_re, a_im, log_dt, b_re, b_im, c_re, c_im, d_skip):
    n_b, n_rows, _ = u.shape
    chunk = 16 if carry else 8
    n_chunks = n_rows // chunk
    n_state = s0.shape[2]
    k_bd, h_bd, g_bd, a_pow = _s5_operators(a_re, a_im, log_dt, b_re, b_im, c_re, c_im, chunk)
    if carry:
        h_bd = h_bd.astype(_BF16)
    per_tile = lambda *shape: pl.BlockSpec((None,) + shape, lambda t, b: (t,) + (0,) * len(shape))
    rows_spec = pl.BlockSpec((None, n_rows, LANES), lambda t, b: (b, 0, t))
    state_spec = pl.BlockSpec((None, None, n_state, 2 * S5_STATE_LANES), lambda t, b: (b, t, 0, 0))
    return pl.pallas_call(
        functools.partial(_s5_kernel, chunk=chunk, carry=carry),
        grid=(S5_TILES, n_b),
        in_specs=[rows_spec, per_tile(chunk, LANES, LANES), per_tile(chunk, LANES, 2 * S5_STATE_LANES),
                  per_tile(chunk, 2 * S5_STATE_LANES, LANES), per_tile(1, 2 * S5_STATE_LANES), per_tile(1, LANES),
                  state_spec],
        out_specs=[rows_spec, state_spec],
        out_shape=[jax.ShapeDtypeStruct(u.shape, _F32), jax.ShapeDtypeStruct(s0.shape, _F32)],
        scratch_shapes=[pltpu.VMEM((n_rows + chunk, LANES), _F32), pltpu.VMEM((n_chunks, 2 * S5_STATE_LANES), _F32),
                        pltpu.VMEM((n_chunks, 2 * S5_STATE_LANES), _F32)],
        compiler_params=pltpu.CompilerParams(dimension_semantics=("parallel", "parallel"),
                                             vmem_limit_bytes=VMEM_LIMIT),
        name="s5_scan",
    )(u, k_bd, h_bd, g_bd, a_pow, d_skip.reshape(S5_TILES, 1, LANES), s0)


def _state_to_tiles(s_re, s_im):
    bsz = s_re.shape[0]
    return jnp.concatenate([s_re.reshape(bsz, S5_TILES, S5_STATE_LANES), s_im.reshape(bsz, S5_TILES, S5_STATE_LANES)],
                           axis=-1)


def _state_from_tiles(s):
    bsz = s.shape[0]
    return (s[..., :S5_STATE_LANES].reshape(bsz, SSM_GROUPS, SSM_STATE),
            s[..., S5_STATE_LANES:].reshape(bsz, SSM_GROUPS, SSM_STATE))


TQ = 256
TK = 128
NEAR_TILES = TQ // TK + 1
NEAR_BUCKET_DIST = 113


def _bucket_table(dist):
    import numpy as np
    max_exact = N_BUCKETS // 2
    d = np.maximum(np.asarray(dist, np.int64), 0)
    df = np.maximum(d, 1).astype(np.float64)
    large = max_exact + (np.log(df / max_exact) / math.log(MAX_DISTANCE / max_exact)
                         * (N_BUCKETS - max_exact)).astype(np.int64)
    return np.where(d < max_exact, d, np.minimum(large, N_BUCKETS - 1)).astype(np.int32)


def _bias_by_distance(rel_bias, dist):
    import numpy as np
    onehot = _bucket_table(dist)[None] == np.arange(N_BUCKETS)[:, None, None]
    return jnp.einsum('kqc,kh->hqc', jnp.asarray(onehot, _BF16).astype(_F32), rel_bias,
                      precision=lax.Precision.HIGHEST)


def _bias_tiles(rel_bias, n_cmp_pad):
    import numpy as np
    rel_bias = rel_bias - rel_bias[N_BUCKETS - 1]
    r = np.arange(TQ)[:, None]
    c = np.arange(n_cmp_pad)[None, :]
    d_cmp = np.where(c < 8 + TQ // CMP_STRIDE, r - (CMP_LEN - 1) + CMP_STRIDE * (8 - c), 10 ** 6)
    g_cmp = _bias_by_distance(rel_bias, d_cmp).reshape(N_KV_HEADS, Q_PER_KV * TQ, n_cmp_pad)
    k = np.arange(NEAR_TILES * TK)[None, :]
    d_near = TK + r - k
    near = _bias_by_distance(rel_bias, d_near) + jnp.asarray(np.where(d_near >= 0, 0.0, NEG_INF), _F32)
    near = near.reshape(N_KV_HEADS, Q_PER_KV * TQ, NEAR_TILES * TK)
    return g_cmp, near


def _stack_heads(q):
    return jnp.concatenate([q[:, g * HEAD_DIM:(g + 1) * HEAD_DIM] for g in range(Q_PER_KV)], axis=0)


def _unstack_heads(o):
    return jnp.concatenate([o[g] for g in range(Q_PER_KV)], axis=1)


def _nsa_cmp_kernel(q_ref, kct_ref, vc_ref, g_ref, ovt_ref, oc_ref, selt_ref):
    i = pl.program_id(2)
    t0 = i * TQ
    n_cmp_pad = kct_ref.shape[-1]
    qs = _stack_heads(q_ref[...] * (HEAD_DIM ** -0.5)).astype(_BF16)
    s = jnp.dot(qs, kct_ref[...], preferred_element_type=_F32)
    shift = (i * (TQ // CMP_STRIDE) + n_cmp_pad - 8) % n_cmp_pad
    bias = pltpu.roll(g_ref[...], shift, 1)
    s3 = (s + bias).reshape(Q_PER_KV, TQ, n_cmp_pad)
    r = lax.broadcasted_iota(jnp.int32, (TQ, n_cmp_pad), 0)
    n = lax.broadcasted_iota(jnp.int32, (TQ, n_cmp_pad), 1)
    valid = (t0 + r - CMP_STRIDE * n - (CMP_LEN - 1)) >= 0
    sm = jnp.where(valid, s3, NEG_INF)
    m = jnp.max(sm, axis=-1, keepdims=True)
    p = jnp.where(valid, jnp.exp(sm - m), 0.0)
    l = jnp.sum(p, axis=-1, keepdims=True)
    pb = (p * jnp.where(l > 0.0, 1.0 / l, 0.0)).astype(_BF16)
    oc = jnp.dot(pb.reshape(Q_PER_KV * TQ, n_cmp_pad), vc_ref[...], preferred_element_type=_F32)
    oc_ref[...] = _unstack_heads(oc.reshape(Q_PER_KV, TQ, HEAD_DIM))
    imp = jnp.zeros((ovt_ref.shape[0], TQ), _F32)
    for g in range(Q_PER_KV):
        imp += lax.dot_general(ovt_ref[...], pb[g], (((1,), (1,)), ((), ())), preferred_element_type=_F32)
    n_sel = imp.shape[0]
    blk = lax.broadcasted_iota(jnp.int32, (n_sel, TQ), 0)
    t = t0 + lax.broadcasted_iota(jnp.int32, (n_sel, TQ), 1)
    cur = t // SEL_BLOCK
    forced = (blk == 0) | (blk == cur) | (blk == cur - 1)
    score = jnp.where(forced, FORCE, jnp.where(blk * SEL_BLOCK <= t, imp, -FORCE))
    rank = jnp.zeros((n_sel, TQ), _F32)
    for sp in range(n_sel):
        row = score[sp:sp + 1, :]
        beats = (row > score) | ((row == score) & (blk > sp))
        rank += jnp.where(beats, 1.0, 0.0)
    selt_ref[...] = (rank < SEL_TOPK).astype(_F32)


ROW_BLOCK = 128


def _flash_step(qs, kt, v, add_rows, m_scr, acc_scr):
    n_chunks = kt.shape[-1] // LANES
    for r0 in range(0, TQ, ROW_BLOCK):
        for g in range(Q_PER_KV):
            s = jnp.dot(qs[g * TQ + r0:g * TQ + r0 + ROW_BLOCK], kt, preferred_element_type=_F32)
            add = add_rows(g, r0)
            if add is not None:
                s = s + add
            chunks = [s[:, k * LANES:(k + 1) * LANES] for k in range(n_chunks)]
            m_old = m_scr[g, r0:r0 + ROW_BLOCK]
            m_new = jnp.maximum(m_old, jnp.max(functools.reduce(jnp.maximum, chunks), axis=-1, keepdims=True))
            alpha = jnp.exp(m_old - m_new)
            m_scr[g, r0:r0 + ROW_BLOCK] = m_new
            pv = None
            for k, sk in enumerate(chunks):
                d = jnp.dot(jnp.exp(sk - m_new).astype(_BF16), v[k * LANES:(k + 1) * LANES],
                            preferred_element_type=_F32)
                pv = d if pv is None else pv + d
            acc_scr[g, r0:r0 + ROW_BLOCK] = alpha * acc_scr[g, r0:r0 + ROW_BLOCK] + pv


FAR_TILES = 4


def _nsa_sel_win_kernel(q_ref, oc_ref, glog_ref, selt_ref, kst_ref, vs_ref, kwt_ref, vw_ref, near_ref,
                        exp_ref, gexp_ref, o_ref, seladd, m_s, acc_s, m_w, acc_w):
    i = pl.program_id(2)
    qs = _stack_heads(q_ref[...] * (HEAD_DIM ** -0.5)).astype(_BF16)
    sel = jnp.dot(selt_ref[...].T.astype(_BF16), exp_ref[...], preferred_element_type=_F32)
    seladd[...] = (sel - 1.0) * (-NEG_INF)
    for m_scr, acc_scr in ((m_s, acc_s), (m_w, acc_w)):
        m_scr[...] = jnp.full(m_scr.shape, NEG_INF, _F32)
        acc_scr[...] = jnp.zeros(acc_scr.shape, _F32)

    def keys(j, n_tiles):
        return pl.ds(pl.multiple_of(j * TK, TK), n_tiles * TK)

    def sel_step(j, n_tiles, bias):
        cols = keys(j, n_tiles)

        def add_rows(g, r0):
            add = seladd[r0:r0 + ROW_BLOCK, cols]
            return add if bias is None else add + bias(g, r0)
        _flash_step(qs, kst_ref[:, cols], vs_ref[cols, :], add_rows, m_s, acc_s)

    def win_step(j, n_tiles, add_rows):
        cols = keys(j, n_tiles)
        _flash_step(qs, kwt_ref[:, cols], vw_ref[cols, :], add_rows, m_w, acc_w)

    near = lambda lo: (lambda g, r0: near_ref[g * TQ + r0:g * TQ + r0 + ROW_BLOCK, lo * TK:])
    no_add = lambda g, r0: None
    own = TQ // TK
    first = i * own

    n_far = jnp.maximum(first - 1, 0)

    def far_many(j, carry):
        sel_step(j * FAR_TILES, FAR_TILES, None)
        return carry

    lax.fori_loop(0, n_far // FAR_TILES, far_many, 0)
    for rest in range(1, FAR_TILES):
        @pl.when(n_far % FAR_TILES == rest)
        def _():
            sel_step((n_far // FAR_TILES) * FAR_TILES, rest, None)

    @pl.when(i >= 1)
    def _():
        sel_step(first - 1, NEAR_TILES, near(0))
        win_step(first - 1, NEAR_TILES, near(0))

    @pl.when(i == 0)
    def _():
        sel_step(0, own, near(1))
        win_step(0, own, near(1))

    n_back = WINDOW // TK

    def back_step(start_tile, n_tiles, offset):
        masks = {}

        def add_rows(g, r0):
            if offset >= r0 + ROW_BLOCK - 1:
                return None
            if r0 not in masks:
                r = r0 + lax.broadcasted_iota(jnp.int32, (ROW_BLOCK, n_tiles * TK), 0)
                kk = offset + lax.broadcasted_iota(jnp.int32, (ROW_BLOCK, n_tiles * TK), 1)
                masks[r0] = jnp.where(kk >= r, 0.0, NEG_INF)
            return masks[r0]
        win_step(start_tile, n_tiles, add_rows)

    @pl.when(first >= n_back)
    def _():
        back_step(first - n_back, n_back - 1, 0)

    for i_small in range(1, -(-n_back // own)):
        if i_small * own > 1:
            @pl.when(i == i_small)
            def _():
                back_step(0, i_small * own - 1, (n_back - i_small * own) * TK)

    def finish(acc_scr):
        acc = acc_scr[...]
        return _unstack_heads(acc[..., :HEAD_DIM] / acc[..., HEAD_DIM:])

    gates = jnp.dot(jax.nn.sigmoid(glog_ref[...]), gexp_ref[...], preferred_element_type=_F32,
                    precision=lax.Precision.HIGHEST)
    width = Q_PER_KV * HEAD_DIM
    o_ref[...] = (gates[:, :width] * oc_ref[...] + gates[:, width:2 * width] * finish(acc_s)
                  + gates[:, 2 * width:] * finish(acc_w))


def _nsa_prompt_side(kv, keys_t, vals, prm):
    import numpy as np
    bsz, t_len = kv.shape[:2]
    n_cmp = (t_len - CMP_LEN) // CMP_STRIDE + 1
    n_cmp_pad = t_len // CMP_STRIDE
    assert n_cmp == n_cmp_pad - 1 and n_cmp_pad % LANES == 0 and t_len % TQ == 0
    n_sel = t_len // SEL_BLOCK
    grouped = lambda slot: kv[:, :, slot].reshape(bsz, n_cmp_pad, CMP_STRIDE * N_KV_HEADS * HEAD_DIM)
    one_page = jnp.arange(bsz, dtype=jnp.int32)[:, None]
    k_cmp = _cmp_pages(grouped(0), one_page, prm['cmp_pos_k'], prm['cmp_w1_k'], prm['cmp_w2_k'])
    v_cmp = _cmp_pages(grouped(1), one_page, prm['cmp_pos_v'], prm['cmp_w1_v'], prm['cmp_w2_v'])
    t_last = lambda a: jnp.transpose(a, (0, 2, 3, 1))
    t_rows = lambda a: jnp.transpose(a, (0, 2, 1, 3))
    pad_c = lambda a: a.reshape(bsz, n_cmp_pad, N_KV_HEADS, HEAD_DIM)
    rel_bias = prm['rel_bias']
    c_start = np.arange(n_cmp_pad)[None, :] * CMP_STRIDE
    s_start = np.arange(n_sel)[:, None] * SEL_BLOCK
    ovt = ((c_start < s_start + SEL_BLOCK) & (c_start + CMP_LEN > s_start) & (np.arange(n_cmp_pad)[None, :] < n_cmp))
    expand = np.arange(n_sel)[:, None] == (np.arange(t_len)[None, :] // SEL_BLOCK)
    lanes = np.arange(LANES)[:, None]
    cols = np.arange(N_BRANCH * Q_PER_KV * HEAD_DIM)[None, :]
    gexp = lanes == (cols // HEAD_DIM % Q_PER_KV) * N_BRANCH + cols // (Q_PER_KV * HEAD_DIM)
    g_cmp, near = _bias_tiles(rel_bias, n_cmp_pad)
    return {
        'kct': t_last(pad_c(k_cmp)), 'vc': t_rows(pad_c(v_cmp)),
        'keys_t': keys_t, 'vals': vals.reshape(bsz, t_len, vals.shape[-1]),
        'ovt': jnp.asarray(ovt, _BF16), 'expand': jnp.asarray(expand, _BF16), 'gexp': jnp.asarray(gexp, _F32),
        'g_cmp': g_cmp, 'near': near,
    }


def _nsa_prompt_attention(q, glog, side):
    bsz, t_len, _ = q.shape
    n_t = t_len // TQ
    width = Q_PER_KV * HEAD_DIM
    n_cmp_pad = side['kct'].shape[-1]
    n_sel = side['ovt'].shape[0]
    grid = (bsz, N_KV_HEADS, n_t)
    params = pltpu.CompilerParams(dimension_semantics=("parallel", "parallel", "arbitrary"),
                                  vmem_limit_bytes=VMEM_LIMIT)
    q_spec = pl.BlockSpec((None, TQ, width), lambda b, h, i: (b, i, h))
    per_bh = lambda *shape: pl.BlockSpec((None, None) + shape, lambda b, h, i: (b, h, 0, 0))
    per_h = lambda *shape: pl.BlockSpec((None,) + shape, lambda b, h, i: (h,) + (0,) * len(shape))
    const = lambda *shape: pl.BlockSpec(shape, lambda b, h, i: (0,) * len(shape))
    selt_spec = pl.BlockSpec((None, None, n_sel, TQ), lambda b, h, i: (b, h, 0, i))
    o_c, selt = pl.pallas_call(
        _nsa_cmp_kernel,
        grid=grid,
        in_specs=[q_spec, per_bh(HEAD_DIM, n_cmp_pad), per_bh(n_cmp_pad, HEAD_DIM),
                  per_h(Q_PER_KV * TQ, n_cmp_pad), const(n_sel, n_cmp_pad)],
        out_specs=[q_spec, selt_spec],
        out_shape=[jax.ShapeDtypeStruct((bsz, t_len, N_HEADS * HEAD_DIM), _F32),
                   jax.ShapeDtypeStruct((bsz, N_KV_HEADS, n_sel, t_len), _F32)],
        compiler_params=params,
        name="nsa_cmp_select",
    )(q, side['kct'], side['vc'], side['g_cmp'], side['ovt'])
    stat = pltpu.VMEM((Q_PER_KV, TQ, 2 * HEAD_DIM), _F32)
    keys_spec = lambda branch: pl.BlockSpec((HEAD_DIM, t_len), lambda b, h, i: (branch * N_KV_HEADS + h, b))
    vals_spec = lambda branch: pl.BlockSpec((None, t_len, 2 * HEAD_DIM),
                                            lambda b, h, i: (b, 0, branch * N_KV_HEADS + h))
    return pl.pallas_call(
        _nsa_sel_win_kernel,
        grid=grid,
        in_specs=[q_spec, q_spec, pl.BlockSpec((None, TQ, LANES), lambda b, h, i: (b, i, h)), selt_spec,
                  keys_spec(0), vals_spec(0), keys_spec(1), vals_spec(1),
                  per_h(Q_PER_KV * TQ, NEAR_TILES * TK), const(n_sel, t_len),
                  const(LANES, N_BRANCH * width)],
        out_specs=q_spec,
        out_shape=jax.ShapeDtypeStruct((bsz, t_len, N_HEADS * HEAD_DIM), _F32),
        scratch_shapes=[pltpu.VMEM((TQ, t_len), _F32), stat, stat, stat, stat],
        compiler_params=params,
        name="nsa_select_window",
    )(q, o_c, glog, selt, side['keys_t'], side['vals'], side['keys_t'], side['vals'], side['near'],
      side['expand'], side['gexp'])


PAGES_PER_STEP = 16


def _dot_nt(a, b):
    return lax.dot_general(a, b, (((1,), (1,)), ((), ())), preferred_element_type=_F32)


def _cmp_pages_kernel(pt_ref, *refs, n_pages):
    del pt_ref
    pages = refs[:n_pages]
    posa_ref, posb_ref, w1a_ref, w1b_ref, w2_ref, o_ref = refs[n_pages:]
    x = jnp.concatenate([pg[...] for pg in pages], axis=0)
    z0 = jnp.dot((x + posa_ref[...]).astype(_BF16), w1a_ref[...], preferred_element_type=_F32)
    z1 = jnp.dot((x + posb_ref[...]).astype(_BF16), w1b_ref[...], preferred_element_type=_F32)
    pre = z0 + pltpu.roll(z1, z1.shape[0] - 1, 0)
    o_ref[...] = jnp.dot(jax.nn.gelu(pre).astype(_BF16), w2_ref[...], preferred_element_type=_F32).astype(o_ref.dtype)


def _cmp_pages(grouped, page_table, pos_emb, w1, w2):
    n_seq, n_pages = page_table.shape
    groups_per_page, flat = grouped.shape[1:]
    width = N_KV_HEADS * HEAD_DIM
    hidden = w1.shape[1]
    eye = jnp.eye(N_KV_HEADS, dtype=_F32)
    w1r = w1.reshape(2, CMP_STRIDE, HEAD_DIM, hidden)
    bd1 = jnp.einsum('hk,abdf->abhdkf', eye, w1r).reshape(2, flat, N_KV_HEADS * hidden).astype(_BF16)
    bd2 = jnp.einsum('hk,fd->hfkd', eye, w2).reshape(N_KV_HEADS * hidden, width).astype(_BF16)
    pos_t = jnp.broadcast_to(pos_emb.reshape(2, CMP_STRIDE, 1, HEAD_DIM),
                             (2, CMP_STRIDE, N_KV_HEADS, HEAD_DIM)).reshape(2, 1, flat)
    n_rows = n_pages * groups_per_page
    page_spec = lambda k: pl.BlockSpec((None, groups_per_page, flat), lambda b, pt: (pt[b, k], 0, 0))
    const = lambda *shape: pl.BlockSpec(shape, lambda b, pt: (0,) * len(shape))
    return pl.pallas_call(
        functools.partial(_cmp_pages_kernel, n_pages=n_pages),
        grid_spec=pltpu.PrefetchScalarGridSpec(
            num_scalar_prefetch=1,
            grid=(n_seq,),
            in_specs=[page_spec(k) for k in range(n_pages)] + [
                const(1, flat), const(1, flat), const(flat, N_KV_HEADS * hidden), const(flat, N_KV_HEADS * hidden),
                const(N_KV_HEADS * hidden, width)],
            out_specs=pl.BlockSpec((None, n_rows, width), lambda b, pt: (b, 0, 0)),
        ),
        out_shape=jax.ShapeDtypeStruct((n_seq, n_rows, width), _BF16),
        compiler_params=pltpu.CompilerParams(dimension_semantics=("parallel",), vmem_limit_bytes=VMEM_LIMIT),
        name="cmp_pages",
    )(page_table, *([grouped] * n_pages), pos_t[0], pos_t[1], bd1[0], bd1[1], bd2)


def _cmp_pages_t_kernel(pt_ref, *refs, n_pages):
    del pt_ref
    pages = refs[:n_pages]
    posa_ref, posb_ref, w1a_ref, w1b_ref, w2_ref, o_ref, rows_scr = refs[n_pages:]
    page_rows = pages[0].shape[1]
    n_groups = n_pages * page_rows // CMP_STRIDE
    for k, pg in enumerate(pages):
        x = pg[...].T
        for half in range(rows_scr.shape[0]):
            rows_scr[half, k * page_rows:(k + 1) * page_rows, :] = x[:, half * LANES:(half + 1) * LANES]
    z0 = z1 = None
    for b in range(CMP_STRIDE):
        xb = jnp.concatenate([rows_scr[half, pl.ds(b, n_groups, stride=CMP_STRIDE), :]
                              for half in range(rows_scr.shape[0])], axis=1)
        d0 = jnp.dot((xb + posa_ref[b]).astype(_BF16), w1a_ref[b], preferred_element_type=_F32)
        d1 = jnp.dot((xb + posb_ref[b]).astype(_BF16), w1b_ref[b], preferred_element_type=_F32)
        z0 = d0 if z0 is None else z0 + d0
        z1 = d1 if z1 is None else z1 + d1
    pre = z0 + pltpu.roll(z1, n_groups - 1, 0)
    o_ref[...] = jnp.dot(jax.nn.gelu(pre).astype(_BF16), w2_ref[...], preferred_element_type=_F32).astype(o_ref.dtype)


def _cmp_pages_t(pages_t, slot, page_table, pos_emb, w1, w2):
    n_seq, n_pages = page_table.shape
    page_rows = pages_t.shape[2]
    width = N_KV_HEADS * HEAD_DIM
    hidden = w1.shape[1]
    eye = jnp.eye(N_KV_HEADS, dtype=_F32)
    w1r = w1.reshape(2, CMP_STRIDE, HEAD_DIM, hidden)
    bd1 = jnp.einsum('hk,abdf->abhdkf', eye, w1r).reshape(2, CMP_STRIDE, width, N_KV_HEADS * hidden).astype(_BF16)
    bd2 = jnp.einsum('hk,fd->hfkd', eye, w2).reshape(N_KV_HEADS * hidden, width).astype(_BF16)
    pos_t = jnp.broadcast_to(pos_emb.reshape(2, CMP_STRIDE, 1, 1, HEAD_DIM),
                             (2, CMP_STRIDE, 1, N_KV_HEADS, HEAD_DIM)).reshape(2, CMP_STRIDE, 1, width)
    n_rows = n_pages * page_rows // CMP_STRIDE
    page_spec = lambda k: pl.BlockSpec((None, width, page_rows), lambda b, pt: (pt[b, k], slot, 0))
    const = lambda *shape: pl.BlockSpec(shape, lambda b, pt: (0,) * len(shape))
    return pl.pallas_call(
        functools.partial(_cmp_pages_t_kernel, n_pages=n_pages),
        grid_spec=pltpu.PrefetchScalarGridSpec(
            num_scalar_prefetch=1,
            grid=(n_seq,),
            in_specs=[page_spec(k) for k in range(n_pages)] + [
                const(CMP_STRIDE, 1, width), const(CMP_STRIDE, 1, width),
                const(CMP_STRIDE, width, N_KV_HEADS * hidden), const(CMP_STRIDE, width, N_KV_HEADS * hidden),
                const(N_KV_HEADS * hidden, width)],
            out_specs=pl.BlockSpec((None, n_rows, width), lambda b, pt: (b, 0, 0)),
            scratch_shapes=[pltpu.VMEM((width // LANES, n_pages * page_rows, LANES), _F32)],
        ),
        out_shape=jax.ShapeDtypeStruct((n_seq, n_rows, width), _BF16),
        compiler_params=pltpu.CompilerParams(dimension_semantics=("parallel",), vmem_limit_bytes=VMEM_LIMIT),
        name="cmp_pages_t",
    )(page_table, *([pages_t] * n_pages), pos_t[0], pos_t[1], bd1[0], bd1[1], bd2)


def _pick_own_kv_head(x, n_q):
    rows_per = Q_PER_KV * n_q
    return jnp.concatenate([x[h * rows_per:(h + 1) * rows_per, h * HEAD_DIM:(h + 1) * HEAD_DIM]
                            for h in range(N_KV_HEADS)], axis=0)


def _nsa_sample_kernel(pt_ref, q_ref, gl_ref, kc_ref, vc_ref, win_ref, new_ref, *rest, n_q, past_len, n_sel):
    del pt_ref
    pages = rest[:PAGES_PER_STEP]
    (tcmp_ref, tsel_ref, twin_ref, exp_ref, ov_ref, o_ref,
     addsel, m_scr, l_scr, acc_scr, oc_scr, ow_scr) = rest[PAGES_PER_STEP:]
    j = pl.program_id(1)
    rows = N_HEADS * n_q
    width = N_KV_HEADS * HEAD_DIM
    w_buf = win_ref.shape[1]
    step_keys = PAGES_PER_STEP * pages[0].shape[1]
    rr = lax.broadcasted_iota(jnp.int32, (rows, width), 0)
    cc = lax.broadcasted_iota(jnp.int32, (rows, width), 1)
    q4 = jnp.concatenate([q_ref[...] * (HEAD_DIM ** -0.5)] * N_KV_HEADS, axis=1)
    qbd = jnp.where(rr // (Q_PER_KV * n_q) == cc // HEAD_DIM, q4, 0.0).astype(_BF16)
    new_rows = jnp.concatenate([new_ref[...], jnp.zeros((LANES - n_q, new_ref.shape[1]), _F32)], axis=0).astype(_BF16)

    @pl.when(j == 0)
    def _first():
        s = _dot_nt(qbd, kc_ref[...]) + tcmp_ref[...]
        p = jnp.exp(s - jnp.max(s, axis=-1, keepdims=True))
        pb = (p / jnp.sum(p, axis=-1, keepdims=True)).astype(_BF16)
        oc_scr[...] = _pick_own_kv_head(jnp.dot(pb, vc_ref[...], preferred_element_type=_F32), n_q)
        pm = jnp.dot(pb, ov_ref[...], preferred_element_type=_F32)
        imp = jnp.concatenate(
            [sum(pm[(h * Q_PER_KV + g) * n_q:(h * Q_PER_KV + g + 1) * n_q] for g in range(Q_PER_KV))
             for h in range(N_KV_HEADS)], axis=0)
        blk = lax.broadcasted_iota(jnp.int32, imp.shape, 1)
        t = past_len + lax.broadcasted_iota(jnp.int32, imp.shape, 0) % n_q
        cur = t // SEL_BLOCK
        forced = (blk == 0) | (blk == cur) | (blk == cur - 1)
        score = jnp.where(forced, FORCE, jnp.where(blk * SEL_BLOCK <= t, imp, -FORCE))
        score = jnp.where(blk < n_sel, score, -jnp.inf)
        rank = jnp.zeros(imp.shape, jnp.int32)
        for sp in range(n_sel):
            col = score[:, sp:sp + 1]
            rank += ((col > score) | ((col == score) & (blk > sp))).astype(jnp.int32)
        sel = (rank < min(SEL_TOPK, n_sel)).astype(_BF16)
        sel_rows = jnp.concatenate([sel[h * n_q:(h + 1) * n_q] for h in range(N_KV_HEADS) for _ in range(Q_PER_KV)],
                                   axis=0)
        addsel[...] = (jnp.dot(sel_rows, exp_ref[...], preferred_element_type=_F32) - 1.0) * (-NEG_INF) + tsel_ref[...]
        kw_t = win_ref[:width, :].astype(_BF16)
        vw_t = win_ref[width:, :].astype(_BF16)
        s1 = jnp.dot(qbd, kw_t, preferred_element_type=_F32) + twin_ref[:, :w_buf]
        s2 = _dot_nt(qbd, new_rows[:, 2 * width:3 * width]) + twin_ref[:, w_buf:]
        m = jnp.maximum(jnp.max(s1, axis=-1, keepdims=True), jnp.max(s2, axis=-1, keepdims=True))
        p1 = jnp.exp(s1 - m)
        p2 = jnp.exp(s2 - m)
        l = jnp.sum(p1, axis=-1, keepdims=True) + jnp.sum(p2, axis=-1, keepdims=True)
        ow = (_dot_nt(p1.astype(_BF16), vw_t)
              + jnp.dot(p2.astype(_BF16), new_rows[:, 3 * width:], preferred_element_type=_F32))
        ow_scr[...] = _pick_own_kv_head(ow / l, n_q)
        m_scr[...] = jnp.full(m_scr.shape, NEG_INF, _F32)
        l_scr[...] = jnp.zeros(l_scr.shape, _F32)
        acc_scr[...] = jnp.zeros(acc_scr.shape, _F32)

    def flash(s, pv):
        m_old = m_scr[...]
        m_new = jnp.maximum(m_old, jnp.max(s, axis=-1, keepdims=True))
        alpha = jnp.exp(m_old - m_new)
        p = jnp.exp(s - m_new)
        m_scr[...] = m_new
        l_scr[...] = alpha * l_scr[...] + jnp.sum(p, axis=-1, keepdims=True)
        acc_scr[...] = alpha * acc_scr[...] + pv(p.astype(_BF16))

    k_t = jnp.concatenate([pg[:width, :] for pg in pages], axis=1).astype(_BF16)
    v_t = jnp.concatenate([pg[width:, :] for pg in pages], axis=1).astype(_BF16)
    flash(jnp.dot(qbd, k_t, preferred_element_type=_F32)
          + addsel[:, pl.ds(pl.multiple_of(j * step_keys, step_keys), step_keys)], lambda p: _dot_nt(p, v_t))

    @pl.when(j == pl.num_programs(1) - 1)
    def _last():
        flash(_dot_nt(qbd, new_rows[:, :width]) + addsel[:, past_len:],
              lambda p: jnp.dot(p, new_rows[:, width:2 * width], preferred_element_type=_F32))
        o_s = _pick_own_kv_head(acc_scr[...] / l_scr[...], n_q)
        gates = jax.nn.sigmoid(gl_ref[...])
        o_ref[...] = gates[:, 0:1] * oc_scr[...] + gates[:, 1:2] * o_s + gates[:, 2:3] * ow_scr[...]


def _nsa_sample_side(cache_kv, page_table, cache_win, kv_new, rel_bias, prm):
    import numpy as np
    n_seq, n_pages = page_table.shape
    n_phys, page_rows = cache_kv.shape[:2]
    n_q = kv_new.shape[1]
    width = N_KV_HEADS * HEAD_DIM
    past_len = n_pages * page_rows
    w_buf = cache_win.shape[1]
    tk = past_len + n_q
    n_cmp = (tk - CMP_LEN) // CMP_STRIDE + 1
    n_cmp_pad = past_len // CMP_STRIDE
    n_sel = -(-tk // SEL_BLOCK)
    assert n_cmp == n_cmp_pad - 1 and n_cmp_pad == LANES and n_sel <= LANES and n_q <= LANES
    assert page_rows % SEL_BLOCK == 0 and n_pages % PAGES_PER_STEP == 0 and w_buf == min(WINDOW, past_len)
    cache_t = jnp.transpose(cache_kv, (0, 2, 3, 4, 1)).reshape(n_phys, KV_SLOTS_PAGED * width, page_rows)
    win_t = jnp.transpose(cache_win, (0, 2, 3, 4, 1)).reshape(n_seq, KV_SLOTS_WIN * width, w_buf)
    kc = _cmp_pages_t(cache_t, 0, page_table, prm['cmp_pos_k'], prm['cmp_w1_k'], prm['cmp_w2_k'])
    vc = _cmp_pages_t(cache_t, 1, page_table, prm['cmp_pos_v'], prm['cmp_w1_v'], prm['cmp_w2_v'])
    rows = N_HEADS * n_q
    t = past_len + np.arange(n_q)[:, None]
    masked = lambda ok, d: jnp.where(jnp.asarray(ok), _bias_by_distance(rel_bias, d), NEG_INF).reshape(rows, -1)
    n = np.arange(n_cmp_pad)[None, :]
    d_c = t - (n * CMP_STRIDE + CMP_LEN - 1)
    t_cmp = masked((d_c >= 0) & (n < n_cmp), d_c)
    pos = np.arange(past_len + LANES)[None, :]
    t_sel = masked((t - pos >= 0) & (pos < tk), t - pos)
    l = np.arange(w_buf + LANES)[None, :]
    d_w = t - (past_len - w_buf + l)
    t_win = masked((d_w >= 0) & (d_w <= WINDOW) & (l < w_buf + n_q), d_w)
    expand = (np.arange(LANES)[:, None] == pos // SEL_BLOCK) & (np.arange(LANES)[:, None] < n_sel)
    c_start = np.arange(n_cmp_pad)[:, None] * CMP_STRIDE
    s_start = np.arange(LANES)[None, :] * SEL_BLOCK
    ov = ((c_start < s_start + SEL_BLOCK) & (c_start + CMP_LEN > s_start) & (np.arange(n_cmp_pad)[:, None] < n_cmp)
          & (np.arange(LANES)[None, :] < n_sel))
    return {
        'cache_t': cache_t, 'page_table': page_table, 'kc': kc, 'vc': vc, 'win_t': win_t,
        'new': kv_new.reshape(n_seq, n_q, -1)[:, :, 2 * width:],
        't_cmp': t_cmp, 't_sel': t_sel, 't_win': t_win,
        'expand': jnp.asarray(expand, _BF16), 'ov': jnp.asarray(ov, _BF16), 'n_sel': n_sel, 'past_len': past_len,
    }


def _nsa_sample_attention(q, glog, side):
    n_seq, n_q, _ = q.shape
    glog = glog.reshape(n_seq, n_q, N_KV_HEADS, LANES)[..., :Q_PER_KV * N_BRANCH]
    rows = N_HEADS * n_q
    width = N_KV_HEADS * HEAD_DIM
    page_table = side['page_table']
    n_pages = page_table.shape[1]
    page_rows = side['cache_t'].shape[2]
    past_len = side['past_len']
    w_buf = side['win_t'].shape[2]
    n_steps = n_pages // PAGES_PER_STEP
    q_rows = jnp.transpose(q.reshape(n_seq, n_q, N_HEADS, HEAD_DIM), (0, 2, 1, 3)).reshape(n_seq, rows, HEAD_DIM)
    gl = jnp.transpose(glog.reshape(n_seq, n_q, N_HEADS, N_BRANCH), (0, 2, 1, 3)).reshape(n_seq, rows, N_BRANCH)
    gl = jnp.pad(gl, ((0, 0), (0, 0), (0, LANES - N_BRANCH)))
    per_seq = lambda *shape: pl.BlockSpec((None,) + shape, lambda b, j, pt: (b,) + (0,) * len(shape))
    const = lambda *shape: pl.BlockSpec(shape, lambda b, j, pt: (0,) * len(shape))
    page_spec = lambda k: pl.BlockSpec((None, 2 * width, page_rows),
                                       lambda b, j, pt: (pt[b, j * PAGES_PER_STEP + k], 1, 0))
    out = pl.pallas_call(
        functools.partial(_nsa_sample_kernel, n_q=n_q, past_len=past_len, n_sel=side['n_sel']),
        grid_spec=pltpu.PrefetchScalarGridSpec(
            num_scalar_prefetch=1,
            grid=(n_seq, n_steps),
            in_specs=[per_seq(rows, HEAD_DIM), per_seq(rows, LANES), per_seq(LANES, width), per_seq(LANES, width),
                      per_seq(2 * width, w_buf), per_seq(n_q, 4 * width)]
            + [page_spec(k) for k in range(PAGES_PER_STEP)]
            + [const(rows, LANES), const(rows, past_len + LANES), const(rows, w_buf + LANES),
               const(LANES, past_len + LANES), const(LANES, LANES)],
            out_specs=per_seq(rows, HEAD_DIM),
            scratch_shapes=[pltpu.VMEM((rows, past_len + LANES), _F32), pltpu.VMEM((rows, 1), _F32),
                            pltpu.VMEM((rows, 1), _F32), pltpu.VMEM((rows, width), _F32),
                            pltpu.VMEM((rows, HEAD_DIM), _F32), pltpu.VMEM((rows, HEAD_DIM), _F32)],
        ),
        out_shape=jax.ShapeDtypeStruct((n_seq, rows, HEAD_DIM), _F32),
        compiler_params=pltpu.CompilerParams(dimension_semantics=("parallel", "arbitrary"),
                                             vmem_limit_bytes=VMEM_LIMIT),
        name="nsa_sample",
    )(page_table, q_rows, gl, side['kc'], side['vc'], side['win_t'], side['new'],
      *([side['cache_t']] * PAGES_PER_STEP), side['t_cmp'], side['t_sel'], side['t_win'], side['expand'], side['ov'])
    return jnp.transpose(out.reshape(n_seq, N_HEADS, n_q, HEAD_DIM), (0, 2, 1, 3)).reshape(n_seq, n_q, N_HEADS * HEAD_DIM)


def _run_trunk(x, p, s0_re, s0_im, paged, win_buf, prm):
    bsz, t_len, _ = x.shape
    ssm_re, ssm_im = [], []
    side = None
    kv_rows_new = None
    win_state = None
    y = None
    x2 = x.reshape(-1, D_MODEL)
    as_seq = lambda a: a.reshape(bsz, t_len, a.shape[-1])
    for i in range(DEPTH):
        if i < N_A_LAYERS:
            u2, = _rows_call(lambda xt, gain: (_norm(xt, gain),), [x2], [prm['g_mix'][i].reshape(1, -1)],
                             [D_MODEL], "s5_norm")
            ssm = (prm['ssm_a_re'][i], prm['ssm_a_im'][i], prm['ssm_log_dt'][i], prm['ssm_b_re'][i],
                   prm['ssm_b_im'][i], prm['ssm_c_re'][i], prm['ssm_c_im'][i], prm['ssm_d'][i])
            s0 = _state_to_tiles(s0_re[i], s0_im[i])
            if t_len == 8:
                ys, s_fin = _s5_scan(u2[None], jnp.transpose(s0, (1, 0, 2))[None], False, *ssm)
                s_re, s_im = _state_from_tiles(jnp.transpose(s_fin[0], (1, 0, 2)))
            else:
                assert t_len % (16 * 8) == 0
                ys, s_fin = _s5_scan(as_seq(u2), s0[:, :, None, :], True, *ssm)
                s_re, s_im = _state_from_tiles(s_fin[:, :, 0, :])
            ssm_re.append(s_re)
            ssm_im.append(s_im)
            x2 = _glu_residual(x2, ys.reshape(-1, D_MODEL), prm['w_glu'][i])
        else:
            j = i - N_A_LAYERS
            q, glog = _qg_project(x2, prm['g_mix'][i], prm['w_qg'][j])
            attend = _nsa_prompt_attention if paged is None else _nsa_sample_attention
            mixed = attend(as_seq(q), as_seq(glog), side)
            x2 = _wo_residual(x2, mixed.reshape(-1, D_MODEL), prm['w_o'][j])
        x2 = _moe_layer(x2, prm['g_ffn'][i], prm['w_route_group'][i], prm['b_route_group'][i],
                        prm['w_route_expert'][i], prm['b_route_expert'][i], prm['w_exp_up'], prm['w_exp_down'], i)
        ple = (x2, p[i].reshape(-1, p.shape[-1]), prm['g_ple'][i], prm['w_ple_gate'][i], prm['w_ple_proj'][i])
        if i == N_A_LAYERS - 1:
            x2, kv2, *operands = _ple_residual(*ple, g_next=prm['g_kv'], w_next=prm['w_kv'],
                                               attention_operands=paged is None)
            kv = kv2.reshape(bsz, t_len, KV_SLOTS_PAGED + KV_SLOTS_WIN, N_KV_HEADS, HEAD_DIM)
            kv_rows_new, win_new = kv[:, :, :KV_SLOTS_PAGED], kv[:, :, KV_SLOTS_PAGED:]
            if paged is None:
                win_state = win_new[:, -min(WINDOW, t_len):]
                side = _nsa_prompt_side(kv, *operands, prm)
            else:
                w_buf = win_buf.shape[1]
                win_state = jnp.concatenate([win_buf, win_new], axis=1)[:, -w_buf:]
                side = _nsa_sample_side(paged[0], paged[1], win_buf, kv, prm['rel_bias'], prm)
        elif i == DEPTH - 1:
            x2, y = _ple_residual(*ple, g_next=prm['g_final'])
        else:
            x2, = _ple_residual(*ple)
    return as_seq(y), kv_rows_new, win_state, jnp.stack(ssm_re), jnp.stack(ssm_im)


def kernel(x_prompt, x_sample, p_prompt, p_sample, cache_kv, cache_win, state_ssm_re, state_ssm_im, page_table,
           g_mix, g_ffn, g_ple, g_kv, g_final,
           ssm_a_re, ssm_a_im, ssm_log_dt, ssm_b_re, ssm_b_im, ssm_c_re, ssm_c_im, ssm_d, w_glu,
           w_kv, cmp_pos_k, cmp_pos_v, cmp_w1_k, cmp_w2_k, cmp_w1_v, cmp_w2_v, w_qg, w_o, rel_bias,
           w_route_group, b_route_group, w_route_expert, b_route_expert, w_exp_up, w_exp_down,
           w_ple_proj, w_ple_gate):
    prm = {
        'g_mix': g_mix, 'g_ffn': g_ffn, 'g_ple': g_ple, 'g_kv': g_kv, 'g_final': g_final,
        'ssm_a_re': ssm_a_re, 'ssm_a_im': ssm_a_im, 'ssm_log_dt': ssm_log_dt,
        'ssm_b_re': ssm_b_re, 'ssm_b_im': ssm_b_im, 'ssm_c_re': ssm_c_re, 'ssm_c_im': ssm_c_im,
        'ssm_d': ssm_d, 'w_glu': w_glu,
        'w_kv': w_kv, 'cmp_pos_k': cmp_pos_k, 'cmp_pos_v': cmp_pos_v,
        'cmp_w1_k': cmp_w1_k, 'cmp_w2_k': cmp_w2_k, 'cmp_w1_v': cmp_w1_v, 'cmp_w2_v': cmp_w2_v,
        'w_qg': w_qg, 'w_o': w_o, 'rel_bias': rel_bias,
        'w_route_group': w_route_group, 'b_route_group': b_route_group,
        'w_route_expert': w_route_expert, 'b_route_expert': b_route_expert,
        'w_exp_up': w_exp_up, 'w_exp_down': w_exp_down,
        'w_ple_proj': w_ple_proj, 'w_ple_gate': w_ple_gate,
    }
    zero_state = jnp.zeros((N_A_LAYERS, x_prompt.shape[0], SSM_GROUPS, SSM_STATE), _F32)
    y_prompt, kv_prompt, win_prompt, ssm_re_prompt, ssm_im_prompt = _run_trunk(
        x_prompt, p_prompt, zero_state, zero_state, None, None, prm)
    y_sample, kv_sample, win_sample, ssm_re_sample, ssm_im_sample = _run_trunk(
        x_sample, p_sample, state_ssm_re, state_ssm_im, (cache_kv, page_table), cache_win, prm)
    return (y_prompt, y_sample, kv_prompt, win_prompt, ssm_re_prompt, ssm_im_prompt,
            kv_sample, win_sample, ssm_re_sample, ssm_im_sample)
```

```python
import functools
import math

import jax
import jax.numpy as jnp
import numpy as np
from jax import lax
from jax.experimental import pallas as pl
from jax.experimental.pallas import tpu as pltpu

D_MODEL = 1024
DEPTH = 4
N_A_LAYERS = DEPTH // 2
SSM_GROUP = 16
SSM_GROUPS = D_MODEL // SSM_GROUP
SSM_STATE = 64
N_HEADS = 16
HEAD_DIM = D_MODEL // N_HEADS
N_KV_HEADS = 4
Q_PER_KV = N_HEADS // N_KV_HEADS
CMP_LEN = 32
CMP_STRIDE = 16
SEL_BLOCK = 64
SEL_TOPK = 16
WINDOW = 512
N_BRANCH = 3
KV_SLOTS_PAGED = 4
KV_SLOTS_WIN = 2
Q_BLOCK = 64
N_BUCKETS = 32
MAX_DISTANCE = 128
N_EXPERT_GROUPS = 4
EXPERTS_PER_GROUP = 4
N_EXPERTS = N_EXPERT_GROUPS * EXPERTS_PER_GROUP
EXPERT_TOPK = 2
D_EXPERT = 256
RMS_EPS = 1e-6
NEG_INF = -1e30
FORCE = 1e4

LANES = 128
SUBLANES = 8
MOE_ROW_TILE = 1024
S5_CHUNK = 16
VMEM_LIMIT = 48 * 1024 * 1024

_F32 = jnp.float32
_BF16 = jnp.bfloat16


def _moe_kernel(x_ref, g_ref, wr_ref, br_ref, wup_ref, wdn_ref, o_ref, h_scr, comb_scr, acc_scr):
    e = pl.program_id(1)
    lane = lax.broadcasted_iota(jnp.int32, comb_scr.shape, 1)

    @pl.when(e == 0)
    def _route():
        x = x_ref[...]
        h = x * lax.rsqrt(jnp.mean(x * x, axis=-1, keepdims=True) + RMS_EPS) * g_ref[...]
        h_scr[...] = h.astype(_BF16)
        logits = jnp.dot(h_scr[...], wr_ref[...].astype(_BF16), preferred_element_type=_F32) + br_ref[...]
        is_grp = (lane >= N_EXPERTS) & (lane < N_EXPERTS + N_EXPERT_GROUPS)
        lg = jnp.where(is_grp, logits, -jnp.inf)
        gmax = jnp.max(lg, axis=-1, keepdims=True)
        gi = jnp.min(jnp.where(lg == gmax, lane, LANES), axis=-1, keepdims=True) - N_EXPERTS
        gp = 1.0 / jnp.sum(jnp.where(is_grp, jnp.exp(lg - gmax), 0.0), axis=-1, keepdims=True)
        in_grp = (lane < N_EXPERTS) & ((lane // EXPERTS_PER_GROUP) == gi)
        le = jnp.where(in_grp, logits, -jnp.inf)
        m1 = jnp.max(le, axis=-1, keepdims=True)
        i1 = jnp.min(jnp.where(le == m1, lane, LANES), axis=-1, keepdims=True)
        le2 = jnp.where(lane == i1, -jnp.inf, le)
        m2 = jnp.max(le2, axis=-1, keepdims=True)
        i2 = jnp.min(jnp.where(le2 == m2, lane, LANES), axis=-1, keepdims=True)
        e2 = jnp.exp(m2 - m1)
        den = 1.0 + e2
        comb_scr[...] = jnp.where(lane == i1, gp / den, jnp.where(lane == i2, gp * e2 / den, 0.0))
        acc_scr[...] = x

    up = jnp.dot(h_scr[...], wup_ref[0].astype(_BF16), preferred_element_type=_F32)
    a = up[:, :D_EXPERT]
    b = up[:, D_EXPERT:]
    c = jnp.sum(jnp.where(lane == e, comb_scr[...], 0.0), axis=-1, keepdims=True)
    act = (a * jax.nn.sigmoid(a)) * b * c
    acc_scr[...] += jnp.dot(act.astype(_BF16), wdn_ref[0].astype(_BF16), preferred_element_type=_F32)

    @pl.when(e == N_EXPERTS - 1)
    def _store():
        o_ref[...] = acc_scr[...]


def _moe_layer(x2, g, w_rg, b_rg, w_re, b_re, w_up, w_dn, layer):
    n_tok, d = x2.shape
    tm = min(n_tok, MOE_ROW_TILE)
    wr = jnp.zeros((d, LANES), _F32).at[:, :N_EXPERTS].set(w_re).at[:, N_EXPERTS:N_EXPERTS + N_EXPERT_GROUPS].set(w_rg)
    br = jnp.zeros((1, LANES), _F32).at[0, :N_EXPERTS].set(b_re).at[0, N_EXPERTS:N_EXPERTS + N_EXPERT_GROUPS].set(b_rg)
    return pl.pallas_call(
        _moe_kernel,
        grid=(n_tok // tm, N_EXPERTS),
        in_specs=[
            pl.BlockSpec((tm, d), lambda i, e: (i, 0)),
            pl.BlockSpec((1, d), lambda i, e: (0, 0)),
            pl.BlockSpec((d, LANES), lambda i, e: (0, 0)),
            pl.BlockSpec((1, LANES), lambda i, e: (0, 0)),
            pl.BlockSpec((None, 1, d, 2 * D_EXPERT), lambda i, e: (layer, e, 0, 0)),
            pl.BlockSpec((None, 1, D_EXPERT, d), lambda i, e: (layer, e, 0, 0)),
        ],
        out_specs=pl.BlockSpec((tm, d), lambda i, e: (i, 0)),
        out_shape=jax.ShapeDtypeStruct((n_tok, d), _F32),
        scratch_shapes=[
            pltpu.VMEM((tm, d), _BF16),
            pltpu.VMEM((tm, LANES), _F32),
            pltpu.VMEM((tm, d), _F32),
        ],
        compiler_params=pltpu.CompilerParams(
            dimension_semantics=("parallel", "arbitrary"), vmem_limit_bytes=VMEM_LIMIT),
        name="hmoe",
    )(x2, g.reshape(1, d), wr, br, w_up, w_dn)


ROW_TILE = 512


def _rows_kernel(*refs, body, n_in):
    outs = body(*[r[...] for r in refs[:n_in]])
    for o_ref, o in zip(refs[n_in:], outs):
        o_ref[...] = o


def _rows_call(body, rows, consts, outs, name):
    n_rows = rows[0].shape[0]
    tm = min(n_rows, ROW_TILE)
    row_spec = lambda width: pl.BlockSpec((tm, width), lambda i: (i, 0))
    const_spec = lambda c: pl.BlockSpec(c.shape, lambda i: (0,) * c.ndim)
    out_specs, out_shape = [], []
    for o in outs:
        if isinstance(o, int):
            o = (o, _F32)
        if o[0] == 'T':
            out_specs.append(pl.BlockSpec((o[1], tm), lambda i: (0, i)))
            out_shape.append(jax.ShapeDtypeStruct((o[1], n_rows), o[2]))
        else:
            out_specs.append(row_spec(o[0]))
            out_shape.append(jax.ShapeDtypeStruct((n_rows, o[0]), o[1]))
    return pl.pallas_call(
        functools.partial(_rows_kernel, body=body, n_in=len(rows) + len(consts)),
        grid=(n_rows // tm,),
        in_specs=[row_spec(r.shape[1]) for r in rows] + [const_spec(c) for c in consts],
        out_specs=out_specs,
        out_shape=out_shape,
        compiler_params=pltpu.CompilerParams(dimension_semantics=("parallel",), vmem_limit_bytes=VMEM_LIMIT),
        name=name,
    )(*rows, *consts)


def _norm(x, g):
    return x * lax.rsqrt(jnp.mean(x * x, axis=-1, keepdims=True) + RMS_EPS) * g


def _mm(x, w):
    return jnp.dot(x.astype(_BF16), w, preferred_element_type=_F32)


def _glu_residual(x2, y2, w_glu):
    def body(x, y, w):
        ag = _mm(jax.nn.gelu(y), w)
        return (x + ag[:, :D_MODEL] * jax.nn.sigmoid(ag[:, D_MODEL:]),)
    return _rows_call(body, [x2, y2], [w_glu.astype(_BF16)], [D_MODEL], "glu_residual")[0]


def _qg_project(x2, g, w_qg):
    per_kv = Q_PER_KV * N_BRANCH
    w_g = w_qg[:, N_HEADS * HEAD_DIM:].reshape(D_MODEL, N_KV_HEADS, per_kv)
    w_g = jnp.pad(w_g, ((0, 0), (0, 0), (0, LANES - per_kv))).reshape(D_MODEL, N_KV_HEADS * LANES)
    w = jnp.concatenate([w_qg[:, :N_HEADS * HEAD_DIM], w_g], axis=1).astype(_BF16)

    def body(x, gain, w):
        qg = _mm(_norm(x, gain), w)
        return qg[:, :N_HEADS * HEAD_DIM], qg[:, N_HEADS * HEAD_DIM:]
    return _rows_call(body, [x2], [g.reshape(1, -1), w], [N_HEADS * HEAD_DIM, N_KV_HEADS * LANES], "qg_project")


def _wo_residual(x2, mixed2, w_o):
    body = lambda x, m, w: (x + _mm(m, w),)
    return _rows_call(body, [x2, mixed2], [w_o.astype(_BF16)], [D_MODEL], "wo_residual")[0]


def _ple_residual(x2, p2, g_ple, w_gate, w_proj, g_next=None, w_next=None, attention_operands=False):
    consts = [g_ple.reshape(1, -1), w_gate.astype(_BF16), w_proj.astype(_BF16)]
    outs = [D_MODEL]
    width = N_KV_HEADS * HEAD_DIM
    if g_next is not None:
        consts.append(g_next.reshape(1, -1))
        outs.append(D_MODEL if w_next is None else w_next.shape[1])
    if w_next is not None:
        consts.append(w_next.astype(_BF16))
    if attention_operands:
        outs += [('T', 2 * width, _BF16), (4 * width, _BF16)]

    def body(x, p, gain, wg, wp, *nxt):
        out = x + _mm(p, wp) * jax.nn.sigmoid(_mm(_norm(x, gain), wg))
        if not nxt:
            return (out,)
        normed = _norm(out, nxt[0])
        if len(nxt) == 1:
            return out, normed
        kv = _mm(normed, nxt[1])
        if not attention_operands:
            return out, kv
        slot = lambda s: kv[:, s * width:(s + 1) * width]
        keys_t = jnp.concatenate([slot(2), slot(4)], axis=1).T.astype(_BF16)
        ones = jnp.ones((kv.shape[0], HEAD_DIM), _F32)
        vals = jnp.concatenate([piece for s in (3, 5) for h in range(N_KV_HEADS)
                                for piece in (slot(s)[:, h * HEAD_DIM:(h + 1) * HEAD_DIM], ones)], axis=1)
        return out, kv, keys_t, vals.astype(_BF16)
    return _rows_call(body, [x2, p2], consts, outs, "ple_residual")


def _cmul(ar, ai, br, bi):
    return ar * br - ai * bi, ar * bi + ai * br


S5_TILE_GROUPS = LANES // SSM_GROUP
S5_TILES = D_MODEL // LANES
S5_STATE_LANES = S5_TILE_GROUPS * SSM_STATE


def _s5_operators(a_re, a_im, log_dt, b_re, b_im, c_re, c_im, chunk):
    hp = lax.Precision.HIGHEST
    dt = jnp.exp(log_dt)[:, None]
    decay = jnp.exp(a_re * dt)
    ab_re, ab_im = decay * jnp.cos(a_im * dt), decay * jnp.sin(a_im * dt)
    den = a_re * a_re + a_im * a_im
    num_re = ab_re - 1.0
    q_re = (num_re * a_re + ab_im * a_im) / den
    q_im = (ab_im * a_re - num_re * a_im) / den
    bb_re, bb_im = _cmul(q_re[..., None], q_im[..., None], b_re, b_im)
    pw_re, pw_im = [jnp.ones_like(ab_re)], [jnp.zeros_like(ab_im)]
    for _ in range(chunk):
        nr, ni = _cmul(pw_re[-1], pw_im[-1], ab_re, ab_im)
        pw_re.append(nr)
        pw_im.append(ni)
    pw_re, pw_im = jnp.stack(pw_re), jnp.stack(pw_im)
    w_re, w_im = _cmul(pw_re[:chunk, :, :, None], pw_im[:chunk, :, :, None], bb_re[None], bb_im[None])
    k_lag = (jnp.einsum('gcp,jgpd->jgcd', c_re, w_re, precision=hp)
             - jnp.einsum('gcp,jgpd->jgcd', c_im, w_im, precision=hp))
    ca_re, ca_im = _cmul(c_re[None], c_im[None], pw_re[1:, :, None, :], pw_im[1:, :, None, :])

    def block_diag(per_group):
        a = per_group.reshape((chunk, S5_TILES, S5_TILE_GROUPS) + per_group.shape[2:])
        a = jnp.transpose(a, (1, 0, 2, 4, 3))
        n_r, n_c = a.shape[3:]
        wide = jnp.tile(a.reshape(S5_TILES, chunk, S5_TILE_GROUPS * n_r, n_c), (1, 1, 1, S5_TILE_GROUPS))
        same = (np.arange(S5_TILE_GROUPS * n_r)[:, None] // n_r) == (np.arange(S5_TILE_GROUPS * n_c)[None, :] // n_c)
        return jnp.where(same, wide, 0.0)

    k_bd = block_diag(k_lag)
    h_bd = jnp.concatenate([block_diag(w_re), block_diag(w_im)], axis=-1)
    g_bd = jnp.concatenate([block_diag(ca_re), block_diag(-ca_im)], axis=2)
    a_pow = jnp.concatenate([pw_re[chunk].reshape(S5_TILES, 1, S5_STATE_LANES),
                             pw_im[chunk].reshape(S5_TILES, 1, S5_STATE_LANES)], axis=-1)
    return k_bd.astype(_BF16), h_bd, g_bd.astype(_BF16), a_pow


def _advance(a_pow, s, e):
    half = S5_STATE_LANES
    ar, ai, sr, si = a_pow[:, :half], a_pow[:, half:], s[:, :half], s[:, half:]
    return jnp.concatenate([ar * sr - ai * si + e[:, :half], ar * si + ai * sr + e[:, half:]], axis=1)


def _s5_kernel(u_ref, k_ref, h_ref, g_ref, a_ref, d_ref, s0_ref, y_ref, sfin_ref, upad, e_scr, sin_scr, *,
               chunk, carry):
    n_rows = u_ref.shape[0]
    n_chunks = n_rows // chunk
    u = u_ref[...]
    upad[:chunk] = jnp.zeros((chunk, LANES), _F32)
    upad[chunk:] = u
    step = lax.broadcasted_iota(jnp.int32, (n_rows, LANES), 0) % chunk
    y = d_ref[...] * u
    for lag in range(chunk):
        part = jnp.dot(upad[chunk - lag:chunk - lag + n_rows].astype(_BF16), k_ref[lag], preferred_element_type=_F32)
        y += part if lag == 0 else jnp.where(step >= lag, part, 0.0)
    y_ref[...] = y
    e = None
    for lag in range(chunk):
        rows = u_ref[pl.ds(chunk - 1 - lag, n_chunks, stride=chunk), :]
        if h_ref.dtype == _F32:
            d = jnp.dot(rows, h_ref[lag], preferred_element_type=_F32, precision=lax.Precision.HIGHEST)
        else:
            d = jnp.dot(rows.astype(_BF16), h_ref[lag], preferred_element_type=_F32)
        e = d if e is None else e + d
    a_pow = a_ref[...]
    if carry:
        e_scr[...] = e
        sub = SUBLANES

        def body(jj, s):
            rows = pl.ds(pl.multiple_of(jj * sub, sub), sub)
            e_tile = e_scr[rows, :]
            starts = []
            for r in range(sub):
                starts.append(s)
                s = _advance(a_pow, s, e_tile[r:r + 1])
            sin_scr[rows, :] = jnp.concatenate(starts, axis=0)
            return s
        sfin_ref[...] = lax.fori_loop(0, n_chunks // sub, body, s0_ref[...])
    else:
        sin_scr[...] = s0_ref[...]
        sfin_ref[...] = _advance(a_pow, s0_ref[...], e)
    s_in = sin_scr[...].astype(_BF16)
    for k in range(chunk):
        rows = pl.ds(k, n_chunks, stride=chunk)
        y_ref[rows, :] = y_ref[rows, :] + jnp.dot(s_in, g_ref[k], preferred_element_type=_F32)


def _s5_scan(u, s0, carry, a_re, a_im, log_dt, b_re, b_im, c_re, c_im, d_skip):
    n_b, n_rows, _ = u.shape
    chunk = S5_CHUNK if carry else n_rows // s0.shape[2]
    n_chunks = n_rows // chunk
    n_state = s0.shape[2]
    k_bd, h_bd, g_bd, a_pow = _s5_operators(a_re, a_im, log_dt, b_re, b_im, c_re, c_im, chunk)
    if carry:
        h_bd = h_bd.astype(_BF16)
    per_tile = lambda *shape: pl.BlockSpec((None,) + shape, lambda t, b: (t,) + (0,) * len(shape))
    rows_spec = pl.BlockSpec((None, n_rows, LANES), lambda t, b: (b, 0, t))
    state_spec = pl.BlockSpec((None, None, n_state, 2 * S5_STATE_LANES), lambda t, b: (b, t, 0, 0))
    return pl.pallas_call(
        functools.partial(_s5_kernel, chunk=chunk, carry=carry),
        grid=(S5_TILES, n_b),
        in_specs=[rows_spec, per_tile(chunk, LANES, LANES), per_tile(chunk, LANES, 2 * S5_STATE_LANES),
                  per_tile(chunk, 2 * S5_STATE_LANES, LANES), per_tile(1, 2 * S5_STATE_LANES), per_tile(1, LANES),
                  state_spec],
        out_specs=[rows_spec, state_spec],
        out_shape=[jax.ShapeDtypeStruct(u.shape, _F32), jax.ShapeDtypeStruct(s0.shape, _F32)],
        scratch_shapes=[pltpu.VMEM((n_rows + chunk, LANES), _F32), pltpu.VMEM((n_chunks, 2 * S5_STATE_LANES), _F32),
                        pltpu.VMEM((n_chunks, 2 * S5_STATE_LANES), _F32)],
        compiler_params=pltpu.CompilerParams(dimension_semantics=("parallel", "parallel"),
                                             vmem_limit_bytes=VMEM_LIMIT),
        name="s5_scan",
    )(u, k_bd, h_bd, g_bd, a_pow, d_skip.reshape(S5_TILES, 1, LANES), s0)


def _state_to_tiles(s_re, s_im):
    bsz = s_re.shape[0]
    return jnp.concatenate([s_re.reshape(bsz, S5_TILES, S5_STATE_LANES), s_im.reshape(bsz, S5_TILES, S5_STATE_LANES)],
                           axis=-1)


def _state_from_tiles(s):
    bsz = s.shape[0]
    return (s[..., :S5_STATE_LANES].reshape(bsz, SSM_GROUPS, SSM_STATE),
            s[..., S5_STATE_LANES:].reshape(bsz, SSM_GROUPS, SSM_STATE))


TQ = 256
TK = 128
NEAR_TILES = TQ // TK + 1
NEAR_BUCKET_DIST = 113
NEAR_CMP_BLOCKS = -(-NEAR_BUCKET_DIST // CMP_STRIDE)


def _bucket_table(dist):
    max_exact = N_BUCKETS // 2
    d = np.maximum(np.asarray(dist, np.int64), 0)
    df = np.maximum(d, 1).astype(np.float64)
    large = max_exact + (np.log(df / max_exact) / math.log(MAX_DISTANCE / max_exact)
                         * (N_BUCKETS - max_exact)).astype(np.int64)
    return np.where(d < max_exact, d, np.minimum(large, N_BUCKETS - 1)).astype(np.int32)


def _bias_by_distance(rel_bias, dist):
    onehot = _bucket_table(dist)[None] == np.arange(N_BUCKETS)[:, None, None]
    return jnp.einsum('kqc,kh->hqc', jnp.asarray(onehot, _BF16).astype(_F32), rel_bias,
                      precision=lax.Precision.HIGHEST)


def _bias_tiles(rel_bias, n_cmp_pad):
    rel_bias = rel_bias - rel_bias[N_BUCKETS - 1]
    r = np.arange(TQ)[:, None]
    c = np.arange(n_cmp_pad)[None, :]
    d_cmp = np.where(c < NEAR_CMP_BLOCKS + TQ // CMP_STRIDE,
                     r - (CMP_LEN - 1) + CMP_STRIDE * (NEAR_CMP_BLOCKS - c), 10 ** 6)
    g_cmp = _bias_by_distance(rel_bias, d_cmp).reshape(N_KV_HEADS, Q_PER_KV * TQ, n_cmp_pad)
    k = np.arange(NEAR_TILES * TK)[None, :]
    d_near = TK + r - k
    near = _bias_by_distance(rel_bias, d_near) + jnp.asarray(np.where(d_near >= 0, 0.0, NEG_INF), _F32)
    near = near.reshape(N_KV_HEADS, Q_PER_KV * TQ, NEAR_TILES * TK)
    return g_cmp, near


def _stack_heads(q):
    return jnp.concatenate([q[:, g * HEAD_DIM:(g + 1) * HEAD_DIM] for g in range(Q_PER_KV)], axis=0)


def _unstack_heads(o):
    return jnp.concatenate([o[g] for g in range(Q_PER_KV)], axis=1)


def _nsa_cmp_kernel(q_ref, kct_ref, vc_ref, g_ref, ovt_ref, oc_ref, selt_ref):
    i = pl.program_id(2)
    t0 = i * TQ
    n_cmp_pad = kct_ref.shape[-1]
    qs = _stack_heads(q_ref[...] * (HEAD_DIM ** -0.5)).astype(_BF16)
    s = jnp.dot(qs, kct_ref[...], preferred_element_type=_F32)
    shift = (i * (TQ // CMP_STRIDE) + n_cmp_pad - NEAR_CMP_BLOCKS) % n_cmp_pad
    bias = pltpu.roll(g_ref[...], shift, 1)
    s3 = (s + bias).reshape(Q_PER_KV, TQ, n_cmp_pad)
    r = lax.broadcasted_iota(jnp.int32, (TQ, n_cmp_pad), 0)
    n = lax.broadcasted_iota(jnp.int32, (TQ, n_cmp_pad), 1)
    valid = (t0 + r - CMP_STRIDE * n - (CMP_LEN - 1)) >= 0
    sm = jnp.where(valid, s3, NEG_INF)
    m = jnp.max(sm, axis=-1, keepdims=True)
    p = jnp.where(valid, jnp.exp(sm - m), 0.0)
    l = jnp.sum(p, axis=-1, keepdims=True)
    pb = (p * jnp.where(l > 0.0, 1.0 / l, 0.0)).astype(_BF16)
    oc = jnp.dot(pb.reshape(Q_PER_KV * TQ, n_cmp_pad), vc_ref[...], preferred_element_type=_F32)
    oc_ref[...] = _unstack_heads(oc.reshape(Q_PER_KV, TQ, HEAD_DIM))
    imp = jnp.zeros((ovt_ref.shape[0], TQ), _F32)
    for g in range(Q_PER_KV):
        imp += lax.dot_general(ovt_ref[...], pb[g], (((1,), (1,)), ((), ())), preferred_element_type=_F32)
    n_sel = imp.shape[0]
    blk = lax.broadcasted_iota(jnp.int32, (n_sel, TQ), 0)
    t = t0 + lax.broadcasted_iota(jnp.int32, (n_sel, TQ), 1)
    cur = t // SEL_BLOCK
    forced = (blk == 0) | (blk == cur) | (blk == cur - 1)
    score = jnp.where(forced, FORCE, jnp.where(blk * SEL_BLOCK <= t, imp, -FORCE))
    rank = jnp.zeros((n_sel, TQ), _F32)
    for sp in range(n_sel):
        row = score[sp:sp + 1, :]
        beats = (row > score) | ((row == score) & (blk > sp))
        rank += jnp.where(beats, 1.0, 0.0)
    selt_ref[...] = (rank < SEL_TOPK).astype(_F32)


ROW_BLOCK = 128


def _flash_step(qs, kt, v, add_rows, m_scr, acc_scr):
    n_chunks = kt.shape[-1] // LANES
    for r0 in range(0, TQ, ROW_BLOCK):
        for g in range(Q_PER_KV):
            s = jnp.dot(qs[g * TQ + r0:g * TQ + r0 + ROW_BLOCK], kt, preferred_element_type=_F32)
            add = add_rows(g, r0)
            if add is not None:
                s = s + add
            chunks = [s[:, k * LANES:(k + 1) * LANES] for k in range(n_chunks)]
            m_old = m_scr[g, r0:r0 + ROW_BLOCK]
            m_new = jnp.maximum(m_old, jnp.max(functools.reduce(jnp.maximum, chunks), axis=-1, keepdims=True))
            alpha = jnp.exp(m_old - m_new)
            m_scr[g, r0:r0 + ROW_BLOCK] = m_new
            pv = None
            for k, sk in enumerate(chunks):
                d = jnp.dot(jnp.exp(sk - m_new).astype(_BF16), v[k * LANES:(k + 1) * LANES],
                            preferred_element_type=_F32)
                pv = d if pv is None else pv + d
            acc_scr[g, r0:r0 + ROW_BLOCK] = alpha * acc_scr[g, r0:r0 + ROW_BLOCK] + pv


FAR_TILES = 4


def _nsa_sel_win_kernel(q_ref, oc_ref, glog_ref, selt_ref, kst_ref, vs_ref, kwt_ref, vw_ref, near_ref,
                        exp_ref, gexp_ref, o_ref, seladd, m_s, acc_s, m_w, acc_w):
    i = pl.program_id(2)
    qs = _stack_heads(q_ref[...] * (HEAD_DIM ** -0.5)).astype(_BF16)
    sel = jnp.dot(selt_ref[...].T.astype(_BF16), exp_ref[...], preferred_element_type=_F32)
    seladd[...] = (sel - 1.0) * (-NEG_INF)
    for m_scr, acc_scr in ((m_s, acc_s), (m_w, acc_w)):
        m_scr[...] = jnp.full(m_scr.shape, NEG_INF, _F32)
        acc_scr[...] = jnp.zeros(acc_scr.shape, _F32)

    def keys(j, n_tiles):
        return pl.ds(pl.multiple_of(j * TK, TK), n_tiles * TK)

    def sel_step(j, n_tiles, bias):
        cols = keys(j, n_tiles)

        def add_rows(g, r0):
            add = seladd[r0:r0 + ROW_BLOCK, cols]
            return add if bias is None else add + bias(g, r0)
        _flash_step(qs, kst_ref[:, cols], vs_ref[cols, :], add_rows, m_s, acc_s)

    def win_step(j, n_tiles, add_rows):
        cols = keys(j, n_tiles)
        _flash_step(qs, kwt_ref[:, cols], vw_ref[cols, :], add_rows, m_w, acc_w)

    near = lambda lo: (lambda g, r0: near_ref[g * TQ + r0:g * TQ + r0 + ROW_BLOCK, lo * TK:])
    no_add = lambda g, r0: None
    own = TQ // TK
    first = i * own

    n_far = jnp.maximum(first - 1, 0)

    def far_many(j, carry):
        sel_step(j * FAR_TILES, FAR_TILES, None)
        return carry

    lax.fori_loop(0, n_far // FAR_TILES, far_many, 0)
    for rest in range(1, FAR_TILES):
        @pl.when(n_far % FAR_TILES == rest)
        def _():
            sel_step((n_far // FAR_TILES) * FAR_TILES, rest, None)

    @pl.when(i >= 1)
    def _():
        sel_step(first - 1, NEAR_TILES, near(0))
        win_step(first - 1, NEAR_TILES, near(0))

    @pl.when(i == 0)
    def _():
        sel_step(0, own, near(1))
        win_step(0, own, near(1))

    n_back = WINDOW // TK

    def back_step(start_tile, n_tiles, offset):
        masks = {}

        def add_rows(g, r0):
            if offset >= r0 + ROW_BLOCK - 1:
                return None
            if r0 not in masks:
                r = r0 + lax.broadcasted_iota(jnp.int32, (ROW_BLOCK, n_tiles * TK), 0)
                kk = offset + lax.broadcasted_iota(jnp.int32, (ROW_BLOCK, n_tiles * TK), 1)
                masks[r0] = jnp.where(kk >= r, 0.0, NEG_INF)
            return masks[r0]
        win_step(start_tile, n_tiles, add_rows)

    @pl.when(first >= n_back)
    def _():
        back_step(first - n_back, n_back - 1, 0)

    for i_small in range(1, -(-n_back // own)):
        if i_small * own > 1:
            @pl.when(i == i_small)
            def _():
                back_step(0, i_small * own - 1, (n_back - i_small * own) * TK)

    def finish(acc_scr):
        acc = acc_scr[...]
        return _unstack_heads(acc[..., :HEAD_DIM] / acc[..., HEAD_DIM:])

    gates = jnp.dot(jax.nn.sigmoid(glog_ref[...]), gexp_ref[...], preferred_element_type=_F32,
                    precision=lax.Precision.HIGHEST)
    width = Q_PER_KV * HEAD_DIM
    o_ref[...] = (gates[:, :width] * oc_ref[...] + gates[:, width:2 * width] * finish(acc_s)
                  + gates[:, 2 * width:] * finish(acc_w))


def _nsa_prompt_side(kv, keys_t, vals, prm):
    bsz, t_len = kv.shape[:2]
    n_cmp = (t_len - CMP_LEN) // CMP_STRIDE + 1
    n_cmp_pad = t_len // CMP_STRIDE
    assert n_cmp == n_cmp_pad - 1 and n_cmp_pad % LANES == 0 and t_len % TQ == 0
    n_sel = t_len // SEL_BLOCK
    grouped = lambda slot: kv[:, :, slot].reshape(bsz, n_cmp_pad, CMP_STRIDE * N_KV_HEADS * HEAD_DIM)
    one_page = jnp.arange(bsz, dtype=jnp.int32)[:, None]
    k_cmp = _cmp_pages(grouped(0), one_page, prm['cmp_pos_k'], prm['cmp_w1_k'], prm['cmp_w2_k'])
    v_cmp = _cmp_pages(grouped(1), one_page, prm['cmp_pos_v'], prm['cmp_w1_v'], prm['cmp_w2_v'])
    t_last = lambda a: jnp.transpose(a, (0, 2, 3, 1))
    t_rows = lambda a: jnp.transpose(a, (0, 2, 1, 3))
    pad_c = lambda a: a.reshape(bsz, n_cmp_pad, N_KV_HEADS, HEAD_DIM)
    rel_bias = prm['rel_bias']
    c_start = np.arange(n_cmp_pad)[None, :] * CMP_STRIDE
    s_start = np.arange(n_sel)[:, None] * SEL_BLOCK
    ovt = ((c_start < s_start + SEL_BLOCK) & (c_start + CMP_LEN > s_start) & (np.arange(n_cmp_pad)[None, :] < n_cmp))
    expand = np.arange(n_sel)[:, None] == (np.arange(t_len)[None, :] // SEL_BLOCK)
    lanes = np.arange(LANES)[:, None]
    cols = np.arange(N_BRANCH * Q_PER_KV * HEAD_DIM)[None, :]
    gexp = lanes == (cols // HEAD_DIM % Q_PER_KV) * N_BRANCH + cols // (Q_PER_KV * HEAD_DIM)
    g_cmp, near = _bias_tiles(rel_bias, n_cmp_pad)
    return {
        'kct': t_last(pad_c(k_cmp)), 'vc': t_rows(pad_c(v_cmp)),
        'keys_t': keys_t, 'vals': vals.reshape(bsz, t_len, vals.shape[-1]),
        'ovt': jnp.asarray(ovt, _BF16), 'expand': jnp.asarray(expand, _BF16), 'gexp': jnp.asarray(gexp, _F32),
        'g_cmp': g_cmp, 'near': near,
    }


def _nsa_prompt_attention(q, glog, side):
    bsz, t_len, _ = q.shape
    n_t = t_len // TQ
    width = Q_PER_KV * HEAD_DIM
    n_cmp_pad = side['kct'].shape[-1]
    n_sel = side['ovt'].shape[0]
    grid = (bsz, N_KV_HEADS, n_t)
    params = pltpu.CompilerParams(dimension_semantics=("parallel", "parallel", "arbitrary"),
                                  vmem_limit_bytes=VMEM_LIMIT)
    q_spec = pl.BlockSpec((None, TQ, width), lambda b, h, i: (b, i, h))
    per_bh = lambda *shape: pl.BlockSpec((None, None) + shape, lambda b, h, i: (b, h, 0, 0))
    per_h = lambda *shape: pl.BlockSpec((None,) + shape, lambda b, h, i: (h,) + (0,) * len(shape))
    const = lambda *shape: pl.BlockSpec(shape, lambda b, h, i: (0,) * len(shape))
    selt_spec = pl.BlockSpec((None, None, n_sel, TQ), lambda b, h, i: (b, h, 0, i))
    o_c, selt = pl.pallas_call(
        _nsa_cmp_kernel,
        grid=grid,
        in_specs=[q_spec, per_bh(HEAD_DIM, n_cmp_pad), per_bh(n_cmp_pad, HEAD_DIM),
                  per_h(Q_PER_KV * TQ, n_cmp_pad), const(n_sel, n_cmp_pad)],
        out_specs=[q_spec, selt_spec],
        out_shape=[jax.ShapeDtypeStruct((bsz, t_len, N_HEADS * HEAD_DIM), _F32),
                   jax.ShapeDtypeStruct((bsz, N_KV_HEADS, n_sel, t_len), _F32)],
        compiler_params=params,
        name="nsa_cmp_select",
    )(q, side['kct'], side['vc'], side['g_cmp'], side['ovt'])
    stat = pltpu.VMEM((Q_PER_KV, TQ, 2 * HEAD_DIM), _F32)
    keys_spec = lambda branch: pl.BlockSpec((HEAD_DIM, t_len), lambda b, h, i: (branch * N_KV_HEADS + h, b))
    vals_spec = lambda branch: pl.BlockSpec((None, t_len, 2 * HEAD_DIM),
                                            lambda b, h, i: (b, 0, branch * N_KV_HEADS + h))
    return pl.pallas_call(
        _nsa_sel_win_kernel,
        grid=grid,
        in_specs=[q_spec, q_spec, pl.BlockSpec((None, TQ, LANES), lambda b, h, i: (b, i, h)), selt_spec,
                  keys_spec(0), vals_spec(0), keys_spec(1), vals_spec(1),
                  per_h(Q_PER_KV * TQ, NEAR_TILES * TK), const(n_sel, t_len),
                  const(LANES, N_BRANCH * width)],
        out_specs=q_spec,
        out_shape=jax.ShapeDtypeStruct((bsz, t_len, N_HEADS * HEAD_DIM), _F32),
        scratch_shapes=[pltpu.VMEM((TQ, t_len), _F32), stat, stat, stat, stat],
        compiler_params=params,
        name="nsa_select_window",
    )(q, o_c, glog, selt, side['keys_t'], side['vals'], side['keys_t'], side['vals'], side['near'],
      side['expand'], side['gexp'])


PAGES_PER_STEP = 16


def _dot_nt(a, b):
    return lax.dot_general(a, b, (((1,), (1,)), ((), ())), preferred_element_type=_F32)


def _cmp_pages_kernel(pt_ref, *refs, n_pages):
    del pt_ref
    pages = refs[:n_pages]
    posa_ref, posb_ref, w1a_ref, w1b_ref, w2_ref, o_ref = refs[n_pages:]
    x = jnp.concatenate([pg[...] for pg in pages], axis=0)
    z0 = jnp.dot((x + posa_ref[...]).astype(_BF16), w1a_ref[...], preferred_element_type=_F32)
    z1 = jnp.dot((x + posb_ref[...]).astype(_BF16), w1b_ref[...], preferred_element_type=_F32)
    pre = z0 + pltpu.roll(z1, z1.shape[0] - 1, 0)
    o_ref[...] = jnp.dot(jax.nn.gelu(pre).astype(_BF16), w2_ref[...], preferred_element_type=_F32).astype(o_ref.dtype)


def _cmp_pages(grouped, page_table, pos_emb, w1, w2):
    n_seq, n_pages = page_table.shape
    groups_per_page, flat = grouped.shape[1:]
    width = N_KV_HEADS * HEAD_DIM
    hidden = w1.shape[1]
    eye = jnp.eye(N_KV_HEADS, dtype=_F32)
    w1r = w1.reshape(2, CMP_STRIDE, HEAD_DIM, hidden)
    bd1 = jnp.einsum('hk,abdf->abhdkf', eye, w1r).reshape(2, flat, N_KV_HEADS * hidden).astype(_BF16)
    bd2 = jnp.einsum('hk,fd->hfkd', eye, w2).reshape(N_KV_HEADS * hidden, width).astype(_BF16)
    pos_t = jnp.broadcast_to(pos_emb.reshape(2, CMP_STRIDE, 1, HEAD_DIM),
                             (2, CMP_STRIDE, N_KV_HEADS, HEAD_DIM)).reshape(2, 1, flat)
    n_rows = n_pages * groups_per_page
    page_spec = lambda k: pl.BlockSpec((None, groups_per_page, flat), lambda b, pt: (pt[b, k], 0, 0))
    const = lambda *shape: pl.BlockSpec(shape, lambda b, pt: (0,) * len(shape))
    return pl.pallas_call(
        functools.partial(_cmp_pages_kernel, n_pages=n_pages),
        grid_spec=pltpu.PrefetchScalarGridSpec(
            num_scalar_prefetch=1,
            grid=(n_seq,),
            in_specs=[page_spec(k) for k in range(n_pages)] + [
                const(1, flat), const(1, flat), const(flat, N_KV_HEADS * hidden), const(flat, N_KV_HEADS * hidden),
                const(N_KV_HEADS * hidden, width)],
            out_specs=pl.BlockSpec((None, n_rows, width), lambda b, pt: (b, 0, 0)),
        ),
        out_shape=jax.ShapeDtypeStruct((n_seq, n_rows, width), _BF16),
        compiler_params=pltpu.CompilerParams(dimension_semantics=("parallel",), vmem_limit_bytes=VMEM_LIMIT),
        name="cmp_pages",
    )(page_table, *([grouped] * n_pages), pos_t[0], pos_t[1], bd1[0], bd1[1], bd2)


def _cmp_pages_t_kernel(pt_ref, *refs, n_pages):
    del pt_ref
    pages = refs[:n_pages]
    posa_ref, posb_ref, w1a_ref, w1b_ref, w2_ref, o_ref, rows_scr = refs[n_pages:]
    page_rows = pages[0].shape[1]
    n_groups = n_pages * page_rows // CMP_STRIDE
    for k, pg in enumerate(pages):
        x = pg[...].T
        for half in range(rows_scr.shape[0]):
            rows_scr[half, k * page_rows:(k + 1) * page_rows, :] = x[:, half * LANES:(half + 1) * LANES]
    z0 = z1 = None
    for b in range(CMP_STRIDE):
        xb = jnp.concatenate([rows_scr[half, pl.ds(b, n_groups, stride=CMP_STRIDE), :]
                              for half in range(rows_scr.shape[0])], axis=1)
        d0 = jnp.dot((xb + posa_ref[b]).astype(_BF16), w1a_ref[b], preferred_element_type=_F32)
        d1 = jnp.dot((xb + posb_ref[b]).astype(_BF16), w1b_ref[b], preferred_element_type=_F32)
        z0 = d0 if z0 is None else z0 + d0
        z1 = d1 if z1 is None else z1 + d1
    pre = z0 + pltpu.roll(z1, n_groups - 1, 0)
    o_ref[...] = jnp.dot(jax.nn.gelu(pre).astype(_BF16), w2_ref[...], preferred_element_type=_F32).astype(o_ref.dtype)


def _cmp_pages_t(pages_t, slot, page_table, pos_emb, w1, w2):
    n_seq, n_pages = page_table.shape
    page_rows = pages_t.shape[2]
    width = N_KV_HEADS * HEAD_DIM
    hidden = w1.shape[1]
    eye = jnp.eye(N_KV_HEADS, dtype=_F32)
    w1r = w1.reshape(2, CMP_STRIDE, HEAD_DIM, hidden)
    bd1 = jnp.einsum('hk,abdf->abhdkf', eye, w1r).reshape(2, CMP_STRIDE, width, N_KV_HEADS * hidden).astype(_BF16)
    bd2 = jnp.einsum('hk,fd->hfkd', eye, w2).reshape(N_KV_HEADS * hidden, width).astype(_BF16)
    pos_t = jnp.broadcast_to(pos_emb.reshape(2, CMP_STRIDE, 1, 1, HEAD_DIM),
                             (2, CMP_STRIDE, 1, N_KV_HEADS, HEAD_DIM)).reshape(2, CMP_STRIDE, 1, width)
    n_rows = n_pages * page_rows // CMP_STRIDE
    page_spec = lambda k: pl.BlockSpec((None, width, page_rows), lambda b, pt: (pt[b, k], slot, 0))
    const = lambda *shape: pl.BlockSpec(shape, lambda b, pt: (0,) * len(shape))
    return pl.pallas_call(
        functools.partial(_cmp_pages_t_kernel, n_pages=n_pages),
        grid_spec=pltpu.PrefetchScalarGridSpec(
            num_scalar_prefetch=1,
            grid=(n_seq,),
            in_specs=[page_spec(k) for k in range(n_pages)] + [
                const(CMP_STRIDE, 1, width), const(CMP_STRIDE, 1, width),
                const(CMP_STRIDE, width, N_KV_HEADS * hidden), const(CMP_STRIDE, width, N_KV_HEADS * hidden),
                const(N_KV_HEADS * hidden, width)],
            out_specs=pl.BlockSpec((None, n_rows, width), lambda b, pt: (b, 0, 0)),
            scratch_shapes=[pltpu.VMEM((width // LANES, n_pages * page_rows, LANES), _F32)],
        ),
        out_shape=jax.ShapeDtypeStruct((n_seq, n_rows, width), _BF16),
        compiler_params=pltpu.CompilerParams(dimension_semantics=("parallel",), vmem_limit_bytes=VMEM_LIMIT),
        name="cmp_pages_t",
    )(page_table, *([pages_t] * n_pages), pos_t[0], pos_t[1], bd1[0], bd1[1], bd2)


def _pick_own_kv_head(x, n_q):
    rows_per = Q_PER_KV * n_q
    return jnp.concatenate([x[h * rows_per:(h + 1) * rows_per, h * HEAD_DIM:(h + 1) * HEAD_DIM]
                            for h in range(N_KV_HEADS)], axis=0)


def _nsa_sample_kernel(pt_ref, q_ref, gl_ref, kc_ref, vc_ref, win_ref, new_ref, *rest, n_q, past_len, n_sel):
    del pt_ref
    pages = rest[:PAGES_PER_STEP]
    (tcmp_ref, tsel_ref, twin_ref, exp_ref, ov_ref, o_ref,
     addsel, m_scr, l_scr, acc_scr, oc_scr, ow_scr) = rest[PAGES_PER_STEP:]
    j = pl.program_id(1)
    rows = N_HEADS * n_q
    width = N_KV_HEADS * HEAD_DIM
    w_buf = win_ref.shape[1]
    step_keys = PAGES_PER_STEP * pages[0].shape[1]
    rr = lax.broadcasted_iota(jnp.int32, (rows, width), 0)
    cc = lax.broadcasted_iota(jnp.int32, (rows, width), 1)
    q4 = jnp.concatenate([q_ref[...] * (HEAD_DIM ** -0.5)] * N_KV_HEADS, axis=1)
    qbd = jnp.where(rr // (Q_PER_KV * n_q) == cc // HEAD_DIM, q4, 0.0).astype(_BF16)
    new_rows = jnp.concatenate([new_ref[...], jnp.zeros((LANES - n_q, new_ref.shape[1]), _F32)], axis=0).astype(_BF16)

    @pl.when(j == 0)
    def _first():
        s = _dot_nt(qbd, kc_ref[...]) + tcmp_ref[...]
        p = jnp.exp(s - jnp.max(s, axis=-1, keepdims=True))
        pb = (p / jnp.sum(p, axis=-1, keepdims=True)).astype(_BF16)
        oc_scr[...] = _pick_own_kv_head(jnp.dot(pb, vc_ref[...], preferred_element_type=_F32), n_q)
        pm = jnp.dot(pb, ov_ref[...], preferred_element_type=_F32)
        imp = jnp.concatenate(
            [sum(pm[(h * Q_PER_KV + g) * n_q:(h * Q_PER_KV + g + 1) * n_q] for g in range(Q_PER_KV))
             for h in range(N_KV_HEADS)], axis=0)
        blk = lax.broadcasted_iota(jnp.int32, imp.shape, 1)
        t = past_len + lax.broadcasted_iota(jnp.int32, imp.shape, 0) % n_q
        cur = t // SEL_BLOCK
        forced = (blk == 0) | (blk == cur) | (blk == cur - 1)
        score = jnp.where(forced, FORCE, jnp.where(blk * SEL_BLOCK <= t, imp, -FORCE))
        score = jnp.where(blk < n_sel, score, -jnp.inf)
        rank = jnp.zeros(imp.shape, jnp.int32)
        for sp in range(n_sel):
            col = score[:, sp:sp + 1]
            rank += ((col > score) | ((col == score) & (blk > sp))).astype(jnp.int32)
        sel = (rank < min(SEL_TOPK, n_sel)).astype(_BF16)
        sel_rows = jnp.concatenate([sel[h * n_q:(h + 1) * n_q] for h in range(N_KV_HEADS) for _ in range(Q_PER_KV)],
                                   axis=0)
        addsel[...] = (jnp.dot(sel_rows, exp_ref[...], preferred_element_type=_F32) - 1.0) * (-NEG_INF) + tsel_ref[...]
        kw_t = win_ref[:width, :].astype(_BF16)
        vw_t = win_ref[width:, :].astype(_BF16)
        s1 = jnp.dot(qbd, kw_t, preferred_element_type=_F32) + twin_ref[:, :w_buf]
        s2 = _dot_nt(qbd, new_rows[:, 2 * width:3 * width]) + twin_ref[:, w_buf:]
        m = jnp.maximum(jnp.max(s1, axis=-1, keepdims=True), jnp.max(s2, axis=-1, keepdims=True))
        p1 = jnp.exp(s1 - m)
        p2 = jnp.exp(s2 - m)
        l = jnp.sum(p1, axis=-1, keepdims=True) + jnp.sum(p2, axis=-1, keepdims=True)
        ow = (_dot_nt(p1.astype(_BF16), vw_t)
              + jnp.dot(p2.astype(_BF16), new_rows[:, 3 * width:], preferred_element_type=_F32))
        ow_scr[...] = _pick_own_kv_head(ow / l, n_q)
        m_scr[...] = jnp.full(m_scr.shape, NEG_INF, _F32)
        l_scr[...] = jnp.zeros(l_scr.shape, _F32)
        acc_scr[...] = jnp.zeros(acc_scr.shape, _F32)

    def flash(s, pv):
        m_old = m_scr[...]
        m_new = jnp.maximum(m_old, jnp.max(s, axis=-1, keepdims=True))
        alpha = jnp.exp(m_old - m_new)
        p = jnp.exp(s - m_new)
        m_scr[...] = m_new
        l_scr[...] = alpha * l_scr[...] + jnp.sum(p, axis=-1, keepdims=True)
        acc_scr[...] = alpha * acc_scr[...] + pv(p.astype(_BF16))

    k_t = jnp.concatenate([pg[:width, :] for pg in pages], axis=1).astype(_BF16)
    v_t = jnp.concatenate([pg[width:, :] for pg in pages], axis=1).astype(_BF16)
    flash(jnp.dot(qbd, k_t, preferred_element_type=_F32)
          + addsel[:, pl.ds(pl.multiple_of(j * step_keys, step_keys), step_keys)], lambda p: _dot_nt(p, v_t))

    @pl.when(j == pl.num_programs(1) - 1)
    def _last():
        flash(_dot_nt(qbd, new_rows[:, :width]) + addsel[:, past_len:],
              lambda p: jnp.dot(p, new_rows[:, width:2 * width], preferred_element_type=_F32))
        o_s = _pick_own_kv_head(acc_scr[...] / l_scr[...], n_q)
        gates = jax.nn.sigmoid(gl_ref[...])
        o_ref[...] = gates[:, 0:1] * oc_scr[...] + gates[:, 1:2] * o_s + gates[:, 2:3] * ow_scr[...]


def _nsa_sample_side(cache_kv, page_table, cache_win, kv_new, rel_bias, prm):
    n_seq, n_pages = page_table.shape
    n_phys, page_rows = cache_kv.shape[:2]
    n_q = kv_new.shape[1]
    width = N_KV_HEADS * HEAD_DIM
    past_len = n_pages * page_rows
    w_buf = cache_win.shape[1]
    tk = past_len + n_q
    n_cmp = (tk - CMP_LEN) // CMP_STRIDE + 1
    n_cmp_pad = past_len // CMP_STRIDE
    n_sel = -(-tk // SEL_BLOCK)
    assert n_cmp == n_cmp_pad - 1 and n_cmp_pad == LANES and n_sel <= LANES and n_q <= LANES
    assert page_rows % SEL_BLOCK == 0 and n_pages % PAGES_PER_STEP == 0 and w_buf == min(WINDOW, past_len)
    cache_t = jnp.transpose(cache_kv, (0, 2, 3, 4, 1)).reshape(n_phys, KV_SLOTS_PAGED * width, page_rows)
    win_t = jnp.transpose(cache_win, (0, 2, 3, 4, 1)).reshape(n_seq, KV_SLOTS_WIN * width, w_buf)
    kc = _cmp_pages_t(cache_t, 0, page_table, prm['cmp_pos_k'], prm['cmp_w1_k'], prm['cmp_w2_k'])
    vc = _cmp_pages_t(cache_t, 1, page_table, prm['cmp_pos_v'], prm['cmp_w1_v'], prm['cmp_w2_v'])
    rows = N_HEADS * n_q
    t = past_len + np.arange(n_q)[:, None]
    masked = lambda ok, d: jnp.where(jnp.asarray(ok), _bias_by_distance(rel_bias, d), NEG_INF).reshape(rows, -1)
    n = np.arange(n_cmp_pad)[None, :]
    d_c = t - (n * CMP_STRIDE + CMP_LEN - 1)
    t_cmp = masked((d_c >= 0) & (n < n_cmp), d_c)
    pos = np.arange(past_len + LANES)[None, :]
    t_sel = masked((t - pos >= 0) & (pos < tk), t - pos)
    l = np.arange(w_buf + LANES)[None, :]
    d_w = t - (past_len - w_buf + l)
    t_win = masked((d_w >= 0) & (d_w <= WINDOW) & (l < w_buf + n_q), d_w)
    expand = (np.arange(LANES)[:, None] == pos // SEL_BLOCK) & (np.arange(LANES)[:, None] < n_sel)
    c_start = np.arange(n_cmp_pad)[:, None] * CMP_STRIDE
    s_start = np.arange(LANES)[None, :] * SEL_BLOCK
    ov = ((c_start < s_start + SEL_BLOCK) & (c_start + CMP_LEN > s_start) & (np.arange(n_cmp_pad)[:, None] < n_cmp)
          & (np.arange(LANES)[None, :] < n_sel))
    return {
        'cache_t': cache_t, 'page_table': page_table, 'kc': kc, 'vc': vc, 'win_t': win_t,
        'new': kv_new.reshape(n_seq, n_q, -1)[:, :, 2 * width:],
        't_cmp': t_cmp, 't_sel': t_sel, 't_win': t_win,
        'expand': jnp.asarray(expand, _BF16), 'ov': jnp.asarray(ov, _BF16), 'n_sel': n_sel, 'past_len': past_len,
    }


def _nsa_sample_attention(q, glog, side):
    n_seq, n_q, _ = q.shape
    glog = glog.reshape(n_seq, n_q, N_KV_HEADS, LANES)[..., :Q_PER_KV * N_BRANCH]
    rows = N_HEADS * n_q
    width = N_KV_HEADS * HEAD_DIM
    page_table = side['page_table']
    n_pages = page_table.shape[1]
    page_rows = side['cache_t'].shape[2]
    past_len = side['past_len']
    w_buf = side['win_t'].shape[2]
    n_steps = n_pages // PAGES_PER_STEP
    q_rows = jnp.transpose(q.reshape(n_seq, n_q, N_HEADS, HEAD_DIM), (0, 2, 1, 3)).reshape(n_seq, rows, HEAD_DIM)
    gl = jnp.transpose(glog.reshape(n_seq, n_q, N_HEADS, N_BRANCH), (0, 2, 1, 3)).reshape(n_seq, rows, N_BRANCH)
    gl = jnp.pad(gl, ((0, 0), (0, 0), (0, LANES - N_BRANCH)))
    per_seq = lambda *shape: pl.BlockSpec((None,) + shape, lambda b, j, pt: (b,) + (0,) * len(shape))
    const = lambda *shape: pl.BlockSpec(shape, lambda b, j, pt: (0,) * len(shape))
    page_spec = lambda k: pl.BlockSpec((None, 2 * width, page_rows),
                                       lambda b, j, pt: (pt[b, j * PAGES_PER_STEP + k], 1, 0))
    out = pl.pallas_call(
        functools.partial(_nsa_sample_kernel, n_q=n_q, past_len=past_len, n_sel=side['n_sel']),
        grid_spec=pltpu.PrefetchScalarGridSpec(
            num_scalar_prefetch=1,
            grid=(n_seq, n_steps),
            in_specs=[per_seq(rows, HEAD_DIM), per_seq(rows, LANES), per_seq(LANES, width), per_seq(LANES, width),
                      per_seq(2 * width, w_buf), per_seq(n_q, 4 * width)]
            + [page_spec(k) for k in range(PAGES_PER_STEP)]
            + [const(rows, LANES), const(rows, past_len + LANES), const(rows, w_buf + LANES),
               const(LANES, past_len + LANES), const(LANES, LANES)],
            out_specs=per_seq(rows, HEAD_DIM),
            scratch_shapes=[pltpu.VMEM((rows, past_len + LANES), _F32), pltpu.VMEM((rows, 1), _F32),
                            pltpu.VMEM((rows, 1), _F32), pltpu.VMEM((rows, width), _F32),
                            pltpu.VMEM((rows, HEAD_DIM), _F32), pltpu.VMEM((rows, HEAD_DIM), _F32)],
        ),
        out_shape=jax.ShapeDtypeStruct((n_seq, rows, HEAD_DIM), _F32),
        compiler_params=pltpu.CompilerParams(dimension_semantics=("parallel", "arbitrary"),
                                             vmem_limit_bytes=VMEM_LIMIT),
        name="nsa_sample",
    )(page_table, q_rows, gl, side['kc'], side['vc'], side['win_t'], side['new'],
      *([side['cache_t']] * PAGES_PER_STEP), side['t_cmp'], side['t_sel'], side['t_win'], side['expand'], side['ov'])
    return jnp.transpose(out.reshape(n_seq, N_HEADS, n_q, HEAD_DIM), (0, 2, 1, 3)).reshape(n_seq, n_q, N_HEADS * HEAD_DIM)


def _run_trunk(x, p, s0_re, s0_im, paged, win_buf, prm):
    bsz, t_len, _ = x.shape
    ssm_re, ssm_im = [], []
    side = None
    kv_rows_new = None
    win_state = None
    y = None
    x2 = x.reshape(-1, D_MODEL)
    as_seq = lambda a: a.reshape(bsz, t_len, a.shape[-1])
    for i in range(DEPTH):
        if i < N_A_LAYERS:
            u2, = _rows_call(lambda xt, gain: (_norm(xt, gain),), [x2], [prm['g_mix'][i].reshape(1, -1)],
                             [D_MODEL], "s5_norm")
            ssm = (prm['ssm_a_re'][i], prm['ssm_a_im'][i], prm['ssm_log_dt'][i], prm['ssm_b_re'][i],
                   prm['ssm_b_im'][i], prm['ssm_c_re'][i], prm['ssm_c_im'][i], prm['ssm_d'][i])
            s0 = _state_to_tiles(s0_re[i], s0_im[i])
            if t_len == 8:
                ys, s_fin = _s5_scan(u2[None], jnp.transpose(s0, (1, 0, 2))[None], False, *ssm)
                s_re, s_im = _state_from_tiles(jnp.transpose(s_fin[0], (1, 0, 2)))
            else:
                assert t_len % (S5_CHUNK * SUBLANES) == 0
                ys, s_fin = _s5_scan(as_seq(u2), s0[:, :, None, :], True, *ssm)
                s_re, s_im = _state_from_tiles(s_fin[:, :, 0, :])
            ssm_re.append(s_re)
            ssm_im.append(s_im)
            x2 = _glu_residual(x2, ys.reshape(-1, D_MODEL), prm['w_glu'][i])
        else:
            j = i - N_A_LAYERS
            q, glog = _qg_project(x2, prm['g_mix'][i], prm['w_qg'][j])
            attend = _nsa_prompt_attention if paged is None else _nsa_sample_attention
            mixed = attend(as_seq(q), as_seq(glog), side)
            x2 = _wo_residual(x2, mixed.reshape(-1, D_MODEL), prm['w_o'][j])
        x2 = _moe_layer(x2, prm['g_ffn'][i], prm['w_route_group'][i], prm['b_route_group'][i],
                        prm['w_route_expert'][i], prm['b_route_expert'][i], prm['w_exp_up'], prm['w_exp_down'], i)
        ple = (x2, p[i].reshape(-1, p.shape[-1]), prm['g_ple'][i], prm['w_ple_gate'][i], prm['w_ple_proj'][i])
        if i == N_A_LAYERS - 1:
            x2, kv2, *operands = _ple_residual(*ple, g_next=prm['g_kv'], w_next=prm['w_kv'],
                                               attention_operands=paged is None)
            kv = kv2.reshape(bsz, t_len, KV_SLOTS_PAGED + KV_SLOTS_WIN, N_KV_HEADS, HEAD_DIM)
            kv_rows_new, win_new = kv[:, :, :KV_SLOTS_PAGED], kv[:, :, KV_SLOTS_PAGED:]
            if paged is None:
                win_state = win_new[:, -min(WINDOW, t_len):]
                side = _nsa_prompt_side(kv, *operands, prm)
            else:
                w_buf = win_buf.shape[1]
                win_state = jnp.concatenate([win_buf, win_new], axis=1)[:, -w_buf:]
                side = _nsa_sample_side(paged[0], paged[1], win_buf, kv, prm['rel_bias'], prm)
        elif i == DEPTH - 1:
            x2, y = _ple_residual(*ple, g_next=prm['g_final'])
        else:
            x2, = _ple_residual(*ple)
    return as_seq(y), kv_rows_new, win_state, jnp.stack(ssm_re), jnp.stack(ssm_im)


def kernel(x_prompt, x_sample, p_prompt, p_sample, cache_kv, cache_win, state_ssm_re, state_ssm_im, page_table,
           g_mix, g_ffn, g_ple, g_kv, g_final,
           ssm_a_re, ssm_a_im, ssm_log_dt, ssm_b_re, ssm_b_im, ssm_c_re, ssm_c_im, ssm_d, w_glu,
           w_kv, cmp_pos_k, cmp_pos_v, cmp_w1_k, cmp_w2_k, cmp_w1_v, cmp_w2_v, w_qg, w_o, rel_bias,
           w_route_group, b_route_group, w_route_expert, b_route_expert, w_exp_up, w_exp_down,
           w_ple_proj, w_ple_gate):
    prm = {
        'g_mix': g_mix, 'g_ffn': g_ffn, 'g_ple': g_ple, 'g_kv': g_kv, 'g_final': g_final,
        'ssm_a_re': ssm_a_re, 'ssm_a_im': ssm_a_im, 'ssm_log_dt': ssm_log_dt,
        'ssm_b_re': ssm_b_re, 'ssm_b_im': ssm_b_im, 'ssm_c_re': ssm_c_re, 'ssm_c_im': ssm_c_im,
        'ssm_d': ssm_d, 'w_glu': w_glu,
        'w_kv': w_kv, 'cmp_pos_k': cmp_pos_k, 'cmp_pos_v': cmp_pos_v,
        'cmp_w1_k': cmp_w1_k, 'cmp_w2_k': cmp_w2_k, 'cmp_w1_v': cmp_w1_v, 'cmp_w2_v': cmp_w2_v,
        'w_qg': w_qg, 'w_o': w_o, 'rel_bias': rel_bias,
        'w_route_group': w_route_group, 'b_route_group': b_route_group,
        'w_route_expert': w_route_expert, 'b_route_expert': b_route_expert,
        'w_exp_up': w_exp_up, 'w_exp_down': w_exp_down,
        'w_ple_proj': w_ple_proj, 'w_ple_gate': w_ple_gate,
    }
    zero_state = jnp.zeros((N_A_LAYERS, x_prompt.shape[0], SSM_GROUPS, SSM_STATE), _F32)
    y_prompt, kv_prompt, win_prompt, ssm_re_prompt, ssm_im_prompt = _run_trunk(
        x_prompt, p_prompt, zero_state, zero_state, None, None, prm)
    y_sample, kv_sample, win_sample, ssm_re_sample, ssm_im_sample = _run_trunk(
        x_sample, p_sample, state_ssm_re, state_ssm_im, (cache_kv, page_table), cache_win, prm)
    return (y_prompt, y_sample, kv_prompt, win_prompt, ssm_re_prompt, ssm_im_prompt,
            kv_sample, win_sample, ssm_re_sample, ssm_im_sample)
```

```python
import functools
import math

import jax
import jax.numpy as jnp
import numpy as np
from jax import lax
from jax.experimental import pallas as pl
from jax.experimental.pallas import tpu as pltpu

D_MODEL = 1024
DEPTH = 4
N_A_LAYERS = DEPTH // 2
SSM_GROUP = 16
SSM_GROUPS = D_MODEL // SSM_GROUP
SSM_STATE = 64
N_HEADS = 16
HEAD_DIM = D_MODEL // N_HEADS
N_KV_HEADS = 4
Q_PER_KV = N_HEADS // N_KV_HEADS
CMP_LEN = 32
CMP_STRIDE = 16
SEL_BLOCK = 64
SEL_TOPK = 16
WINDOW = 512
N_BRANCH = 3
KV_SLOTS_PAGED = 4
KV_SLOTS_WIN = 2
Q_BLOCK = 64
N_BUCKETS = 32
MAX_DISTANCE = 128
N_EXPERT_GROUPS = 4
EXPERTS_PER_GROUP = 4
N_EXPERTS = N_EXPERT_GROUPS * EXPERTS_PER_GROUP
EXPERT_TOPK = 2
D_EXPERT = 256
RMS_EPS = 1e-6
NEG_INF = -1e30
FORCE = 1e4

LANES = 128
SUBLANES = 8
MOE_ROW_TILE = 1024
S5_CHUNK = 16
VMEM_LIMIT = 48 * 1024 * 1024

_F32 = jnp.float32
_BF16 = jnp.bfloat16


def _moe_kernel(x_ref, g_ref, wr_ref, br_ref, wup_ref, wdn_ref, o_ref, h_scr, comb_scr, acc_scr):
    e = pl.program_id(1)
    lane = lax.broadcasted_iota(jnp.int32, comb_scr.shape, 1)

    @pl.when(e == 0)
    def _route():
        x = x_ref[...]
        h = x * lax.rsqrt(jnp.mean(x * x, axis=-1, keepdims=True) + RMS_EPS) * g_ref[...]
        h_scr[...] = h.astype(_BF16)
        logits = jnp.dot(h_scr[...], wr_ref[...].astype(_BF16), preferred_element_type=_F32) + br_ref[...]
        is_grp = (lane >= N_EXPERTS) & (lane < N_EXPERTS + N_EXPERT_GROUPS)
        lg = jnp.where(is_grp, logits, -jnp.inf)
        gmax = jnp.max(lg, axis=-1, keepdims=True)
        gi = jnp.min(jnp.where(lg == gmax, lane, LANES), axis=-1, keepdims=True) - N_EXPERTS
        gp = 1.0 / jnp.sum(jnp.where(is_grp, jnp.exp(lg - gmax), 0.0), axis=-1, keepdims=True)
        in_grp = (lane < N_EXPERTS) & ((lane // EXPERTS_PER_GROUP) == gi)
        le = jnp.where(in_grp, logits, -jnp.inf)
        m1 = jnp.max(le, axis=-1, keepdims=True)
        i1 = jnp.min(jnp.where(le == m1, lane, LANES), axis=-1, keepdims=True)
        le2 = jnp.where(lane == i1, -jnp.inf, le)
        m2 = jnp.max(le2, axis=-1, keepdims=True)
        i2 = jnp.min(jnp.where(le2 == m2, lane, LANES), axis=-1, keepdims=True)
        e2 = jnp.exp(m2 - m1)
        den = 1.0 + e2
        comb_scr[...] = jnp.where(lane == i1, gp / den, jnp.where(lane == i2, gp * e2 / den, 0.0))
        acc_scr[...] = x

    up = jnp.dot(h_scr[...], wup_ref[0].astype(_BF16), preferred_element_type=_F32)
    a = up[:, :D_EXPERT]
    b = up[:, D_EXPERT:]
    c = jnp.sum(jnp.where(lane == e, comb_scr[...], 0.0), axis=-1, keepdims=True)
    act = (a * jax.nn.sigmoid(a)) * b * c
    acc_scr[...] += jnp.dot(act.astype(_BF16), wdn_ref[0].astype(_BF16), preferred_element_type=_F32)

    @pl.when(e == N_EXPERTS - 1)
    def _store():
        o_ref[...] = acc_scr[...]


def _moe_layer(x2, g, w_rg, b_rg, w_re, b_re, w_up, w_dn, layer):
    n_tok, d = x2.shape
    tm = min(n_tok, MOE_ROW_TILE)
    wr = jnp.zeros((d, LANES), _F32).at[:, :N_EXPERTS].set(w_re).at[:, N_EXPERTS:N_EXPERTS + N_EXPERT_GROUPS].set(w_rg)
    br = jnp.zeros((1, LANES), _F32).at[0, :N_EXPERTS].set(b_re).at[0, N_EXPERTS:N_EXPERTS + N_EXPERT_GROUPS].set(b_rg)
    return pl.pallas_call(
        _moe_kernel,
        grid=(n_tok // tm, N_EXPERTS),
        in_specs=[
            pl.BlockSpec((tm, d), lambda i, e: (i, 0)),
            pl.BlockSpec((1, d), lambda i, e: (0, 0)),
            pl.BlockSpec((d, LANES), lambda i, e: (0, 0)),
            pl.BlockSpec((1, LANES), lambda i, e: (0, 0)),
            pl.BlockSpec((None, 1, d, 2 * D_EXPERT), lambda i, e: (layer, e, 0, 0)),
            pl.BlockSpec((None, 1, D_EXPERT, d), lambda i, e: (layer, e, 0, 0)),
        ],
        out_specs=pl.BlockSpec((tm, d), lambda i, e: (i, 0)),
        out_shape=jax.ShapeDtypeStruct((n_tok, d), _F32),
        scratch_shapes=[
            pltpu.VMEM((tm, d), _BF16),
            pltpu.VMEM((tm, LANES), _F32),
            pltpu.VMEM((tm, d), _F32),
        ],
        compiler_params=pltpu.CompilerParams(
            dimension_semantics=("parallel", "arbitrary"), vmem_limit_bytes=VMEM_LIMIT),
        name="hmoe",
    )(x2, g.reshape(1, d), wr, br, w_up, w_dn)


ROW_TILE = 512


def _rows_kernel(*refs, body, n_in):
    outs = body(*[r[...] for r in refs[:n_in]])
    for o_ref, o in zip(refs[n_in:], outs):
        o_ref[...] = o


def _rows_call(body, rows, consts, outs, name):
    n_rows = rows[0].shape[0]
    tm = min(n_rows, ROW_TILE)
    row_spec = lambda width: pl.BlockSpec((tm, width), lambda i: (i, 0))
    const_spec = lambda c: pl.BlockSpec(c.shape, lambda i: (0,) * c.ndim)
    out_specs, out_shape = [], []
    for o in outs:
        if isinstance(o, int):
            o = (o, _F32)
        if o[0] == 'T':
            out_specs.append(pl.BlockSpec((o[1], tm), lambda i: (0, i)))
            out_shape.append(jax.ShapeDtypeStruct((o[1], n_rows), o[2]))
        else:
            out_specs.append(row_spec(o[0]))
            out_shape.append(jax.ShapeDtypeStruct((n_rows, o[0]), o[1]))
    return pl.pallas_call(
        functools.partial(_rows_kernel, body=body, n_in=len(rows) + len(consts)),
        grid=(n_rows // tm,),
        in_specs=[row_spec(r.shape[1]) for r in rows] + [const_spec(c) for c in consts],
        out_specs=out_specs,
        out_shape=out_shape,
        compiler_params=pltpu.CompilerParams(dimension_semantics=("parallel",), vmem_limit_bytes=VMEM_LIMIT),
        name=name,
    )(*rows, *consts)


def _norm(x, g):
    return x * lax.rsqrt(jnp.mean(x * x, axis=-1, keepdims=True) + RMS_EPS) * g


def _mm(x, w):
    return jnp.dot(x.astype(_BF16), w, preferred_element_type=_F32)


def _glu_residual(x2, y2, w_glu):
    def body(x, y, w):
        ag = _mm(jax.nn.gelu(y), w)
        return (x + ag[:, :D_MODEL] * jax.nn.sigmoid(ag[:, D_MODEL:]),)
    return _rows_call(body, [x2, y2], [w_glu.astype(_BF16)], [D_MODEL], "glu_residual")[0]


def _qg_project(x2, g, w_qg):
    per_kv = Q_PER_KV * N_BRANCH
    w_g = w_qg[:, N_HEADS * HEAD_DIM:].reshape(D_MODEL, N_KV_HEADS, per_kv)
    w_g = jnp.pad(w_g, ((0, 0), (0, 0), (0, LANES - per_kv))).reshape(D_MODEL, N_KV_HEADS * LANES)
    w = jnp.concatenate([w_qg[:, :N_HEADS * HEAD_DIM], w_g], axis=1).astype(_BF16)

    def body(x, gain, w):
        qg = _mm(_norm(x, gain), w)
        return qg[:, :N_HEADS * HEAD_DIM], qg[:, N_HEADS * HEAD_DIM:]
    return _rows_call(body, [x2], [g.reshape(1, -1), w], [N_HEADS * HEAD_DIM, N_KV_HEADS * LANES], "qg_project")


def _wo_residual(x2, mixed2, w_o):
    body = lambda x, m, w: (x + _mm(m, w),)
    return _rows_call(body, [x2, mixed2], [w_o.astype(_BF16)], [D_MODEL], "wo_residual")[0]


def _ple_residual(x2, p2, g_ple, w_gate, w_proj, g_next=None, w_next=None, attention_operands=False):
    consts = [g_ple.reshape(1, -1), w_gate.astype(_BF16), w_proj.astype(_BF16)]
    outs = [D_MODEL]
    width = N_KV_HEADS * HEAD_DIM
    if g_next is not None:
        consts.append(g_next.reshape(1, -1))
        outs.append(D_MODEL if w_next is None else w_next.shape[1])
    if w_next is not None:
        consts.append(w_next.astype(_BF16))
    if attention_operands:
        outs += [('T', 2 * width, _BF16), (4 * width, _BF16)]

    def body(x, p, gain, wg, wp, *nxt):
        out = x + _mm(p, wp) * jax.nn.sigmoid(_mm(_norm(x, gain), wg))
        if not nxt:
            return (out,)
        normed = _norm(out, nxt[0])
        if len(nxt) == 1:
            return out, normed
        kv = _mm(normed, nxt[1])
        if not attention_operands:
            return out, kv
        slot = lambda s: kv[:, s * width:(s + 1) * width]
        keys_t = jnp.concatenate([slot(2), slot(4)], axis=1).T.astype(_BF16)
        ones = jnp.ones((kv.shape[0], HEAD_DIM), _F32)
        vals = jnp.concatenate([piece for s in (3, 5) for h in range(N_KV_HEADS)
                                for piece in (slot(s)[:, h * HEAD_DIM:(h + 1) * HEAD_DIM], ones)], axis=1)
        return out, kv, keys_t, vals.astype(_BF16)
    return _rows_call(body, [x2, p2], consts, outs, "ple_residual")


def _cmul(ar, ai, br, bi):
    return ar * br - ai * bi, ar * bi + ai * br


S5_TILE_GROUPS = LANES // SSM_GROUP
S5_TILES = D_MODEL // LANES
S5_STATE_LANES = S5_TILE_GROUPS * SSM_STATE


def _s5_operators(a_re, a_im, log_dt, b_re, b_im, c_re, c_im, chunk):
    hp = lax.Precision.HIGHEST
    dt = jnp.exp(log_dt)[:, None]
    decay = jnp.exp(a_re * dt)
    ab_re, ab_im = decay * jnp.cos(a_im * dt), decay * jnp.sin(a_im * dt)
    den = a_re * a_re + a_im * a_im
    num_re = ab_re - 1.0
    q_re = (num_re * a_re + ab_im * a_im) / den
    q_im = (ab_im * a_re - num_re * a_im) / den
    bb_re, bb_im = _cmul(q_re[..., None], q_im[..., None], b_re, b_im)
    pw_re, pw_im = [jnp.ones_like(ab_re)], [jnp.zeros_like(ab_im)]
    for _ in range(chunk):
        nr, ni = _cmul(pw_re[-1], pw_im[-1], ab_re, ab_im)
        pw_re.append(nr)
        pw_im.append(ni)
    pw_re, pw_im = jnp.stack(pw_re), jnp.stack(pw_im)
    w_re, w_im = _cmul(pw_re[:chunk, :, :, None], pw_im[:chunk, :, :, None], bb_re[None], bb_im[None])
    k_lag = (jnp.einsum('gcp,jgpd->jgcd', c_re, w_re, precision=hp)
             - jnp.einsum('gcp,jgpd->jgcd', c_im, w_im, precision=hp))
    ca_re, ca_im = _cmul(c_re[None], c_im[None], pw_re[1:, :, None, :], pw_im[1:, :, None, :])

    def block_diag(per_group):
        a = per_group.reshape((chunk, S5_TILES, S5_TILE_GROUPS) + per_group.shape[2:])
        a = jnp.transpose(a, (1, 0, 2, 4, 3))
        n_r, n_c = a.shape[3:]
        wide = jnp.tile(a.reshape(S5_TILES, chunk, S5_TILE_GROUPS * n_r, n_c), (1, 1, 1, S5_TILE_GROUPS))
        same = (np.arange(S5_TILE_GROUPS * n_r)[:, None] // n_r) == (np.arange(S5_TILE_GROUPS * n_c)[None, :] // n_c)
        return jnp.where(same, wide, 0.0)

    k_bd = block_diag(k_lag)
    h_bd = jnp.concatenate([block_diag(w_re), block_diag(w_im)], axis=-1)
    g_bd = jnp.concatenate([block_diag(ca_re), block_diag(-ca_im)], axis=2)
    a_pow = jnp.concatenate([pw_re[chunk].reshape(S5_TILES, 1, S5_STATE_LANES),
                             pw_im[chunk].reshape(S5_TILES, 1, S5_STATE_LANES)], axis=-1)
    return k_bd.astype(_BF16), h_bd, g_bd.astype(_BF16), a_pow


def _advance(a_pow, s, e):
    half = S5_STATE_LANES
    ar, ai, sr, si = a_pow[:, :half], a_pow[:, half:], s[:, :half], s[:, half:]
    return jnp.concatenate([ar * sr - ai * si + e[:, :half], ar * si + ai * sr + e[:, half:]], axis=1)


def _s5_kernel(u_ref, k_ref, h_ref, g_ref, a_ref, d_ref, s0_ref, y_ref, sfin_ref, upad, e_scr, sin_scr, *,
               chunk, carry):
    n_rows = u_ref.shape[0]
    n_chunks = n_rows // chunk
    u = u_ref[...]
    upad[:chunk] = jnp.zeros((chunk, LANES), _F32)
    upad[chunk:] = u
    step = lax.broadcasted_iota(jnp.int32, (n_rows, LANES), 0) % chunk
    y = d_ref[...] * u
    for lag in range(chunk):
        part = jnp.dot(upad[chunk - lag:chunk - lag + n_rows].astype(_BF16), k_ref[lag], preferred_element_type=_F32)
        y += part if lag == 0 else jnp.where(step >= lag, part, 0.0)
    y_ref[...] = y
    e = None
    for lag in range(chunk):
        rows = u_ref[pl.ds(chunk - 1 - lag, n_chunks, stride=chunk), :]
        if h_ref.dtype == _F32:
            d = jnp.dot(rows, h_ref[lag], preferred_element_type=_F32, precision=lax.Precision.HIGHEST)
        else:
            d = jnp.dot(rows.astype(_BF16), h_ref[lag], preferred_element_type=_F32)
        e = d if e is None else e + d
    a_pow = a_ref[...]
    if carry:
        e_scr[...] = e
        sub = SUBLANES

        def body(jj, s):
            rows = pl.ds(pl.multiple_of(jj * sub, sub), sub)
            e_tile = e_scr[rows, :]
            starts = []
            for r in range(sub):
                starts.append(s)
                s = _advance(a_pow, s, e_tile[r:r + 1])
            sin_scr[rows, :] = jnp.concatenate(starts, axis=0)
            return s
        sfin_ref[...] = lax.fori_loop(0, n_chunks // sub, body, s0_ref[...])
    else:
        sin_scr[...] = s0_ref[...]
        sfin_ref[...] = _advance(a_pow, s0_ref[...], e)
    s_in = sin_scr[...].astype(_BF16)
    for k in range(chunk):
        rows = pl.ds(k, n_chunks, stride=chunk)
        y_ref[rows, :] = y_ref[rows, :] + jnp.dot(s_in, g_ref[k], preferred_element_type=_F32)


def _s5_scan(u, s0, carry, a_re, a_im, log_dt, b_re, b_im, c_re, c_im, d_skip):
    n_b, n_rows, _ = u.shape
    chunk = S5_CHUNK if carry else n_rows // s0.shape[2]
    n_chunks = n_rows // chunk
    n_state = s0.shape[2]
    k_bd, h_bd, g_bd, a_pow = _s5_operators(a_re, a_im, log_dt, b_re, b_im, c_re, c_im, chunk)
    if carry:
        h_bd = h_bd.astype(_BF16)
    per_tile = lambda *shape: pl.BlockSpec((None,) + shape, lambda t, b: (t,) + (0,) * len(shape))
    rows_spec = pl.BlockSpec((None, n_rows, LANES), lambda t, b: (b, 0, t))
    state_spec = pl.BlockSpec((None, None, n_state, 2 * S5_STATE_LANES), lambda t, b: (b, t, 0, 0))
    return pl.pallas_call(
        functools.partial(_s5_kernel, chunk=chunk, carry=carry),
        grid=(S5_TILES, n_b),
        in_specs=[rows_spec, per_tile(chunk, LANES, LANES), per_tile(chunk, LANES, 2 * S5_STATE_LANES),
                  per_tile(chunk, 2 * S5_STATE_LANES, LANES), per_tile(1, 2 * S5_STATE_LANES), per_tile(1, LANES),
                  state_spec],
        out_specs=[rows_spec, state_spec],
        out_shape=[jax.ShapeDtypeStruct(u.shape, _F32), jax.ShapeDtypeStruct(s0.shape, _F32)],
        scratch_shapes=[pltpu.VMEM((n_rows + chunk, LANES), _F32), pltpu.VMEM((n_chunks, 2 * S5_STATE_LANES), _F32),
                        pltpu.VMEM((n_chunks, 2 * S5_STATE_LANES), _F32)],
        compiler_params=pltpu.CompilerParams(dimension_semantics=("parallel", "parallel"),
                                             vmem_limit_bytes=VMEM_LIMIT),
        name="s5_scan",
    )(u, k_bd, h_bd, g_bd, a_pow, d_skip.reshape(S5_TILES, 1, LANES), s0)


def _state_to_tiles(s_re, s_im):
    bsz = s_re.shape[0]
    return jnp.concatenate([s_re.reshape(bsz, S5_TILES, S5_STATE_LANES), s_im.reshape(bsz, S5_TILES, S5_STATE_LANES)],
                           axis=-1)


def _state_from_tiles(s):
    bsz = s.shape[0]
    return (s[..., :S5_STATE_LANES].reshape(bsz, SSM_GROUPS, SSM_STATE),
            s[..., S5_STATE_LANES:].reshape(bsz, SSM_GROUPS, SSM_STATE))


TQ = 256
TK = 128
NEAR_TILES = TQ // TK + 1
NEAR_BUCKET_DIST = 113
NEAR_CMP_BLOCKS = -(-NEAR_BUCKET_DIST // CMP_STRIDE)


def _bucket_table(dist):
    max_exact = N_BUCKETS // 2
    d = np.maximum(np.asarray(dist, np.int64), 0)
    df = np.maximum(d, 1).astype(np.float64)
    large = max_exact + (np.log(df / max_exact) / math.log(MAX_DISTANCE / max_exact)
                         * (N_BUCKETS - max_exact)).astype(np.int64)
    return np.where(d < max_exact, d, np.minimum(large, N_BUCKETS - 1)).astype(np.int32)


def _bias_by_distance(rel_bias, dist):
    onehot = _bucket_table(dist)[None] == np.arange(N_BUCKETS)[:, None, None]
    return jnp.einsum('kqc,kh->hqc', jnp.asarray(onehot, _BF16).astype(_F32), rel_bias,
                      precision=lax.Precision.HIGHEST)


def _bias_tiles(rel_bias, n_cmp_pad):
    rel_bias = rel_bias - rel_bias[N_BUCKETS - 1]
    r = np.arange(TQ)[:, None]
    c = np.arange(n_cmp_pad)[None, :]
    d_cmp = np.where(c < NEAR_CMP_BLOCKS + TQ // CMP_STRIDE,
                     r - (CMP_LEN - 1) + CMP_STRIDE * (NEAR_CMP_BLOCKS - c), 10 ** 6)
    g_cmp = _bias_by_distance(rel_bias, d_cmp).reshape(N_KV_HEADS, Q_PER_KV * TQ, n_cmp_pad)
    k = np.arange(NEAR_TILES * TK)[None, :]
    d_near = TK + r - k
    near = _bias_by_distance(rel_bias, d_near) + jnp.asarray(np.where(d_near >= 0, 0.0, NEG_INF), _F32)
    near = near.reshape(N_KV_HEADS, Q_PER_KV * TQ, NEAR_TILES * TK)
    return g_cmp, near


def _stack_heads(q):
    return jnp.concatenate([q[:, g * HEAD_DIM:(g + 1) * HEAD_DIM] for g in range(Q_PER_KV)], axis=0)


def _unstack_heads(o):
    return jnp.concatenate([o[g] for g in range(Q_PER_KV)], axis=1)


def _nsa_cmp_kernel(q_ref, kct_ref, vc_ref, g_ref, ovt_ref, oc_ref, selt_ref):
    i = pl.program_id(2)
    t0 = i * TQ
    n_cmp_pad = kct_ref.shape[-1]
    qs = _stack_heads(q_ref[...] * (HEAD_DIM ** -0.5)).astype(_BF16)
    s = jnp.dot(qs, kct_ref[...], preferred_element_type=_F32)
    shift = (i * (TQ // CMP_STRIDE) + n_cmp_pad - NEAR_CMP_BLOCKS) % n_cmp_pad
    bias = pltpu.roll(g_ref[...], shift, 1)
    s3 = (s + bias).reshape(Q_PER_KV, TQ, n_cmp_pad)
    r = lax.broadcasted_iota(jnp.int32, (TQ, n_cmp_pad), 0)
    n = lax.broadcasted_iota(jnp.int32, (TQ, n_cmp_pad), 1)
    valid = (t0 + r - CMP_STRIDE * n - (CMP_LEN - 1)) >= 0
    sm = jnp.where(valid, s3, NEG_INF)
    m = jnp.max(sm, axis=-1, keepdims=True)
    p = jnp.where(valid, jnp.exp(sm - m), 0.0)
    l = jnp.sum(p, axis=-1, keepdims=True)
    pb = (p * jnp.where(l > 0.0, 1.0 / l, 0.0)).astype(_BF16)
    oc = jnp.dot(pb.reshape(Q_PER_KV * TQ, n_cmp_pad), vc_ref[...], preferred_element_type=_F32)
    oc_ref[...] = _unstack_heads(oc.reshape(Q_PER_KV, TQ, HEAD_DIM))
    imp = jnp.zeros((ovt_ref.shape[0], TQ), _F32)
    for g in range(Q_PER_KV):
        imp += lax.dot_general(ovt_ref[...], pb[g], (((1,), (1,)), ((), ())), preferred_element_type=_F32)
    n_sel = imp.shape[0]
    blk = lax.broadcasted_iota(jnp.int32, (n_sel, TQ), 0)
    t = t0 + lax.broadcasted_iota(jnp.int32, (n_sel, TQ), 1)
    cur = t // SEL_BLOCK
    forced = (blk == 0) | (blk == cur) | (blk == cur - 1)
    score = jnp.where(forced, FORCE, jnp.where(blk * SEL_BLOCK <= t, imp, -FORCE))
    rank = jnp.zeros((n_sel, TQ), _F32)
    for sp in range(n_sel):
        row = score[sp:sp + 1, :]
        beats = (row > score) | ((row == score) & (blk > sp))
        rank += jnp.where(beats, 1.0, 0.0)
    selt_ref[...] = (rank < SEL_TOPK).astype(_F32)


ROW_BLOCK = 128


def _flash_step(qs, kt, v, add_rows, m_scr, acc_scr):
    n_chunks = kt.shape[-1] // LANES
    for r0 in range(0, TQ, ROW_BLOCK):
        for g in range(Q_PER_KV):
            s = jnp.dot(qs[g * TQ + r0:g * TQ + r0 + ROW_BLOCK], kt, preferred_element_type=_F32)
            add = add_rows(g, r0)
            if add is not None:
                s = s + add
            chunks = [s[:, k * LANES:(k + 1) * LANES] for k in range(n_chunks)]
            m_old = m_scr[g, r0:r0 + ROW_BLOCK]
            m_new = jnp.maximum(m_old, jnp.max(functools.reduce(jnp.maximum, chunks), axis=-1, keepdims=True))
            alpha = jnp.exp(m_old - m_new)
            m_scr[g, r0:r0 + ROW_BLOCK] = m_new
            pv = None
            for k, sk in enumerate(chunks):
                d = jnp.dot(jnp.exp((sk - m_new).astype(_BF16)), v[k * LANES:(k + 1) * LANES],
                            preferred_element_type=_F32)
                pv = d if pv is None else pv + d
            acc_scr[g, r0:r0 + ROW_BLOCK] = alpha * acc_scr[g, r0:r0 + ROW_BLOCK] + pv


FAR_TILES = 4


def _nsa_sel_win_kernel(q_ref, oc_ref, glog_ref, selt_ref, kst_ref, vs_ref, kwt_ref, vw_ref, near_ref,
                        exp_ref, gexp_ref, o_ref, seladd, m_s, acc_s, m_w, acc_w):
    i = pl.program_id(2)
    qs = _stack_heads(q_ref[...] * (HEAD_DIM ** -0.5)).astype(_BF16)
    sel = jnp.dot(selt_ref[...].T.astype(_BF16), exp_ref[...], preferred_element_type=_F32)
    seladd[...] = (sel - 1.0) * (-NEG_INF)
    for m_scr, acc_scr in ((m_s, acc_s), (m_w, acc_w)):
        m_scr[...] = jnp.full(m_scr.shape, NEG_INF, _F32)
        acc_scr[...] = jnp.zeros(acc_scr.shape, _F32)

    def keys(j, n_tiles):
        return pl.ds(pl.multiple_of(j * TK, TK), n_tiles * TK)

    def sel_step(j, n_tiles, bias):
        cols = keys(j, n_tiles)

        def add_rows(g, r0):
            add = seladd[r0:r0 + ROW_BLOCK, cols]
            return add if bias is None else add + bias(g, r0)
        _flash_step(qs, kst_ref[:, cols], vs_ref[cols, :], add_rows, m_s, acc_s)

    def win_step(j, n_tiles, add_rows):
        cols = keys(j, n_tiles)
        _flash_step(qs, kwt_ref[:, cols], vw_ref[cols, :], add_rows, m_w, acc_w)

    near = lambda lo: (lambda g, r0: near_ref[g * TQ + r0:g * TQ + r0 + ROW_BLOCK, lo * TK:])
    no_add = lambda g, r0: None
    own = TQ // TK
    first = i * own

    n_far = jnp.maximum(first - 1, 0)

    def far_many(j, carry):
        sel_step(j * FAR_TILES, FAR_TILES, None)
        return carry

    lax.fori_loop(0, n_far // FAR_TILES, far_many, 0)
    for rest in range(1, FAR_TILES):
        @pl.when(n_far % FAR_TILES == rest)
        def _():
            sel_step((n_far // FAR_TILES) * FAR_TILES, rest, None)

    @pl.when(i >= 1)
    def _():
        sel_step(first - 1, NEAR_TILES, near(0))
        win_step(first - 1, NEAR_TILES, near(0))

    @pl.when(i == 0)
    def _():
        sel_step(0, own, near(1))
        win_step(0, own, near(1))

    n_back = WINDOW // TK

    def back_step(start_tile, n_tiles, offset):
        masks = {}

        def add_rows(g, r0):
            if offset >= r0 + ROW_BLOCK - 1:
                return None
            if r0 not in masks:
                r = r0 + lax.broadcasted_iota(jnp.int32, (ROW_BLOCK, n_tiles * TK), 0)
                kk = offset + lax.broadcasted_iota(jnp.int32, (ROW_BLOCK, n_tiles * TK), 1)
                masks[r0] = jnp.where(kk >= r, 0.0, NEG_INF)
            return masks[r0]
        win_step(start_tile, n_tiles, add_rows)

    @pl.when(first >= n_back)
    def _():
        back_step(first - n_back, n_back - 1, 0)

    for i_small in range(1, -(-n_back // own)):
        if i_small * own > 1:
            @pl.when(i == i_small)
            def _():
                back_step(0, i_small * own - 1, (n_back - i_small * own) * TK)

    def finish(acc_scr):
        acc = acc_scr[...]
        return _unstack_heads(acc[..., :HEAD_DIM] / acc[..., HEAD_DIM:])

    gates = jnp.dot(jax.nn.sigmoid(glog_ref[...]), gexp_ref[...], preferred_element_type=_F32,
                    precision=lax.Precision.HIGHEST)
    width = Q_PER_KV * HEAD_DIM
    o_ref[...] = (gates[:, :width] * oc_ref[...] + gates[:, width:2 * width] * finish(acc_s)
                  + gates[:, 2 * width:] * finish(acc_w))


def _nsa_prompt_side(kv, keys_t, vals, prm):
    bsz, t_len = kv.shape[:2]
    n_cmp = (t_len - CMP_LEN) // CMP_STRIDE + 1
    n_cmp_pad = t_len // CMP_STRIDE
    assert n_cmp == n_cmp_pad - 1 and n_cmp_pad % LANES == 0 and t_len % TQ == 0
    n_sel = t_len // SEL_BLOCK
    grouped = lambda slot: kv[:, :, slot].reshape(bsz, n_cmp_pad, CMP_STRIDE * N_KV_HEADS * HEAD_DIM)
    one_page = jnp.arange(bsz, dtype=jnp.int32)[:, None]
    k_cmp = _cmp_pages(grouped(0), one_page, prm['cmp_pos_k'], prm['cmp_w1_k'], prm['cmp_w2_k'])
    v_cmp = _cmp_pages(grouped(1), one_page, prm['cmp_pos_v'], prm['cmp_w1_v'], prm['cmp_w2_v'])
    t_last = lambda a: jnp.transpose(a, (0, 2, 3, 1))
    t_rows = lambda a: jnp.transpose(a, (0, 2, 1, 3))
    pad_c = lambda a: a.reshape(bsz, n_cmp_pad, N_KV_HEADS, HEAD_DIM)
    rel_bias = prm['rel_bias']
    c_start = np.arange(n_cmp_pad)[None, :] * CMP_STRIDE
    s_start = np.arange(n_sel)[:, None] * SEL_BLOCK
    ovt = ((c_start < s_start + SEL_BLOCK) & (c_start + CMP_LEN > s_start) & (np.arange(n_cmp_pad)[None, :] < n_cmp))
    expand = np.arange(n_sel)[:, None] == (np.arange(t_len)[None, :] // SEL_BLOCK)
    lanes = np.arange(LANES)[:, None]
    cols = np.arange(N_BRANCH * Q_PER_KV * HEAD_DIM)[None, :]
    gexp = lanes == (cols // HEAD_DIM % Q_PER_KV) * N_BRANCH + cols // (Q_PER_KV * HEAD_DIM)
    g_cmp, near = _bias_tiles(rel_bias, n_cmp_pad)
    return {
        'kct': t_last(pad_c(k_cmp)), 'vc': t_rows(pad_c(v_cmp)),
        'keys_t': keys_t, 'vals': vals.reshape(bsz, t_len, vals.shape[-1]),
        'ovt': jnp.asarray(ovt, _BF16), 'expand': jnp.asarray(expand, _BF16), 'gexp': jnp.asarray(gexp, _F32),
        'g_cmp': g_cmp, 'near': near,
    }


def _nsa_prompt_attention(q, glog, side):
    bsz, t_len, _ = q.shape
    n_t = t_len // TQ
    width = Q_PER_KV * HEAD_DIM
    n_cmp_pad = side['kct'].shape[-1]
    n_sel = side['ovt'].shape[0]
    grid = (bsz, N_KV_HEADS, n_t)
    params = pltpu.CompilerParams(dimension_semantics=("parallel", "parallel", "arbitrary"),
                                  vmem_limit_bytes=VMEM_LIMIT)
    q_spec = pl.BlockSpec((None, TQ, width), lambda b, h, i: (b, i, h))
    per_bh = lambda *shape: pl.BlockSpec((None, None) + shape, lambda b, h, i: (b, h, 0, 0))
    per_h = lambda *shape: pl.BlockSpec((None,) + shape, lambda b, h, i: (h,) + (0,) * len(shape))
    const = lambda *shape: pl.BlockSpec(shape, lambda b, h, i: (0,) * len(shape))
    selt_spec = pl.BlockSpec((None, None, n_sel, TQ), lambda b, h, i: (b, h, 0, i))
    o_c, selt = pl.pallas_call(
        _nsa_cmp_kernel,
        grid=grid,
        in_specs=[q_spec, per_bh(HEAD_DIM, n_cmp_pad), per_bh(n_cmp_pad, HEAD_DIM),
                  per_h(Q_PER_KV * TQ, n_cmp_pad), const(n_sel, n_cmp_pad)],
        out_specs=[q_spec, selt_spec],
        out_shape=[jax.ShapeDtypeStruct((bsz, t_len, N_HEADS * HEAD_DIM), _F32),
                   jax.ShapeDtypeStruct((bsz, N_KV_HEADS, n_sel, t_len), _F32)],
        compiler_params=params,
        name="nsa_cmp_select",
    )(q, side['kct'], side['vc'], side['g_cmp'], side['ovt'])
    stat = pltpu.VMEM((Q_PER_KV, TQ, 2 * HEAD_DIM), _F32)
    keys_spec = lambda branch: pl.BlockSpec((HEAD_DIM, t_len), lambda b, h, i: (branch * N_KV_HEADS + h, b))
    vals_spec = lambda branch: pl.BlockSpec((None, t_len, 2 * HEAD_DIM),
                                            lambda b, h, i: (b, 0, branch * N_KV_HEADS + h))
    return pl.pallas_call(
        _nsa_sel_win_kernel,
        grid=grid,
        in_specs=[q_spec, q_spec, pl.BlockSpec((None, TQ, LANES), lambda b, h, i: (b, i, h)), selt_spec,
                  keys_spec(0), vals_spec(0), keys_spec(1), vals_spec(1),
                  per_h(Q_PER_KV * TQ, NEAR_TILES * TK), const(n_sel, t_len),
                  const(LANES, N_BRANCH * width)],
        out_specs=q_spec,
        out_shape=jax.ShapeDtypeStruct((bsz, t_len, N_HEADS * HEAD_DIM), _F32),
        scratch_shapes=[pltpu.VMEM((TQ, t_len), _F32), stat, stat, stat, stat],
        compiler_params=params,
        name="nsa_select_window",
    )(q, o_c, glog, selt, side['keys_t'], side['vals'], side['keys_t'], side['vals'], side['near'],
      side['expand'], side['gexp'])


PAGES_PER_STEP = 16


def _dot_nt(a, b):
    return lax.dot_general(a, b, (((1,), (1,)), ((), ())), preferred_element_type=_F32)


def _cmp_pages_kernel(pt_ref, *refs, n_pages):
    del pt_ref
    pages = refs[:n_pages]
    posa_ref, posb_ref, w1a_ref, w1b_ref, w2_ref, o_ref = refs[n_pages:]
    x = jnp.concatenate([pg[...] for pg in pages], axis=0)
    z0 = jnp.dot((x + posa_ref[...]).astype(_BF16), w1a_ref[...], preferred_element_type=_F32)
    z1 = jnp.dot((x + posb_ref[...]).astype(_BF16), w1b_ref[...], preferred_element_type=_F32)
    pre = z0 + pltpu.roll(z1, z1.shape[0] - 1, 0)
    o_ref[...] = jnp.dot(jax.nn.gelu(pre).astype(_BF16), w2_ref[...], preferred_element_type=_F32).astype(o_ref.dtype)


def _cmp_pages(grouped, page_table, pos_emb, w1, w2):
    n_seq, n_pages = page_table.shape
    groups_per_page, flat = grouped.shape[1:]
    width = N_KV_HEADS * HEAD_DIM
    hidden = w1.shape[1]
    eye = jnp.eye(N_KV_HEADS, dtype=_F32)
    w1r = w1.reshape(2, CMP_STRIDE, HEAD_DIM, hidden)
    bd1 = jnp.einsum('hk,abdf->abhdkf', eye, w1r).reshape(2, flat, N_KV_HEADS * hidden).astype(_BF16)
    bd2 = jnp.einsum('hk,fd->hfkd', eye, w2).reshape(N_KV_HEADS * hidden, width).astype(_BF16)
    pos_t = jnp.broadcast_to(pos_emb.reshape(2, CMP_STRIDE, 1, HEAD_DIM),
                             (2, CMP_STRIDE, N_KV_HEADS, HEAD_DIM)).reshape(2, 1, flat)
    n_rows = n_pages * groups_per_page
    page_spec = lambda k: pl.BlockSpec((None, groups_per_page, flat), lambda b, pt: (pt[b, k], 0, 0))
    const = lambda *shape: pl.BlockSpec(shape, lambda b, pt: (0,) * len(shape))
    return pl.pallas_call(
        functools.partial(_cmp_pages_kernel, n_pages=n_pages),
        grid_spec=pltpu.PrefetchScalarGridSpec(
            num_scalar_prefetch=1,
            grid=(n_seq,),
            in_specs=[page_spec(k) for k in range(n_pages)] + [
                const(1, flat), const(1, flat), const(flat, N_KV_HEADS * hidden), const(flat, N_KV_HEADS * hidden),
                const(N_KV_HEADS * hidden, width)],
            out_specs=pl.BlockSpec((None, n_rows, width), lambda b, pt: (b, 0, 0)),
        ),
        out_shape=jax.ShapeDtypeStruct((n_seq, n_rows, width), _BF16),
        compiler_params=pltpu.CompilerParams(dimension_semantics=("parallel",), vmem_limit_bytes=VMEM_LIMIT),
        name="cmp_pages",
    )(page_table, *([grouped] * n_pages), pos_t[0], pos_t[1], bd1[0], bd1[1], bd2)


def _cmp_pages_t_kernel(pt_ref, *refs, n_pages):
    del pt_ref
    pages = refs[:n_pages]
    posa_ref, posb_ref, w1a_ref, w1b_ref, w2_ref, o_ref, rows_scr = refs[n_pages:]
    page_rows = pages[0].shape[1]
    n_groups = n_pages * page_rows // CMP_STRIDE
    for k, pg in enumerate(pages):
        x = pg[...].T
        for half in range(rows_scr.shape[0]):
            rows_scr[half, k * page_rows:(k + 1) * page_rows, :] = x[:, half * LANES:(half + 1) * LANES]
    z0 = z1 = None
    for b in range(CMP_STRIDE):
        xb = jnp.concatenate([rows_scr[half, pl.ds(b, n_groups, stride=CMP_STRIDE), :]
                              for half in range(rows_scr.shape[0])], axis=1)
        d0 = jnp.dot((xb + posa_ref[b]).astype(_BF16), w1a_ref[b], preferred_element_type=_F32)
        d1 = jnp.dot((xb + posb_ref[b]).astype(_BF16), w1b_ref[b], preferred_element_type=_F32)
        z0 = d0 if z0 is None else z0 + d0
        z1 = d1 if z1 is None else z1 + d1
    pre = z0 + pltpu.roll(z1, n_groups - 1, 0)
    o_ref[...] = jnp.dot(jax.nn.gelu(pre).astype(_BF16), w2_ref[...], preferred_element_type=_F32).astype(o_ref.dtype)


def _cmp_pages_t(pages_t, slot, page_table, pos_emb, w1, w2):
    n_seq, n_pages = page_table.shape
    page_rows = pages_t.shape[2]
    width = N_KV_HEADS * HEAD_DIM
    hidden = w1.shape[1]
    eye = jnp.eye(N_KV_HEADS, dtype=_F32)
    w1r = w1.reshape(2, CMP_STRIDE, HEAD_DIM, hidden)
    bd1 = jnp.einsum('hk,abdf->abhdkf', eye, w1r).reshape(2, CMP_STRIDE, width, N_KV_HEADS * hidden).astype(_BF16)
    bd2 = jnp.einsum('hk,fd->hfkd', eye, w2).reshape(N_KV_HEADS * hidden, width).astype(_BF16)
    pos_t = jnp.broadcast_to(pos_emb.reshape(2, CMP_STRIDE, 1, 1, HEAD_DIM),
                             (2, CMP_STRIDE, 1, N_KV_HEADS, HEAD_DIM)).reshape(2, CMP_STRIDE, 1, width)
    n_rows = n_pages * page_rows // CMP_STRIDE
    page_spec = lambda k: pl.BlockSpec((None, width, page_rows), lambda b, pt: (pt[b, k], slot, 0))
    const = lambda *shape: pl.BlockSpec(shape, lambda b, pt: (0,) * len(shape))
    return pl.pallas_call(
        functools.partial(_cmp_pages_t_kernel, n_pages=n_pages),
        grid_spec=pltpu.PrefetchScalarGridSpec(
            num_scalar_prefetch=1,
            grid=(n_seq,),
            in_specs=[page_spec(k) for k in range(n_pages)] + [
                const(CMP_STRIDE, 1, width), const(CMP_STRIDE, 1, width),
                const(CMP_STRIDE, width, N_KV_HEADS * hidden), const(CMP_STRIDE, width, N_KV_HEADS * hidden),
                const(N_KV_HEADS * hidden, width)],
            out_specs=pl.BlockSpec((None, n_rows, width), lambda b, pt: (b, 0, 0)),
            scratch_shapes=[pltpu.VMEM((width // LANES, n_pages * page_rows, LANES), _F32)],
        ),
        out_shape=jax.ShapeDtypeStruct((n_seq, n_rows, width), _BF16),
        compiler_params=pltpu.CompilerParams(dimension_semantics=("parallel",), vmem_limit_bytes=VMEM_LIMIT),
        name="cmp_pages_t",
    )(page_table, *([pages_t] * n_pages), pos_t[0], pos_t[1], bd1[0], bd1[1], bd2)


def _pick_own_kv_head(x, n_q):
    rows_per = Q_PER_KV * n_q
    return jnp.concatenate([x[h * rows_per:(h + 1) * rows_per, h * HEAD_DIM:(h + 1) * HEAD_DIM]
                            for h in range(N_KV_HEADS)], axis=0)


def _nsa_sample_kernel(pt_ref, q_ref, gl_ref, kc_ref, vc_ref, win_ref, new_ref, *rest, n_q, past_len, n_sel):
    del pt_ref
    pages = rest[:PAGES_PER_STEP]
    (tcmp_ref, tsel_ref, twin_ref, exp_ref, ov_ref, o_ref,
     addsel, m_scr, l_scr, acc_scr, oc_scr, ow_scr) = rest[PAGES_PER_STEP:]
    j = pl.program_id(1)
    rows = N_HEADS * n_q
    width = N_KV_HEADS * HEAD_DIM
    w_buf = win_ref.shape[1]
    step_keys = PAGES_PER_STEP * pages[0].shape[1]
    rr = lax.broadcasted_iota(jnp.int32, (rows, width), 0)
    cc = lax.broadcasted_iota(jnp.int32, (rows, width), 1)
    q4 = jnp.concatenate([q_ref[...] * (HEAD_DIM ** -0.5)] * N_KV_HEADS, axis=1)
    qbd = jnp.where(rr // (Q_PER_KV * n_q) == cc // HEAD_DIM, q4, 0.0).astype(_BF16)
    new_rows = jnp.concatenate([new_ref[...], jnp.zeros((LANES - n_q, new_ref.shape[1]), _F32)], axis=0).astype(_BF16)

    @pl.when(j == 0)
    def _first():
        s = _dot_nt(qbd, kc_ref[...]) + tcmp_ref[...]
        p = jnp.exp(s - jnp.max(s, axis=-1, keepdims=True))
        pb = (p / jnp.sum(p, axis=-1, keepdims=True)).astype(_BF16)
        oc_scr[...] = _pick_own_kv_head(jnp.dot(pb, vc_ref[...], preferred_element_type=_F32), n_q)
        pm = jnp.dot(pb, ov_ref[...], preferred_element_type=_F32)
        imp = jnp.concatenate(
            [sum(pm[(h * Q_PER_KV + g) * n_q:(h * Q_PER_KV + g + 1) * n_q] for g in range(Q_PER_KV))
             for h in range(N_KV_HEADS)], axis=0)
        blk = lax.broadcasted_iota(jnp.int32, imp.shape, 1)
        t = past_len + lax.broadcasted_iota(jnp.int32, imp.shape, 0) % n_q
        cur = t // SEL_BLOCK
        forced = (blk == 0) | (blk == cur) | (blk == cur - 1)
        score = jnp.where(forced, FORCE, jnp.where(blk * SEL_BLOCK <= t, imp, -FORCE))
        score = jnp.where(blk < n_sel, score, -jnp.inf)
        rank = jnp.zeros(imp.shape, jnp.int32)
        for sp in range(n_sel):
            col = score[:, sp:sp + 1]
            rank += ((col > score) | ((col == score) & (blk > sp))).astype(jnp.int32)
        sel = (rank < min(SEL_TOPK, n_sel)).astype(_BF16)
        sel_rows = jnp.concatenate([sel[h * n_q:(h + 1) * n_q] for h in range(N_KV_HEADS) for _ in range(Q_PER_KV)],
                                   axis=0)
        addsel[...] = (jnp.dot(sel_rows, exp_ref[...], preferred_element_type=_F32) - 1.0) * (-NEG_INF) + tsel_ref[...]
        kw_t = win_ref[:width, :].astype(_BF16)
        vw_t = win_ref[width:, :].astype(_BF16)
        s1 = jnp.dot(qbd, kw_t, preferred_element_type=_F32) + twin_ref[:, :w_buf]
        s2 = _dot_nt(qbd, new_rows[:, 2 * width:3 * width]) + twin_ref[:, w_buf:]
        m = jnp.maximum(jnp.max(s1, axis=-1, keepdims=True), jnp.max(s2, axis=-1, keepdims=True))
        p1 = jnp.exp(s1 - m)
        p2 = jnp.exp(s2 - m)
        l = jnp.sum(p1, axis=-1, keepdims=True) + jnp.sum(p2, axis=-1, keepdims=True)
        ow = (_dot_nt(p1.astype(_BF16), vw_t)
              + jnp.dot(p2.astype(_BF16), new_rows[:, 3 * width:], preferred_element_type=_F32))
        ow_scr[...] = _pick_own_kv_head(ow / l, n_q)
        m_scr[...] = jnp.full(m_scr.shape, NEG_INF, _F32)
        l_scr[...] = jnp.zeros(l_scr.shape, _F32)
        acc_scr[...] = jnp.zeros(acc_scr.shape, _F32)

    def flash(s, pv):
        m_old = m_scr[...]
        m_new = jnp.maximum(m_old, jnp.max(s, axis=-1, keepdims=True))
        alpha = jnp.exp(m_old - m_new)
        p = jnp.exp(s - m_new)
        m_scr[...] = m_new
        l_scr[...] = alpha * l_scr[...] + jnp.sum(p, axis=-1, keepdims=True)
        acc_scr[...] = alpha * acc_scr[...] + pv(p.astype(_BF16))

    k_t = jnp.concatenate([pg[:width, :] for pg in pages], axis=1).astype(_BF16)
    v_t = jnp.concatenate([pg[width:, :] for pg in pages], axis=1).astype(_BF16)
    flash(jnp.dot(qbd, k_t, preferred_element_type=_F32)
          + addsel[:, pl.ds(pl.multiple_of(j * step_keys, step_keys), step_keys)], lambda p: _dot_nt(p, v_t))

    @pl.when(j == pl.num_programs(1) - 1)
    def _last():
        flash(_dot_nt(qbd, new_rows[:, :width]) + addsel[:, past_len:],
              lambda p: jnp.dot(p, new_rows[:, width:2 * width], preferred_element_type=_F32))
        o_s = _pick_own_kv_head(acc_scr[...] / l_scr[...], n_q)
        gates = jax.nn.sigmoid(gl_ref[...])
        o_ref[...] = gates[:, 0:1] * oc_scr[...] + gates[:, 1:2] * o_s + gates[:, 2:3] * ow_scr[...]


def _nsa_sample_side(cache_kv, page_table, cache_win, kv_new, rel_bias, prm):
    n_seq, n_pages = page_table.shape
    n_phys, page_rows = cache_kv.shape[:2]
    n_q = kv_new.shape[1]
    width = N_KV_HEADS * HEAD_DIM
    past_len = n_pages * page_rows
    w_buf = cache_win.shape[1]
    tk = past_len + n_q
    n_cmp = (tk - CMP_LEN) // CMP_STRIDE + 1
    n_cmp_pad = past_len // CMP_STRIDE
    n_sel = -(-tk // SEL_BLOCK)
    assert n_cmp == n_cmp_pad - 1 and n_cmp_pad == LANES and n_sel <= LANES and n_q <= LANES
    assert page_rows % SEL_BLOCK == 0 and n_pages % PAGES_PER_STEP == 0 and w_buf == min(WINDOW, past_len)
    cache_t = jnp.transpose(cache_kv, (0, 2, 3, 4, 1)).reshape(n_phys, KV_SLOTS_PAGED * width, page_rows)
    win_t = jnp.transpose(cache_win, (0, 2, 3, 4, 1)).reshape(n_seq, KV_SLOTS_WIN * width, w_buf)
    kc = _cmp_pages_t(cache_t, 0, page_table, prm['cmp_pos_k'], prm['cmp_w1_k'], prm['cmp_w2_k'])
    vc = _cmp_pages_t(cache_t, 1, page_table, prm['cmp_pos_v'], prm['cmp_w1_v'], prm['cmp_w2_v'])
    rows = N_HEADS * n_q
    t = past_len + np.arange(n_q)[:, None]
    masked = lambda ok, d: jnp.where(jnp.asarray(ok), _bias_by_distance(rel_bias, d), NEG_INF).reshape(rows, -1)
    n = np.arange(n_cmp_pad)[None, :]
    d_c = t - (n * CMP_STRIDE + CMP_LEN - 1)
    t_cmp = masked((d_c >= 0) & (n < n_cmp), d_c)
    pos = np.arange(past_len + LANES)[None, :]
    t_sel = masked((t - pos >= 0) & (pos < tk), t - pos)
    l = np.arange(w_buf + LANES)[None, :]
    d_w = t - (past_len - w_buf + l)
    t_win = masked((d_w >= 0) & (d_w <= WINDOW) & (l < w_buf + n_q), d_w)
    expand = (np.arange(LANES)[:, None] == pos // SEL_BLOCK) & (np.arange(LANES)[:, None] < n_sel)
    c_start = np.arange(n_cmp_pad)[:, None] * CMP_STRIDE
    s_start = np.arange(LANES)[None, :] * SEL_BLOCK
    ov = ((c_start < s_start + SEL_BLOCK) & (c_start + CMP_LEN > s_start) & (np.arange(n_cmp_pad)[:, None] < n_cmp)
          & (np.arange(LANES)[None, :] < n_sel))
    return {
        'cache_t': cache_t, 'page_table': page_table, 'kc': kc, 'vc': vc, 'win_t': win_t,
        'new': kv_new.reshape(n_seq, n_q, -1)[:, :, 2 * width:],
        't_cmp': t_cmp, 't_sel': t_sel, 't_win': t_win,
        'expand': jnp.asarray(expand, _BF16), 'ov': jnp.asarray(ov, _BF16), 'n_sel': n_sel, 'past_len': past_len,
    }


def _nsa_sample_attention(q, glog, side):
    n_seq, n_q, _ = q.shape
    glog = glog.reshape(n_seq, n_q, N_KV_HEADS, LANES)[..., :Q_PER_KV * N_BRANCH]
    rows = N_HEADS * n_q
    width = N_KV_HEADS * HEAD_DIM
    page_table = side['page_table']
    n_pages = page_table.shape[1]
    page_rows = side['cache_t'].shape[2]
    past_len = side['past_len']
    w_buf = side['win_t'].shape[2]
    n_steps = n_pages // PAGES_PER_STEP
    q_rows = jnp.transpose(q.reshape(n_seq, n_q, N_HEADS, HEAD_DIM), (0, 2, 1, 3)).reshape(n_seq, rows, HEAD_DIM)
    gl = jnp.transpose(glog.reshape(n_seq, n_q, N_HEADS, N_BRANCH), (0, 2, 1, 3)).reshape(n_seq, rows, N_BRANCH)
    gl = jnp.pad(gl, ((0, 0), (0, 0), (0, LANES - N_BRANCH)))
    per_seq = lambda *shape: pl.BlockSpec((None,) + shape, lambda b, j, pt: (b,) + (0,) * len(shape))
    const = lambda *shape: pl.BlockSpec(shape, lambda b, j, pt: (0,) * len(shape))
    page_spec = lambda k: pl.BlockSpec((None, 2 * width, page_rows),
                                       lambda b, j, pt: (pt[b, j * PAGES_PER_STEP + k], 1, 0))
    out = pl.pallas_call(
        functools.partial(_nsa_sample_kernel, n_q=n_q, past_len=past_len, n_sel=side['n_sel']),
        grid_spec=pltpu.PrefetchScalarGridSpec(
            num_scalar_prefetch=1,
            grid=(n_seq, n_steps),
            in_specs=[per_seq(rows, HEAD_DIM), per_seq(rows, LANES), per_seq(LANES, width), per_seq(LANES, width),
                      per_seq(2 * width, w_buf), per_seq(n_q, 4 * width)]
            + [page_spec(k) for k in range(PAGES_PER_STEP)]
            + [const(rows, LANES), const(rows, past_len + LANES), const(rows, w_buf + LANES),
               const(LANES, past_len + LANES), const(LANES, LANES)],
            out_specs=per_seq(rows, HEAD_DIM),
            scratch_shapes=[pltpu.VMEM((rows, past_len + LANES), _F32), pltpu.VMEM((rows, 1), _F32),
                            pltpu.VMEM((rows, 1), _F32), pltpu.VMEM((rows, width), _F32),
                            pltpu.VMEM((rows, HEAD_DIM), _F32), pltpu.VMEM((rows, HEAD_DIM), _F32)],
        ),
        out_shape=jax.ShapeDtypeStruct((n_seq, rows, HEAD_DIM), _F32),
        compiler_params=pltpu.CompilerParams(dimension_semantics=("parallel", "arbitrary"),
                                             vmem_limit_bytes=VMEM_LIMIT),
        name="nsa_sample",
    )(page_table, q_rows, gl, side['kc'], side['vc'], side['win_t'], side['new'],
      *([side['cache_t']] * PAGES_PER_STEP), side['t_cmp'], side['t_sel'], side['t_win'], side['expand'], side['ov'])
    return jnp.transpose(out.reshape(n_seq, N_HEADS, n_q, HEAD_DIM), (0, 2, 1, 3)).reshape(n_seq, n_q, N_HEADS * HEAD_DIM)


def _run_trunk(x, p, s0_re, s0_im, paged, win_buf, prm):
    bsz, t_len, _ = x.shape
    ssm_re, ssm_im = [], []
    side = None
    kv_rows_new = None
    win_state = None
    y = None
    x2 = x.reshape(-1, D_MODEL)
    as_seq = lambda a: a.reshape(bsz, t_len, a.shape[-1])
    for i in range(DEPTH):
        if i < N_A_LAYERS:
            u2, = _rows_call(lambda xt, gain: (_norm(xt, gain),), [x2], [prm['g_mix'][i].reshape(1, -1)],
                             [D_MODEL], "s5_norm")
            ssm = (prm['ssm_a_re'][i], prm['ssm_a_im'][i], prm['ssm_log_dt'][i], prm['ssm_b_re'][i],
                   prm['ssm_b_im'][i], prm['ssm_c_re'][i], prm['ssm_c_im'][i], prm['ssm_d'][i])
            s0 = _state_to_tiles(s0_re[i], s0_im[i])
            if t_len == 8:
                ys, s_fin = _s5_scan(u2[None], jnp.transpose(s0, (1, 0, 2))[None], False, *ssm)
                s_re, s_im = _state_from_tiles(jnp.transpose(s_fin[0], (1, 0, 2)))
            else:
                assert t_len % (S5_CHUNK * SUBLANES) == 0
                ys, s_fin = _s5_scan(as_seq(u2), s0[:, :, None, :], True, *ssm)
                s_re, s_im = _state_from_tiles(s_fin[:, :, 0, :])
            ssm_re.append(s_re)
            ssm_im.append(s_im)
            x2 = _glu_residual(x2, ys.reshape(-1, D_MODEL), prm['w_glu'][i])
        else:
            j = i - N_A_LAYERS
            q, glog = _qg_project(x2, prm['g_mix'][i], prm['w_qg'][j])
            attend = _nsa_prompt_attention if paged is None else _nsa_sample_attention
            mixed = attend(as_seq(q), as_seq(glog), side)
            x2 = _wo_residual(x2, mixed.reshape(-1, D_MODEL), prm['w_o'][j])
        x2 = _moe_layer(x2, prm['g_ffn'][i], prm['w_route_group'][i], prm['b_route_group'][i],
                        prm['w_route_expert'][i], prm['b_route_expert'][i], prm['w_exp_up'], prm['w_exp_down'], i)
        ple = (x2, p[i].reshape(-1, p.shape[-1]), prm['g_ple'][i], prm['w_ple_gate'][i], prm['w_ple_proj'][i])
        if i == N_A_LAYERS - 1:
            x2, kv2, *operands = _ple_residual(*ple, g_next=prm['g_kv'], w_next=prm['w_kv'],
                                               attention_operands=paged is None)
            kv = kv2.reshape(bsz, t_len, KV_SLOTS_PAGED + KV_SLOTS_WIN, N_KV_HEADS, HEAD_DIM)
            kv_rows_new, win_new = kv[:, :, :KV_SLOTS_PAGED], kv[:, :, KV_SLOTS_PAGED:]
            if paged is None:
                win_state = win_new[:, -min(WINDOW, t_len):]
                side = _nsa_prompt_side(kv, *operands, prm)
            else:
                w_buf = win_buf.shape[1]
                win_state = jnp.concatenate([win_buf, win_new], axis=1)[:, -w_buf:]
                side = _nsa_sample_side(paged[0], paged[1], win_buf, kv, prm['rel_bias'], prm)
        elif i == DEPTH - 1:
            x2, y = _ple_residual(*ple, g_next=prm['g_final'])
        else:
            x2, = _ple_residual(*ple)
    return as_seq(y), kv_rows_new, win_state, jnp.stack(ssm_re), jnp.stack(ssm_im)


def kernel(x_prompt, x_sample, p_prompt, p_sample, cache_kv, cache_win, state_ssm_re, state_ssm_im, page_table,
           g_mix, g_ffn, g_ple, g_kv, g_final,
           ssm_a_re, ssm_a_im, ssm_log_dt, ssm_b_re, ssm_b_im, ssm_c_re, ssm_c_im, ssm_d, w_glu,
           w_kv, cmp_pos_k, cmp_pos_v, cmp_w1_k, cmp_w2_k, cmp_w1_v, cmp_w2_v, w_qg, w_o, rel_bias,
           w_route_group, b_route_group, w_route_expert, b_route_expert, w_exp_up, w_exp_down,
           w_ple_proj, w_ple_gate):
    prm = {
        'g_mix': g_mix, 'g_ffn': g_ffn, 'g_ple': g_ple, 'g_kv': g_kv, 'g_final': g_final,
        'ssm_a_re': ssm_a_re, 'ssm_a_im': ssm_a_im, 'ssm_log_dt': ssm_log_dt,
        'ssm_b_re': ssm_b_re, 'ssm_b_im': ssm_b_im, 'ssm_c_re': ssm_c_re, 'ssm_c_im': ssm_c_im,
        'ssm_d': ssm_d, 'w_glu': w_glu,
        'w_kv': w_kv, 'cmp_pos_k': cmp_pos_k, 'cmp_pos_v': cmp_pos_v,
        'cmp_w1_k': cmp_w1_k, 'cmp_w2_k': cmp_w2_k, 'cmp_w1_v': cmp_w1_v, 'cmp_w2_v': cmp_w2_v,
        'w_qg': w_qg, 'w_o': w_o, 'rel_bias': rel_bias,
        'w_route_group': w_route_group, 'b_route_group': b_route_group,
        'w_route_expert': w_route_expert, 'b_route_expert': b_route_expert,
        'w_exp_up': w_exp_up, 'w_exp_down': w_exp_down,
        'w_ple_proj': w_ple_proj, 'w_ple_gate': w_ple_gate,
    }
    zero_state = jnp.zeros((N_A_LAYERS, x_prompt.shape[0], SSM_GROUPS, SSM_STATE), _F32)
    y_prompt, kv_prompt, win_prompt, ssm_re_prompt, ssm_im_prompt = _run_trunk(
        x_prompt, p_prompt, zero_state, zero_state, None, None, prm)
    y_sample, kv_sample, win_sample, ssm_re_sample, ssm_im_sample = _run_trunk(
        x_sample, p_sample, state_ssm_re, state_ssm_im, (cache_kv, page_table), cache_win, prm)
    return (y_prompt, y_sample, kv_prompt, win_prompt, ssm_re_prompt, ssm_im_prompt,
            kv_sample, win_sample, ssm_re_sample, ssm_im_sample)
```
